```python
import math
import jax
import jax.numpy as jnp
from jax import lax
import numpy as np

D_MODEL = 1024
BATCH = 32
SEQ = 256
DEPTH = 4
DEC_BATCH = 4
DEC_SEQ = 4096
PAST_LEN = 256

GRID_W = 64
N_HEADS = 8
QK_NOPE = 64
QK_ROPE = 32
QK_DIM = QK_NOPE + QK_ROPE
V_DIM = 64
Q_LORA = 384
KV_LORA = 256
ROPE_BASE = 10000.0
BLOCK_Q = 128
SSM_WIDTH = 512
SSM_GROUP = 16
SSM_GROUPS = SSM_WIDTH // SSM_GROUP
SSM_STATE = 64
N_EXPERTS = 16
N_EXPERT_GROUPS = 4
EXPERTS_PER_GROUP = N_EXPERTS // N_EXPERT_GROUPS
TOP_K_GROUPS = 1
TOP_K = 2
D_EXPERT = 512

N_MLA_LAYERS = (DEPTH + 1) // 2
N_SSM_LAYERS = DEPTH // 2
EPS = 1e-6

kernel_name = 'hybrid_mla_s5_moe_diffusion_step'

F32 = jnp.float32


def rms_norm(x, w):
    xf = x.astype(F32)
    y = xf * lax.rsqrt(jnp.mean(xf * xf, axis=-1, keepdims=True) + EPS)
    return (y * w.astype(F32)).astype(x.dtype)


def adaln(cond, w, b):
    m = (jax.nn.silu(cond) @ w + b)[:, None, :]
    return jnp.split(m, 6, axis=-1)


def modulate(h, shift, scale):
    return h * (1 + scale) + shift


def axial_rope_tables(length):
    rows = length // GRID_W
    t_row = jnp.repeat(jnp.arange(rows, dtype=F32), GRID_W)
    t_col = jnp.tile(jnp.arange(GRID_W, dtype=F32), rows)
    half = QK_ROPE // 2
    inv = ROPE_BASE ** (-jnp.arange(0, half, 2, dtype=F32) / half)
    ang = jnp.concatenate([t_row[:, None] * inv, t_col[:, None] * inv], axis=-1)
    return jnp.cos(ang), jnp.sin(ang)


def apply_axial_rope(x, cos, sin):
    shp = x.shape
    nf = QK_ROPE // 4
    xr = x.astype(F32).reshape(shp[:-1] + (2, 2, nf))
    c = cos.reshape(cos.shape[0], 1, 2, nf)
    s = sin.reshape(sin.shape[0], 1, 2, nf)
    x1, x2 = xr[..., 0, :], xr[..., 1, :]
    out = jnp.stack([x1 * c - x2 * s, x1 * s + x2 * c], axis=-2)
    return out.reshape(shp).astype(x.dtype)


def rotate_rope_part(x, cos, sin):
    return jnp.concatenate([x[..., :QK_NOPE], apply_axial_rope(x[..., QK_NOPE:], cos, sin)], axis=-1)


def block_attention(q, k, v):
    b, lq, h, dk = q.shape
    nb = lq // BLOCK_Q
    qb = jnp.moveaxis(q.reshape(b, nb, BLOCK_Q, h, dk), 1, 0)
    scale = dk ** -0.5

    def one_block(qi):
        s = jnp.einsum('bqhd,bkhd->bhqk', qi, k, preferred_element_type=F32) * scale
        p = jax.nn.softmax(s, axis=-1).astype(v.dtype)
        return jnp.einsum('bhqk,bkhd->bqhd', p, v)

    out = lax.map(one_block, qb)
    return jnp.moveaxis(out, 0, 1).reshape(b, lq, h, v.shape[-1])


def mla_queries(h, w_q_a, q_a_norm, w_q_b, q_norm):
    b, l, _ = h.shape
    cq = rms_norm(h @ w_q_a, q_a_norm)
    q = (cq @ w_q_b).reshape(b, l, N_HEADS, QK_DIM)
    return rms_norm(q, q_norm)


def mla_compress_kv(h, w_kv_a, kv_a_norm):
    ckv_kr = h @ w_kv_a
    return rms_norm(ckv_kr[..., :KV_LORA], kv_a_norm), ckv_kr[..., KV_LORA:]


def mla_expand_kv(ckv, k_rope, w_kv_b, k_norm):
    b, l, _ = ckv.shape
    kv = (ckv @ w_kv_b).reshape(b, l, N_HEADS, QK_NOPE + V_DIM)
    k_nope, v = kv[..., :QK_NOPE], kv[..., QK_NOPE:]
    k_r = jnp.broadcast_to(k_rope[:, :, None, :], (b, l, N_HEADS, QK_ROPE))
    k = rms_norm(jnp.concatenate([k_nope, k_r], axis=-1), k_norm)
    return k, v


def mla_output(o, w_o):
    b, l = o.shape[:2]
    return o.reshape(b, l, N_HEADS * V_DIM) @ w_o


def mla_context(h, w_q_a, q_a_norm, w_q_b, w_kv_a, kv_a_norm, w_kv_b, q_norm, k_norm, w_o):
    q = mla_queries(h, w_q_a, q_a_norm, w_q_b, q_norm)
    ckv, k_rope = mla_compress_kv(h, w_kv_a, kv_a_norm)
    k, v = mla_expand_kv(ckv, k_rope, w_kv_b, k_norm)
    return mla_output(block_attention(q, k, v), w_o), ckv, k_rope


def mla_latent(h, ctx_ckv, ctx_krope, cos, sin, w_q_a, q_a_norm, w_q_b, w_kv_a, kv_a_norm, w_kv_b, q_norm, k_norm, w_o):
    q = rotate_rope_part(mla_queries(h, w_q_a, q_a_norm, w_q_b, q_norm), cos, sin)
    ckv, k_rope = mla_compress_kv(h, w_kv_a, kv_a_norm)
    k, v = mla_expand_kv(ckv, k_rope, w_kv_b, k_norm)
    k = rotate_rope_part(k, cos, sin)
    kc, vc = mla_expand_kv(ctx_ckv.astype(h.dtype), ctx_krope.astype(h.dtype), w_kv_b, k_norm)
    o = block_attention(q, jnp.concatenate([kc, k], axis=1), jnp.concatenate([vc, v], axis=1))
    return mla_output(o, w_o)


def s5_discretize(a_re, a_im, log_dt, b_re, b_im):
    a_re, a_im, b_re, b_im = (t.astype(F32) for t in (a_re, a_im, b_re, b_im))
    dt = jnp.exp(log_dt.astype(F32))[:, None]
    mag = jnp.exp(dt * a_re)
    ab_re, ab_im = mag * jnp.cos(dt * a_im), mag * jnp.sin(dt * a_im)
    den = a_re * a_re + a_im * a_im
    nr = ab_re - 1.0
    coef_re = (nr * a_re + ab_im * a_im) / den
    coef_im = (ab_im * a_re - nr * a_im) / den
    bb_re = coef_re[..., None] * b_re - coef_im[..., None] * b_im
    bb_im = coef_re[..., None] * b_im + coef_im[..., None] * b_re
    return ab_re, ab_im, bb_re, bb_im


def _ssm_combine(e1, e2):
    a1r, a1i, b1r, b1i = e1
    a2r, a2i, b2r, b2i = e2
    return (a2r * a1r - a2i * a1i,
            a2r * a1i + a2i * a1r,
            a2r * b1r - a2i * b1i + b2r,
            a2r * b1i + a2i * b1r + b2i)


def s5_scan(u, h0_re, h0_im, a_re, a_im, log_dt, b_re, b_im, c_re, c_im):
    ab_re, ab_im, bb_re, bb_im = s5_discretize(a_re, a_im, log_dt, b_re, b_im)
    bu_re = jnp.einsum('gpk,blgk->blgp', bb_re, u)
    bu_im = jnp.einsum('gpk,blgk->blgp', bb_im, u)
    h0_re, h0_im = h0_re.astype(F32), h0_im.astype(F32)
    bu_re = bu_re.at[:, 0].add(ab_re * h0_re - ab_im * h0_im)
    bu_im = bu_im.at[:, 0].add(ab_re * h0_im + ab_im * h0_re)
    l = u.shape[1]
    a_r = jnp.broadcast_to(ab_re[None, None], (1, l) + ab_re.shape)
    a_i = jnp.broadcast_to(ab_im[None, None], (1, l) + ab_im.shape)
    _, _, h_re, h_im = lax.associative_scan(_ssm_combine, (a_r, a_i, bu_re, bu_im), axis=1)
    y = (jnp.einsum('gkp,blgp->blgk', c_re.astype(F32), h_re)
         - jnp.einsum('gkp,blgp->blgk', c_im.astype(F32), h_im))
    return y, h_re[:, -1], h_im[:, -1]


def s5_mixer(h, h0_re, h0_im, w_in, a_re, a_im, log_dt, b_re, b_im, c_re, c_im, d_skip, w_glu):
    b, l, _ = h.shape
    u = (h @ w_in).astype(F32).reshape(b, l, SSM_GROUPS, SSM_GROUP)
    y_f, hf_re, hf_im = s5_scan(u, h0_re[:, 0], h0_im[:, 0], a_re[0], a_im[0], log_dt[0],
                                b_re[0], b_im[0], c_re[0], c_im[0])
    y_b, hb_re, hb_im = s5_scan(u[:, ::-1], h0_re[:, 1], h0_im[:, 1], a_re[1], a_im[1], log_dt[1],
                                b_re[1], b_im[1], c_re[1], c_im[1])
    y = y_f + y_b[:, ::-1] + d_skip.astype(F32).reshape(SSM_GROUPS, SSM_GROUP) * u
    y = jax.nn.gelu(y.reshape(b, l, SSM_WIDTH)).astype(h.dtype)
    z = y @ w_glu
    out = z[..., :D_MODEL] * jax.nn.sigmoid(z[..., D_MODEL:])
    return out, jnp.stack([hf_re, hb_re], axis=1), jnp.stack([hf_im, hb_im], axis=1)


def grouped_moe(h, w_router, b_router, w_gate, w_up, w_down):
    b, l, d = h.shape
    t = h.reshape(-1, d)
    scores = jax.nn.sigmoid((t @ w_router).astype(F32))
    sel = scores + b_router.astype(F32)
    grp = sel.reshape(-1, N_EXPERT_GROUPS, EXPERTS_PER_GROUP)
    grp_score = jnp.sum(lax.top_k(grp, TOP_K)[0], axis=-1)
    _, g_idx = lax.top_k(grp_score, TOP_K_GROUPS)
    g_mask = jnp.sum(jax.nn.one_hot(g_idx, N_EXPERT_GROUPS, dtype=F32), axis=1)
    masked = jnp.where(g_mask[..., None] > 0, grp, -jnp.inf).reshape(-1, N_EXPERTS)
    _, e_idx = lax.top_k(masked, TOP_K)
    w_sel = jnp.take_along_axis(scores, e_idx, axis=-1)
    w_sel = w_sel / jnp.sum(w_sel, axis=-1, keepdims=True)
    gates = jnp.sum(jax.nn.one_hot(e_idx, N_EXPERTS, dtype=F32) * w_sel[..., None], axis=1)
    hid = jax.nn.silu(jnp.einsum('td,edf->tef', t, w_gate)) * jnp.einsum('td,edf->tef', t, w_up)
    hid = hid * gates[..., None].astype(hid.dtype)
    return jnp.einsum('tef,efd->td', hid, w_down).reshape(b, l, d)


def setup_inputs(seed: int = 0) -> dict:
    key = jax.random.key(seed)
    ks = iter(jax.random.split(key, 48))

    def nrm(shape, scale):
        return jax.random.normal(next(ks), shape, F32) * scale

    G, P, K = SSM_GROUPS, SSM_STATE, SSM_GROUP
    nm, ns = N_MLA_LAYERS, N_SSM_LAYERS
    a_im_base = jnp.pi * jnp.arange(P, dtype=F32)
    return {
        'x_prompt': nrm((BATCH, SEQ, D_MODEL), 1.0),
        'x_sample': nrm((DEC_BATCH, DEC_SEQ, D_MODEL), 1.0),
        'c': nrm((DEC_BATCH, D_MODEL), 1.0),
        'cache_ckv': nrm((DEC_BATCH, nm, PAST_LEN, KV_LORA), 1.0),
        'cache_krope': nrm((DEC_BATCH, nm, PAST_LEN, QK_ROPE), 1.0),
        'state_ssm_re': nrm((DEC_BATCH, ns, 2, G, P), 0.1),
        'state_ssm_im': nrm((DEC_BATCH, ns, 2, G, P), 0.1),
        'c_ctx': nrm((D_MODEL,), 1.0),
        'w_mod': nrm((DEPTH, D_MODEL, 6 * D_MODEL), 0.5 * D_MODEL ** -0.5),
        'b_mod': nrm((DEPTH, 6 * D_MODEL), 0.02),
        'norm1_w': 1.0 + nrm((DEPTH, D_MODEL), 0.05),
        'norm2_w': 1.0 + nrm((DEPTH, D_MODEL), 0.05),
        'mla_w_q_a': nrm((nm, D_MODEL, Q_LORA), D_MODEL ** -0.5),
        'mla_q_a_norm': 1.0 + nrm((nm, Q_LORA), 0.05),
        'mla_w_q_b': nrm((nm, Q_LORA, N_HEADS * QK_DIM), Q_LORA ** -0.5),
        'mla_w_kv_a': nrm((nm, D_MODEL, KV_LORA + QK_ROPE), D_MODEL ** -0.5),
        'mla_kv_a_norm': 1.0 + nrm((nm, KV_LORA), 0.05),
        'mla_w_kv_b': nrm((nm, KV_LORA, N_HEADS * (QK_NOPE + V_DIM)), KV_LORA ** -0.5),
        'mla_q_norm': 1.0 + nrm((nm, QK_DIM), 0.05),
        'mla_k_norm': 1.0 + nrm((nm, QK_DIM), 0.05),
        'mla_w_o': nrm((nm, N_HEADS * V_DIM, D_MODEL), (N_HEADS * V_DIM) ** -0.5),
        'ssm_w_in': nrm((ns, D_MODEL, SSM_WIDTH), D_MODEL ** -0.5),
        'ssm_a_re': -0.5 * (1.0 + nrm((ns, 2, G, P), 0.01)),
        'ssm_a_im': a_im_base + nrm((ns, 2, G, P), 0.01),
        'ssm_log_dt': jax.random.uniform(next(ks), (ns, 2, G), F32, math.log(1e-3), math.log(1e-1)),
        'ssm_b_re': nrm((ns, 2, G, P, K), (2 * K) ** -0.5),
        'ssm_b_im': nrm((ns, 2, G, P, K), (2 * K) ** -0.5),
        'ssm_c_re': nrm((ns, 2, G, K, P), 0.5 ** 0.5),
        'ssm_c_im': nrm((ns, 2, G, K, P), 0.5 ** 0.5),
        'ssm_d': nrm((ns, SSM_WIDTH), 1.0),
        'ssm_w_glu': nrm((ns, SSM_WIDTH, 2 * D_MODEL), SSM_WIDTH ** -0.5),
        'w_router': nrm((D_MODEL, N_EXPERTS), D_MODEL ** -0.5),
        'b_router': nrm((N_EXPERTS,), 0.01),
        'moe_w_gate': nrm((DEPTH, N_EXPERTS, D_MODEL, D_EXPERT), D_MODEL ** -0.5),
        'moe_w_up': nrm((DEPTH, N_EXPERTS, D_MODEL, D_EXPERT), D_MODEL ** -0.5),
        'moe_w_down': nrm((DEPTH, N_EXPERTS, D_EXPERT, D_MODEL), D_EXPERT ** -0.5),
    }


def reference(x_prompt, x_sample, c, cache_ckv, cache_krope, state_ssm_re, state_ssm_im,
              c_ctx, w_mod, b_mod, norm1_w, norm2_w,
              mla_w_q_a, mla_q_a_norm, mla_w_q_b, mla_w_kv_a, mla_kv_a_norm, mla_w_kv_b,
              mla_q_norm, mla_k_norm, mla_w_o,
              ssm_w_in, ssm_a_re, ssm_a_im, ssm_log_dt, ssm_b_re, ssm_b_im, ssm_c_re, ssm_c_im,
              ssm_d, ssm_w_glu,
              w_router, b_router, moe_w_gate, moe_w_up, moe_w_down):
    cos, sin = axial_rope_tables(x_sample.shape[1])
    yp, ys = x_prompt, x_sample
    ckv_out, krope_out, sre_out, sim_out = [], [], [], []
    for i in range(DEPTH):
        j = i // 2
        sh1p, sc1p, g1p, sh2p, sc2p, g2p = adaln(c_ctx[None, :], w_mod[i], b_mod[i])
        sh1s, sc1s, g1s, sh2s, sc2s, g2s = adaln(c, w_mod[i], b_mod[i])
        hp = modulate(rms_norm(yp, norm1_w[i]), sh1p, sc1p)
        hs = modulate(rms_norm(ys, norm1_w[i]), sh1s, sc1s)
        if i % 2 == 0:
            mla_p = (mla_w_q_a[j], mla_q_a_norm[j], mla_w_q_b[j], mla_w_kv_a[j], mla_kv_a_norm[j],
                     mla_w_kv_b[j], mla_q_norm[j], mla_k_norm[j], mla_w_o[j])
            op, ckv_p, kr_p = mla_context(hp, *mla_p)
            os_ = mla_latent(hs, cache_ckv[:, j], cache_krope[:, j], cos, sin, *mla_p)
            ckv_out.append(ckv_p)
            krope_out.append(kr_p)
        else:
            ssm_p = (ssm_w_in[j], ssm_a_re[j], ssm_a_im[j], ssm_log_dt[j], ssm_b_re[j], ssm_b_im[j],
                     ssm_c_re[j], ssm_c_im[j], ssm_d[j], ssm_w_glu[j])
            h0 = jnp.zeros((yp.shape[0], 2, SSM_GROUPS, SSM_STATE), F32)
            op, s_re, s_im = s5_mixer(hp, h0, h0, *ssm_p)
            os_, _, _ = s5_mixer(hs, state_ssm_re[:, j], state_ssm_im[:, j], *ssm_p)
            sre_out.append(s_re)
            sim_out.append(s_im)
        yp = yp + g1p * op
        ys = ys + g1s * os_
        hp = modulate(rms_norm(yp, norm2_w[i]), sh2p, sc2p)
        hs = modulate(rms_norm(ys, norm2_w[i]), sh2s, sc2s)
        yp = yp + g2p * grouped_moe(hp, w_router, b_router, moe_w_gate[i], moe_w_up[i], moe_w_down[i])
        ys = ys + g2s * grouped_moe(hs, w_router, b_router, moe_w_gate[i], moe_w_up[i], moe_w_down[i])
    new_cache_ckv = jnp.stack(ckv_out, axis=1)
    new_cache_krope = jnp.stack(krope_out, axis=1)
    new_state_ssm_re = jnp.stack(sre_out, axis=1)
    new_state_ssm_im = jnp.stack(sim_out, axis=1)
    return (yp, ys, new_cache_ckv, new_cache_krope, new_state_ssm_re, new_state_ssm_im)
```

```python
import functools
import math

import jax
import jax.numpy as jnp
import numpy as np
from jax import lax
from jax.experimental import pallas as pl
from jax.experimental.pallas import tpu as pltpu

F32 = jnp.float32
BF16 = jnp.bfloat16
EPS = 1e-6

GRID_W = 64
N_HEADS = 8
QK_NOPE = 64
QK_ROPE = 32
QK_DIM = QK_NOPE + QK_ROPE
V_DIM = 64
ROPE_BASE = 10000.0
SSM_GROUP = 16
SSM_STATE = 64
N_EXPERT_GROUPS = 4
EXPERTS_PER_GROUP = 4

LANES = 128
HEAD_PAD = LANES
SSM_CHUNK = 16
VMEM_LIMIT = 48 * 1024 * 1024


def _cparams(sem, vmem=VMEM_LIMIT):
    return pltpu.CompilerParams(dimension_semantics=sem, vmem_limit_bytes=vmem)


def _dot(a, b):
    return jnp.dot(a, b, preferred_element_type=F32)


def _dot_nt(a, b):
    return lax.dot_general(a, b, (((1,), (1,)), ((), ())), preferred_element_type=F32)


def _dot_hi(a, b):
    return jnp.dot(a, b, preferred_element_type=F32, precision=lax.Precision.HIGHEST)


def _norm_mod(x, nw, shift, scale):
    ms = jnp.mean(x * x, axis=-1, keepdims=True)
    return (x * lax.rsqrt(ms + EPS) * nw) * (1.0 + scale) + shift


class Layout:
    def __init__(self, n_ctx_b, ctx_len, n_lat_b, lat_len, tm):
        self.n_ctx_b, self.ctx_len, self.n_lat_b, self.lat_len = n_ctx_b, ctx_len, n_lat_b, lat_len
        self.t_ctx = n_ctx_b * ctx_len
        self.t_lat = n_lat_b * lat_len
        self.t = self.t_ctx + self.t_lat
        self.tm = tm
        assert self.t_ctx % tm == 0 and lat_len % tm == 0
        self.ctx_tiles = self.t_ctx // tm
        self.lat_tiles_per_b = lat_len // tm
        self.n_tiles = self.t // tm

    def seg(self, i):
        return jnp.where(i < self.ctx_tiles, 0, 1 + (i - self.ctx_tiles) // self.lat_tiles_per_b)

    def rope_blk(self, i):
        return jnp.where(i < self.ctx_tiles, 0, 1 + (i - self.ctx_tiles) % self.lat_tiles_per_b)


def _adaln_kernel(c_ref, w_ref, b_ref, o_ref):
    c = c_ref[...]
    s = c * jax.nn.sigmoid(c)
    o_ref[...] = _dot_hi(s, w_ref[...]) + b_ref[...]


def adaln_all(cond8, w_mod, b_mod, tn=1536):
    depth, d, n6 = w_mod.shape
    return pl.pallas_call(
        _adaln_kernel,
        out_shape=jax.ShapeDtypeStruct((depth, 8, n6), F32),
        grid=(depth, n6 // tn),
        in_specs=[pl.BlockSpec((8, d), lambda l, j: (0, 0)),
                  pl.BlockSpec((None, d, tn), lambda l, j: (l, 0, j)),
                  pl.BlockSpec((None, 1, tn), lambda l, j: (l, 0, j))],
        out_specs=pl.BlockSpec((None, 8, tn), lambda l, j: (l, 0, j)),
        compiler_params=_cparams(("parallel", "parallel")),
        name="adaln",
    )(cond8, w_mod, b_mod.reshape(depth, 1, n6))


def _mla_a_kernel(y_ref, mod_ref, nw_ref, w_ref, qan_ref, kvan_ref, cq_ref, ckv_ref, kr_ref, krp_ref,
                  *, q_lora, kv_lora):
    m = mod_ref[...]
    h = _norm_mod(y_ref[...], nw_ref[...], m[0:1], m[1:2])
    z = _dot(h.astype(BF16), w_ref[...])
    cq = z[:, :q_lora]
    cq = cq * lax.rsqrt(jnp.mean(cq * cq, axis=-1, keepdims=True) + EPS) * qan_ref[...]
    cq_ref[...] = cq.astype(BF16)
    ckv = z[:, q_lora:q_lora + kv_lora]
    ckv_ref[...] = ckv * lax.rsqrt(jnp.mean(ckv * ckv, axis=-1, keepdims=True) + EPS) * kvan_ref[...]
    kr_ref[...] = z[:, q_lora + kv_lora:q_lora + kv_lora + LANES]
    krp_ref[...] = z[:, q_lora + kv_lora + LANES:]


def mla_a(lay, y, mod, nw, w_a, qan, kvan, q_lora, kv_lora):
    t, d = y.shape
    tm = lay.tm
    na = w_a.shape[1]
    row = lambda i: (i, 0)
    full = lambda i: (0, 0)
    return pl.pallas_call(
        functools.partial(_mla_a_kernel, q_lora=q_lora, kv_lora=kv_lora),
        out_shape=(jax.ShapeDtypeStruct((t, q_lora), BF16), jax.ShapeDtypeStruct((t, kv_lora), F32),
                   jax.ShapeDtypeStruct((t, LANES), F32), jax.ShapeDtypeStruct((t, LANES), F32)),
        grid=(lay.n_tiles,),
        in_specs=[pl.BlockSpec((tm, d), row),
                  pl.BlockSpec((None, 8, d), lambda i: (lay.seg(i), 0, 0)),
                  pl.BlockSpec((1, d), full), pl.BlockSpec((d, na), full),
                  pl.BlockSpec((1, q_lora), full), pl.BlockSpec((1, kv_lora), full)],
        out_specs=(pl.BlockSpec((tm, q_lora), row), pl.BlockSpec((tm, kv_lora), row),
                   pl.BlockSpec((tm, LANES), row), pl.BlockSpec((tm, LANES), row)),
        compiler_params=_cparams(("parallel",)),
        name="mla_a",
    )(y, mod, nw, w_a, qan, kvan)


def _kv_heads(ckv_b, kr, krp, wk_ref, wv_ref, knw, knwp, cos, sin, k_ref, v_ref):
    kc = knw * cos
    ks = knwp * sin
    for h in range(N_HEADS):
        kz = _dot(ckv_b, wk_ref[h]) + kr
        r = lax.rsqrt(jnp.sum(kz * kz, axis=-1, keepdims=True) * (1.0 / QK_DIM) + EPS)
        k_ref[h] = (r * (kz * kc + krp * ks)).astype(BF16)
    for hp in range(N_HEADS // 2):
        v_ref[hp] = _dot(ckv_b, wv_ref[hp]).astype(BF16)


def _mla_b_kernel(cq_ref, ckv_ref, kr_ref, krp_ref, wq_ref, wk_ref, wv_ref, nrm_ref, cos_ref, sin_ref,
                  q_ref, k_ref, v_ref):
    nrm = nrm_ref[...]
    cos = cos_ref[...]
    sin = sin_ref[...]
    cq = cq_ref[...]
    qc = nrm[0:1] * cos * (QK_DIM ** -0.5)
    qs = nrm[1:2] * sin * (QK_DIM ** -0.5)
    for h in range(N_HEADS):
        z = _dot(cq, wq_ref[h])
        qm = z[:, :HEAD_PAD]
        qp = z[:, HEAD_PAD:]
        r = lax.rsqrt(jnp.sum(qm * qm, axis=-1, keepdims=True) * (1.0 / QK_DIM) + EPS)
        q_ref[h] = (r * (qm * qc + qp * qs)).astype(BF16)
    _kv_heads(ckv_ref[...].astype(BF16), kr_ref[...], krp_ref[...], wk_ref, wv_ref, nrm[2:3], nrm[3:4],
              cos, sin, k_ref, v_ref)


def mla_b(lay, cq, ckv, kr, krp, wq, wk, wv, nrm, cos_t, sin_t):
    t = cq.shape[0]
    tm = lay.tm
    row = lambda i: (i, 0)
    hrow = lambda i: (0, i, 0)
    full3 = lambda i: (0, 0, 0)
    rope = lambda i: (lay.rope_blk(i), 0)
    nh = N_HEADS
    return pl.pallas_call(
        _mla_b_kernel,
        out_shape=(jax.ShapeDtypeStruct((nh, t, HEAD_PAD), BF16), jax.ShapeDtypeStruct((nh, t, HEAD_PAD), BF16),
                   jax.ShapeDtypeStruct((nh // 2, t, LANES), BF16)),
        grid=(lay.n_tiles,),
        in_specs=[pl.BlockSpec((tm, cq.shape[1]), row), pl.BlockSpec((tm, ckv.shape[1]), row),
                  pl.BlockSpec((tm, LANES), row), pl.BlockSpec((tm, LANES), row),
                  pl.BlockSpec(wq.shape, full3), pl.BlockSpec(wk.shape, full3), pl.BlockSpec(wv.shape, full3),
                  pl.BlockSpec((8, LANES), lambda i: (0, 0)),
                  pl.BlockSpec((tm, LANES), rope), pl.BlockSpec((tm, LANES), rope)],
        out_specs=(pl.BlockSpec((nh, tm, HEAD_PAD), hrow), pl.BlockSpec((nh, tm, HEAD_PAD), hrow),
                   pl.BlockSpec((nh // 2, tm, LANES), hrow)),
        compiler_params=_cparams(("parallel",)),
        name="mla_b",
    )(cq, ckv, kr, krp, wq, wk, wv, nrm, cos_t, sin_t)


def _ctx_kv_kernel(ckv_ref, kr_ref, wk_ref, wv_ref, nrm_ref, k_ref, v_ref):
    nrm = nrm_ref[...]
    kr = kr_ref[...]
    one = jnp.ones((1, LANES), F32)
    _kv_heads(ckv_ref[...].astype(BF16), kr, kr, wk_ref, wv_ref, nrm[2:3], nrm[3:4],
              one, jnp.zeros((1, LANES), F32), k_ref, v_ref)


def ctx_kv(ckv, kr, wk, wv, nrm, tm):
    t = ckv.shape[0]
    row = lambda i: (i, 0)
    hrow = lambda i: (0, i, 0)
    full3 = lambda i: (0, 0, 0)
    nh = N_HEADS
    return pl.pallas_call(
        _ctx_kv_kernel,
        out_shape=(jax.ShapeDtypeStruct((nh, t, HEAD_PAD), BF16), jax.ShapeDtypeStruct((nh // 2, t, LANES), BF16)),
        grid=(t // tm,),
        in_specs=[pl.BlockSpec((tm, ckv.shape[1]), row), pl.BlockSpec((tm, LANES), row),
                  pl.BlockSpec(wk.shape, full3), pl.BlockSpec(wv.shape, full3),
                  pl.BlockSpec((8, LANES), lambda i: (0, 0))],
        out_specs=(pl.BlockSpec((nh, tm, HEAD_PAD), hrow), pl.BlockSpec((nh // 2, tm, LANES), hrow)),
        compiler_params=_cparams(("parallel",)),
        name="ctx_kv",
    )(ckv, kr, wk, wv, nrm)


def _attn_kernel(*refs, with_ctx):
    if with_ctx:
        q_ref, k_ref, v_ref, kc_ref, vc_ref, o_ref = refs
    else:
        q_ref, k_ref, v_ref, o_ref = refs
    v = v_ref[0]
    outs = []
    for j in range(2):
        q = q_ref[j]
        s = _dot_nt(q, k_ref[j])
        m = jnp.max(s, axis=-1, keepdims=True)
        if with_ctx:
            sc = _dot_nt(q, kc_ref[j])
            m = jnp.maximum(m, jnp.max(sc, axis=-1, keepdims=True))
        p = jnp.exp(s - m)
        l = jnp.sum(p, axis=-1, keepdims=True)
        acc = _dot(p.astype(BF16), v)
        if with_ctx:
            pc = jnp.exp(sc - m)
            l = l + jnp.sum(pc, axis=-1, keepdims=True)
            acc = acc + _dot(pc.astype(BF16), vc_ref[0])
        outs.append(acc / l)
    lane = lax.broadcasted_iota(jnp.int32, outs[0].shape, 1)
    o_ref[...] = jnp.where(lane < V_DIM, outs[0], outs[1]).astype(BF16)


def attention(q, k, v2, row0, n_b, seq, tq, kc=None, vc2=None, ctx_len=0):
    assert row0 % seq == 0 and seq % tq == 0
    nq = seq // tq
    qb0 = row0 // tq
    kb0 = row0 // seq
    with_ctx = kc is not None
    in_specs = [pl.BlockSpec((2, tq, HEAD_PAD), lambda b, hp, qi: (hp, qb0 + b * nq + qi, 0)),
                pl.BlockSpec((2, seq, HEAD_PAD), lambda b, hp, qi: (hp, kb0 + b, 0)),
                pl.BlockSpec((1, seq, LANES), lambda b, hp, qi: (hp, kb0 + b, 0))]
    args = [q, k, v2]
    if with_ctx:
        in_specs += [pl.BlockSpec((2, ctx_len, HEAD_PAD), lambda b, hp, qi: (hp, b, 0)),
                     pl.BlockSpec((1, ctx_len, LANES), lambda b, hp, qi: (hp, b, 0))]
        args += [kc, vc2]
    return pl.pallas_call(
        functools.partial(_attn_kernel, with_ctx=with_ctx),
        out_shape=jax.ShapeDtypeStruct((n_b * seq, (N_HEADS // 2) * LANES), BF16),
        grid=(n_b, N_HEADS // 2, nq),
        in_specs=in_specs,
        out_specs=pl.BlockSpec((tq, LANES), lambda b, hp, qi: (b * nq + qi, hp)),
        compiler_params=_cparams(("parallel", "parallel", "parallel")),
        name="attn_ctx" if with_ctx else "attn",
    )(*args)


def _proj_res_kernel(y_ref, mod_ref, o_ref, w_ref, out_ref, *, gate_row):
    g = mod_ref[...][gate_row:gate_row + 1]
    out_ref[...] = y_ref[...] + g * _dot(o_ref[...], w_ref[...])


def proj_residual(lay, y, mod, o, w, gate_row):
    t, d = y.shape
    tm = lay.tm
    row = lambda i: (i, 0)
    return pl.pallas_call(
        functools.partial(_proj_res_kernel, gate_row=gate_row),
        out_shape=jax.ShapeDtypeStruct((t, d), F32),
        grid=(lay.n_tiles,),
        in_specs=[pl.BlockSpec((tm, d), row), pl.BlockSpec((None, 8, d), lambda i: (lay.seg(i), 0, 0)),
                  pl.BlockSpec((tm, o.shape[1]), row), pl.BlockSpec(w.shape, lambda i: (0, 0))],
        out_specs=pl.BlockSpec((tm, d), row),
        input_output_aliases={0: 0},
        compiler_params=_cparams(("parallel",)),
        name="proj_res",
    )(y, mod, o, w)


def _ssm_in_kernel(y_ref, mod_ref, nw_ref, w_ref, u_ref):
    m = mod_ref[...]
    h = _norm_mod(y_ref[...], nw_ref[...], m[0:1], m[1:2])
    u_ref[...] = _dot(h.astype(BF16), w_ref[...])


def ssm_in(lay, y, mod, nw, w_in):
    t, d = y.shape
    tm = lay.tm
    n = w_in.shape[1]
    row = lambda i: (i, 0)
    return pl.pallas_call(
        _ssm_in_kernel,
        out_shape=jax.ShapeDtypeStruct((t, n), F32),
        grid=(lay.n_tiles,),
        in_specs=[pl.BlockSpec((tm, d), row), pl.BlockSpec((None, 8, d), lambda i: (lay.seg(i), 0, 0)),
                  pl.BlockSpec((1, d), lambda i: (0, 0)), pl.BlockSpec((d, n), lambda i: (0, 0))],
        out_specs=pl.BlockSpec((tm, n), row),
        compiler_params=_cparams(("parallel",)),
        name="ssm_in",
    )(y, mod, nw, w_in)


def _ssm_prep_kernel(ar_ref, ai_ref, ld_ref, arc_ref, aic_ref, ldc_ref, btr_ref, bti_ref, ctr_ref, cti_ref,
                     mt_ref, ett_ref, ft_ref, coef_ref):
    q = SSM_CHUNK
    k = SSM_GROUP
    p = SSM_STATE
    qk = q * k
    dt = jnp.exp(ld_ref[...])
    are = ar_ref[...]
    aim = ai_ref[...]
    mag = jnp.exp(dt * are)
    abr = mag * jnp.cos(dt * aim)
    abi = mag * jnp.sin(dt * aim)
    den = are * are + aim * aim
    nr = abr - 1.0
    cf_re = (nr * are + abi * aim) / den
    cf_im = (abi * are - nr * aim) / den
    btr = btr_ref[...]
    bti = bti_ref[...]
    bbt_re = cf_re * btr - cf_im * bti
    bbt_im = cf_re * bti + cf_im * btr
    bbt_re_t = jnp.concatenate([bbt_re] * q, axis=0)
    bbt_im_t = jnp.concatenate([bbt_im] * q, axis=0)
    s_idx = (lax.broadcasted_iota(jnp.int32, (qk, 1), 0) // k).astype(F32)
    pw = (q - 1.0) - s_idx
    pm = jnp.exp(pw * dt * are)
    pr = pm * jnp.cos(pw * dt * aim)
    pi = pm * jnp.sin(pw * dt * aim)
    et_re = pr * bbt_re_t - pi * bbt_im_t
    et_im = pr * bbt_im_t + pi * bbt_re_t
    ett_ref[...] = jnp.concatenate([et_re, et_im, et_im, et_re], axis=1).astype(BF16)
    qm = jnp.exp(q * dt * are)
    aq_re = qm * jnp.cos(q * dt * aim)
    aq_im = qm * jnp.sin(q * dt * aim)
    c1 = jnp.concatenate([aq_re, aq_re], axis=1)
    c2 = jnp.concatenate([-aq_im, aq_im], axis=1)
    coef_ref[...] = jnp.concatenate([c1, c2, jnp.zeros((6, 2 * p), F32)], axis=0)
    dtc = jnp.exp(ldc_ref[...])
    arec = arc_ref[...]
    aimc = aic_ref[...]
    tau = (lax.broadcasted_iota(jnp.int32, (1, qk), 1) // k).astype(F32)
    ctr = ctr_ref[...]
    cti = cti_ref[...]
    m1 = jnp.exp((tau + 1.0) * dtc * arec)
    p1r = m1 * jnp.cos((tau + 1.0) * dtc * aimc)
    p1i = m1 * jnp.sin((tau + 1.0) * dtc * aimc)
    ft_ref[...] = jnp.concatenate([ctr * p1r - cti * p1i, -(ctr * p1i + cti * p1r)], axis=0).astype(BF16)
    m0 = jnp.exp(tau * dtc * arec)
    p0r = m0 * jnp.cos(tau * dtc * aimc)
    p0i = m0 * jnp.sin(tau * dtc * aimc)
    left_re = ctr * p0r - cti * p0i
    left_im = ctr * p0i + cti * p0r
    kt = _dot_hi(bbt_re, left_re) - _dot_hi(bbt_im, left_im)
    lane = lax.broadcasted_iota(jnp.int32, (k, qk), 1)
    rows = []
    for s in range(q):
        if s == 0:
            rows.append(kt)
        else:
            rows.append(jnp.where(lane >= s * k, pltpu.roll(kt, s * k, 1), 0.0))
    mt_ref[...] = jnp.concatenate(rows, axis=0).astype(BF16)


def ssm_prep(a_re, a_im, log_dt, b_re, b_im, c_re, c_im):
    nd, g, p = a_re.shape
    k = b_re.shape[-1]
    qk = SSM_CHUNK * k
    ld = jnp.broadcast_to(log_dt[..., None], (nd, g, p))
    rowv = lambda x: x.reshape(nd, g, 1, p)
    colv = lambda x: x.reshape(nd, g, p, 1)
    bt = lambda x: jnp.swapaxes(x, -1, -2)
    ct = lambda x: jnp.tile(jnp.swapaxes(x, -1, -2), (1, 1, 1, SSM_CHUNK))
    spec = lambda r, c: pl.BlockSpec((None, None, r, c), lambda d, j: (d, j, 0, 0))
    return pl.pallas_call(
        _ssm_prep_kernel,
        out_shape=(jax.ShapeDtypeStruct((nd, g, qk, qk), BF16), jax.ShapeDtypeStruct((nd, g, qk, 4 * p), BF16),
                   jax.ShapeDtypeStruct((nd, g, 2 * p, qk), BF16), jax.ShapeDtypeStruct((nd, g, 8, 2 * p), F32)),
        grid=(nd, g),
        in_specs=[spec(1, p), spec(1, p), spec(1, p), spec(p, 1), spec(p, 1), spec(p, 1),
                  spec(k, p), spec(k, p), spec(p, qk), spec(p, qk)],
        out_specs=(spec(qk, qk), spec(qk, 4 * p), spec(2 * p, qk), spec(8, 2 * p)),
        compiler_params=_cparams(("parallel", "parallel")),
        name="ssm_prep",
    )(rowv(a_re), rowv(a_im), rowv(ld), colv(a_re), colv(a_im), colv(ld), bt(b_re), bt(b_im), ct(c_re), ct(c_im))


def _ssm_scan_kernel(u_ref, mt_ref, ett_ref, ft_ref, coef_ref, x0_ref, x0s_ref, y_ref, hfin_ref,
                     sx_ref, sxs_ref, hp_ref, *, gpb, ctx_b, ctx_chunks, lat_rows, lat_chunks, rchunk):
    w = 2 * SSM_STATE
    n_rows = u_ref.shape[1]
    ctx_rows = ctx_b * ctx_chunks
    for g in range(gpb):
        for r0 in range(0, n_rows, rchunk):
            ss = _dot(u_ref[g, r0:r0 + rchunk, :], ett_ref[g])
            sx_ref[r0:r0 + rchunk, g * w:(g + 1) * w] = ss[:, :w]
            sxs_ref[r0:r0 + rchunk, g * w:(g + 1) * w] = ss[:, w:]
    c1 = jnp.concatenate([coef_ref[g][0:1] for g in range(gpb)], axis=1)
    c2 = jnp.concatenate([coef_ref[g][1:2] for g in range(gpb)], axis=1)

    def make_body(base, nb):
        def body(c, carry):
            x, xs = carry
            r = pl.multiple_of(base + c * nb, 8)
            hp_ref[pl.ds(r, nb), :] = x
            xn = c1 * x + c2 * xs + sx_ref[pl.ds(r, nb), :]
            xsn = c1 * xs - c2 * x + sxs_ref[pl.ds(r, nb), :]
            return xn, xsn
        return body

    z = jnp.zeros((ctx_b, gpb * w), F32)
    xf, _ = lax.fori_loop(0, ctx_chunks, make_body(0, ctx_b), (z, z))
    hfin_ref[...] = xf
    lax.fori_loop(0, lat_chunks, make_body(ctx_rows, lat_rows), (x0_ref[...], x0s_ref[...]), unroll=4)
    for g in range(gpb):
        for r0 in range(0, n_rows, rchunk):
            y = _dot(u_ref[g, r0:r0 + rchunk, :], mt_ref[g])
            y = y + _dot(hp_ref[r0:r0 + rchunk, g * w:(g + 1) * w].astype(BF16), ft_ref[g])
            y_ref[g, r0:r0 + rchunk, :] = y


def ssm_scan(u, mt, ett, ft, coef, x0, x0s, ctx_b, ctx_chunks, lat_rows, lat_chunks, gpb=2):
    nd, g, n_rows, qk = u.shape
    w = 2 * SSM_STATE
    rchunk = math.gcd(n_rows, 512)
    assert n_rows == ctx_b * ctx_chunks + lat_rows * lat_chunks and n_rows % rchunk == 0
    blk = lambda r, c: pl.BlockSpec((None, gpb, r, c), lambda d, j: (d, j, 0, 0))
    vec = lambda r: pl.BlockSpec((None, r, gpb * w), lambda d, j: (d, 0, j))
    return pl.pallas_call(
        functools.partial(_ssm_scan_kernel, gpb=gpb, ctx_b=ctx_b, ctx_chunks=ctx_chunks, lat_rows=lat_rows,
                          lat_chunks=lat_chunks, rchunk=rchunk),
        out_shape=(jax.ShapeDtypeStruct((nd, g, n_rows, qk), F32), jax.ShapeDtypeStruct((nd, ctx_b, g * w), F32)),
        grid=(nd, g // gpb),
        in_specs=[blk(n_rows, qk), blk(qk, qk), blk(qk, 2 * w), blk(w, qk), blk(8, w), vec(lat_rows), vec(lat_rows)],
        out_specs=(blk(n_rows, qk), vec(ctx_b)),
        scratch_shapes=[pltpu.VMEM((n_rows, gpb * w), F32), pltpu.VMEM((n_rows, gpb * w), F32),
                        pltpu.VMEM((n_rows, gpb * w), F32)],
        compiler_params=_cparams(("parallel", "parallel")),
        name="ssm_scan",
    )(u, mt, ett, ft, coef, x0, x0s)


def _ssm_out_kernel(y_ref, mod_ref, yf_ref, yb_ref, u_ref, d_ref, w_ref, out_ref, *, d_model):
    g = mod_ref[...][2:3]
    s = yf_ref[...] + yb_ref[...] + d_ref[...] * u_ref[...]
    a = jax.nn.gelu(s, approximate=True).astype(BF16)
    z = _dot(a, w_ref[...])
    out_ref[...] = y_ref[...] + g * (z[:, :d_model] * jax.nn.sigmoid(z[:, d_model:]))


def ssm_out(lay, y, mod, yf, yb, u, d_skip, w_glu):
    t, d = y.shape
    tm = lay.tm
    n = u.shape[1]
    row = lambda i: (i, 0)
    return pl.pallas_call(
        functools.partial(_ssm_out_kernel, d_model=d),
        out_shape=jax.ShapeDtypeStruct((t, d), F32),
        grid=(lay.n_tiles,),
        in_specs=[pl.BlockSpec((tm, d), row), pl.BlockSpec((None, 8, d), lambda i: (lay.seg(i), 0, 0)),
                  pl.BlockSpec((tm, n), row), pl.BlockSpec((tm, n), row), pl.BlockSpec((tm, n), row),
                  pl.BlockSpec((1, n), lambda i: (0, 0)), pl.BlockSpec(w_glu.shape, lambda i: (0, 0))],
        out_specs=pl.BlockSpec((tm, d), row),
        input_output_aliases={0: 0},
        compiler_params=_cparams(("parallel",)),
        name="ssm_out",
    )(y, mod, yf, yb, u, d_skip, w_glu)


def _to_chunks(u2d, n_b, seq, pad_b):
    g = u2d.shape[1] // SSM_GROUP
    x = u2d.reshape(n_b, seq // SSM_CHUNK, SSM_CHUNK, g, SSM_GROUP)
    x = jnp.transpose(x, (3, 1, 0, 2, 4))
    if pad_b > n_b:
        x = jnp.pad(x, ((0, 0), (0, 0), (0, pad_b - n_b), (0, 0), (0, 0)))
    return x.reshape(g, (seq // SSM_CHUNK) * pad_b, SSM_CHUNK * SSM_GROUP)


def _from_chunks(y, n_b, seq, pad_b):
    g = y.shape[0]
    x = y.reshape(g, seq // SSM_CHUNK, pad_b, SSM_CHUNK, SSM_GROUP)[:, :, :n_b]
    return jnp.transpose(x, (2, 1, 3, 0, 4)).reshape(n_b * seq, g * SSM_GROUP)


def s5_layer(lay, y, mod, nw, state_re, state_im, w_in, a_re, a_im, log_dt, b_re, b_im, c_re, c_im, d_skip, w_glu):
    u = ssm_in(lay, y, mod, nw, w_in.astype(BF16))
    g = a_re.shape[1]
    p = SSM_STATE
    lat_rows = 8
    assert lay.n_lat_b <= lat_rows and lay.n_ctx_b % 8 == 0
    u_ctx = u[:lay.t_ctx].reshape(lay.n_ctx_b, lay.ctx_len, -1)
    u_lat = u[lay.t_ctx:].reshape(lay.n_lat_b, lay.lat_len, -1)
    per_dir = []
    for rev in (False, True):
        uc = u_ctx[:, ::-1] if rev else u_ctx
        ul = u_lat[:, ::-1] if rev else u_lat
        per_dir.append(jnp.concatenate(
            [_to_chunks(uc.reshape(lay.t_ctx, -1).astype(BF16), lay.n_ctx_b, lay.ctx_len, lay.n_ctx_b),
             _to_chunks(ul.reshape(lay.t_lat, -1).astype(BF16), lay.n_lat_b, lay.lat_len, lat_rows)], axis=1))
    uch = jnp.stack(per_dir)
    mt, ett, ft, coef = ssm_prep(a_re, a_im, log_dt, b_re, b_im, c_re, c_im)
    st = jnp.concatenate([state_re, state_im], axis=-1)
    sts = jnp.concatenate([state_im, state_re], axis=-1)
    padb = ((0, 0), (0, lat_rows - lay.n_lat_b), (0, 0))
    x0 = jnp.pad(jnp.transpose(st, (1, 0, 2, 3)).reshape(2, lay.n_lat_b, g * 2 * p), padb)
    x0s = jnp.pad(jnp.transpose(sts, (1, 0, 2, 3)).reshape(2, lay.n_lat_b, g * 2 * p), padb)
    ctx_chunks = lay.ctx_len // SSM_CHUNK
    lat_chunks = lay.lat_len // SSM_CHUNK
    ych, hfin = ssm_scan(uch, mt, ett, ft, coef, x0, x0s, lay.n_ctx_b, ctx_chunks, lat_rows, lat_chunks)
    ctx_r = lay.n_ctx_b * ctx_chunks
    ys = []
    for di in range(2):
        yc = _from_chunks(ych[di, :, :ctx_r], lay.n_ctx_b, lay.ctx_len, lay.n_ctx_b)
        yl = _from_chunks(ych[di, :, ctx_r:], lay.n_lat_b, lay.lat_len, lat_rows)
        if di == 1:
            yc = yc.reshape(lay.n_ctx_b, lay.ctx_len, -1)[:, ::-1].reshape(lay.t_ctx, -1)
            yl = yl.reshape(lay.n_lat_b, lay.lat_len, -1)[:, ::-1].reshape(lay.t_lat, -1)
        ys.append(jnp.concatenate([yc, yl], axis=0))
    y = ssm_out(lay, y, mod, ys[0], ys[1], u, d_skip.reshape(1, -1), w_glu.astype(BF16))
    hf = hfin.reshape(2, lay.n_ctx_b, g, 2, p)
    new_re = jnp.transpose(hf[:, :, :, 0], (1, 0, 2, 3))
    new_im = jnp.transpose(hf[:, :, :, 1], (1, 0, 2, 3))
    return y, new_re, new_im


def _route_kernel(y_ref, mod_ref, nw_ref, wr_ref, br_ref, h_ref, gate_ref, *, n_exp):
    m = mod_ref[...]
    h = _norm_mod(y_ref[...], nw_ref[...], m[3:4], m[4:5])
    h_hi = h.astype(BF16)
    h_ref[...] = h_hi
    h_lo = (h - h_hi.astype(F32)).astype(BF16)
    wr = wr_ref[...]
    lt = _dot_nt(wr, h_hi)
    logits = lt[:n_exp] + lt[n_exp:] + _dot_nt(wr[:n_exp], h_lo)
    scores = jax.nn.sigmoid(logits)
    sel = scores + br_ref[...]
    epg = EXPERTS_PER_GROUP
    row = lambda x, e: x[e:e + 1, :]
    gscore = []
    for g in range(N_EXPERT_GROUPS):
        a, b, c, d = (row(sel, g * epg + j) for j in range(epg))
        m1, n1, m2, n2 = jnp.maximum(a, b), jnp.minimum(a, b), jnp.maximum(c, d), jnp.minimum(c, d)
        gscore.append(jnp.maximum(m1, m2) + jnp.maximum(jnp.minimum(m1, m2), jnp.maximum(n1, n2)))
    best = gscore[0]
    gi = jnp.zeros_like(best, dtype=jnp.int32)
    for g in range(1, N_EXPERT_GROUPS):
        better = gscore[g] > best
        gi = jnp.where(better, g, gi)
        best = jnp.where(better, gscore[g], best)

    def pick(x, j):
        out = row(x, j)
        for g in range(1, N_EXPERT_GROUPS):
            out = jnp.where(gi == g, row(x, g * epg + j), out)
        return out

    sv = [pick(sel, j) for j in range(epg)]
    cv = [pick(scores, j) for j in range(epg)]
    b1, i1, w1 = sv[0], jnp.zeros_like(gi), cv[0]
    for j in range(1, epg):
        better = sv[j] > b1
        i1 = jnp.where(better, j, i1)
        w1 = jnp.where(better, cv[j], w1)
        b1 = jnp.where(better, sv[j], b1)
    neg = jnp.full_like(b1, -jnp.inf)
    b2, i2, w2 = neg, jnp.zeros_like(gi), jnp.zeros_like(w1)
    for j in range(epg):
        better = (i1 != j) & (sv[j] > b2)
        i2 = jnp.where(better, j, i2)
        w2 = jnp.where(better, cv[j], w2)
        b2 = jnp.where(better, sv[j], b2)
    tot = w1 + w2
    e1 = gi * epg + i1
    e2 = gi * epg + i2
    eid = lax.broadcasted_iota(jnp.int32, logits.shape, 0)
    gate_ref[...] = jnp.where(eid == e1, w1 / tot, 0.0) + jnp.where(eid == e2, w2 / tot, 0.0)


def moe_route(lay, y, mod, nw, wr_t, br):
    t, d = y.shape
    tm = lay.tm
    n_exp = br.shape[0]
    return pl.pallas_call(
        functools.partial(_route_kernel, n_exp=n_exp),
        out_shape=(jax.ShapeDtypeStruct((t, d), BF16), jax.ShapeDtypeStruct((n_exp, t), F32)),
        grid=(lay.n_tiles,),
        in_specs=[pl.BlockSpec((tm, d), lambda i: (i, 0)),
                  pl.BlockSpec((None, 8, d), lambda i: (lay.seg(i), 0, 0)),
                  pl.BlockSpec((1, d), lambda i: (0, 0)), pl.BlockSpec(wr_t.shape, lambda i: (0, 0)),
                  pl.BlockSpec((n_exp, 1), lambda i: (0, 0))],
        out_specs=(pl.BlockSpec((tm, d), lambda i: (i, 0)), pl.BlockSpec((n_exp, tm), lambda i: (0, i))),
        compiler_params=_cparams(("parallel",)),
        name="moe_route",
    )(y, mod, nw, wr_t, br)


def _moe_dense_kernel(y_ref, mod_ref, h_ref, gate_ref, wg_ref, wu_ref, wd_ref, out_ref, acc_ref):
    e = pl.program_id(1)

    @pl.when(e == 0)
    def _():
        acc_ref[...] = jnp.zeros_like(acc_ref)

    x = h_ref[...]
    a = _dot(x, wg_ref[...])
    b = _dot(x, wu_ref[...])
    gates = gate_ref[...]
    lane = lax.broadcasted_iota(jnp.int32, gates.shape, 1)
    ge = jnp.sum(jnp.where(lane == e, gates, 0.0), axis=1, keepdims=True)
    hid = (a * jax.nn.sigmoid(a)) * b * ge
    acc_ref[...] += _dot(hid.astype(BF16), wd_ref[...])

    @pl.when(e == pl.num_programs(1) - 1)
    def _():
        out_ref[...] = y_ref[...] + mod_ref[...][5:6] * acc_ref[...]


def moe_dense(lay, y, mod, h, gates, wg, wu, wd, tmm):
    t, d = y.shape
    n_exp, _, f = wg.shape
    per = tmm // lay.tm
    row = lambda i, e: (i, 0)
    return pl.pallas_call(
        _moe_dense_kernel,
        out_shape=jax.ShapeDtypeStruct((t, d), F32),
        grid=(t // tmm, n_exp),
        in_specs=[pl.BlockSpec((tmm, d), row),
                  pl.BlockSpec((None, 8, d), lambda i, e: (lay.seg(i * per), 0, 0)),
                  pl.BlockSpec((tmm, d), row), pl.BlockSpec((tmm, n_exp), row),
                  pl.BlockSpec((None, d, f), lambda i, e: (e, 0, 0)),
                  pl.BlockSpec((None, d, f), lambda i, e: (e, 0, 0)),
                  pl.BlockSpec((None, f, d), lambda i, e: (e, 0, 0))],
        out_specs=pl.BlockSpec((tmm, d), row),
        scratch_shapes=[pltpu.VMEM((tmm, d), F32)],
        input_output_aliases={0: 0},
        compiler_params=_cparams(("parallel", "arbitrary")),
        name="moe_dense",
    )(y, mod, h, gates, wg, wu, wd)


def moe_layer(lay, y, mod, nw, wr_t, br, wg, wu, wd, tmm):
    h, gates_t = moe_route(lay, y, mod, nw, wr_t, br)
    return moe_dense(lay, y, mod, h, gates_t.T, wg, wu, wd, tmm)


def _rope_perm():
    d = np.arange(QK_ROPE)
    return QK_NOPE + (d ^ (QK_ROPE // 4))


def _pad_cols(x, n):
    return jnp.pad(x, ((0, 0),) * (x.ndim - 1) + ((0, n - x.shape[-1]),))


def _mla_weights(w_q_a, w_q_b, w_kv_a, w_kv_b, q_norm, k_norm, kv_lora):
    perm = _rope_perm()
    nh = N_HEADS
    wkr = w_kv_a[:, kv_lora:]
    d = w_kv_a.shape[0]
    zl = jnp.zeros((d, QK_NOPE), F32)
    kr_blk = _pad_cols(jnp.concatenate([zl, wkr], axis=1), LANES)
    krp_blk = _pad_cols(jnp.concatenate([zl, wkr[:, perm - QK_NOPE]], axis=1), LANES)
    w_a = jnp.concatenate([w_q_a, w_kv_a[:, :kv_lora], kr_blk, krp_blk], axis=1).astype(BF16)
    wq = w_q_b.reshape(-1, nh, QK_DIM).transpose(1, 0, 2)
    wq_main = _pad_cols(wq, HEAD_PAD)
    wq_part = _pad_cols(jnp.concatenate([jnp.zeros_like(wq[..., :QK_NOPE]), wq[..., perm]], axis=-1), HEAD_PAD)
    wq_h = jnp.concatenate([wq_main, wq_part], axis=-1).astype(BF16)
    wkv = w_kv_b.reshape(-1, nh, QK_NOPE + V_DIM).transpose(1, 0, 2)
    wk_h = _pad_cols(wkv[..., :QK_NOPE], HEAD_PAD).astype(BF16)
    wv = wkv[..., QK_NOPE:]
    wv_h = jnp.concatenate([wv[0::2], wv[1::2]], axis=-1).astype(BF16)
    nrm = jnp.stack([_pad_cols(q_norm, LANES), _pad_cols(jnp.concatenate([jnp.zeros((QK_NOPE,), F32), q_norm[perm]]), LANES),
                     _pad_cols(k_norm, LANES), _pad_cols(jnp.concatenate([jnp.zeros((QK_NOPE,), F32), k_norm[perm]]), LANES)])
    nrm = jnp.pad(nrm, ((0, 4), (0, 0)))
    return w_a, wq_h, wk_h, wv_h, nrm


def _rope_tables(lat_len, tm):
    rows = lat_len // GRID_W
    t_row = np.repeat(np.arange(rows, dtype=np.float32), GRID_W)
    t_col = np.tile(np.arange(GRID_W, dtype=np.float32), rows)
    half = QK_ROPE // 2
    inv = (ROPE_BASE ** (-np.arange(0, half, 2, dtype=np.float32) / half)).astype(np.float32)
    ang = jnp.concatenate([jnp.asarray(t_row)[:, None] * inv, jnp.asarray(t_col)[:, None] * inv], axis=-1)
    cos, sin = jnp.cos(ang), jnp.sin(ang)
    nf = QK_ROPE // 4
    cos_l = jnp.concatenate([cos[:, :nf], cos[:, :nf], cos[:, nf:], cos[:, nf:]], axis=-1)
    sin_l = jnp.concatenate([-sin[:, :nf], sin[:, :nf], -sin[:, nf:], sin[:, nf:]], axis=-1)
    one = jnp.ones((lat_len, QK_NOPE), F32)
    cos_t = _pad_cols(jnp.concatenate([one, cos_l], axis=-1), LANES)
    cos_t = cos_t.at[:, QK_DIM:].set(1.0)
    sin_t = _pad_cols(jnp.concatenate([jnp.zeros((lat_len, QK_NOPE), F32), sin_l], axis=-1), LANES)
    cos_t = jnp.concatenate([jnp.ones((tm, LANES), F32), cos_t], axis=0)
    sin_t = jnp.concatenate([jnp.zeros((tm, LANES), F32), sin_t], axis=0)
    return cos_t, sin_t


def mla_layer(lay, y, mod, nw, cache_ckv, cache_krope, cos_t, sin_t, w_q_a, q_a_norm, w_q_b, w_kv_a, kv_a_norm,
              w_kv_b, q_norm, k_norm, w_o):
    q_lora = w_q_a.shape[1]
    kv_lora = kv_a_norm.shape[0]
    w_a, wq_h, wk_h, wv_h, nrm = _mla_weights(w_q_a, w_q_b, w_kv_a, w_kv_b, q_norm, k_norm, kv_lora)
    cq, ckv, kr, krp = mla_a(lay, y, mod, nw, w_a, q_a_norm.reshape(1, -1), kv_a_norm.reshape(1, -1), q_lora, kv_lora)
    q, k, v2 = mla_b(lay, cq, ckv, kr, krp, wq_h, wk_h, wv_h, nrm, cos_t, sin_t)
    past = cache_ckv.shape[1]
    ckr = jnp.pad(cache_krope.reshape(-1, QK_ROPE), ((0, 0), (QK_NOPE, LANES - QK_DIM)))
    kc, vc2 = ctx_kv(cache_ckv.reshape(-1, kv_lora), ckr, wk_h, wv_h, nrm, past)
    o_ctx = attention(q, k, v2, 0, lay.n_ctx_b, lay.ctx_len, min(lay.ctx_len, 256))
    o_lat = attention(q, k, v2, lay.t_ctx, lay.n_lat_b, lay.lat_len, min(lay.lat_len, 256), kc, vc2, past)
    o = jnp.concatenate([o_ctx, o_lat], axis=0)
    y = proj_residual(lay, y, mod, o, w_o.astype(BF16), 2)
    new_ckv = ckv[:lay.t_ctx].reshape(lay.n_ctx_b, lay.ctx_len, kv_lora)
    new_kr = kr[:lay.t_ctx, QK_NOPE:QK_DIM].reshape(lay.n_ctx_b, lay.ctx_len, QK_ROPE)
    return y, new_ckv, new_kr


def kernel(x_prompt, x_sample, c, cache_ckv, cache_krope, state_ssm_re, state_ssm_im, c_ctx, w_mod, b_mod, norm1_w, norm2_w, mla_w_q_a, mla_q_a_norm, mla_w_q_b, mla_w_kv_a, mla_kv_a_norm, mla_w_kv_b, mla_q_norm, mla_k_norm, mla_w_o, ssm_w_in, ssm_a_re, ssm_a_im, ssm_log_dt, ssm_b_re, ssm_b_im, ssm_c_re, ssm_c_im, ssm_d, ssm_w_glu, w_router, b_router, moe_w_gate, moe_w_up, moe_w_down):
    n_ctx_b, ctx_len, d = x_prompt.shape
    n_lat_b, lat_len, _ = x_sample.shape
    depth = w_mod.shape[0]
    lay = Layout(n_ctx_b, ctx_len, n_lat_b, lat_len, tm=256)
    assert n_lat_b + 1 <= 8
    y = jnp.concatenate([x_prompt.reshape(-1, d), x_sample.reshape(-1, d)], axis=0)
    cond8 = jnp.pad(jnp.concatenate([c_ctx[None, :], c], axis=0), ((0, 7 - n_lat_b), (0, 0)))
    mods = adaln_all(cond8, w_mod, b_mod)
    mods = jnp.pad(mods.reshape(depth, 8, 6, d), ((0, 0), (0, 0), (0, 2), (0, 0)))
    cos_t, sin_t = _rope_tables(lat_len, lay.tm)
    n_exp = b_router.shape[0]
    wr_hi = w_router.astype(BF16)
    wr_lo = (w_router - wr_hi.astype(F32)).astype(BF16)
    wr_t = jnp.concatenate([wr_hi.T, wr_lo.T], axis=0)
    br = b_router.reshape(n_exp, 1)
    ckv_out, kr_out, sre_out, sim_out = [], [], [], []
    for i in range(depth):
        j = i // 2
        mod = mods[i]
        nw1 = norm1_w[i].reshape(1, d)
        if i % 2 == 0:
            y, ckv_p, kr_p = mla_layer(lay, y, mod, nw1, cache_ckv[:, j], cache_krope[:, j], cos_t, sin_t,
                                       mla_w_q_a[j], mla_q_a_norm[j], mla_w_q_b[j], mla_w_kv_a[j], mla_kv_a_norm[j],
                                       mla_w_kv_b[j], mla_q_norm[j], mla_k_norm[j], mla_w_o[j])
            ckv_out.append(ckv_p)
            kr_out.append(kr_p)
        else:
            y, s_re, s_im = s5_layer(lay, y, mod, nw1, state_ssm_re[:, j], state_ssm_im[:, j], ssm_w_in[j],
                                     ssm_a_re[j], ssm_a_im[j], ssm_log_dt[j], ssm_b_re[j], ssm_b_im[j],
                                     ssm_c_re[j], ssm_c_im[j], ssm_d[j], ssm_w_glu[j])
            sre_out.append(s_re)
            sim_out.append(s_im)
        y = moe_layer(lay, y, mod, norm2_w[i].reshape(1, d), wr_t, br, moe_w_gate[i].astype(BF16),
                      moe_w_up[i].astype(BF16), moe_w_down[i].astype(BF16), tmm=math.gcd(math.gcd(lay.t_ctx, lat_len), 1024))
    yp = y[:lay.t_ctx].reshape(n_ctx_b, ctx_len, d)
    ys = y[lay.t_ctx:].reshape(n_lat_b, lat_len, d)
    return (yp, ys, jnp.stack(ckv_out, axis=1), jnp.stack(kr_out, axis=1),
            jnp.stack(sre_out, axis=1), jnp.stack(sim_out, axis=1))
```

```python
import functools
import math

import jax
import jax.numpy as jnp
import numpy as np
from jax import lax
from jax.experimental import pallas as pl
from jax.experimental.pallas import tpu as pltpu

F32 = jnp.float32
BF16 = jnp.bfloat16
EPS = 1e-6

GRID_W = 64
N_HEADS = 8
QK_NOPE = 64
QK_ROPE = 32
QK_DIM = QK_NOPE + QK_ROPE
V_DIM = 64
ROPE_BASE = 10000.0
SSM_GROUP = 16
SSM_STATE = 64
N_EXPERT_GROUPS = 4
EXPERTS_PER_GROUP = 4

LANES = 128
HEAD_PAD = LANES
SSM_CHUNK = 16
VMEM_LIMIT = 48 * 1024 * 1024


def _cparams(sem, vmem=VMEM_LIMIT):
    return pltpu.CompilerParams(dimension_semantics=sem, vmem_limit_bytes=vmem)


def _dot(a, b):
    return jnp.dot(a, b, preferred_element_type=F32)


def _dot_nt(a, b):
    return lax.dot_general(a, b, (((1,), (1,)), ((), ())), preferred_element_type=F32)


def _dot_hi(a, b):
    return jnp.dot(a, b, preferred_element_type=F32, precision=lax.Precision.HIGHEST)


def _norm_mod(x, nw, shift, scale):
    ms = jnp.mean(x * x, axis=-1, keepdims=True)
    return (x * lax.rsqrt(ms + EPS) * nw) * (1.0 + scale) + shift


class Layout:
    def __init__(self, n_ctx_b, ctx_len, n_lat_b, lat_len, tm):
        self.n_ctx_b, self.ctx_len, self.n_lat_b, self.lat_len = n_ctx_b, ctx_len, n_lat_b, lat_len
        self.t_ctx = n_ctx_b * ctx_len
        self.t_lat = n_lat_b * lat_len
        self.t = self.t_ctx + self.t_lat
        self.tm = tm
        assert self.t_ctx % tm == 0 and lat_len % tm == 0
        self.ctx_tiles = self.t_ctx // tm
        self.lat_tiles_per_b = lat_len // tm
        self.n_tiles = self.t // tm

    def seg(self, i):
        return jnp.where(i < self.ctx_tiles, 0, 1 + (i - self.ctx_tiles) // self.lat_tiles_per_b)

    def rope_blk(self, i):
        return jnp.where(i < self.ctx_tiles, 0, 1 + (i - self.ctx_tiles) % self.lat_tiles_per_b)


def _adaln_kernel(c_ref, w_ref, b_ref, o_ref):
    c = c_ref[...]
    s = c * jax.nn.sigmoid(c)
    o_ref[...] = _dot_hi(s, w_ref[...]) + b_ref[...]


def adaln_all(cond8, w_mod, b_mod, tn=1536):
    depth, d, n6 = w_mod.shape
    return pl.pallas_call(
        _adaln_kernel,
        out_shape=jax.ShapeDtypeStruct((depth, 8, n6), F32),
        grid=(depth, n6 // tn),
        in_specs=[pl.BlockSpec((8, d), lambda l, j: (0, 0)),
                  pl.BlockSpec((None, d, tn), lambda l, j: (l, 0, j)),
                  pl.BlockSpec((None, 1, tn), lambda l, j: (l, 0, j))],
        out_specs=pl.BlockSpec((None, 8, tn), lambda l, j: (l, 0, j)),
        compiler_params=_cparams(("parallel", "parallel")),
        name="adaln",
    )(cond8, w_mod, b_mod.reshape(depth, 1, n6))


def _mla_a_kernel(y_ref, mod_ref, nw_ref, w_ref, qan_ref, kvan_ref, cq_ref, ckv_ref, kr_ref, krp_ref,
                  *, q_lora, kv_lora):
    m = mod_ref[...]
    h = _norm_mod(y_ref[...], nw_ref[...], m[0:1], m[1:2])
    z = _dot(h.astype(BF16), w_ref[...])
    cq = z[:, :q_lora]
    cq = cq * lax.rsqrt(jnp.mean(cq * cq, axis=-1, keepdims=True) + EPS) * qan_ref[...]
    cq_ref[...] = cq.astype(BF16)
    ckv = z[:, q_lora:q_lora + kv_lora]
    ckv_ref[...] = ckv * lax.rsqrt(jnp.mean(ckv * ckv, axis=-1, keepdims=True) + EPS) * kvan_ref[...]
    kr_ref[...] = z[:, q_lora + kv_lora:q_lora + kv_lora + LANES]
    krp_ref[...] = z[:, q_lora + kv_lora + LANES:]


def mla_a(lay, y, mod, nw, w_a, qan, kvan, q_lora, kv_lora):
    t, d = y.shape
    tm = lay.tm
    na = w_a.shape[1]
    row = lambda i: (i, 0)
    full = lambda i: (0, 0)
    return pl.pallas_call(
        functools.partial(_mla_a_kernel, q_lora=q_lora, kv_lora=kv_lora),
        out_shape=(jax.ShapeDtypeStruct((t, q_lora), BF16), jax.ShapeDtypeStruct((t, kv_lora), F32),
                   jax.ShapeDtypeStruct((t, LANES), F32), jax.ShapeDtypeStruct((t, LANES), F32)),
        grid=(lay.n_tiles,),
        in_specs=[pl.BlockSpec((tm, d), row),
                  pl.BlockSpec((None, 8, d), lambda i: (lay.seg(i), 0, 0)),
                  pl.BlockSpec((1, d), full), pl.BlockSpec((d, na), full),
                  pl.BlockSpec((1, q_lora), full), pl.BlockSpec((1, kv_lora), full)],
        out_specs=(pl.BlockSpec((tm, q_lora), row), pl.BlockSpec((tm, kv_lora), row),
                   pl.BlockSpec((tm, LANES), row), pl.BlockSpec((tm, LANES), row)),
        compiler_params=_cparams(("parallel",)),
        name="mla_a",
    )(y, mod, nw, w_a, qan, kvan)


def _kv_heads(ckv_b, kr, krp, wk_ref, wv_ref, knw, knwp, cos, sin, k_ref, v_ref):
    kc = knw * cos
    ks = knwp * sin
    for h in range(N_HEADS):
        kz = _dot(ckv_b, wk_ref[h]) + kr
        r = lax.rsqrt(jnp.sum(kz * kz, axis=-1, keepdims=True) * (1.0 / QK_DIM) + EPS)
        k_ref[h] = (r * (kz * kc + krp * ks)).astype(BF16)
    for hp in range(N_HEADS // 2):
        v_ref[hp] = _dot(ckv_b, wv_ref[hp]).astype(BF16)


def _mla_b_kernel(cq_ref, ckv_ref, kr_ref, krp_ref, wq_ref, wk_ref, wv_ref, nrm_ref, cos_ref, sin_ref,
                  q_ref, k_ref, v_ref):
    nrm = nrm_ref[...]
    cos = cos_ref[...]
    sin = sin_ref[...]
    cq = cq_ref[...]
    qc = nrm[0:1] * cos * (QK_DIM ** -0.5)
    qs = nrm[1:2] * sin * (QK_DIM ** -0.5)
    for h in range(N_HEADS):
        z = _dot(cq, wq_ref[h])
        qm = z[:, :HEAD_PAD]
        qp = z[:, HEAD_PAD:]
        r = lax.rsqrt(jnp.sum(qm * qm, axis=-1, keepdims=True) * (1.0 / QK_DIM) + EPS)
        q_ref[h] = (r * (qm * qc + qp * qs)).astype(BF16)
    _kv_heads(ckv_ref[...].astype(BF16), kr_ref[...], krp_ref[...], wk_ref, wv_ref, nrm[2:3], nrm[3:4],
              cos, sin, k_ref, v_ref)


def mla_b(lay, cq, ckv, kr, krp, wq, wk, wv, nrm, cos_t, sin_t):
    t = cq.shape[0]
    tm = lay.tm
    row = lambda i: (i, 0)
    hrow = lambda i: (0, i, 0)
    full3 = lambda i: (0, 0, 0)
    rope = lambda i: (lay.rope_blk(i), 0)
    nh = N_HEADS
    return pl.pallas_call(
        _mla_b_kernel,
        out_shape=(jax.ShapeDtypeStruct((nh, t, HEAD_PAD), BF16), jax.ShapeDtypeStruct((nh, t, HEAD_PAD), BF16),
                   jax.ShapeDtypeStruct((nh // 2, t, LANES), BF16)),
        grid=(lay.n_tiles,),
        in_specs=[pl.BlockSpec((tm, cq.shape[1]), row), pl.BlockSpec((tm, ckv.shape[1]), row),
                  pl.BlockSpec((tm, LANES), row), pl.BlockSpec((tm, LANES), row),
                  pl.BlockSpec(wq.shape, full3), pl.BlockSpec(wk.shape, full3), pl.BlockSpec(wv.shape, full3),
                  pl.BlockSpec((8, LANES), lambda i: (0, 0)),
                  pl.BlockSpec((tm, LANES), rope), pl.BlockSpec((tm, LANES), rope)],
        out_specs=(pl.BlockSpec((nh, tm, HEAD_PAD), hrow), pl.BlockSpec((nh, tm, HEAD_PAD), hrow),
                   pl.BlockSpec((nh // 2, tm, LANES), hrow)),
        compiler_params=_cparams(("parallel",)),
        name="mla_b",
    )(cq, ckv, kr, krp, wq, wk, wv, nrm, cos_t, sin_t)


def _ctx_kv_kernel(ckv_ref, kr_ref, wk_ref, wv_ref, nrm_ref, k_ref, v_ref):
    nrm = nrm_ref[...]
    kr = kr_ref[...]
    one = jnp.ones((1, LANES), F32)
    _kv_heads(ckv_ref[...].astype(BF16), kr, kr, wk_ref, wv_ref, nrm[2:3], nrm[3:4],
              one, jnp.zeros((1, LANES), F32), k_ref, v_ref)


def ctx_kv(ckv, kr, wk, wv, nrm, tm):
    t = ckv.shape[0]
    row = lambda i: (i, 0)
    hrow = lambda i: (0, i, 0)
    full3 = lambda i: (0, 0, 0)
    nh = N_HEADS
    return pl.pallas_call(
        _ctx_kv_kernel,
        out_shape=(jax.ShapeDtypeStruct((nh, t, HEAD_PAD), BF16), jax.ShapeDtypeStruct((nh // 2, t, LANES), BF16)),
        grid=(t // tm,),
        in_specs=[pl.BlockSpec((tm, ckv.shape[1]), row), pl.BlockSpec((tm, LANES), row),
                  pl.BlockSpec(wk.shape, full3), pl.BlockSpec(wv.shape, full3),
                  pl.BlockSpec((8, LANES), lambda i: (0, 0))],
        out_specs=(pl.BlockSpec((nh, tm, HEAD_PAD), hrow), pl.BlockSpec((nh // 2, tm, LANES), hrow)),
        compiler_params=_cparams(("parallel",)),
        name="ctx_kv",
    )(ckv, kr, wk, wv, nrm)


def _attn_kernel(*refs, with_ctx):
    if with_ctx:
        q_ref, k_ref, v_ref, kc_ref, vc_ref, o_ref = refs
    else:
        q_ref, k_ref, v_ref, o_ref = refs
    v = v_ref[0]
    outs = []
    for j in range(2):
        q = q_ref[j]
        s = _dot_nt(q, k_ref[j])
        m = jnp.max(s, axis=-1, keepdims=True)
        if with_ctx:
            sc = _dot_nt(q, kc_ref[j])
            m = jnp.maximum(m, jnp.max(sc, axis=-1, keepdims=True))
        p = jnp.exp(s - m)
        l = jnp.sum(p, axis=-1, keepdims=True)
        acc = _dot(p.astype(BF16), v)
        if with_ctx:
            pc = jnp.exp(sc - m)
            l = l + jnp.sum(pc, axis=-1, keepdims=True)
            acc = acc + _dot(pc.astype(BF16), vc_ref[0])
        outs.append(acc / l)
    lane = lax.broadcasted_iota(jnp.int32, outs[0].shape, 1)
    o_ref[...] = jnp.where(lane < V_DIM, outs[0], outs[1]).astype(BF16)


def attention(q, k, v2, row0, n_b, seq, tq, kc=None, vc2=None, ctx_len=0):
    assert row0 % seq == 0 and seq % tq == 0
    nq = seq // tq
    qb0 = row0 // tq
    kb0 = row0 // seq
    with_ctx = kc is not None
    in_specs = [pl.BlockSpec((2, tq, HEAD_PAD), lambda b, hp, qi: (hp, qb0 + b * nq + qi, 0)),
                pl.BlockSpec((2, seq, HEAD_PAD), lambda b, hp, qi: (hp, kb0 + b, 0)),
                pl.BlockSpec((1, seq, LANES), lambda b, hp, qi: (hp, kb0 + b, 0))]
    args = [q, k, v2]
    if with_ctx:
        in_specs += [pl.BlockSpec((2, ctx_len, HEAD_PAD), lambda b, hp, qi: (hp, b, 0)),
                     pl.BlockSpec((1, ctx_len, LANES), lambda b, hp, qi: (hp, b, 0))]
        args += [kc, vc2]
    return pl.pallas_call(
        functools.partial(_attn_kernel, with_ctx=with_ctx),
        out_shape=jax.ShapeDtypeStruct((n_b * seq, (N_HEADS // 2) * LANES), BF16),
        grid=(n_b, N_HEADS // 2, nq),
        in_specs=in_specs,
        out_specs=pl.BlockSpec((tq, LANES), lambda b, hp, qi: (b * nq + qi, hp)),
        compiler_params=_cparams(("parallel", "parallel", "parallel")),
        name="attn_ctx" if with_ctx else "attn",
    )(*args)


def _proj_res_kernel(y_ref, mod_ref, o_ref, w_ref, out_ref, *, gate_row):
    g = mod_ref[...][gate_row:gate_row + 1]
    out_ref[...] = y_ref[...] + g * _dot(o_ref[...], w_ref[...])


def proj_residual(lay, y, mod, o, w, gate_row):
    t, d = y.shape
    tm = lay.tm
    row = lambda i: (i, 0)
    return pl.pallas_call(
        functools.partial(_proj_res_kernel, gate_row=gate_row),
        out_shape=jax.ShapeDtypeStruct((t, d), F32),
        grid=(lay.n_tiles,),
        in_specs=[pl.BlockSpec((tm, d), row), pl.BlockSpec((None, 8, d), lambda i: (lay.seg(i), 0, 0)),
                  pl.BlockSpec((tm, o.shape[1]), row), pl.BlockSpec(w.shape, lambda i: (0, 0))],
        out_specs=pl.BlockSpec((tm, d), row),
        input_output_aliases={0: 0},
        compiler_params=_cparams(("parallel",)),
        name="proj_res",
    )(y, mod, o, w)


def _ssm_in_kernel(y_ref, mod_ref, nw_ref, w_ref, u_ref, us_ref, ug_ref):
    m = mod_ref[...]
    h = _norm_mod(y_ref[...], nw_ref[...], m[0:1], m[1:2])
    z = _dot(h.astype(BF16), w_ref[...])
    n_lt = us_ref.shape[0]
    for j in range(n_lt):
        us_ref[j] = z[:, j * LANES:(j + 1) * LANES]
    n_ch = us_ref.shape[1] // SSM_CHUNK
    kk = SSM_GROUP
    gpt = LANES // kk
    for t in range(SSM_CHUNK):
        for j in range(n_lt):
            piece = us_ref[j, pl.ds(t, n_ch, stride=SSM_CHUNK), :]
            for gl in range(gpt):
                ug_ref[j * gpt + gl, :, t * kk:(t + 1) * kk] = piece[:, gl * kk:(gl + 1) * kk]
    u_ref[...] = ug_ref[...].astype(BF16)


def ssm_in(lay, y, mod, nw, w_in):
    t, d = y.shape
    tm = lay.tm
    n = w_in.shape[1]
    g = n // SSM_GROUP
    n_ch = tm // SSM_CHUNK
    qk = SSM_CHUNK * SSM_GROUP
    row = lambda i: (i, 0)
    return pl.pallas_call(
        _ssm_in_kernel,
        out_shape=jax.ShapeDtypeStruct((g, t // SSM_CHUNK, qk), BF16),
        grid=(lay.n_tiles,),
        in_specs=[pl.BlockSpec((tm, d), row), pl.BlockSpec((None, 8, d), lambda i: (lay.seg(i), 0, 0)),
                  pl.BlockSpec((1, d), lambda i: (0, 0)), pl.BlockSpec((d, n), lambda i: (0, 0))],
        out_specs=pl.BlockSpec((g, n_ch, qk), lambda i: (0, i, 0)),
        scratch_shapes=[pltpu.VMEM((n // LANES, tm, LANES), F32), pltpu.VMEM((g, n_ch, qk), F32)],
        compiler_params=_cparams(("parallel",)),
        name="ssm_in",
    )(y, mod, nw, w_in)


def _ssm_prep_kernel(ar_ref, ai_ref, ld_ref, arc_ref, aic_ref, ldc_ref, btr_ref, bti_ref, ctr_ref, cti_ref, dsk_ref,
                     mt_ref, ett_ref, ft_ref, coef_ref):
    q = SSM_CHUNK
    k = SSM_GROUP
    p = SSM_STATE
    qk = q * k
    fwd = pl.program_id(0) == 0
    dt = jnp.exp(ld_ref[...])
    are = ar_ref[...]
    aim = ai_ref[...]
    mag = jnp.exp(dt * are)
    abr = mag * jnp.cos(dt * aim)
    abi = mag * jnp.sin(dt * aim)
    den = are * are + aim * aim
    nr = abr - 1.0
    cf_re = (nr * are + abi * aim) / den
    cf_im = (abi * are - nr * aim) / den
    btr = btr_ref[...]
    bti = bti_ref[...]
    bbt_re = cf_re * btr - cf_im * bti
    bbt_im = cf_re * bti + cf_im * btr
    bbt_re_t = jnp.concatenate([bbt_re] * q, axis=0)
    bbt_im_t = jnp.concatenate([bbt_im] * q, axis=0)
    s_idx = (lax.broadcasted_iota(jnp.int32, (qk, 1), 0) // k).astype(F32)
    pw = jnp.where(fwd, (q - 1.0) - s_idx, s_idx)
    pm = jnp.exp(pw * dt * are)
    pr = pm * jnp.cos(pw * dt * aim)
    pi = pm * jnp.sin(pw * dt * aim)
    et_re = pr * bbt_re_t - pi * bbt_im_t
    et_im = pr * bbt_im_t + pi * bbt_re_t
    ett_ref[...] = jnp.concatenate([et_re, et_im, et_im, et_re], axis=1).astype(BF16)
    qm = jnp.exp(q * dt * are)
    aq_re = qm * jnp.cos(q * dt * aim)
    aq_im = qm * jnp.sin(q * dt * aim)
    c1 = jnp.concatenate([aq_re, aq_re], axis=1)
    c2 = jnp.concatenate([-aq_im, aq_im], axis=1)
    coef_ref[...] = jnp.concatenate([c1, c2, jnp.zeros((6, 2 * p), F32)], axis=0)
    dtc = jnp.exp(ldc_ref[...])
    arec = arc_ref[...]
    aimc = aic_ref[...]
    tau = (lax.broadcasted_iota(jnp.int32, (1, qk), 1) // k).astype(F32)
    ctr = ctr_ref[...]
    cti = cti_ref[...]
    tau1 = jnp.where(fwd, tau + 1.0, q - tau)
    m1 = jnp.exp(tau1 * dtc * arec)
    p1r = m1 * jnp.cos(tau1 * dtc * aimc)
    p1i = m1 * jnp.sin(tau1 * dtc * aimc)
    ft_ref[...] = jnp.concatenate([ctr * p1r - cti * p1i, -(ctr * p1i + cti * p1r)], axis=0).astype(BF16)
    tau0 = jnp.where(fwd, tau, (q - 1.0) - tau)
    m0 = jnp.exp(tau0 * dtc * arec)
    p0r = m0 * jnp.cos(tau0 * dtc * aimc)
    p0i = m0 * jnp.sin(tau0 * dtc * aimc)
    left_re = ctr * p0r - cti * p0i
    left_im = ctr * p0i + cti * p0r
    kt = _dot_hi(bbt_re, left_re) - _dot_hi(bbt_im, left_im)
    lane = lax.broadcasted_iota(jnp.int32, (k, qk), 1)
    rowi = lax.broadcasted_iota(jnp.int32, (k, qk), 0)
    kt_f = kt + jnp.where(lane == rowi, dsk_ref[...], 0.0)
    rows_f, rows_b = [], []
    for s in range(q):
        rows_f.append(kt_f if s == 0 else jnp.where(lane >= s * k, pltpu.roll(kt_f, s * k, 1), 0.0))
        sh = ((s + 1) * k) % qk
        rows_b.append(kt if sh == 0 else jnp.where(lane < (s + 1) * k, pltpu.roll(kt, sh, 1), 0.0))
    mt = jnp.where(fwd, jnp.concatenate(rows_f, axis=0), jnp.concatenate(rows_b, axis=0))
    mt_ref[...] = mt.astype(BF16)


def ssm_prep(a_re, a_im, log_dt, b_re, b_im, c_re, c_im, d_skip):
    nd, g, p = a_re.shape
    k = b_re.shape[-1]
    qk = SSM_CHUNK * k
    ld = jnp.broadcast_to(log_dt[..., None], (nd, g, p))
    dsk = jnp.broadcast_to(jnp.tile(d_skip.reshape(g, 1, k), (1, 1, SSM_CHUNK))[None], (nd, g, 1, qk))
    rowv = lambda x: x.reshape(nd, g, 1, p)
    colv = lambda x: x.reshape(nd, g, p, 1)
    bt = lambda x: jnp.swapaxes(x, -1, -2)
    ct = lambda x: jnp.tile(jnp.swapaxes(x, -1, -2), (1, 1, 1, SSM_CHUNK))
    spec = lambda r, c: pl.BlockSpec((None, None, r, c), lambda d, j: (d, j, 0, 0))
    return pl.pallas_call(
        _ssm_prep_kernel,
        out_shape=(jax.ShapeDtypeStruct((nd, g, qk, qk), BF16), jax.ShapeDtypeStruct((nd, g, qk, 4 * p), BF16),
                   jax.ShapeDtypeStruct((nd, g, 2 * p, qk), BF16), jax.ShapeDtypeStruct((nd, g, 8, 2 * p), F32)),
        grid=(nd, g),
        in_specs=[spec(1, p), spec(1, p), spec(1, p), spec(p, 1), spec(p, 1), spec(p, 1),
                  spec(k, p), spec(k, p), spec(p, qk), spec(p, qk), spec(1, qk)],
        out_specs=(spec(qk, qk), spec(qk, 4 * p), spec(2 * p, qk), spec(8, 2 * p)),
        compiler_params=_cparams(("parallel", "parallel")),
        name="ssm_prep",
    )(rowv(a_re), rowv(a_im), rowv(ld), colv(a_re), colv(a_im), colv(ld), bt(b_re), bt(b_im), ct(c_re), ct(c_im), dsk)


def _ssm_scan_kernel(u_ref, mt_ref, ett_ref, ft_ref, coef_ref, x0_ref, x0s_ref, y_ref, hfin_ref,
                     sx_ref, sxs_ref, hp_ref, *, gpb, ctx_b, ctx_chunks, lat_b, lat_chunks, rchunk):
    w = 2 * SSM_STATE
    n_rows = u_ref.shape[1]
    ctx_rows = ctx_b * ctx_chunks
    for d in range(2):
        for g in range(gpb):
            for r0 in range(0, n_rows, rchunk):
                ss = _dot(u_ref[g, r0:r0 + rchunk, :], ett_ref[d, g])
                sx_ref[g, r0:r0 + rchunk, :] = ss[:, :w]
                sxs_ref[g, r0:r0 + rchunk, :] = ss[:, w:]
        c1 = [coef_ref[d, g][0:1] for g in range(gpb)]
        c2 = [coef_ref[d, g][1:2] for g in range(gpb)]

        def make_body(base, nb, nc):
            def body(i, carry):
                c = i if d == 0 else nc - 1 - i
                rows = pl.ds(base + c, nb, stride=nc)
                out = []
                for g in range(gpb):
                    x, xs = carry[g]
                    hp_ref[g, rows, :] = x
                    xn = c1[g] * x + c2[g] * xs + sx_ref[g, rows, :]
                    xsn = c1[g] * xs - c2[g] * x + sxs_ref[g, rows, :]
                    out.append((xn, xsn))
                return tuple(out)
            return body

        z = jnp.zeros((ctx_b, w), F32)
        fin = lax.fori_loop(0, ctx_chunks, make_body(0, ctx_b, ctx_chunks), tuple((z, z) for _ in range(gpb)))
        hfin_ref[d] = jnp.concatenate([fin[g][0] for g in range(gpb)], axis=1)
        init = tuple((x0_ref[d][:, g * w:(g + 1) * w], x0s_ref[d][:, g * w:(g + 1) * w]) for g in range(gpb))
        lax.fori_loop(0, lat_chunks, make_body(ctx_rows, lat_b, lat_chunks), init, unroll=4)
        for g in range(gpb):
            for r0 in range(0, n_rows, rchunk):
                y = _dot(u_ref[g, r0:r0 + rchunk, :], mt_ref[d, g])
                y = y + _dot(hp_ref[g, r0:r0 + rchunk, :].astype(BF16), ft_ref[d, g])
                if d == 0:
                    y_ref[g, r0:r0 + rchunk, :] = y
                else:
                    y_ref[g, r0:r0 + rchunk, :] += y


def ssm_scan(u, mt, ett, ft, coef, x0, x0s, ctx_b, ctx_chunks, lat_b, lat_chunks, gpb=4):
    g, n_rows, qk = u.shape
    w = 2 * SSM_STATE
    rchunk = math.gcd(n_rows, 512)
    assert n_rows == ctx_b * ctx_chunks + lat_b * lat_chunks
    ublk = pl.BlockSpec((gpb, n_rows, qk), lambda j: (j, 0, 0))
    blk = lambda r, c: pl.BlockSpec((2, gpb, r, c), lambda j: (0, j, 0, 0))
    vec = lambda r: pl.BlockSpec((2, r, gpb * w), lambda j: (0, 0, j))
    return pl.pallas_call(
        functools.partial(_ssm_scan_kernel, gpb=gpb, ctx_b=ctx_b, ctx_chunks=ctx_chunks, lat_b=lat_b,
                          lat_chunks=lat_chunks, rchunk=rchunk),
        out_shape=(jax.ShapeDtypeStruct((g, n_rows, qk), F32), jax.ShapeDtypeStruct((2, ctx_b, g * w), F32)),
        grid=(g // gpb,),
        in_specs=[ublk, blk(qk, qk), blk(qk, 2 * w), blk(w, qk), blk(8, w), vec(lat_b), vec(lat_b)],
        out_specs=(ublk, vec(ctx_b)),
        scratch_shapes=[pltpu.VMEM((gpb, n_rows, w), F32), pltpu.VMEM((gpb, n_rows, w), F32),
                        pltpu.VMEM((gpb, n_rows, w), F32)],
        compiler_params=_cparams(("parallel",)),
        name="ssm_scan",
    )(u, mt, ett, ft, coef, x0, x0s)


def _ssm_out_kernel(y_ref, mod_ref, yc_ref, w_ref, out_ref, s_ref, *, d_model):
    kk = SSM_GROUP
    n_ch = yc_ref.shape[1]
    gpt = LANES // kk
    n_lt = s_ref.shape[0]
    for t in range(SSM_CHUNK):
        for j in range(n_lt):
            piece = jnp.concatenate([yc_ref[j * gpt + gl, :, t * kk:(t + 1) * kk] for gl in range(gpt)], axis=1)
            s_ref[j, pl.ds(t, n_ch, stride=SSM_CHUNK), :] = piece
    s = jnp.concatenate([s_ref[j] for j in range(n_lt)], axis=1)
    a = jax.nn.gelu(s, approximate=True).astype(BF16)
    z = _dot(a, w_ref[...])
    gate = mod_ref[...][2:3]
    out_ref[...] = y_ref[...] + gate * (z[:, :d_model] * jax.nn.sigmoid(z[:, d_model:]))


def ssm_out(lay, y, mod, ych, w_glu):
    t, d = y.shape
    tm = lay.tm
    g, _, qk = ych.shape
    n_ch = tm // SSM_CHUNK
    row = lambda i: (i, 0)
    return pl.pallas_call(
        functools.partial(_ssm_out_kernel, d_model=d),
        out_shape=jax.ShapeDtypeStruct((t, d), F32),
        grid=(lay.n_tiles,),
        in_specs=[pl.BlockSpec((tm, d), row), pl.BlockSpec((None, 8, d), lambda i: (lay.seg(i), 0, 0)),
                  pl.BlockSpec((g, n_ch, qk), lambda i: (0, i, 0)),
                  pl.BlockSpec(w_glu.shape, lambda i: (0, 0))],
        out_specs=pl.BlockSpec((tm, d), row),
        scratch_shapes=[pltpu.VMEM((g * SSM_GROUP // LANES, tm, LANES), F32)],
        input_output_aliases={0: 0},
        compiler_params=_cparams(("parallel",)),
        name="ssm_out",
    )(y, mod, ych, w_glu)


def s5_layer(lay, y, mod, nw, state_re, state_im, w_in, a_re, a_im, log_dt, b_re, b_im, c_re, c_im, d_skip, w_glu):
    uch = ssm_in(lay, y, mod, nw, w_in.astype(BF16))
    g = a_re.shape[1]
    p = SSM_STATE
    mt, ett, ft, coef = ssm_prep(a_re, a_im, log_dt, b_re, b_im, c_re, c_im, d_skip)
    st = jnp.concatenate([state_re, state_im], axis=-1)
    sts = jnp.concatenate([state_im, state_re], axis=-1)
    x0 = jnp.transpose(st, (1, 0, 2, 3)).reshape(2, lay.n_lat_b, g * 2 * p)
    x0s = jnp.transpose(sts, (1, 0, 2, 3)).reshape(2, lay.n_lat_b, g * 2 * p)
    ych, hfin = ssm_scan(uch, mt, ett, ft, coef, x0, x0s, lay.n_ctx_b, lay.ctx_len // SSM_CHUNK,
                         lay.n_lat_b, lay.lat_len // SSM_CHUNK)
    y = ssm_out(lay, y, mod, ych, w_glu.astype(BF16))
    hf = hfin.reshape(2, lay.n_ctx_b, g, 2, p)
    new_re = jnp.transpose(hf[:, :, :, 0], (1, 0, 2, 3))
    new_im = jnp.transpose(hf[:, :, :, 1], (1, 0, 2, 3))
    return y, new_re, new_im


def _route_kernel(y_ref, mod_ref, nw_ref, wr_ref, br_ref, h_ref, gate_ref, *, n_exp):
    m = mod_ref[...]
    h = _norm_mod(y_ref[...], nw_ref[...], m[3:4], m[4:5])
    h_hi = h.astype(BF16)
    h_ref[...] = h_hi
    h_lo = (h - h_hi.astype(F32)).astype(BF16)
    wr = wr_ref[...]
    lt = _dot_nt(wr, h_hi)
    logits = lt[:n_exp] + lt[n_exp:] + _dot_nt(wr[:n_exp], h_lo)
    scores = jax.nn.sigmoid(logits)
    sel = scores + br_ref[...]
    epg = EXPERTS_PER_GROUP
    row = lambda x, e: x[e:e + 1, :]
    gscore = []
    for g in range(N_EXPERT_GROUPS):
        a, b, c, d = (row(sel, g * epg + j) for j in range(epg))
        m1, n1, m2, n2 = jnp.maximum(a, b), jnp.minimum(a, b), jnp.maximum(c, d), jnp.minimum(c, d)
        gscore.append(jnp.maximum(m1, m2) + jnp.maximum(jnp.minimum(m1, m2), jnp.maximum(n1, n2)))
    best = gscore[0]
    gi = jnp.zeros_like(best, dtype=jnp.int32)
    for g in range(1, N_EXPERT_GROUPS):
        better = gscore[g] > best
        gi = jnp.where(better, g, gi)
        best = jnp.where(better, gscore[g], best)

    def pick(x, j):
        out = row(x, j)
        for g in range(1, N_EXPERT_GROUPS):
            out = jnp.where(gi == g, row(x, g * epg + j), out)
        return out

    sv = [pick(sel, j) for j in range(epg)]
    cv = [pick(scores, j) for j in range(epg)]
    b1, i1, w1 = sv[0], jnp.zeros_like(gi), cv[0]
    for j in range(1, epg):
        better = sv[j] > b1
        i1 = jnp.where(better, j, i1)
        w1 = jnp.where(better, cv[j], w1)
        b1 = jnp.where(better, sv[j], b1)
    neg = jnp.full_like(b1, -jnp.inf)
    b2, i2, w2 = neg, jnp.zeros_like(gi), jnp.zeros_like(w1)
    for j in range(epg):
        better = (i1 != j) & (sv[j] > b2)
        i2 = jnp.where(better, j, i2)
        w2 = jnp.where(better, cv[j], w2)
        b2 = jnp.where(better, sv[j], b2)
    tot = w1 + w2
    e1 = gi * epg + i1
    e2 = gi * epg + i2
    eid = lax.broadcasted_iota(jnp.int32, logits.shape, 0)
    gate_ref[...] = jnp.where(eid == e1, w1 / tot, 0.0) + jnp.where(eid == e2, w2 / tot, 0.0)


def moe_route(lay, y, mod, nw, wr_t, br):
    t, d = y.shape
    tm = lay.tm
    n_exp = br.shape[0]
    return pl.pallas_call(
        functools.partial(_route_kernel, n_exp=n_exp),
        out_shape=(jax.ShapeDtypeStruct((t, d), BF16), jax.ShapeDtypeStruct((n_exp, t), F32)),
        grid=(lay.n_tiles,),
        in_specs=[pl.BlockSpec((tm, d), lambda i: (i, 0)),
                  pl.BlockSpec((None, 8, d), lambda i: (lay.seg(i), 0, 0)),
                  pl.BlockSpec((1, d), lambda i: (0, 0)), pl.BlockSpec(wr_t.shape, lambda i: (0, 0)),
                  pl.BlockSpec((n_exp, 1), lambda i: (0, 0))],
        out_specs=(pl.BlockSpec((tm, d), lambda i: (i, 0)), pl.BlockSpec((n_exp, tm), lambda i: (0, i))),
        compiler_params=_cparams(("parallel",)),
        name="moe_route",
    )(y, mod, nw, wr_t, br)


def _moe_dense_kernel(y_ref, mod_ref, h_ref, gate_ref, wg_ref, wu_ref, wd_ref, out_ref, acc_ref):
    e = pl.program_id(1)

    @pl.when(e == 0)
    def _():
        acc_ref[...] = jnp.zeros_like(acc_ref)

    x = h_ref[...]
    a = _dot(x, wg_ref[...])
    b = _dot(x, wu_ref[...])
    gates = gate_ref[...]
    lane = lax.broadcasted_iota(jnp.int32, gates.shape, 1)
    ge = jnp.sum(jnp.where(lane == e, gates, 0.0), axis=1, keepdims=True)
    hid = (a * jax.nn.sigmoid(a)) * b * ge
    acc_ref[...] += _dot(hid.astype(BF16), wd_ref[...])

    @pl.when(e == pl.num_programs(1) - 1)
    def _():
        out_ref[...] = y_ref[...] + mod_ref[...][5:6] * acc_ref[...]


def moe_dense(lay, y, mod, h, gates, wg, wu, wd, tmm):
    t, d = y.shape
    n_exp, _, f = wg.shape
    per = tmm // lay.tm
    row = lambda i, e: (i, 0)
    return pl.pallas_call(
        _moe_dense_kernel,
        out_shape=jax.ShapeDtypeStruct((t, d), F32),
        grid=(t // tmm, n_exp),
        in_specs=[pl.BlockSpec((tmm, d), row),
                  pl.BlockSpec((None, 8, d), lambda i, e: (lay.seg(i * per), 0, 0)),
                  pl.BlockSpec((tmm, d), row), pl.BlockSpec((tmm, n_exp), row),
                  pl.BlockSpec((None, d, f), lambda i, e: (e, 0, 0)),
                  pl.BlockSpec((None, d, f), lambda i, e: (e, 0, 0)),
                  pl.BlockSpec((None, f, d), lambda i, e: (e, 0, 0))],
        out_specs=pl.BlockSpec((tmm, d), row),
        scratch_shapes=[pltpu.VMEM((tmm, d), F32)],
        input_output_aliases={0: 0},
        compiler_params=_cparams(("parallel", "arbitrary")),
        name="moe_dense",
    )(y, mod, h, gates, wg, wu, wd)


def moe_layer(lay, y, mod, nw, wr_t, br, wg, wu, wd, tmm):
    h, gates_t = moe_route(lay, y, mod, nw, wr_t, br)
    return moe_dense(lay, y, mod, h, gates_t.T, wg, wu, wd, tmm)


def _rope_perm():
    d = np.arange(QK_ROPE)
    return QK_NOPE + (d ^ (QK_ROPE // 4))


def _pad_cols(x, n):
    return jnp.pad(x, ((0, 0),) * (x.ndim - 1) + ((0, n - x.shape[-1]),))


def _mla_weights(w_q_a, w_q_b, w_kv_a, w_kv_b, q_norm, k_norm, kv_lora):
    perm = _rope_perm()
    nh = N_HEADS
    wkr = w_kv_a[:, kv_lora:]
    d = w_kv_a.shape[0]
    zl = jnp.zeros((d, QK_NOPE), F32)
    kr_blk = _pad_cols(jnp.concatenate([zl, wkr], axis=1), LANES)
    krp_blk = _pad_cols(jnp.concatenate([zl, wkr[:, perm - QK_NOPE]], axis=1), LANES)
    w_a = jnp.concatenate([w_q_a, w_kv_a[:, :kv_lora], kr_blk, krp_blk], axis=1).astype(BF16)
    wq = w_q_b.reshape(-1, nh, QK_DIM).transpose(1, 0, 2)
    wq_main = _pad_cols(wq, HEAD_PAD)
    wq_part = _pad_cols(jnp.concatenate([jnp.zeros_like(wq[..., :QK_NOPE]), wq[..., perm]], axis=-1), HEAD_PAD)
    wq_h = jnp.concatenate([wq_main, wq_part], axis=-1).astype(BF16)
    wkv = w_kv_b.reshape(-1, nh, QK_NOPE + V_DIM).transpose(1, 0, 2)
    wk_h = _pad_cols(wkv[..., :QK_NOPE], HEAD_PAD).astype(BF16)
    wv = wkv[..., QK_NOPE:]
    wv_h = jnp.concatenate([wv[0::2], wv[1::2]], axis=-1).astype(BF16)
    nrm = jnp.stack([_pad_cols(q_norm, LANES), _pad_cols(jnp.concatenate([jnp.zeros((QK_NOPE,), F32), q_norm[perm]]), LANES),
                     _pad_cols(k_norm, LANES), _pad_cols(jnp.concatenate([jnp.zeros((QK_NOPE,), F32), k_norm[perm]]), LANES)])
    nrm = jnp.pad(nrm, ((0, 4), (0, 0)))
    return w_a, wq_h, wk_h, wv_h, nrm


def _rope_tables(lat_len, tm):
    rows = lat_len // GRID_W
    t_row = np.repeat(np.arange(rows, dtype=np.float32), GRID_W)
    t_col = np.tile(np.arange(GRID_W, dtype=np.float32), rows)
    half = QK_ROPE // 2
    inv = (ROPE_BASE ** (-np.arange(0, half, 2, dtype=np.float32) / half)).astype(np.float32)
    ang = jnp.concatenate([jnp.asarray(t_row)[:, None] * inv, jnp.asarray(t_col)[:, None] * inv], axis=-1)
    cos, sin = jnp.cos(ang), jnp.sin(ang)
    nf = QK_ROPE // 4
    cos_l = jnp.concatenate([cos[:, :nf], cos[:, :nf], cos[:, nf:], cos[:, nf:]], axis=-1)
    sin_l = jnp.concatenate([-sin[:, :nf], sin[:, :nf], -sin[:, nf:], sin[:, nf:]], axis=-1)
    one = jnp.ones((lat_len, QK_NOPE), F32)
    cos_t = _pad_cols(jnp.concatenate([one, cos_l], axis=-1), LANES)
    cos_t = cos_t.at[:, QK_DIM:].set(1.0)
    sin_t = _pad_cols(jnp.concatenate([jnp.zeros((lat_len, QK_NOPE), F32), sin_l], axis=-1), LANES)
    cos_t = jnp.concatenate([jnp.ones((tm, LANES), F32), cos_t], axis=0)
    sin_t = jnp.concatenate([jnp.zeros((tm, LANES), F32), sin_t], axis=0)
    return cos_t, sin_t


def mla_layer(lay, y, mod, nw, cache_ckv, cache_krope, cos_t, sin_t, w_q_a, q_a_norm, w_q_b, w_kv_a, kv_a_norm,
              w_kv_b, q_norm, k_norm, w_o):
    q_lora = w_q_a.shape[1]
    kv_lora = kv_a_norm.shape[0]
    w_a, wq_h, wk_h, wv_h, nrm = _mla_weights(w_q_a, w_q_b, w_kv_a, w_kv_b, q_norm, k_norm, kv_lora)
    cq, ckv, kr, krp = mla_a(lay, y, mod, nw, w_a, q_a_norm.reshape(1, -1), kv_a_norm.reshape(1, -1), q_lora, kv_lora)
    q, k, v2 = mla_b(lay, cq, ckv, kr, krp, wq_h, wk_h, wv_h, nrm, cos_t, sin_t)
    past = cache_ckv.shape[1]
    ckr = jnp.pad(cache_krope.reshape(-1, QK_ROPE), ((0, 0), (QK_NOPE, LANES - QK_DIM)))
    kc, vc2 = ctx_kv(cache_ckv.reshape(-1, kv_lora), ckr, wk_h, wv_h, nrm, past)
    o_ctx = attention(q, k, v2, 0, lay.n_ctx_b, lay.ctx_len, min(lay.ctx_len, 256))
    o_lat = attention(q, k, v2, lay.t_ctx, lay.n_lat_b, lay.lat_len, min(lay.lat_len, 256), kc, vc2, past)
    o = jnp.concatenate([o_ctx, o_lat], axis=0)
    y = proj_residual(lay, y, mod, o, w_o.astype(BF16), 2)
    new_ckv = ckv[:lay.t_ctx].reshape(lay.n_ctx_b, lay.ctx_len, kv_lora)
    new_kr = kr[:lay.t_ctx, QK_NOPE:QK_DIM].reshape(lay.n_ctx_b, lay.ctx_len, QK_ROPE)
    return y, new_ckv, new_kr


def kernel(x_prompt, x_sample, c, cache_ckv, cache_krope, state_ssm_re, state_ssm_im, c_ctx, w_mod, b_mod, norm1_w, norm2_w, mla_w_q_a, mla_q_a_norm, mla_w_q_b, mla_w_kv_a, mla_kv_a_norm, mla_w_kv_b, mla_q_norm, mla_k_norm, mla_w_o, ssm_w_in, ssm_a_re, ssm_a_im, ssm_log_dt, ssm_b_re, ssm_b_im, ssm_c_re, ssm_c_im, ssm_d, ssm_w_glu, w_router, b_router, moe_w_gate, moe_w_up, moe_w_down):
    n_ctx_b, ctx_len, d = x_prompt.shape
    n_lat_b, lat_len, _ = x_sample.shape
    depth = w_mod.shape[0]
    lay = Layout(n_ctx_b, ctx_len, n_lat_b, lat_len, tm=256)
    assert n_lat_b + 1 <= 8
    y = jnp.concatenate([x_prompt.reshape(-1, d), x_sample.reshape(-1, d)], axis=0)
    cond8 = jnp.pad(jnp.concatenate([c_ctx[None, :], c], axis=0), ((0, 7 - n_lat_b), (0, 0)))
    mods = adaln_all(cond8, w_mod, b_mod)
    mods = jnp.pad(mods.reshape(depth, 8, 6, d), ((0, 0), (0, 0), (0, 2), (0, 0)))
    cos_t, sin_t = _rope_tables(lat_len, lay.tm)
    n_exp = b_router.shape[0]
    wr_hi = w_router.astype(BF16)
    wr_lo = (w_router - wr_hi.astype(F32)).astype(BF16)
    wr_t = jnp.concatenate([wr_hi.T, wr_lo.T], axis=0)
    br = b_router.reshape(n_exp, 1)
    ckv_out, kr_out, sre_out, sim_out = [], [], [], []
    for i in range(depth):
        j = i // 2
        mod = mods[i]
        nw1 = norm1_w[i].reshape(1, d)
        if i % 2 == 0:
            y, ckv_p, kr_p = mla_layer(lay, y, mod, nw1, cache_ckv[:, j], cache_krope[:, j], cos_t, sin_t,
                                       mla_w_q_a[j], mla_q_a_norm[j], mla_w_q_b[j], mla_w_kv_a[j], mla_kv_a_norm[j],
                                       mla_w_kv_b[j], mla_q_norm[j], mla_k_norm[j], mla_w_o[j])
            ckv_out.append(ckv_p)
            kr_out.append(kr_p)
        else:
            y, s_re, s_im = s5_layer(lay, y, mod, nw1, state_ssm_re[:, j], state_ssm_im[:, j], ssm_w_in[j],
                                     ssm_a_re[j], ssm_a_im[j], ssm_log_dt[j], ssm_b_re[j], ssm_b_im[j],
                                     ssm_c_re[j], ssm_c_im[j], ssm_d[j], ssm_w_glu[j])
            sre_out.append(s_re)
            sim_out.append(s_im)
        y = moe_layer(lay, y, mod, norm2_w[i].reshape(1, d), wr_t, br, moe_w_gate[i].astype(BF16),
                      moe_w_up[i].astype(BF16), moe_w_down[i].astype(BF16), tmm=math.gcd(math.gcd(lay.t_ctx, lat_len), 1024))
    yp = y[:lay.t_ctx].reshape(n_ctx_b, ctx_len, d)
    ys = y[lay.t_ctx:].reshape(n_lat_b, lat_len, d)
    return (yp, ys, jnp.stack(ckv_out, axis=1), jnp.stack(kr_out, axis=1),
            jnp.stack(sre_out, axis=1), jnp.stack(sim_out, axis=1))
```

```python
import functools
import math

import jax
import jax.numpy as jnp
import numpy as np
from jax import lax
from jax.experimental import pallas as pl
from jax.experimental.pallas import tpu as pltpu

F32 = jnp.float32
BF16 = jnp.bfloat16
EPS = 1e-6

GRID_W = 64
N_HEADS = 8
QK_NOPE = 64
QK_ROPE = 32
QK_DIM = QK_NOPE + QK_ROPE
V_DIM = 64
ROPE_BASE = 10000.0
SSM_GROUP = 16
SSM_STATE = 64
N_EXPERT_GROUPS = 4
EXPERTS_PER_GROUP = 4

LANES = 128
HEAD_PAD = LANES
SSM_CHUNK = 16
SEG_ALIGN = 16
VMEM_LIMIT = 48 * 1024 * 1024


def _cparams(sem, vmem=VMEM_LIMIT):
    return pltpu.CompilerParams(dimension_semantics=sem, vmem_limit_bytes=vmem)


def _dot(a, b):
    return jnp.dot(a, b, preferred_element_type=F32)


def _dot_nt(a, b):
    return lax.dot_general(a, b, (((1,), (1,)), ((), ())), preferred_element_type=F32)


def _dot_hi(a, b):
    return jnp.dot(a, b, preferred_element_type=F32, precision=lax.Precision.HIGHEST)


def _norm_mod(x, nw, shift, scale):
    ms = jnp.mean(x * x, axis=-1, keepdims=True)
    return (x * lax.rsqrt(ms + EPS) * nw) * (1.0 + scale) + shift


class Layout:
    def __init__(self, n_ctx_b, ctx_len, n_lat_b, lat_len, tm):
        self.n_ctx_b, self.ctx_len, self.n_lat_b, self.lat_len = n_ctx_b, ctx_len, n_lat_b, lat_len
        self.t_ctx = n_ctx_b * ctx_len
        self.t_lat = n_lat_b * lat_len
        self.t = self.t_ctx + self.t_lat
        self.tm = tm
        assert self.t_ctx % tm == 0 and lat_len % tm == 0
        self.ctx_tiles = self.t_ctx // tm
        self.lat_tiles_per_b = lat_len // tm
        self.n_tiles = self.t // tm

    def seg(self, i):
        return jnp.where(i < self.ctx_tiles, 0, 1 + (i - self.ctx_tiles) // self.lat_tiles_per_b)

    def rope_blk(self, i):
        return jnp.where(i < self.ctx_tiles, 0, 1 + (i - self.ctx_tiles) % self.lat_tiles_per_b)


def _adaln_kernel(c_ref, w_ref, b_ref, o_ref):
    c = c_ref[...]
    s = c * jax.nn.sigmoid(c)
    o_ref[...] = _dot_hi(s, w_ref[...]) + b_ref[...]


def adaln_all(cond8, w_mod, b_mod, tn=1536):
    depth, d, n6 = w_mod.shape
    return pl.pallas_call(
        _adaln_kernel,
        out_shape=jax.ShapeDtypeStruct((depth, 8, n6), F32),
        grid=(depth, n6 // tn),
        in_specs=[pl.BlockSpec((8, d), lambda l, j: (0, 0)),
                  pl.BlockSpec((None, d, tn), lambda l, j: (l, 0, j)),
                  pl.BlockSpec((None, 1, tn), lambda l, j: (l, 0, j))],
        out_specs=pl.BlockSpec((None, 8, tn), lambda l, j: (l, 0, j)),
        compiler_params=_cparams(("parallel", "parallel")),
        name="adaln",
    )(cond8, w_mod, b_mod.reshape(depth, 1, n6))


def _mla_a_kernel(y_ref, mod_ref, nw_ref, w_ref, qan_ref, kvan_ref, cq_ref, ckv_ref, kr_ref, krp_ref,
                  *, q_lora, kv_lora):
    m = mod_ref[...]
    h = _norm_mod(y_ref[...], nw_ref[...], m[0:1], m[1:2])
    z = _dot(h.astype(BF16), w_ref[...])
    cq = z[:, :q_lora]
    cq = cq * lax.rsqrt(jnp.mean(cq * cq, axis=-1, keepdims=True) + EPS) * qan_ref[...]
    cq_ref[...] = cq.astype(BF16)
    ckv = z[:, q_lora:q_lora + kv_lora]
    ckv_ref[...] = ckv * lax.rsqrt(jnp.mean(ckv * ckv, axis=-1, keepdims=True) + EPS) * kvan_ref[...]
    kr_ref[...] = z[:, q_lora + kv_lora:q_lora + kv_lora + LANES]
    krp_ref[...] = z[:, q_lora + kv_lora + LANES:]


def mla_a(lay, y, mod, nw, w_a, qan, kvan, q_lora, kv_lora):
    t, d = y.shape
    tm = lay.tm
    na = w_a.shape[1]
    row = lambda i: (i, 0)
    full = lambda i: (0, 0)
    return pl.pallas_call(
        functools.partial(_mla_a_kernel, q_lora=q_lora, kv_lora=kv_lora),
        out_shape=(jax.ShapeDtypeStruct((t, q_lora), BF16), jax.ShapeDtypeStruct((t, kv_lora), F32),
                   jax.ShapeDtypeStruct((t, LANES), F32), jax.ShapeDtypeStruct((t, LANES), F32)),
        grid=(lay.n_tiles,),
        in_specs=[pl.BlockSpec((tm, d), row),
                  pl.BlockSpec((None, 8, d), lambda i: (lay.seg(i), 0, 0)),
                  pl.BlockSpec((1, d), full), pl.BlockSpec((d, na), full),
                  pl.BlockSpec((1, q_lora), full), pl.BlockSpec((1, kv_lora), full)],
        out_specs=(pl.BlockSpec((tm, q_lora), row), pl.BlockSpec((tm, kv_lora), row),
                   pl.BlockSpec((tm, LANES), row), pl.BlockSpec((tm, LANES), row)),
        compiler_params=_cparams(("parallel",)),
        name="mla_a",
    )(y, mod, nw, w_a, qan, kvan)


def _kv_heads(ckv_b, kr, krp, wk_ref, wv_ref, knw, knwp, cos, sin, k_ref, v_ref):
    kc = knw * cos
    ks = knwp * sin
    for h in range(N_HEADS):
        kz = _dot(ckv_b, wk_ref[h]) + kr
        r = lax.rsqrt(jnp.sum(kz * kz, axis=-1, keepdims=True) * (1.0 / QK_DIM) + EPS)
        k_ref[h] = (r * (kz * kc + krp * ks)).astype(BF16)
    for hp in range(N_HEADS // 2):
        v_ref[hp] = _dot(ckv_b, wv_ref[hp]).astype(BF16)


def _mla_b_kernel(cq_ref, ckv_ref, kr_ref, krp_ref, wq_ref, wk_ref, wv_ref, nrm_ref, cos_ref, sin_ref,
                  q_ref, k_ref, v_ref):
    nrm = nrm_ref[...]
    cos = cos_ref[...]
    sin = sin_ref[...]
    cq = cq_ref[...]
    qc = nrm[0:1] * cos * (QK_DIM ** -0.5)
    qs = nrm[1:2] * sin * (QK_DIM ** -0.5)
    for h in range(N_HEADS):
        z = _dot(cq, wq_ref[h])
        qm = z[:, :HEAD_PAD]
        qp = z[:, HEAD_PAD:]
        r = lax.rsqrt(jnp.sum(qm * qm, axis=-1, keepdims=True) * (1.0 / QK_DIM) + EPS)
        q_ref[h] = (r * (qm * qc + qp * qs)).astype(BF16)
    _kv_heads(ckv_ref[...].astype(BF16), kr_ref[...], krp_ref[...], wk_ref, wv_ref, nrm[2:3], nrm[3:4],
              cos, sin, k_ref, v_ref)


def mla_b(lay, cq, ckv, kr, krp, wq, wk, wv, nrm, cos_t, sin_t):
    t = cq.shape[0]
    tm = lay.tm
    row = lambda i: (i, 0)
    hrow = lambda i: (0, i, 0)
    full3 = lambda i: (0, 0, 0)
    rope = lambda i: (lay.rope_blk(i), 0)
    nh = N_HEADS
    return pl.pallas_call(
        _mla_b_kernel,
        out_shape=(jax.ShapeDtypeStruct((nh, t, HEAD_PAD), BF16), jax.ShapeDtypeStruct((nh, t, HEAD_PAD), BF16),
                   jax.ShapeDtypeStruct((nh // 2, t, LANES), BF16)),
        grid=(lay.n_tiles,),
        in_specs=[pl.BlockSpec((tm, cq.shape[1]), row), pl.BlockSpec((tm, ckv.shape[1]), row),
                  pl.BlockSpec((tm, LANES), row), pl.BlockSpec((tm, LANES), row),
                  pl.BlockSpec(wq.shape, full3), pl.BlockSpec(wk.shape, full3), pl.BlockSpec(wv.shape, full3),
                  pl.BlockSpec((8, LANES), lambda i: (0, 0)),
                  pl.BlockSpec((tm, LANES), rope), pl.BlockSpec((tm, LANES), rope)],
        out_specs=(pl.BlockSpec((nh, tm, HEAD_PAD), hrow), pl.BlockSpec((nh, tm, HEAD_PAD), hrow),
                   pl.BlockSpec((nh // 2, tm, LANES), hrow)),
        compiler_params=_cparams(("parallel",)),
        name="mla_b",
    )(cq, ckv, kr, krp, wq, wk, wv, nrm, cos_t, sin_t)


def _ctx_kv_kernel(ckv_ref, kr_ref, wk_ref, wv_ref, nrm_ref, k_ref, v_ref):
    nrm = nrm_ref[...]
    kr = kr_ref[...]
    one = jnp.ones((1, LANES), F32)
    _kv_heads(ckv_ref[...].astype(BF16), kr, kr, wk_ref, wv_ref, nrm[2:3], nrm[3:4],
              one, jnp.zeros((1, LANES), F32), k_ref, v_ref)


def ctx_kv(ckv, kr, wk, wv, nrm, tm):
    t = ckv.shape[0]
    row = lambda i: (i, 0)
    hrow = lambda i: (0, i, 0)
    full3 = lambda i: (0, 0, 0)
    nh = N_HEADS
    return pl.pallas_call(
        _ctx_kv_kernel,
        out_shape=(jax.ShapeDtypeStruct((nh, t, HEAD_PAD), BF16), jax.ShapeDtypeStruct((nh // 2, t, LANES), BF16)),
        grid=(t // tm,),
        in_specs=[pl.BlockSpec((tm, ckv.shape[1]), row), pl.BlockSpec((tm, LANES), row),
                  pl.BlockSpec(wk.shape, full3), pl.BlockSpec(wv.shape, full3),
                  pl.BlockSpec((8, LANES), lambda i: (0, 0))],
        out_specs=(pl.BlockSpec((nh, tm, HEAD_PAD), hrow), pl.BlockSpec((nh // 2, tm, LANES), hrow)),
        compiler_params=_cparams(("parallel",)),
        name="ctx_kv",
    )(ckv, kr, wk, wv, nrm)


def _attn_kernel(*refs, with_ctx):
    if with_ctx:
        q_ref, k_ref, v_ref, kc_ref, vc_ref, o_ref = refs
    else:
        q_ref, k_ref, v_ref, o_ref = refs
    v = v_ref[0]
    outs = []
    for j in range(2):
        q = q_ref[j]
        s = _dot_nt(q, k_ref[j])
        m = jnp.max(s, axis=-1, keepdims=True)
        if with_ctx:
            sc = _dot_nt(q, kc_ref[j])
            m = jnp.maximum(m, jnp.max(sc, axis=-1, keepdims=True))
        p = jnp.exp(s - m)
        l = jnp.sum(p, axis=-1, keepdims=True)
        acc = _dot(p.astype(BF16), v)
        if with_ctx:
            pc = jnp.exp(sc - m)
            l = l + jnp.sum(pc, axis=-1, keepdims=True)
            acc = acc + _dot(pc.astype(BF16), vc_ref[0])
        outs.append(acc / l)
    lane = lax.broadcasted_iota(jnp.int32, outs[0].shape, 1)
    o_ref[...] = jnp.where(lane < V_DIM, outs[0], outs[1]).astype(BF16)


def attention(q, k, v2, row0, n_b, seq, tq, kc=None, vc2=None, ctx_len=0):
    assert row0 % seq == 0 and seq % tq == 0
    nq = seq // tq
    qb0 = row0 // tq
    kb0 = row0 // seq
    with_ctx = kc is not None
    in_specs = [pl.BlockSpec((2, tq, HEAD_PAD), lambda b, hp, qi: (hp, qb0 + b * nq + qi, 0)),
                pl.BlockSpec((2, seq, HEAD_PAD), lambda b, hp, qi: (hp, kb0 + b, 0)),
                pl.BlockSpec((1, seq, LANES), lambda b, hp, qi: (hp, kb0 + b, 0))]
    args = [q, k, v2]
    if with_ctx:
        in_specs += [pl.BlockSpec((2, ctx_len, HEAD_PAD), lambda b, hp, qi: (hp, b, 0)),
                     pl.BlockSpec((1, ctx_len, LANES), lambda b, hp, qi: (hp, b, 0))]
        args += [kc, vc2]
    return pl.pallas_call(
        functools.partial(_attn_kernel, with_ctx=with_ctx),
        out_shape=jax.ShapeDtypeStruct((n_b * seq, (N_HEADS // 2) * LANES), BF16),
        grid=(n_b, N_HEADS // 2, nq),
        in_specs=in_specs,
        out_specs=pl.BlockSpec((tq, LANES), lambda b, hp, qi: (b * nq + qi, hp)),
        compiler_params=_cparams(("parallel", "parallel", "parallel")),
        name="attn_ctx" if with_ctx else "attn",
    )(*args)


def _proj_res_kernel(y_ref, mod_ref, o_ref, w_ref, out_ref, *, gate_row):
    g = mod_ref[...][gate_row:gate_row + 1]
    out_ref[...] = y_ref[...] + g * _dot(o_ref[...], w_ref[...])


def proj_residual(lay, y, mod, o, w, gate_row):
    t, d = y.shape
    tm = lay.tm
    row = lambda i: (i, 0)
    return pl.pallas_call(
        functools.partial(_proj_res_kernel, gate_row=gate_row),
        out_shape=jax.ShapeDtypeStruct((t, d), F32),
        grid=(lay.n_tiles,),
        in_specs=[pl.BlockSpec((tm, d), row), pl.BlockSpec((None, 8, d), lambda i: (lay.seg(i), 0, 0)),
                  pl.BlockSpec((tm, o.shape[1]), row), pl.BlockSpec(w.shape, lambda i: (0, 0))],
        out_specs=pl.BlockSpec((tm, d), row),
        input_output_aliases={0: 0},
        compiler_params=_cparams(("parallel",)),
        name="proj_res",
    )(y, mod, o, w)


def _ssm_in_kernel(y_ref, mod_ref, nw_ref, w_ref, u_ref, us_ref, ug_ref):
    m = mod_ref[...]
    h = _norm_mod(y_ref[...], nw_ref[...], m[0:1], m[1:2])
    z = _dot(h.astype(BF16), w_ref[...])
    n_lt = us_ref.shape[0]
    for j in range(n_lt):
        us_ref[j] = z[:, j * LANES:(j + 1) * LANES]
    n_ch = us_ref.shape[1] // SSM_CHUNK
    kk = SSM_GROUP
    gpt = LANES // kk
    for t in range(SSM_CHUNK):
        for j in range(n_lt):
            piece = us_ref[j, pl.ds(t, n_ch, stride=SSM_CHUNK), :]
            for gl in range(gpt):
                ug_ref[j * gpt + gl, :, t * kk:(t + 1) * kk] = piece[:, gl * kk:(gl + 1) * kk]
    u_ref[...] = ug_ref[...].astype(BF16)


def ssm_in(lay, y, mod, nw, w_in):
    t, d = y.shape
    tm = lay.tm
    n = w_in.shape[1]
    g = n // SSM_GROUP
    n_ch = tm // SSM_CHUNK
    qk = SSM_CHUNK * SSM_GROUP
    row = lambda i: (i, 0)
    return pl.pallas_call(
        _ssm_in_kernel,
        out_shape=jax.ShapeDtypeStruct((g, t // SSM_CHUNK, qk), BF16),
        grid=(lay.n_tiles,),
        in_specs=[pl.BlockSpec((tm, d), row), pl.BlockSpec((None, 8, d), lambda i: (lay.seg(i), 0, 0)),
                  pl.BlockSpec((1, d), lambda i: (0, 0)), pl.BlockSpec((d, n), lambda i: (0, 0))],
        out_specs=pl.BlockSpec((g, n_ch, qk), lambda i: (0, i, 0)),
        scratch_shapes=[pltpu.VMEM((n // LANES, tm, LANES), F32), pltpu.VMEM((g, n_ch, qk), F32)],
        compiler_params=_cparams(("parallel",)),
        name="ssm_in",
    )(y, mod, nw, w_in)


def _ssm_prep_kernel(ar_ref, ai_ref, ld_ref, arc_ref, aic_ref, ldc_ref, btr_ref, bti_ref, ctr_ref, cti_ref, dsk_ref,
                     mt_ref, ett_ref, ft_ref, coef_ref):
    q = SSM_CHUNK
    k = SSM_GROUP
    p = SSM_STATE
    qk = q * k
    fwd = pl.program_id(0) == 0
    dt = jnp.exp(ld_ref[...])
    are = ar_ref[...]
    aim = ai_ref[...]
    mag = jnp.exp(dt * are)
    abr = mag * jnp.cos(dt * aim)
    abi = mag * jnp.sin(dt * aim)
    den = are * are + aim * aim
    nr = abr - 1.0
    cf_re = (nr * are + abi * aim) / den
    cf_im = (abi * are - nr * aim) / den
    btr = btr_ref[...]
    bti = bti_ref[...]
    bbt_re = cf_re * btr - cf_im * bti
    bbt_im = cf_re * bti + cf_im * btr
    bbt_re_t = jnp.concatenate([bbt_re] * q, axis=0)
    bbt_im_t = jnp.concatenate([bbt_im] * q, axis=0)
    s_idx = (lax.broadcasted_iota(jnp.int32, (qk, 1), 0) // k).astype(F32)
    pw = jnp.where(fwd, (q - 1.0) - s_idx, s_idx)
    pm = jnp.exp(pw * dt * are)
    pr = pm * jnp.cos(pw * dt * aim)
    pi = pm * jnp.sin(pw * dt * aim)
    et_re = pr * bbt_re_t - pi * bbt_im_t
    et_im = pr * bbt_im_t + pi * bbt_re_t
    ett_ref[...] = jnp.concatenate([et_re, et_im, et_im, et_re], axis=1).astype(BF16)
    qm = jnp.exp(q * dt * are)
    aq_re = qm * jnp.cos(q * dt * aim)
    aq_im = qm * jnp.sin(q * dt * aim)
    c1 = jnp.concatenate([aq_re, aq_re], axis=1)
    c2 = jnp.concatenate([-aq_im, aq_im], axis=1)
    coef_ref[...] = jnp.concatenate([c1, c2, jnp.zeros((6, 2 * p), F32)], axis=0)
    dtc = jnp.exp(ldc_ref[...])
    arec = arc_ref[...]
    aimc = aic_ref[...]
    tau = (lax.broadcasted_iota(jnp.int32, (1, qk), 1) // k).astype(F32)
    ctr = ctr_ref[...]
    cti = cti_ref[...]
    tau1 = jnp.where(fwd, tau + 1.0, q - tau)
    m1 = jnp.exp(tau1 * dtc * arec)
    p1r = m1 * jnp.cos(tau1 * dtc * aimc)
    p1i = m1 * jnp.sin(tau1 * dtc * aimc)
    ft_ref[...] = jnp.concatenate([ctr * p1r - cti * p1i, -(ctr * p1i + cti * p1r)], axis=0).astype(BF16)
    tau0 = jnp.where(fwd, tau, (q - 1.0) - tau)
    m0 = jnp.exp(tau0 * dtc * arec)
    p0r = m0 * jnp.cos(tau0 * dtc * aimc)
    p0i = m0 * jnp.sin(tau0 * dtc * aimc)
    left_re = ctr * p0r - cti * p0i
    left_im = ctr * p0i + cti * p0r
    kt = _dot_hi(bbt_re, left_re) - _dot_hi(bbt_im, left_im)
    lane = lax.broadcasted_iota(jnp.int32, (k, qk), 1)
    rowi = lax.broadcasted_iota(jnp.int32, (k, qk), 0)
    kt_f = kt + jnp.where(lane == rowi, dsk_ref[...], 0.0)
    rows_f, rows_b = [], []
    for s in range(q):
        rows_f.append(kt_f if s == 0 else jnp.where(lane >= s * k, pltpu.roll(kt_f, s * k, 1), 0.0))
        sh = ((s + 1) * k) % qk
        rows_b.append(kt if sh == 0 else jnp.where(lane < (s + 1) * k, pltpu.roll(kt, sh, 1), 0.0))
    mt = jnp.where(fwd, jnp.concatenate(rows_f, axis=0), jnp.concatenate(rows_b, axis=0))
    mt_ref[...] = mt.astype(BF16)


def ssm_prep(a_re, a_im, log_dt, b_re, b_im, c_re, c_im, d_skip):
    nd, g, p = a_re.shape
    k = b_re.shape[-1]
    qk = SSM_CHUNK * k
    ld = jnp.broadcast_to(log_dt[..., None], (nd, g, p))
    dsk = jnp.broadcast_to(jnp.tile(d_skip.reshape(g, 1, k), (1, 1, SSM_CHUNK))[None], (nd, g, 1, qk))
    rowv = lambda x: x.reshape(nd, g, 1, p)
    colv = lambda x: x.reshape(nd, g, p, 1)
    bt = lambda x: jnp.swapaxes(x, -1, -2)
    ct = lambda x: jnp.tile(jnp.swapaxes(x, -1, -2), (1, 1, 1, SSM_CHUNK))
    spec = lambda r, c: pl.BlockSpec((None, None, r, c), lambda d, j: (d, j, 0, 0))
    return pl.pallas_call(
        _ssm_prep_kernel,
        out_shape=(jax.ShapeDtypeStruct((nd, g, qk, qk), BF16), jax.ShapeDtypeStruct((nd, g, qk, 4 * p), BF16),
                   jax.ShapeDtypeStruct((nd, g, 2 * p, qk), BF16), jax.ShapeDtypeStruct((nd, g, 8, 2 * p), F32)),
        grid=(nd, g),
        in_specs=[spec(1, p), spec(1, p), spec(1, p), spec(p, 1), spec(p, 1), spec(p, 1),
                  spec(k, p), spec(k, p), spec(p, qk), spec(p, qk), spec(1, qk)],
        out_specs=(spec(qk, qk), spec(qk, 4 * p), spec(2 * p, qk), spec(8, 2 * p)),
        compiler_params=_cparams(("parallel", "parallel")),
        name="ssm_prep",
    )(rowv(a_re), rowv(a_im), rowv(ld), colv(a_re), colv(a_im), colv(ld), bt(b_re), bt(b_im), ct(c_re), ct(c_im), dsk)


def _ssm_scan_kernel(u_ref, mt_ref, ett_ref, ft_ref, coef_ref, x0_ref, x0s_ref, y_ref, hfin_ref,
                     sx_ref, sxs_ref, hp_ref, *, gpb, ctx_b, ctx_chunks, lat_b, lat_chunks, rchunk):
    w = 2 * SSM_STATE
    n_rows = u_ref.shape[1]
    ctx_rows = ctx_b * ctx_chunks
    for d in range(2):
        for g in range(gpb):
            for r0 in range(0, n_rows, rchunk):
                ss = _dot(u_ref[g, r0:r0 + rchunk, :], ett_ref[d, g])
                sx_ref[g, r0:r0 + rchunk, :] = ss[:, :w]
                sxs_ref[g, r0:r0 + rchunk, :] = ss[:, w:]
        c1 = [coef_ref[d, g][0:1] for g in range(gpb)]
        c2 = [coef_ref[d, g][1:2] for g in range(gpb)]

        def make_body(base, nb, nc):
            def body(i, carry):
                c = i if d == 0 else nc - 1 - i
                rows = pl.ds(base + c, nb, stride=nc)
                out = []
                for g in range(gpb):
                    x, xs = carry[g]
                    hp_ref[g, rows, :] = x
                    xn = c1[g] * x + c2[g] * xs + sx_ref[g, rows, :]
                    xsn = c1[g] * xs - c2[g] * x + sxs_ref[g, rows, :]
                    out.append((xn, xsn))
                return tuple(out)
            return body

        z = jnp.zeros((ctx_b, w), F32)
        fin = lax.fori_loop(0, ctx_chunks, make_body(0, ctx_b, ctx_chunks), tuple((z, z) for _ in range(gpb)))
        hfin_ref[d] = jnp.concatenate([fin[g][0] for g in range(gpb)], axis=1)
        init = tuple((x0_ref[d][:, g * w:(g + 1) * w], x0s_ref[d][:, g * w:(g + 1) * w]) for g in range(gpb))
        lax.fori_loop(0, lat_chunks, make_body(ctx_rows, lat_b, lat_chunks), init, unroll=4)
        for g in range(gpb):
            for r0 in range(0, n_rows, rchunk):
                y = _dot(u_ref[g, r0:r0 + rchunk, :], mt_ref[d, g])
                y = y + _dot(hp_ref[g, r0:r0 + rchunk, :].astype(BF16), ft_ref[d, g])
                if d == 0:
                    y_ref[g, r0:r0 + rchunk, :] = y
                else:
                    y_ref[g, r0:r0 + rchunk, :] += y


def ssm_scan(u, mt, ett, ft, coef, x0, x0s, ctx_b, ctx_chunks, lat_b, lat_chunks, gpb=4):
    g, n_rows, qk = u.shape
    w = 2 * SSM_STATE
    rchunk = math.gcd(n_rows, 512)
    assert n_rows == ctx_b * ctx_chunks + lat_b * lat_chunks
    ublk = pl.BlockSpec((gpb, n_rows, qk), lambda j: (j, 0, 0))
    blk = lambda r, c: pl.BlockSpec((2, gpb, r, c), lambda j: (0, j, 0, 0))
    vec = lambda r: pl.BlockSpec((2, r, gpb * w), lambda j: (0, 0, j))
    return pl.pallas_call(
        functools.partial(_ssm_scan_kernel, gpb=gpb, ctx_b=ctx_b, ctx_chunks=ctx_chunks, lat_b=lat_b,
                          lat_chunks=lat_chunks, rchunk=rchunk),
        out_shape=(jax.ShapeDtypeStruct((g, n_rows, qk), F32), jax.ShapeDtypeStruct((2, ctx_b, g * w), F32)),
        grid=(g // gpb,),
        in_specs=[ublk, blk(qk, qk), blk(qk, 2 * w), blk(w, qk), blk(8, w), vec(lat_b), vec(lat_b)],
        out_specs=(ublk, vec(ctx_b)),
        scratch_shapes=[pltpu.VMEM((gpb, n_rows, w), F32), pltpu.VMEM((gpb, n_rows, w), F32),
                        pltpu.VMEM((gpb, n_rows, w), F32)],
        compiler_params=_cparams(("parallel",)),
        name="ssm_scan",
    )(u, mt, ett, ft, coef, x0, x0s)


def _ssm_out_kernel(y_ref, mod_ref, yc_ref, w_ref, out_ref, s_ref, *, d_model):
    kk = SSM_GROUP
    n_ch = yc_ref.shape[1]
    gpt = LANES // kk
    n_lt = s_ref.shape[0]
    for t in range(SSM_CHUNK):
        for j in range(n_lt):
            piece = jnp.concatenate([yc_ref[j * gpt + gl, :, t * kk:(t + 1) * kk] for gl in range(gpt)], axis=1)
            s_ref[j, pl.ds(t, n_ch, stride=SSM_CHUNK), :] = piece
    s = jnp.concatenate([s_ref[j] for j in range(n_lt)], axis=1)
    a = jax.nn.gelu(s, approximate=True).astype(BF16)
    z = _dot(a, w_ref[...])
    gate = mod_ref[...][2:3]
    out_ref[...] = y_ref[...] + gate * (z[:, :d_model] * jax.nn.sigmoid(z[:, d_model:]))


def ssm_out(lay, y, mod, ych, w_glu):
    t, d = y.shape
    tm = lay.tm
    g, _, qk = ych.shape
    n_ch = tm // SSM_CHUNK
    row = lambda i: (i, 0)
    return pl.pallas_call(
        functools.partial(_ssm_out_kernel, d_model=d),
        out_shape=jax.ShapeDtypeStruct((t, d), F32),
        grid=(lay.n_tiles,),
        in_specs=[pl.BlockSpec((tm, d), row), pl.BlockSpec((None, 8, d), lambda i: (lay.seg(i), 0, 0)),
                  pl.BlockSpec((g, n_ch, qk), lambda i: (0, i, 0)),
                  pl.BlockSpec(w_glu.shape, lambda i: (0, 0))],
        out_specs=pl.BlockSpec((tm, d), row),
        scratch_shapes=[pltpu.VMEM((g * SSM_GROUP // LANES, tm, LANES), F32)],
        input_output_aliases={0: 0},
        compiler_params=_cparams(("parallel",)),
        name="ssm_out",
    )(y, mod, ych, w_glu)


def s5_layer(lay, y, mod, nw, state_re, state_im, w_in, a_re, a_im, log_dt, b_re, b_im, c_re, c_im, d_skip, w_glu):
    uch = ssm_in(lay, y, mod, nw, w_in.astype(BF16))
    g = a_re.shape[1]
    p = SSM_STATE
    mt, ett, ft, coef = ssm_prep(a_re, a_im, log_dt, b_re, b_im, c_re, c_im, d_skip)
    st = jnp.concatenate([state_re, state_im], axis=-1)
    sts = jnp.concatenate([state_im, state_re], axis=-1)
    x0 = jnp.transpose(st, (1, 0, 2, 3)).reshape(2, lay.n_lat_b, g * 2 * p)
    x0s = jnp.transpose(sts, (1, 0, 2, 3)).reshape(2, lay.n_lat_b, g * 2 * p)
    ych, hfin = ssm_scan(uch, mt, ett, ft, coef, x0, x0s, lay.n_ctx_b, lay.ctx_len // SSM_CHUNK,
                         lay.n_lat_b, lay.lat_len // SSM_CHUNK)
    y = ssm_out(lay, y, mod, ych, w_glu.astype(BF16))
    hf = hfin.reshape(2, lay.n_ctx_b, g, 2, p)
    new_re = jnp.transpose(hf[:, :, :, 0], (1, 0, 2, 3))
    new_im = jnp.transpose(hf[:, :, :, 1], (1, 0, 2, 3))
    return y, new_re, new_im


def _route_kernel(y_ref, mod_ref, nw_ref, wr_ref, br_ref, tri_ref, h_ref, ri_ref, pc_ref, *, n_exp):
    m = mod_ref[...]
    h = _norm_mod(y_ref[...], nw_ref[...], m[3:4], m[4:5])
    h_hi = h.astype(BF16)
    h_ref[...] = h_hi
    h_lo = (h - h_hi.astype(F32)).astype(BF16)
    wr = wr_ref[...]
    lt = _dot_nt(wr, h_hi)
    logits = lt[:n_exp] + lt[n_exp:] + _dot_nt(wr[:n_exp], h_lo)
    scores = jax.nn.sigmoid(logits)
    sel = scores + br_ref[...]
    epg = EXPERTS_PER_GROUP
    row = lambda x, e: x[e:e + 1, :]
    gscore = []
    for g in range(N_EXPERT_GROUPS):
        a, b, c, d = (row(sel, g * epg + j) for j in range(epg))
        m1, n1, m2, n2 = jnp.maximum(a, b), jnp.minimum(a, b), jnp.maximum(c, d), jnp.minimum(c, d)
        gscore.append(jnp.maximum(m1, m2) + jnp.maximum(jnp.minimum(m1, m2), jnp.maximum(n1, n2)))
    best = gscore[0]
    gi = jnp.zeros_like(best, dtype=jnp.int32)
    for g in range(1, N_EXPERT_GROUPS):
        better = gscore[g] > best
        gi = jnp.where(better, g, gi)
        best = jnp.where(better, gscore[g], best)

    def pick(x, j):
        out = row(x, j)
        for g in range(1, N_EXPERT_GROUPS):
            out = jnp.where(gi == g, row(x, g * epg + j), out)
        return out

    sv = [pick(sel, j) for j in range(epg)]
    cv = [pick(scores, j) for j in range(epg)]
    b1, i1, w1 = sv[0], jnp.zeros_like(gi), cv[0]
    for j in range(1, epg):
        better = sv[j] > b1
        i1 = jnp.where(better, j, i1)
        w1 = jnp.where(better, cv[j], w1)
        b1 = jnp.where(better, sv[j], b1)
    neg = jnp.full_like(b1, -jnp.inf)
    b2, i2, w2 = neg, jnp.zeros_like(gi), jnp.zeros_like(w1)
    for j in range(epg):
        better = (i1 != j) & (sv[j] > b2)
        i2 = jnp.where(better, j, i2)
        w2 = jnp.where(better, cv[j], w2)
        b2 = jnp.where(better, sv[j], b2)
    tot = w1 + w2
    e1 = gi * epg + i1
    e2 = gi * epg + i2
    eid = lax.broadcasted_iota(jnp.int32, logits.shape, 0)
    m1h = eid == e1
    m2h = eid == e2
    mc = jnp.where(m1h | m2h, 1.0, 0.0)
    pref = _dot(mc.astype(BF16), tri_ref[...])
    cnt = jnp.sum(mc, axis=1, keepdims=True)
    pc_al = jnp.ceil(cnt * (1.0 / SEG_ALIGN))
    pcb = jnp.broadcast_to(pc_al, (n_exp, LANES))
    er = lax.broadcasted_iota(jnp.int32, (n_exp, n_exp), 0)
    ec = lax.broadcasted_iota(jnp.int32, (n_exp, n_exp), 1)
    lower = jnp.where(ec < er, 1.0, 0.0).astype(BF16)
    seg = _dot(lower, pcb.astype(BF16))[:, 0:1] * SEG_ALIGN
    slot = seg + pref
    pos1 = jnp.sum(jnp.where(m1h, slot, 0.0), axis=0, keepdims=True)
    pos2 = jnp.sum(jnp.where(m2h, slot, 0.0), axis=0, keepdims=True)
    zero = jnp.zeros_like(pos1)
    ri_ref[...] = jnp.concatenate([pos1, pos2, w1 / tot, w2 / tot, zero, zero, zero, zero], axis=0)
    pc_ref[...] = pcb * SEG_ALIGN


def moe_route(lay, y, mod, nw, wr_t, br, tri):
    t, d = y.shape
    tm = lay.tm
    n_exp = br.shape[0]
    return pl.pallas_call(
        functools.partial(_route_kernel, n_exp=n_exp),
        out_shape=(jax.ShapeDtypeStruct((t, d), BF16), jax.ShapeDtypeStruct((8, t), F32),
                   jax.ShapeDtypeStruct((lay.n_tiles, n_exp, LANES), F32)),
        grid=(lay.n_tiles,),
        in_specs=[pl.BlockSpec((tm, d), lambda i: (i, 0)),
                  pl.BlockSpec((None, 8, d), lambda i: (lay.seg(i), 0, 0)),
                  pl.BlockSpec((1, d), lambda i: (0, 0)), pl.BlockSpec(wr_t.shape, lambda i: (0, 0)),
                  pl.BlockSpec((n_exp, 1), lambda i: (0, 0)), pl.BlockSpec((tm, tm), lambda i: (0, 0))],
        out_specs=(pl.BlockSpec((tm, d), lambda i: (i, 0)), pl.BlockSpec((8, tm), lambda i: (0, i)),
                   pl.BlockSpec((None, n_exp, LANES), lambda i: (i, 0, 0))),
        compiler_params=_cparams(("parallel",)),
        name="moe_route",
    )(y, mod, nw, wr_t, br, tri)


def _segment_pieces(i, off_ref, pc_ref, n_exp, sizes):
    local = 0
    for e in range(n_exp):
        n = pc_ref[i * n_exp + e]
        g0 = off_ref[i * n_exp + e]
        for size in sizes:
            above = n & (-2 * size)
            yield (e, (n & size) != 0, pl.multiple_of(local + above, SEG_ALIGN),
                   pl.multiple_of(g0 + above, SEG_ALIGN), size)
        local = local + n


def _moe_permute_kernel(off_ref, pc_ref, h_ref, ri_ref, xs_in_ref, xs_ref, ys_ref, sem, *, n_exp, sizes, rchunk):
    del xs_in_ref
    i = pl.program_id(0)
    ri = ri_ref[...]
    pos1 = ri[0:1]
    pos2 = ri[1:2]
    h = h_ref[...]
    n_loc = ys_ref.shape[0]
    for r0 in range(0, n_loc, rchunk):
        r = (lax.broadcasted_iota(jnp.int32, (rchunk, h.shape[0]), 0) + r0).astype(F32)
        p = jnp.where(pos1 == r, 1.0, 0.0) + jnp.where(pos2 == r, 1.0, 0.0)
        ys_ref[r0:r0 + rchunk, :] = _dot(p.astype(BF16), h).astype(BF16)

    def copies():
        for e, present, lrow, grow, size in _segment_pieces(i, off_ref, pc_ref, n_exp, sizes):
            yield present, pltpu.make_async_copy(ys_ref.at[pl.ds(lrow, size)], xs_ref.at[pl.ds(grow, size)], sem.at[e])

    for present, cp in copies():
        pl.when(present)(cp.start)
    for present, cp in copies():
        pl.when(present)(cp.wait)


def moe_permute(lay, h, rinfo, off, pcs, n_sorted, n_loc, n_exp):
    t, d = h.shape
    tm = lay.tm
    sizes = [s for s in (1 << b for b in range(12, 3, -1)) if s <= tm]
    return pl.pallas_call(
        functools.partial(_moe_permute_kernel, n_exp=n_exp, sizes=sizes, rchunk=256),
        out_shape=jax.ShapeDtypeStruct((n_sorted, d), BF16),
        grid_spec=pltpu.PrefetchScalarGridSpec(
            num_scalar_prefetch=2, grid=(lay.n_tiles,),
            in_specs=[pl.BlockSpec((tm, d), lambda i, o, p: (i, 0)), pl.BlockSpec((8, tm), lambda i, o, p: (0, i)),
                      pl.BlockSpec(memory_space=pl.ANY)],
            out_specs=pl.BlockSpec(memory_space=pl.ANY),
            scratch_shapes=[pltpu.VMEM((n_loc, d), BF16), pltpu.SemaphoreType.DMA((n_exp,))]),
        input_output_aliases={4: 0},
        compiler_params=_cparams(("arbitrary",)),
        name="moe_permute",
    )(off, pcs, h, rinfo, jnp.zeros((n_sorted, d), BF16))


def _moe_expert_kernel(te_ref, tv_ref, x_ref, wg_ref, wu_ref, wd_ref, z_ref):
    r = pl.program_id(0)

    @pl.when(tv_ref[r] > 0)
    def _():
        x = x_ref[...]
        a = _dot(x, wg_ref[...])
        b = _dot(x, wu_ref[...])
        hid = (a * jax.nn.sigmoid(a)) * b
        z_ref[...] = _dot(hid.astype(BF16), wd_ref[...]).astype(BF16)

    @pl.when(tv_ref[r] == 0)
    def _():
        z_ref[...] = jnp.zeros_like(z_ref)


def moe_experts(xs, tile_expert, tile_valid, wg, wu, wd, tmm):
    n_sorted, d = xs.shape
    _, _, f = wg.shape
    return pl.pallas_call(
        _moe_expert_kernel,
        out_shape=jax.ShapeDtypeStruct((n_sorted, d), BF16),
        grid_spec=pltpu.PrefetchScalarGridSpec(
            num_scalar_prefetch=2, grid=(n_sorted // tmm,),
            in_specs=[pl.BlockSpec((tmm, d), lambda r, te, tv: (r, 0)),
                      pl.BlockSpec((None, d, f), lambda r, te, tv: (te[r], 0, 0)),
                      pl.BlockSpec((None, d, f), lambda r, te, tv: (te[r], 0, 0)),
                      pl.BlockSpec((None, f, d), lambda r, te, tv: (te[r], 0, 0))],
            out_specs=pl.BlockSpec((tmm, d), lambda r, te, tv: (r, 0))),
        compiler_params=_cparams(("arbitrary",)),
        name="moe_experts",
    )(tile_expert, tile_valid, xs, wg, wu, wd)


def _moe_combine_kernel(off_ref, pc_ref, y_ref, mod_ref, rit_ref, zs_ref, out_ref, zt_ref, sem, *, n_exp, sizes):
    i = pl.program_id(0)
    zt_ref[...] = jnp.zeros_like(zt_ref)

    def copies():
        for e, present, lrow, grow, size in _segment_pieces(i, off_ref, pc_ref, n_exp, sizes):
            yield present, pltpu.make_async_copy(zs_ref.at[pl.ds(grow, size)], zt_ref.at[pl.ds(lrow, size)], sem.at[e])

    for present, cp in copies():
        pl.when(present)(cp.start)
    rit = rit_ref[...]
    n_loc = zt_ref.shape[0]
    r = lax.broadcasted_iota(jnp.int32, (rit.shape[0], n_loc), 1).astype(F32)
    pw = jnp.where(rit[:, 0:1] == r, rit[:, 2:3], 0.0) + jnp.where(rit[:, 1:2] == r, rit[:, 3:4], 0.0)
    pw = pw.astype(BF16)
    for present, cp in copies():
        pl.when(present)(cp.wait)
    out_ref[...] = y_ref[...] + mod_ref[...][5:6] * _dot(pw, zt_ref[...])


def moe_combine(lay, y, mod, rinfo_t, zs, off, pcs, n_loc, n_exp):
    t, d = y.shape
    tm = lay.tm
    sizes = [s for s in (1 << b for b in range(12, 3, -1)) if s <= tm]
    return pl.pallas_call(
        functools.partial(_moe_combine_kernel, n_exp=n_exp, sizes=sizes),
        out_shape=jax.ShapeDtypeStruct((t, d), F32),
        grid_spec=pltpu.PrefetchScalarGridSpec(
            num_scalar_prefetch=2, grid=(lay.n_tiles,),
            in_specs=[pl.BlockSpec((tm, d), lambda i, o, p: (i, 0)),
                      pl.BlockSpec((None, 8, d), lambda i, o, p: (lay.seg(i), 0, 0)),
                      pl.BlockSpec((tm, 8), lambda i, o, p: (i, 0)),
                      pl.BlockSpec(memory_space=pl.ANY)],
            out_specs=pl.BlockSpec((tm, d), lambda i, o, p: (i, 0)),
            scratch_shapes=[pltpu.VMEM((n_loc, d), BF16), pltpu.SemaphoreType.DMA((n_exp,))]),
        input_output_aliases={2: 0},
        compiler_params=_cparams(("arbitrary",)),
        name="moe_combine",
    )(off, pcs, y, mod, rinfo_t, zs)


def _round_up(x, m):
    return (x + m - 1) // m * m


def moe_layer(lay, y, mod, nw, wr_t, br, tri, wg, wu, wd, tmm):
    t, d = y.shape
    n_exp = br.shape[0]
    h, rinfo, pc = moe_route(lay, y, mod, nw, wr_t, br, tri)
    pc = pc[:, :, 0].astype(jnp.int32)
    n_loc = 2 * lay.tm + SEG_ALIGN * n_exp
    n_sorted = _round_up(2 * t + lay.n_tiles * n_exp * (SEG_ALIGN - 1) + n_exp * (tmm - 1), tmm)
    tot = jnp.sum(pc, axis=0)
    region = _round_up(tot, tmm)
    ends = jnp.cumsum(region)
    base = ends - region
    off = (base[None, :] + jnp.cumsum(pc, axis=0) - pc).reshape(-1)
    starts = jnp.arange(n_sorted // tmm, dtype=jnp.int32) * tmm
    te = jnp.minimum(jnp.sum(starts[:, None] >= ends[None, :], axis=1), n_exp - 1).astype(jnp.int32)
    tv = jnp.clip(tot[te] - (starts - base[te]), 0, tmm).astype(jnp.int32)
    pcs = pc.reshape(-1)
    xs = moe_permute(lay, h, rinfo, off, pcs, n_sorted, n_loc, n_exp)
    zs = moe_experts(xs, te, tv, wg, wu, wd, tmm)
    return moe_combine(lay, y, mod, rinfo.T, zs, off, pcs, n_loc, n_exp)


def _rope_perm():
    d = np.arange(QK_ROPE)
    return QK_NOPE + (d ^ (QK_ROPE // 4))


def _pad_cols(x, n):
    return jnp.pad(x, ((0, 0),) * (x.ndim - 1) + ((0, n - x.shape[-1]),))


def _mla_weights(w_q_a, w_q_b, w_kv_a, w_kv_b, q_norm, k_norm, kv_lora):
    perm = _rope_perm()
    nh = N_HEADS
    wkr = w_kv_a[:, kv_lora:]
    d = w_kv_a.shape[0]
    zl = jnp.zeros((d, QK_NOPE), F32)
    kr_blk = _pad_cols(jnp.concatenate([zl, wkr], axis=1), LANES)
    krp_blk = _pad_cols(jnp.concatenate([zl, wkr[:, perm - QK_NOPE]], axis=1), LANES)
    w_a = jnp.concatenate([w_q_a, w_kv_a[:, :kv_lora], kr_blk, krp_blk], axis=1).astype(BF16)
    wq = w_q_b.reshape(-1, nh, QK_DIM).transpose(1, 0, 2)
    wq_main = _pad_cols(wq, HEAD_PAD)
    wq_part = _pad_cols(jnp.concatenate([jnp.zeros_like(wq[..., :QK_NOPE]), wq[..., perm]], axis=-1), HEAD_PAD)
    wq_h = jnp.concatenate([wq_main, wq_part], axis=-1).astype(BF16)
    wkv = w_kv_b.reshape(-1, nh, QK_NOPE + V_DIM).transpose(1, 0, 2)
    wk_h = _pad_cols(wkv[..., :QK_NOPE], HEAD_PAD).astype(BF16)
    wv = wkv[..., QK_NOPE:]
    wv_h = jnp.concatenate([wv[0::2], wv[1::2]], axis=-1).astype(BF16)
    nrm = jnp.stack([_pad_cols(q_norm, LANES), _pad_cols(jnp.concatenate([jnp.zeros((QK_NOPE,), F32), q_norm[perm]]), LANES),
                     _pad_cols(k_norm, LANES), _pad_cols(jnp.concatenate([jnp.zeros((QK_NOPE,), F32), k_norm[perm]]), LANES)])
    nrm = jnp.pad(nrm, ((0, 4), (0, 0)))
    return w_a, wq_h, wk_h, wv_h, nrm


def _rope_tables(lat_len, tm):
    rows = lat_len // GRID_W
    t_row = np.repeat(np.arange(rows, dtype=np.float32), GRID_W)
    t_col = np.tile(np.arange(GRID_W, dtype=np.float32), rows)
    half = QK_ROPE // 2
    inv = (ROPE_BASE ** (-np.arange(0, half, 2, dtype=np.float32) / half)).astype(np.float32)
    ang = jnp.concatenate([jnp.asarray(t_row)[:, None] * inv, jnp.asarray(t_col)[:, None] * inv], axis=-1)
    cos, sin = jnp.cos(ang), jnp.sin(ang)
    nf = QK_ROPE // 4
    cos_l = jnp.concatenate([cos[:, :nf], cos[:, :nf], cos[:, nf:], cos[:, nf:]], axis=-1)
    sin_l = jnp.concatenate([-sin[:, :nf], sin[:, :nf], -sin[:, nf:], sin[:, nf:]], axis=-1)
    one = jnp.ones((lat_len, QK_NOPE), F32)
    cos_t = _pad_cols(jnp.concatenate([one, cos_l], axis=-1), LANES)
    cos_t = cos_t.at[:, QK_DIM:].set(1.0)
    sin_t = _pad_cols(jnp.concatenate([jnp.zeros((lat_len, QK_NOPE), F32), sin_l], axis=-1), LANES)
    cos_t = jnp.concatenate([jnp.ones((tm, LANES), F32), cos_t], axis=0)
    sin_t = jnp.concatenate([jnp.zeros((tm, LANES), F32), sin_t], axis=0)
    return cos_t, sin_t


def mla_layer(lay, y, mod, nw, cache_ckv, cache_krope, cos_t, sin_t, w_q_a, q_a_norm, w_q_b, w_kv_a, kv_a_norm,
              w_kv_b, q_norm, k_norm, w_o):
    q_lora = w_q_a.shape[1]
    kv_lora = kv_a_norm.shape[0]
    w_a, wq_h, wk_h, wv_h, nrm = _mla_weights(w_q_a, w_q_b, w_kv_a, w_kv_b, q_norm, k_norm, kv_lora)
    cq, ckv, kr, krp = mla_a(lay, y, mod, nw, w_a, q_a_norm.reshape(1, -1), kv_a_norm.reshape(1, -1), q_lora, kv_lora)
    q, k, v2 = mla_b(lay, cq, ckv, kr, krp, wq_h, wk_h, wv_h, nrm, cos_t, sin_t)
    past = cache_ckv.shape[1]
    ckr = jnp.pad(cache_krope.reshape(-1, QK_ROPE), ((0, 0), (QK_NOPE, LANES - QK_DIM)))
    kc, vc2 = ctx_kv(cache_ckv.reshape(-1, kv_lora), ckr, wk_h, wv_h, nrm, past)
    o_ctx = attention(q, k, v2, 0, lay.n_ctx_b, lay.ctx_len, min(lay.ctx_len, 256))
    o_lat = attention(q, k, v2, lay.t_ctx, lay.n_lat_b, lay.lat_len, min(lay.lat_len, 256), kc, vc2, past)
    o = jnp.concatenate([o_ctx, o_lat], axis=0)
    y = proj_residual(lay, y, mod, o, w_o.astype(BF16), 2)
    new_ckv = ckv[:lay.t_ctx].reshape(lay.n_ctx_b, lay.ctx_len, kv_lora)
    new_kr = kr[:lay.t_ctx, QK_NOPE:QK_DIM].reshape(lay.n_ctx_b, lay.ctx_len, QK_ROPE)
    return y, new_ckv, new_kr


def kernel(x_prompt, x_sample, c, cache_ckv, cache_krope, state_ssm_re, state_ssm_im, c_ctx, w_mod, b_mod, norm1_w, norm2_w, mla_w_q_a, mla_q_a_norm, mla_w_q_b, mla_w_kv_a, mla_kv_a_norm, mla_w_kv_b, mla_q_norm, mla_k_norm, mla_w_o, ssm_w_in, ssm_a_re, ssm_a_im, ssm_log_dt, ssm_b_re, ssm_b_im, ssm_c_re, ssm_c_im, ssm_d, ssm_w_glu, w_router, b_router, moe_w_gate, moe_w_up, moe_w_down):
    n_ctx_b, ctx_len, d = x_prompt.shape
    n_lat_b, lat_len, _ = x_sample.shape
    depth = w_mod.shape[0]
    lay = Layout(n_ctx_b, ctx_len, n_lat_b, lat_len, tm=256)
    assert n_lat_b + 1 <= 8
    y = jnp.concatenate([x_prompt.reshape(-1, d), x_sample.reshape(-1, d)], axis=0)
    cond8 = jnp.pad(jnp.concatenate([c_ctx[None, :], c], axis=0), ((0, 7 - n_lat_b), (0, 0)))
    mods = adaln_all(cond8, w_mod, b_mod)
    mods = jnp.pad(mods.reshape(depth, 8, 6, d), ((0, 0), (0, 0), (0, 2), (0, 0)))
    cos_t, sin_t = _rope_tables(lat_len, lay.tm)
    n_exp = b_router.shape[0]
    wr_hi = w_router.astype(BF16)
    wr_lo = (w_router - wr_hi.astype(F32)).astype(BF16)
    wr_t = jnp.concatenate([wr_hi.T, wr_lo.T], axis=0)
    br = b_router.reshape(n_exp, 1)
    lay_s = Layout(n_ctx_b, ctx_len, n_lat_b, lat_len, tm=512)
    tri = jnp.asarray(np.triu(np.ones((lay_s.tm, lay_s.tm), np.float32), k=1), BF16)
    ckv_out, kr_out, sre_out, sim_out = [], [], [], []
    for i in range(depth):
        j = i // 2
        mod = mods[i]
        nw1 = norm1_w[i].reshape(1, d)
        if i % 2 == 0:
            y, ckv_p, kr_p = mla_layer(lay, y, mod, nw1, cache_ckv[:, j], cache_krope[:, j], cos_t, sin_t,
                                       mla_w_q_a[j], mla_q_a_norm[j], mla_w_q_b[j], mla_w_kv_a[j], mla_kv_a_norm[j],
                                       mla_w_kv_b[j], mla_q_norm[j], mla_k_norm[j], mla_w_o[j])
            ckv_out.append(ckv_p)
            kr_out.append(kr_p)
        else:
            y, s_re, s_im = s5_layer(lay, y, mod, nw1, state_ssm_re[:, j], state_ssm_im[:, j], ssm_w_in[j],
                                     ssm_a_re[j], ssm_a_im[j], ssm_log_dt[j], ssm_b_re[j], ssm_b_im[j],
                                     ssm_c_re[j], ssm_c_im[j], ssm_d[j], ssm_w_glu[j])
            sre_out.append(s_re)
            sim_out.append(s_im)
        y = moe_layer(lay_s, y, mod, norm2_w[i].reshape(1, d), wr_t, br, tri, moe_w_gate[i].astype(BF16),
                      moe_w_up[i].astype(BF16), moe_w_down[i].astype(BF16), tmm=512)
    yp = y[:lay.t_ctx].reshape(n_ctx_b, ctx_len, d)
    ys = y[lay.t_ctx:].reshape(n_lat_b, lat_len, d)
    return (yp, ys, jnp.stack(ckv_out, axis=1), jnp.stack(kr_out, axis=1),
            jnp.stack(sre_out, axis=1), jnp.stack(sim_out, axis=1))
```

```python
import functools
import math

import jax
import jax.numpy as jnp
import numpy as np
from jax import lax
from jax.experimental import pallas as pl
from jax.experimental.pallas import tpu as pltpu

F32 = jnp.float32
BF16 = jnp.bfloat16
EPS = 1e-6

GRID_W = 64
N_HEADS = 8
QK_NOPE = 64
QK_ROPE = 32
QK_DIM = QK_NOPE + QK_ROPE
V_DIM = 64
ROPE_BASE = 10000.0
SSM_GROUP = 16
SSM_STATE = 64
N_EXPERT_GROUPS = 4
EXPERTS_PER_GROUP = 4

LANES = 128
HEAD_PAD = LANES
SSM_CHUNK = 16
SEG_ALIGN = 16
VMEM_LIMIT = 48 * 1024 * 1024


def _cparams(sem, vmem=VMEM_LIMIT):
    return pltpu.CompilerParams(dimension_semantics=sem, vmem_limit_bytes=vmem)


def _dot(a, b):
    return jnp.dot(a, b, preferred_element_type=F32)


def _dot_nt(a, b):
    return lax.dot_general(a, b, (((1,), (1,)), ((), ())), preferred_element_type=F32)


def _dot_hi(a, b):
    return jnp.dot(a, b, preferred_element_type=F32, precision=lax.Precision.HIGHEST)


def _norm_mod(x, nw, shift, scale):
    ms = jnp.mean(x * x, axis=-1, keepdims=True)
    return (x * lax.rsqrt(ms + EPS) * nw) * (1.0 + scale) + shift


class Layout:
    def __init__(self, n_ctx_b, ctx_len, n_lat_b, lat_len, tm):
        self.n_ctx_b, self.ctx_len, self.n_lat_b, self.lat_len = n_ctx_b, ctx_len, n_lat_b, lat_len
        self.t_ctx = n_ctx_b * ctx_len
        self.t_lat = n_lat_b * lat_len
        self.t = self.t_ctx + self.t_lat
        self.tm = tm
        assert self.t_ctx % tm == 0 and lat_len % tm == 0
        self.ctx_tiles = self.t_ctx // tm
        self.lat_tiles_per_b = lat_len // tm
        self.n_tiles = self.t // tm

    def seg(self, i):
        return jnp.where(i < self.ctx_tiles, 0, 1 + (i - self.ctx_tiles) // self.lat_tiles_per_b)

    def rope_blk(self, i):
        return jnp.where(i < self.ctx_tiles, 0, 1 + (i - self.ctx_tiles) % self.lat_tiles_per_b)

    def stream_specs(self, d, pair):
        tm = self.tm
        if not pair:
            return [pl.BlockSpec((tm, d), lambda i, *_: (i, 0))]
        return [pl.BlockSpec((tm, d), lambda i, *_: (jnp.minimum(i, self.ctx_tiles - 1), 0)),
                pl.BlockSpec((tm, d), lambda i, *_: (jnp.maximum(i - self.ctx_tiles, 0), 0))]

    def stream_load(self, refs):
        if len(refs) == 1:
            return refs[0][...]
        return jnp.where(pl.program_id(0) < self.ctx_tiles, refs[0][...], refs[1][...])

    def stream_store(self, refs, val):
        if len(refs) == 1:
            refs[0][...] = val
            return
        i = pl.program_id(0)

        @pl.when(i < self.ctx_tiles)
        def _():
            refs[0][...] = val

        @pl.when(i >= self.ctx_tiles)
        def _():
            refs[1][...] = val


def _as_list(y):
    return list(y) if isinstance(y, (tuple, list)) else [y]


def _adaln_kernel(c_ref, w_ref, b_ref, o_ref):
    c = c_ref[...]
    s = c * jax.nn.sigmoid(c)
    o_ref[...] = _dot_hi(s, w_ref[...]) + b_ref[...]


def adaln_all(cond8, w_mod, b_mod, tn=1536):
    depth, d, n6 = w_mod.shape
    return pl.pallas_call(
        _adaln_kernel,
        out_shape=jax.ShapeDtypeStruct((depth, 8, n6), F32),
        grid=(depth, n6 // tn),
        in_specs=[pl.BlockSpec((8, d), lambda l, j: (0, 0)),
                  pl.BlockSpec((None, d, tn), lambda l, j: (l, 0, j)),
                  pl.BlockSpec((None, 1, tn), lambda l, j: (l, 0, j))],
        out_specs=pl.BlockSpec((None, 8, tn), lambda l, j: (l, 0, j)),
        compiler_params=_cparams(("parallel", "parallel")),
        name="adaln",
    )(cond8, w_mod, b_mod.reshape(depth, 1, n6))


def _mla_a_kernel(*refs, lay, n_y, q_lora, kv_lora):
    y_refs = refs[:n_y]
    mod_ref, nw_ref, w_ref, qan_ref, kvan_ref, cq_ref, ckv_ref, kr_ref, krp_ref = refs[n_y:]
    m = mod_ref[...]
    h = _norm_mod(lay.stream_load(y_refs), nw_ref[...], m[0:1], m[1:2])
    z = _dot(h.astype(BF16), w_ref[...])
    cq = z[:, :q_lora]
    cq = cq * lax.rsqrt(jnp.mean(cq * cq, axis=-1, keepdims=True) + EPS) * qan_ref[...]
    cq_ref[...] = cq.astype(BF16)
    ckv = z[:, q_lora:q_lora + kv_lora]
    ckv_ref[...] = ckv * lax.rsqrt(jnp.mean(ckv * ckv, axis=-1, keepdims=True) + EPS) * kvan_ref[...]
    kr_ref[...] = z[:, q_lora + kv_lora:q_lora + kv_lora + LANES]
    krp_ref[...] = z[:, q_lora + kv_lora + LANES:]


def mla_a(lay, y, mod, nw, w_a, qan, kvan, q_lora, kv_lora):
    ys = _as_list(y)
    t, d = lay.t, ys[0].shape[1]
    tm = lay.tm
    na = w_a.shape[1]
    row = lambda i: (i, 0)
    full = lambda i: (0, 0)
    return pl.pallas_call(
        functools.partial(_mla_a_kernel, lay=lay, n_y=len(ys), q_lora=q_lora, kv_lora=kv_lora),
        out_shape=(jax.ShapeDtypeStruct((t, q_lora), BF16), jax.ShapeDtypeStruct((t, kv_lora), F32),
                   jax.ShapeDtypeStruct((t, LANES), F32), jax.ShapeDtypeStruct((t, LANES), F32)),
        grid=(lay.n_tiles,),
        in_specs=lay.stream_specs(d, len(ys) == 2) + [
            pl.BlockSpec((None, 8, d), lambda i: (lay.seg(i), 0, 0)),
            pl.BlockSpec((1, d), full), pl.BlockSpec((d, na), full),
            pl.BlockSpec((1, q_lora), full), pl.BlockSpec((1, kv_lora), full)],
        out_specs=(pl.BlockSpec((tm, q_lora), row), pl.BlockSpec((tm, kv_lora), row),
                   pl.BlockSpec((tm, LANES), row), pl.BlockSpec((tm, LANES), row)),
        compiler_params=_cparams(("parallel",)),
        name="mla_a",
    )(*ys, mod, nw, w_a, qan, kvan)


def _kv_heads(ckv_b, kr, krp, wk_ref, wv_ref, knw, knwp, cos, sin, k_ref, v_ref):
    kc = knw * cos
    ks = knwp * sin
    for h in range(N_HEADS):
        kz = _dot(ckv_b, wk_ref[h]) + kr
        r = lax.rsqrt(jnp.sum(kz * kz, axis=-1, keepdims=True) * (1.0 / QK_DIM) + EPS)
        k_ref[h] = (r * (kz * kc + krp * ks)).astype(BF16)
    for hp in range(N_HEADS // 2):
        v_ref[hp] = _dot(ckv_b, wv_ref[hp]).astype(BF16)


def _mla_b_kernel(cq_ref, ckv_ref, kr_ref, krp_ref, wq_ref, wk_ref, wv_ref, nrm_ref, cos_ref, sin_ref,
                  q_ref, k_ref, v_ref):
    nrm = nrm_ref[...]
    cos = cos_ref[...]
    sin = sin_ref[...]
    cq = cq_ref[...]
    qc = nrm[0:1] * cos * (QK_DIM ** -0.5)
    qs = nrm[1:2] * sin * (QK_DIM ** -0.5)
    for h in range(N_HEADS):
        z = _dot(cq, wq_ref[h])
        qm = z[:, :HEAD_PAD]
        qp = z[:, HEAD_PAD:]
        r = lax.rsqrt(jnp.sum(qm * qm, axis=-1, keepdims=True) * (1.0 / QK_DIM) + EPS)
        q_ref[h] = (r * (qm * qc + qp * qs)).astype(BF16)
    _kv_heads(ckv_ref[...].astype(BF16), kr_ref[...], krp_ref[...], wk_ref, wv_ref, nrm[2:3], nrm[3:4],
              cos, sin, k_ref, v_ref)


def mla_b(lay, cq, ckv, kr, krp, wq, wk, wv, nrm, cos_t, sin_t):
    t = cq.shape[0]
    tm = lay.tm
    row = lambda i: (i, 0)
    hrow = lambda i: (0, i, 0)
    full3 = lambda i: (0, 0, 0)
    rope = lambda i: (lay.rope_blk(i), 0)
    nh = N_HEADS
    return pl.pallas_call(
        _mla_b_kernel,
        out_shape=(jax.ShapeDtypeStruct((nh, t, HEAD_PAD), BF16), jax.ShapeDtypeStruct((nh, t, HEAD_PAD), BF16),
                   jax.ShapeDtypeStruct((nh // 2, t, LANES), BF16)),
        grid=(lay.n_tiles,),
        in_specs=[pl.BlockSpec((tm, cq.shape[1]), row), pl.BlockSpec((tm, ckv.shape[1]), row),
                  pl.BlockSpec((tm, LANES), row), pl.BlockSpec((tm, LANES), row),
                  pl.BlockSpec(wq.shape, full3), pl.BlockSpec(wk.shape, full3), pl.BlockSpec(wv.shape, full3),
                  pl.BlockSpec((8, LANES), lambda i: (0, 0)),
                  pl.BlockSpec((tm, LANES), rope), pl.BlockSpec((tm, LANES), rope)],
        out_specs=(pl.BlockSpec((nh, tm, HEAD_PAD), hrow), pl.BlockSpec((nh, tm, HEAD_PAD), hrow),
                   pl.BlockSpec((nh // 2, tm, LANES), hrow)),
        compiler_params=_cparams(("parallel",)),
        name="mla_b",
    )(cq, ckv, kr, krp, wq, wk, wv, nrm, cos_t, sin_t)


def _ctx_kv_kernel(ckv_ref, kr_ref, wk_ref, wv_ref, nrm_ref, k_ref, v_ref):
    nrm = nrm_ref[...]
    kr = kr_ref[...]
    one = jnp.ones((1, LANES), F32)
    _kv_heads(ckv_ref[...].astype(BF16), kr, kr, wk_ref, wv_ref, nrm[2:3], nrm[3:4],
              one, jnp.zeros((1, LANES), F32), k_ref, v_ref)


def ctx_kv(ckv, kr, wk, wv, nrm, tm):
    t = ckv.shape[0]
    row = lambda i: (i, 0)
    hrow = lambda i: (0, i, 0)
    full3 = lambda i: (0, 0, 0)
    nh = N_HEADS
    return pl.pallas_call(
        _ctx_kv_kernel,
        out_shape=(jax.ShapeDtypeStruct((nh, t, HEAD_PAD), BF16), jax.ShapeDtypeStruct((nh // 2, t, LANES), BF16)),
        grid=(t // tm,),
        in_specs=[pl.BlockSpec((tm, ckv.shape[1]), row), pl.BlockSpec((tm, LANES), row),
                  pl.BlockSpec(wk.shape, full3), pl.BlockSpec(wv.shape, full3),
                  pl.BlockSpec((8, LANES), lambda i: (0, 0))],
        out_specs=(pl.BlockSpec((nh, tm, HEAD_PAD), hrow), pl.BlockSpec((nh // 2, tm, LANES), hrow)),
        compiler_params=_cparams(("parallel",)),
        name="ctx_kv",
    )(ckv, kr, wk, wv, nrm)


def _attn_kernel(*refs, with_ctx):
    if with_ctx:
        q_ref, k_ref, v_ref, kc_ref, vc_ref, o_ref = refs
    else:
        q_ref, k_ref, v_ref, o_ref = refs
    v = v_ref[0]
    outs = []
    for j in range(2):
        q = q_ref[j]
        s = _dot_nt(q, k_ref[j])
        m = jnp.max(s, axis=-1, keepdims=True)
        if with_ctx:
            sc = _dot_nt(q, kc_ref[j])
            m = jnp.maximum(m, jnp.max(sc, axis=-1, keepdims=True))
        p = jnp.exp(s - m)
        l = jnp.sum(p, axis=-1, keepdims=True)
        acc = _dot(p.astype(BF16), v)
        if with_ctx:
            pc = jnp.exp(sc - m)
            l = l + jnp.sum(pc, axis=-1, keepdims=True)
            acc = acc + _dot(pc.astype(BF16), vc_ref[0])
        outs.append(acc / l)
    lane = lax.broadcasted_iota(jnp.int32, outs[0].shape, 1)
    o_ref[...] = jnp.where(lane < V_DIM, outs[0], outs[1]).astype(BF16)


def attention(q, k, v2, row0, n_b, seq, tq, kc=None, vc2=None, ctx_len=0):
    assert row0 % seq == 0 and seq % tq == 0
    nq = seq // tq
    qb0 = row0 // tq
    kb0 = row0 // seq
    with_ctx = kc is not None
    in_specs = [pl.BlockSpec((2, tq, HEAD_PAD), lambda b, hp, qi: (hp, qb0 + b * nq + qi, 0)),
                pl.BlockSpec((2, seq, HEAD_PAD), lambda b, hp, qi: (hp, kb0 + b, 0)),
                pl.BlockSpec((1, seq, LANES), lambda b, hp, qi: (hp, kb0 + b, 0))]
    args = [q, k, v2]
    if with_ctx:
        in_specs += [pl.BlockSpec((2, ctx_len, HEAD_PAD), lambda b, hp, qi: (hp, b, 0)),
                     pl.BlockSpec((1, ctx_len, LANES), lambda b, hp, qi: (hp, b, 0))]
        args += [kc, vc2]
    return pl.pallas_call(
        functools.partial(_attn_kernel, with_ctx=with_ctx),
        out_shape=jax.ShapeDtypeStruct((n_b * seq, (N_HEADS // 2) * LANES), BF16),
        grid=(n_b, N_HEADS // 2, nq),
        in_specs=in_specs,
        out_specs=pl.BlockSpec((tq, LANES), lambda b, hp, qi: (b * nq + qi, hp)),
        compiler_params=_cparams(("parallel", "parallel", "parallel")),
        name="attn_ctx" if with_ctx else "attn",
    )(*args)


def _proj_res_kernel(*refs, lay, n_y, gate_row):
    y_refs = refs[:n_y]
    mod_ref, o_ref, w_ref, out_ref = refs[n_y:]
    g = mod_ref[...][gate_row:gate_row + 1]
    out_ref[...] = lay.stream_load(y_refs) + g * _dot(o_ref[...], w_ref[...])


def proj_residual(lay, y, mod, o, w, gate_row):
    ys = _as_list(y)
    t, d = lay.t, ys[0].shape[1]
    tm = lay.tm
    row = lambda i: (i, 0)
    return pl.pallas_call(
        functools.partial(_proj_res_kernel, lay=lay, n_y=len(ys), gate_row=gate_row),
        out_shape=jax.ShapeDtypeStruct((t, d), F32),
        grid=(lay.n_tiles,),
        in_specs=lay.stream_specs(d, len(ys) == 2) + [
            pl.BlockSpec((None, 8, d), lambda i: (lay.seg(i), 0, 0)),
            pl.BlockSpec((tm, o.shape[1]), row), pl.BlockSpec(w.shape, lambda i: (0, 0))],
        out_specs=pl.BlockSpec((tm, d), row),
        input_output_aliases={0: 0} if len(ys) == 1 else {},
        compiler_params=_cparams(("parallel",)),
        name="proj_res",
    )(*ys, mod, o, w)


def _ssm_in_kernel(y_ref, mod_ref, nw_ref, w_ref, u_ref, us_ref, ug_ref):
    m = mod_ref[...]
    h = _norm_mod(y_ref[...], nw_ref[...], m[0:1], m[1:2])
    z = _dot(h.astype(BF16), w_ref[...])
    n_lt = us_ref.shape[0]
    for j in range(n_lt):
        us_ref[j] = z[:, j * LANES:(j + 1) * LANES]
    n_ch = us_ref.shape[1] // SSM_CHUNK
    kk = SSM_GROUP
    gpt = LANES // kk
    for t in range(SSM_CHUNK):
        for j in range(n_lt):
            piece = us_ref[j, pl.ds(t, n_ch, stride=SSM_CHUNK), :]
            for gl in range(gpt):
                ug_ref[j * gpt + gl, :, t * kk:(t + 1) * kk] = piece[:, gl * kk:(gl + 1) * kk]
    u_ref[...] = ug_ref[...].astype(BF16)


def ssm_in(lay, y, mod, nw, w_in):
    t, d = y.shape
    tm = lay.tm
    n = w_in.shape[1]
    g = n // SSM_GROUP
    n_ch = tm // SSM_CHUNK
    qk = SSM_CHUNK * SSM_GROUP
    row = lambda i: (i, 0)
    return pl.pallas_call(
        _ssm_in_kernel,
        out_shape=jax.ShapeDtypeStruct((g, t // SSM_CHUNK, qk), BF16),
        grid=(lay.n_tiles,),
        in_specs=[pl.BlockSpec((tm, d), row), pl.BlockSpec((None, 8, d), lambda i: (lay.seg(i), 0, 0)),
                  pl.BlockSpec((1, d), lambda i: (0, 0)), pl.BlockSpec((d, n), lambda i: (0, 0))],
        out_specs=pl.BlockSpec((g, n_ch, qk), lambda i: (0, i, 0)),
        scratch_shapes=[pltpu.VMEM((n // LANES, tm, LANES), F32), pltpu.VMEM((g, n_ch, qk), F32)],
        compiler_params=_cparams(("parallel",)),
        name="ssm_in",
    )(y, mod, nw, w_in)


def _ssm_prep_kernel(ar_ref, ai_ref, ld_ref, arc_ref, aic_ref, ldc_ref, btr_ref, bti_ref, ctr_ref, cti_ref, dsk_ref,
                     mt_ref, ett_ref, ft_ref, coef_ref):
    q = SSM_CHUNK
    k = SSM_GROUP
    p = SSM_STATE
    qk = q * k
    fwd = pl.program_id(0) == 0
    dt = jnp.exp(ld_ref[...])
    are = ar_ref[...]
    aim = ai_ref[...]
    mag = jnp.exp(dt * are)
    abr = mag * jnp.cos(dt * aim)
    abi = mag * jnp.sin(dt * aim)
    den = are * are + aim * aim
    nr = abr - 1.0
    cf_re = (nr * are + abi * aim) / den
    cf_im = (abi * are - nr * aim) / den
    btr = btr_ref[...]
    bti = bti_ref[...]
    bbt_re = cf_re * btr - cf_im * bti
    bbt_im = cf_re * bti + cf_im * btr
    bbt_re_t = jnp.concatenate([bbt_re] * q, axis=0)
    bbt_im_t = jnp.concatenate([bbt_im] * q, axis=0)
    s_idx = (lax.broadcasted_iota(jnp.int32, (qk, 1), 0) // k).astype(F32)
    pw = jnp.where(fwd, (q - 1.0) - s_idx, s_idx)
    pm = jnp.exp(pw * dt * are)
    pr = pm * jnp.cos(pw * dt * aim)
    pi = pm * jnp.sin(pw * dt * aim)
    et_re = pr * bbt_re_t - pi * bbt_im_t
    et_im = pr * bbt_im_t + pi * bbt_re_t
    ett_ref[...] = jnp.concatenate([et_re, et_im, et_im, et_re], axis=1).astype(BF16)
    qm = jnp.exp(q * dt * are)
    aq_re = qm * jnp.cos(q * dt * aim)
    aq_im = qm * jnp.sin(q * dt * aim)
    c1 = jnp.concatenate([aq_re, aq_re], axis=1)
    c2 = jnp.concatenate([-aq_im, aq_im], axis=1)
    coef_ref[...] = jnp.concatenate([c1, c2, jnp.zeros((6, 2 * p), F32)], axis=0)
    dtc = jnp.exp(ldc_ref[...])
    arec = arc_ref[...]
    aimc = aic_ref[...]
    tau = (lax.broadcasted_iota(jnp.int32, (1, qk), 1) // k).astype(F32)
    ctr = ctr_ref[...]
    cti = cti_ref[...]
    tau1 = jnp.where(fwd, tau + 1.0, q - tau)
    m1 = jnp.exp(tau1 * dtc * arec)
    p1r = m1 * jnp.cos(tau1 * dtc * aimc)
    p1i = m1 * jnp.sin(tau1 * dtc * aimc)
    ft_ref[...] = jnp.concatenate([ctr * p1r - cti * p1i, -(ctr * p1i + cti * p1r)], axis=0).astype(BF16)
    tau0 = jnp.where(fwd, tau, (q - 1.0) - tau)
    m0 = jnp.exp(tau0 * dtc * arec)
    p0r = m0 * jnp.cos(tau0 * dtc * aimc)
    p0i = m0 * jnp.sin(tau0 * dtc * aimc)
    left_re = ctr * p0r - cti * p0i
    left_im = ctr * p0i + cti * p0r
    kt = _dot_hi(bbt_re, left_re) - _dot_hi(bbt_im, left_im)
    lane = lax.broadcasted_iota(jnp.int32, (k, qk), 1)
    rowi = lax.broadcasted_iota(jnp.int32, (k, qk), 0)
    kt_f = kt + jnp.where(lane == rowi, dsk_ref[...], 0.0)
    rows_f, rows_b = [], []
    for s in range(q):
        rows_f.append(kt_f if s == 0 else jnp.where(lane >= s * k, pltpu.roll(kt_f, s * k, 1), 0.0))
        sh = ((s + 1) * k) % qk
        rows_b.append(kt if sh == 0 else jnp.where(lane < (s + 1) * k, pltpu.roll(kt, sh, 1), 0.0))
    mt = jnp.where(fwd, jnp.concatenate(rows_f, axis=0), jnp.concatenate(rows_b, axis=0))
    mt_ref[...] = mt.astype(BF16)


def ssm_prep(a_re, a_im, log_dt, b_re, b_im, c_re, c_im, d_skip):
    nd, g, p = a_re.shape
    k = b_re.shape[-1]
    qk = SSM_CHUNK * k
    ld = jnp.broadcast_to(log_dt[..., None], (nd, g, p))
    dsk = jnp.broadcast_to(jnp.tile(d_skip.reshape(g, 1, k), (1, 1, SSM_CHUNK))[None], (nd, g, 1, qk))
    rowv = lambda x: x.reshape(nd, g, 1, p)
    colv = lambda x: x.reshape(nd, g, p, 1)
    bt = lambda x: jnp.swapaxes(x, -1, -2)
    ct = lambda x: jnp.tile(jnp.swapaxes(x, -1, -2), (1, 1, 1, SSM_CHUNK))
    spec = lambda r, c: pl.BlockSpec((None, None, r, c), lambda d, j: (d, j, 0, 0))
    return pl.pallas_call(
        _ssm_prep_kernel,
        out_shape=(jax.ShapeDtypeStruct((nd, g, qk, qk), BF16), jax.ShapeDtypeStruct((nd, g, qk, 4 * p), BF16),
                   jax.ShapeDtypeStruct((nd, g, 2 * p, qk), BF16), jax.ShapeDtypeStruct((nd, g, 8, 2 * p), F32)),
        grid=(nd, g),
        in_specs=[spec(1, p), spec(1, p), spec(1, p), spec(p, 1), spec(p, 1), spec(p, 1),
                  spec(k, p), spec(k, p), spec(p, qk), spec(p, qk), spec(1, qk)],
        out_specs=(spec(qk, qk), spec(qk, 4 * p), spec(2 * p, qk), spec(8, 2 * p)),
        compiler_params=_cparams(("parallel", "parallel")),
        name="ssm_prep",
    )(rowv(a_re), rowv(a_im), rowv(ld), colv(a_re), colv(a_im), colv(ld), bt(b_re), bt(b_im), ct(c_re), ct(c_im), dsk)


def _ssm_scan_kernel(u_ref, mt_ref, ett_ref, ft_ref, coef_ref, x0_ref, x0s_ref, y_ref, hfin_ref,
                     sx_ref, sxs_ref, hp_ref, *, gpb, ctx_b, ctx_chunks, lat_b, lat_chunks, rchunk):
    w = 2 * SSM_STATE
    n_rows = u_ref.shape[1]
    ctx_rows = ctx_b * ctx_chunks
    for d in range(2):
        for g in range(gpb):
            for r0 in range(0, n_rows, rchunk):
                ss = _dot(u_ref[g, r0:r0 + rchunk, :], ett_ref[d, g])
                sx_ref[g, r0:r0 + rchunk, :] = ss[:, :w]
                sxs_ref[g, r0:r0 + rchunk, :] = ss[:, w:]
        c1 = [coef_ref[d, g][0:1] for g in range(gpb)]
        c2 = [coef_ref[d, g][1:2] for g in range(gpb)]

        def make_body(base, nb, nc):
            def body(i, carry):
                c = i if d == 0 else nc - 1 - i
                rows = pl.ds(base + c, nb, stride=nc)
                out = []
                for g in range(gpb):
                    x, xs = carry[g]
                    hp_ref[g, rows, :] = x
                    xn = c1[g] * x + c2[g] * xs + sx_ref[g, rows, :]
                    xsn = c1[g] * xs - c2[g] * x + sxs_ref[g, rows, :]
                    out.append((xn, xsn))
                return tuple(out)
            return body

        z = jnp.zeros((ctx_b, w), F32)
        fin = lax.fori_loop(0, ctx_chunks, make_body(0, ctx_b, ctx_chunks), tuple((z, z) for _ in range(gpb)))
        hfin_ref[d] = jnp.concatenate([fin[g][0] for g in range(gpb)], axis=1)
        init = tuple((x0_ref[d][:, g * w:(g + 1) * w], x0s_ref[d][:, g * w:(g + 1) * w]) for g in range(gpb))
        lax.fori_loop(0, lat_chunks, make_body(ctx_rows, lat_b, lat_chunks), init, unroll=4)
        for g in range(gpb):
            for r0 in range(0, n_rows, rchunk):
                y = _dot(u_ref[g, r0:r0 + rchunk, :], mt_ref[d, g])
                y = y + _dot(hp_ref[g, r0:r0 + rchunk, :].astype(BF16), ft_ref[d, g])
                if d == 0:
                    y_ref[g, r0:r0 + rchunk, :] = y
                else:
                    y_ref[g, r0:r0 + rchunk, :] += y


def ssm_scan(u, mt, ett, ft, coef, x0, x0s, ctx_b, ctx_chunks, lat_b, lat_chunks, gpb=4):
    g, n_rows, qk = u.shape
    w = 2 * SSM_STATE
    rchunk = math.gcd(n_rows, 512)
    assert n_rows == ctx_b * ctx_chunks + lat_b * lat_chunks
    ublk = pl.BlockSpec((gpb, n_rows, qk), lambda j: (j, 0, 0))
    blk = lambda r, c: pl.BlockSpec((2, gpb, r, c), lambda j: (0, j, 0, 0))
    vec = lambda r: pl.BlockSpec((2, r, gpb * w), lambda j: (0, 0, j))
    return pl.pallas_call(
        functools.partial(_ssm_scan_kernel, gpb=gpb, ctx_b=ctx_b, ctx_chunks=ctx_chunks, lat_b=lat_b,
                          lat_chunks=lat_chunks, rchunk=rchunk),
        out_shape=(jax.ShapeDtypeStruct((g, n_rows, qk), F32), jax.ShapeDtypeStruct((2, ctx_b, g * w), F32)),
        grid=(g // gpb,),
        in_specs=[ublk, blk(qk, qk), blk(qk, 2 * w), blk(w, qk), blk(8, w), vec(lat_b), vec(lat_b)],
        out_specs=(ublk, vec(ctx_b)),
        scratch_shapes=[pltpu.VMEM((gpb, n_rows, w), F32), pltpu.VMEM((gpb, n_rows, w), F32),
                        pltpu.VMEM((gpb, n_rows, w), F32)],
        compiler_params=_cparams(("parallel",)),
        name="ssm_scan",
    )(u, mt, ett, ft, coef, x0, x0s)


def _ssm_out_kernel(y_ref, mod_ref, yc_ref, w_ref, out_ref, s_ref, *, d_model):
    kk = SSM_GROUP
    n_ch = yc_ref.shape[1]
    gpt = LANES // kk
    n_lt = s_ref.shape[0]
    for t in range(SSM_CHUNK):
        for j in range(n_lt):
            piece = jnp.concatenate([yc_ref[j * gpt + gl, :, t * kk:(t + 1) * kk] for gl in range(gpt)], axis=1)
            s_ref[j, pl.ds(t, n_ch, stride=SSM_CHUNK), :] = piece
    s = jnp.concatenate([s_ref[j] for j in range(n_lt)], axis=1)
    a = jax.nn.gelu(s, approximate=True).astype(BF16)
    z = _dot(a, w_ref[...])
    gate = mod_ref[...][2:3]
    out_ref[...] = y_ref[...] + gate * (z[:, :d_model] * jax.nn.sigmoid(z[:, d_model:]))


def ssm_out(lay, y, mod, ych, w_glu):
    t, d = y.shape
    tm = lay.tm
    g, _, qk = ych.shape
    n_ch = tm // SSM_CHUNK
    row = lambda i: (i, 0)
    return pl.pallas_call(
        functools.partial(_ssm_out_kernel, d_model=d),
        out_shape=jax.ShapeDtypeStruct((t, d), F32),
        grid=(lay.n_tiles,),
        in_specs=[pl.BlockSpec((tm, d), row), pl.BlockSpec((None, 8, d), lambda i: (lay.seg(i), 0, 0)),
                  pl.BlockSpec((g, n_ch, qk), lambda i: (0, i, 0)),
                  pl.BlockSpec(w_glu.shape, lambda i: (0, 0))],
        out_specs=pl.BlockSpec((tm, d), row),
        scratch_shapes=[pltpu.VMEM((g * SSM_GROUP // LANES, tm, LANES), F32)],
        input_output_aliases={0: 0},
        compiler_params=_cparams(("parallel",)),
        name="ssm_out",
    )(y, mod, ych, w_glu)


def s5_layer(lay, y, mod, nw, state_re, state_im, w_in, a_re, a_im, log_dt, b_re, b_im, c_re, c_im, d_skip, w_glu):
    uch = ssm_in(lay, y, mod, nw, w_in.astype(BF16))
    g = a_re.shape[1]
    p = SSM_STATE
    mt, ett, ft, coef = ssm_prep(a_re, a_im, log_dt, b_re, b_im, c_re, c_im, d_skip)
    st = jnp.concatenate([state_re, state_im], axis=-1)
    sts = jnp.concatenate([state_im, state_re], axis=-1)
    x0 = jnp.transpose(st, (1, 0, 2, 3)).reshape(2, lay.n_lat_b, g * 2 * p)
    x0s = jnp.transpose(sts, (1, 0, 2, 3)).reshape(2, lay.n_lat_b, g * 2 * p)
    ych, hfin = ssm_scan(uch, mt, ett, ft, coef, x0, x0s, lay.n_ctx_b, lay.ctx_len // SSM_CHUNK,
                         lay.n_lat_b, lay.lat_len // SSM_CHUNK)
    y = ssm_out(lay, y, mod, ych, w_glu.astype(BF16))
    hf = hfin.reshape(2, lay.n_ctx_b, g, 2, p)
    new_re = jnp.transpose(hf[:, :, :, 0], (1, 0, 2, 3))
    new_im = jnp.transpose(hf[:, :, :, 1], (1, 0, 2, 3))
    return y, new_re, new_im


def _route_kernel(y_ref, mod_ref, nw_ref, wr_ref, br_ref, tri_ref, h_ref, ri_ref, pc_ref, *, n_exp):
    m = mod_ref[...]
    h = _norm_mod(y_ref[...], nw_ref[...], m[3:4], m[4:5])
    h_hi = h.astype(BF16)
    h_ref[...] = h_hi
    h_lo = (h - h_hi.astype(F32)).astype(BF16)
    wr = wr_ref[...]
    lt = _dot_nt(wr, h_hi)
    logits = lt[:n_exp] + lt[n_exp:] + _dot_nt(wr[:n_exp], h_lo)
    scores = jax.nn.sigmoid(logits)
    sel = scores + br_ref[...]
    epg = EXPERTS_PER_GROUP
    row = lambda x, e: x[e:e + 1, :]
    gscore = []
    for g in range(N_EXPERT_GROUPS):
        a, b, c, d = (row(sel, g * epg + j) for j in range(epg))
        m1, n1, m2, n2 = jnp.maximum(a, b), jnp.minimum(a, b), jnp.maximum(c, d), jnp.minimum(c, d)
        gscore.append(jnp.maximum(m1, m2) + jnp.maximum(jnp.minimum(m1, m2), jnp.maximum(n1, n2)))
    best = gscore[0]
    gi = jnp.zeros_like(best, dtype=jnp.int32)
    for g in range(1, N_EXPERT_GROUPS):
        better = gscore[g] > best
        gi = jnp.where(better, g, gi)
        best = jnp.where(better, gscore[g], best)

    def pick(x, j):
        out = row(x, j)
        for g in range(1, N_EXPERT_GROUPS):
            out = jnp.where(gi == g, row(x, g * epg + j), out)
        return out

    sv = [pick(sel, j) for j in range(epg)]
    cv = [pick(scores, j) for j in range(epg)]
    b1, i1, w1 = sv[0], jnp.zeros_like(gi), cv[0]
    for j in range(1, epg):
        better = sv[j] > b1
        i1 = jnp.where(better, j, i1)
        w1 = jnp.where(better, cv[j], w1)
        b1 = jnp.where(better, sv[j], b1)
    neg = jnp.full_like(b1, -jnp.inf)
    b2, i2, w2 = neg, jnp.zeros_like(gi), jnp.zeros_like(w1)
    for j in range(epg):
        better = (i1 != j) & (sv[j] > b2)
        i2 = jnp.where(better, j, i2)
        w2 = jnp.where(better, cv[j], w2)
        b2 = jnp.where(better, sv[j], b2)
    tot = w1 + w2
    e1 = gi * epg + i1
    e2 = gi * epg + i2
    eid = lax.broadcasted_iota(jnp.int32, logits.shape, 0)
    m1h = eid == e1
    m2h = eid == e2
    mc = jnp.where(m1h | m2h, 1.0, 0.0)
    pref = _dot(mc.astype(BF16), tri_ref[...])
    cnt = jnp.sum(mc, axis=1, keepdims=True)
    pc_al = jnp.ceil(cnt * (1.0 / SEG_ALIGN))
    pcb = jnp.broadcast_to(pc_al, (n_exp, LANES))
    er = lax.broadcasted_iota(jnp.int32, (n_exp, n_exp), 0)
    ec = lax.broadcasted_iota(jnp.int32, (n_exp, n_exp), 1)
    lower = jnp.where(ec < er, 1.0, 0.0).astype(BF16)
    seg = _dot(lower, pcb.astype(BF16))[:, 0:1] * SEG_ALIGN
    slot = seg + pref
    pos1 = jnp.sum(jnp.where(m1h, slot, 0.0), axis=0, keepdims=True)
    pos2 = jnp.sum(jnp.where(m2h, slot, 0.0), axis=0, keepdims=True)
    zero = jnp.zeros_like(pos1)
    ri_ref[...] = jnp.concatenate([pos1, pos2, w1 / tot, w2 / tot, zero, zero, zero, zero], axis=0)
    pc_ref[...] = pcb * SEG_ALIGN


def moe_route(lay, y, mod, nw, wr_t, br, tri):
    t, d = y.shape
    tm = lay.tm
    n_exp = br.shape[0]
    return pl.pallas_call(
        functools.partial(_route_kernel, n_exp=n_exp),
        out_shape=(jax.ShapeDtypeStruct((t, d), BF16), jax.ShapeDtypeStruct((8, t), F32),
                   jax.ShapeDtypeStruct((lay.n_tiles, n_exp, LANES), F32)),
        grid=(lay.n_tiles,),
        in_specs=[pl.BlockSpec((tm, d), lambda i: (i, 0)),
                  pl.BlockSpec((None, 8, d), lambda i: (lay.seg(i), 0, 0)),
                  pl.BlockSpec((1, d), lambda i: (0, 0)), pl.BlockSpec(wr_t.shape, lambda i: (0, 0)),
                  pl.BlockSpec((n_exp, 1), lambda i: (0, 0)), pl.BlockSpec((tm, tm), lambda i: (0, 0))],
        out_specs=(pl.BlockSpec((tm, d), lambda i: (i, 0)), pl.BlockSpec((8, tm), lambda i: (0, i)),
                   pl.BlockSpec((None, n_exp, LANES), lambda i: (i, 0, 0))),
        compiler_params=_cparams(("parallel",)),
        name="moe_route",
    )(y, mod, nw, wr_t, br, tri)


def _segment_pieces(i, off_ref, pc_ref, n_exp, sizes):
    local = 0
    for e in range(n_exp):
        n = pc_ref[i * n_exp + e]
        g0 = off_ref[i * n_exp + e]
        for size in sizes:
            above = n & (-2 * size)
            yield (e, (n & size) != 0, pl.multiple_of(local + above, SEG_ALIGN),
                   pl.multiple_of(g0 + above, SEG_ALIGN), size)
        local = local + n


def _pieces(n, sizes):
    for size in sizes:
        yield (n & size) != 0, n & (-2 * size), size


def _moe_permute_kernel(off_ref, pc_ref, goff_ref, gn_ref, h_ref, ri_ref, xs_ref, ys_ref, zp_ref, sem, gsem,
                        *, n_exp, sizes, gap_sizes, rchunk):
    i = pl.program_id(0)
    last = pl.num_programs(0) - 1
    slot = i % 2
    ri = ri_ref[...]
    pos1 = ri[0:1]
    pos2 = ri[1:2]
    h = h_ref[...]
    n_loc = ys_ref.shape[1]
    for r0 in range(0, n_loc, rchunk):
        r = (lax.broadcasted_iota(jnp.int32, (rchunk, h.shape[0]), 0) + r0).astype(F32)
        p = jnp.where(pos1 == r, 1.0, 0.0) + jnp.where(pos2 == r, 1.0, 0.0)
        ys_ref[slot, r0:r0 + rchunk, :] = _dot(p.astype(BF16), h).astype(BF16)

    def copies(tile, sl):
        for e, present, lrow, grow, size in _segment_pieces(tile, off_ref, pc_ref, n_exp, sizes):
            yield present, pltpu.make_async_copy(ys_ref.at[sl, pl.ds(lrow, size)], xs_ref.at[pl.ds(grow, size)],
                                                 sem.at[sl, e])

    for present, cp in copies(i, slot):
        pl.when(present)(cp.start)

    @pl.when(i > 0)
    def _():
        for present, cp in copies(i - 1, 1 - slot):
            pl.when(present)(cp.wait)

    @pl.when(i == last)
    def _():
        for present, cp in copies(i, slot):
            pl.when(present)(cp.wait)
        zp_ref[...] = jnp.zeros_like(zp_ref)

        def gaps():
            for e in range(n_exp):
                g0 = goff_ref[e]
                for present, above, size in _pieces(gn_ref[e], gap_sizes):
                    yield present, pltpu.make_async_copy(
                        zp_ref.at[pl.ds(0, size)], xs_ref.at[pl.ds(pl.multiple_of(g0 + above, SEG_ALIGN), size)],
                        gsem.at[e])

        for present, cp in gaps():
            pl.when(present)(cp.start)
        for present, cp in gaps():
            pl.when(present)(cp.wait)

        zrows = zp_ref.shape[0]
        tail0 = goff_ref[n_exp]
        tail = lambda c: pltpu.make_async_copy(
            zp_ref, xs_ref.at[pl.ds(pl.multiple_of(tail0 + c * zrows, zrows), zrows)], gsem.at[0])

        @pl.loop(0, gn_ref[n_exp])
        def _(c):
            tail(c).start()

        @pl.loop(0, gn_ref[n_exp])
        def _(c):
            tail(c).wait()


def moe_permute(lay, h, rinfo, off, pcs, gap_off, gap_n, n_sorted, n_loc, n_exp, tmm):
    t, d = h.shape
    tm = lay.tm
    sizes = [s for s in (1 << b for b in range(12, 3, -1)) if s <= tm]
    gap_sizes = [s for s in (1 << b for b in range(12, 3, -1)) if s < tmm]
    return pl.pallas_call(
        functools.partial(_moe_permute_kernel, n_exp=n_exp, sizes=sizes, gap_sizes=gap_sizes, rchunk=256),
        out_shape=jax.ShapeDtypeStruct((n_sorted, d), BF16),
        grid_spec=pltpu.PrefetchScalarGridSpec(
            num_scalar_prefetch=4, grid=(lay.n_tiles,),
            in_specs=[pl.BlockSpec((tm, d), lambda i, *_: (i, 0)), pl.BlockSpec((8, tm), lambda i, *_: (0, i))],
            out_specs=pl.BlockSpec(memory_space=pl.ANY),
            scratch_shapes=[pltpu.VMEM((2, n_loc, d), BF16), pltpu.VMEM((gap_sizes[0], d), BF16),
                            pltpu.SemaphoreType.DMA((2, n_exp)), pltpu.SemaphoreType.DMA((n_exp,))]),
        compiler_params=_cparams(("arbitrary",)),
        name="moe_permute",
    )(off, pcs, gap_off, gap_n, h, rinfo)


def _moe_expert_kernel(te_ref, tv_ref, tf_ref, xb_ref, x_ref, wg_ref, wu_ref, wd_ref, z_ref, wgb_ref, wub_ref, wdb_ref):
    del te_ref, xb_ref
    r = pl.program_id(0)

    @pl.when(tf_ref[r] == 1)
    def _():
        wgb_ref[...] = wg_ref[...].astype(BF16)
        wub_ref[...] = wu_ref[...].astype(BF16)
        wdb_ref[...] = wd_ref[...].astype(BF16)

    @pl.when(tv_ref[r] > 0)
    def _():
        x = x_ref[...]
        a = _dot(x, wgb_ref[...])
        b = _dot(x, wub_ref[...])
        hid = (a * jax.nn.sigmoid(a)) * b
        z_ref[...] = _dot(hid.astype(BF16), wdb_ref[...]).astype(BF16)

    @pl.when(tv_ref[r] == 0)
    def _():
        z_ref[...] = jnp.zeros_like(z_ref)


def moe_experts(xs, tile_expert, tile_valid, tile_first, x_block, wg, wu, wd, tmm):
    n_sorted, d = xs.shape
    _, _, f = wg.shape
    wmap = lambda r, te, tv, tf, xb: (te[r], 0, 0)
    return pl.pallas_call(
        _moe_expert_kernel,
        out_shape=jax.ShapeDtypeStruct((n_sorted, d), BF16),
        grid_spec=pltpu.PrefetchScalarGridSpec(
            num_scalar_prefetch=4, grid=(n_sorted // tmm,),
            in_specs=[pl.BlockSpec((tmm, d), lambda r, te, tv, tf, xb: (xb[r], 0)),
                      pl.BlockSpec((None, d, f), wmap), pl.BlockSpec((None, d, f), wmap),
                      pl.BlockSpec((None, f, d), wmap)],
            out_specs=pl.BlockSpec((tmm, d), lambda r, te, tv, tf, xb: (r, 0)),
            scratch_shapes=[pltpu.VMEM((d, f), BF16), pltpu.VMEM((d, f), BF16), pltpu.VMEM((f, d), BF16)]),
        compiler_params=_cparams(("arbitrary",)),
        name="moe_experts",
    )(tile_expert, tile_valid, tile_first, x_block, xs, wg, wu, wd)


def _moe_combine_kernel(off_ref, pc_ref, y_ref, mod_ref, rit_ref, zs_ref, *rest, lay, n_exp, sizes):
    out_refs, (zt_ref, sem) = rest[:-2], rest[-2:]
    i = pl.program_id(0)
    slot = i % 2

    def copies(tile, sl):
        for e, present, lrow, grow, size in _segment_pieces(tile, off_ref, pc_ref, n_exp, sizes):
            yield present, pltpu.make_async_copy(zs_ref.at[pl.ds(grow, size)], zt_ref.at[sl, pl.ds(lrow, size)],
                                                 sem.at[sl, e])

    def fetch(tile, sl):
        zt_ref[sl] = jnp.zeros(zt_ref.shape[1:], BF16)
        for present, cp in copies(tile, sl):
            pl.when(present)(cp.start)

    @pl.when(i == 0)
    def _():
        fetch(i, slot)

    @pl.when(i + 1 < pl.num_programs(0))
    def _():
        fetch(i + 1, 1 - slot)

    rit = rit_ref[...]
    n_loc = zt_ref.shape[1]
    r = lax.broadcasted_iota(jnp.int32, (rit.shape[0], n_loc), 1).astype(F32)
    pw = jnp.where(rit[:, 0:1] == r, rit[:, 2:3], 0.0) + jnp.where(rit[:, 1:2] == r, rit[:, 3:4], 0.0)
    pw = pw.astype(BF16)
    for present, cp in copies(i, slot):
        pl.when(present)(cp.wait)
    lay.stream_store(out_refs, y_ref[...] + mod_ref[...][5:6] * _dot(pw, zt_ref[slot]))


def moe_combine(lay, y, mod, rinfo_t, zs, off, pcs, n_loc, n_exp, pair_out):
    t, d = y.shape
    tm = lay.tm
    sizes = [s for s in (1 << b for b in range(12, 3, -1)) if s <= tm]
    if pair_out:
        out_shape = (jax.ShapeDtypeStruct((lay.t_ctx, d), F32), jax.ShapeDtypeStruct((lay.t_lat, d), F32))
    else:
        out_shape = (jax.ShapeDtypeStruct((t, d), F32),)
    out = pl.pallas_call(
        functools.partial(_moe_combine_kernel, lay=lay, n_exp=n_exp, sizes=sizes),
        out_shape=out_shape,
        grid_spec=pltpu.PrefetchScalarGridSpec(
            num_scalar_prefetch=2, grid=(lay.n_tiles,),
            in_specs=[pl.BlockSpec((tm, d), lambda i, o, p: (i, 0)),
                      pl.BlockSpec((None, 8, d), lambda i, o, p: (lay.seg(i), 0, 0)),
                      pl.BlockSpec((tm, 8), lambda i, o, p: (i, 0)),
                      pl.BlockSpec(memory_space=pl.ANY)],
            out_specs=tuple(lay.stream_specs(d, pair_out)),
            scratch_shapes=[pltpu.VMEM((2, n_loc, d), BF16), pltpu.SemaphoreType.DMA((2, n_exp))]),
        input_output_aliases={} if pair_out else {2: 0},
        compiler_params=_cparams(("arbitrary",)),
        name="moe_combine",
    )(off, pcs, y, mod, rinfo_t, zs)
    return out if pair_out else out[0]


def _round_up(x, m):
    return (x + m - 1) // m * m


def moe_layer(lay, y, mod, nw, wr_t, br, tri, wg, wu, wd, tmm, pair_out=False):
    t, d = y.shape
    n_exp = br.shape[0]
    h, rinfo, pc = moe_route(lay, y, mod, nw, wr_t, br, tri)
    pc = pc[:, :, 0].astype(jnp.int32)
    n_loc = 2 * lay.tm + SEG_ALIGN * n_exp
    n_sorted = _round_up(2 * t + lay.n_tiles * n_exp * (SEG_ALIGN - 1) + n_exp * (tmm - 1), tmm)
    tot = jnp.sum(pc, axis=0)
    region = _round_up(tot, tmm)
    ends = jnp.cumsum(region)
    base = ends - region
    off = (base[None, :] + jnp.cumsum(pc, axis=0) - pc).reshape(-1)
    starts = jnp.arange(n_sorted // tmm, dtype=jnp.int32) * tmm
    te = jnp.minimum(jnp.sum(starts[:, None] >= ends[None, :], axis=1), n_exp - 1).astype(jnp.int32)
    tv = jnp.clip(tot[te] - (starts - base[te]), 0, tmm).astype(jnp.int32)
    tf = jnp.concatenate([jnp.ones((1,), jnp.int32), (te[1:] != te[:-1]).astype(jnp.int32)])
    xb = jnp.minimum(starts // tmm, jnp.maximum(ends[-1] // tmm - 1, 0)).astype(jnp.int32)
    pcs = pc.reshape(-1)
    assert tmm & (tmm - 1) == 0
    zrows = tmm // 2
    gap_off = jnp.concatenate([base + tot, ends[-1:]]).astype(jnp.int32)
    gap_n = jnp.concatenate([region - tot, (n_sorted - ends[-1:]) // zrows]).astype(jnp.int32)
    xs = moe_permute(lay, h, rinfo, off, pcs, gap_off, gap_n, n_sorted, n_loc, n_exp, tmm)
    zs = moe_experts(xs, te, tv, tf, xb, wg, wu, wd, tmm)
    return moe_combine(lay, y, mod, rinfo.T, zs, off, pcs, n_loc, n_exp, pair_out)


def _rope_perm():
    d = np.arange(QK_ROPE)
    return QK_NOPE + (d ^ (QK_ROPE // 4))


def _pad_cols(x, n):
    return jnp.pad(x, ((0, 0),) * (x.ndim - 1) + ((0, n - x.shape[-1]),))


def _mla_weights(w_q_a, w_q_b, w_kv_a, w_kv_b, q_norm, k_norm, kv_lora):
    perm = _rope_perm()
    nh = N_HEADS
    wkr = w_kv_a[:, kv_lora:]
    d = w_kv_a.shape[0]
    zl = jnp.zeros((d, QK_NOPE), F32)
    kr_blk = _pad_cols(jnp.concatenate([zl, wkr], axis=1), LANES)
    krp_blk = _pad_cols(jnp.concatenate([zl, wkr[:, perm - QK_NOPE]], axis=1), LANES)
    w_a = jnp.concatenate([w_q_a, w_kv_a[:, :kv_lora], kr_blk, krp_blk], axis=1).astype(BF16)
    wq = w_q_b.reshape(-1, nh, QK_DIM).transpose(1, 0, 2)
    wq_main = _pad_cols(wq, HEAD_PAD)
    wq_part = _pad_cols(jnp.concatenate([jnp.zeros_like(wq[..., :QK_NOPE]), wq[..., perm]], axis=-1), HEAD_PAD)
    wq_h = jnp.concatenate([wq_main, wq_part], axis=-1).astype(BF16)
    wkv = w_kv_b.reshape(-1, nh, QK_NOPE + V_DIM).transpose(1, 0, 2)
    wk_h = _pad_cols(wkv[..., :QK_NOPE], HEAD_PAD).astype(BF16)
    wv = wkv[..., QK_NOPE:]
    wv_h = jnp.concatenate([wv[0::2], wv[1::2]], axis=-1).astype(BF16)
    nrm = jnp.stack([_pad_cols(q_norm, LANES), _pad_cols(jnp.concatenate([jnp.zeros((QK_NOPE,), F32), q_norm[perm]]), LANES),
                     _pad_cols(k_norm, LANES), _pad_cols(jnp.concatenate([jnp.zeros((QK_NOPE,), F32), k_norm[perm]]), LANES)])
    nrm = jnp.pad(nrm, ((0, 4), (0, 0)))
    return w_a, wq_h, wk_h, wv_h, nrm


def _rope_tables(lat_len, tm):
    rows = lat_len // GRID_W
    t_row = np.repeat(np.arange(rows, dtype=np.float32), GRID_W)
    t_col = np.tile(np.arange(GRID_W, dtype=np.float32), rows)
    half = QK_ROPE // 2
    inv = (ROPE_BASE ** (-np.arange(0, half, 2, dtype=np.float32) / half)).astype(np.float32)
    ang = jnp.concatenate([jnp.asarray(t_row)[:, None] * inv, jnp.asarray(t_col)[:, None] * inv], axis=-1)
    cos, sin = jnp.cos(ang), jnp.sin(ang)
    nf = QK_ROPE // 4
    cos_l = jnp.concatenate([cos[:, :nf], cos[:, :nf], cos[:, nf:], cos[:, nf:]], axis=-1)
    sin_l = jnp.concatenate([-sin[:, :nf], sin[:, :nf], -sin[:, nf:], sin[:, nf:]], axis=-1)
    one = jnp.ones((lat_len, QK_NOPE), F32)
    cos_t = _pad_cols(jnp.concatenate([one, cos_l], axis=-1), LANES)
    cos_t = cos_t.at[:, QK_DIM:].set(1.0)
    sin_t = _pad_cols(jnp.concatenate([jnp.zeros((lat_len, QK_NOPE), F32), sin_l], axis=-1), LANES)
    cos_t = jnp.concatenate([jnp.ones((tm, LANES), F32), cos_t], axis=0)
    sin_t = jnp.concatenate([jnp.zeros((tm, LANES), F32), sin_t], axis=0)
    return cos_t, sin_t


def mla_layer(lay, y, mod, nw, cache_ckv, cache_krope, cos_t, sin_t, w_q_a, q_a_norm, w_q_b, w_kv_a, kv_a_norm,
              w_kv_b, q_norm, k_norm, w_o):
    q_lora = w_q_a.shape[1]
    kv_lora = kv_a_norm.shape[0]
    w_a, wq_h, wk_h, wv_h, nrm = _mla_weights(w_q_a, w_q_b, w_kv_a, w_kv_b, q_norm, k_norm, kv_lora)
    cq, ckv, kr, krp = mla_a(lay, y, mod, nw, w_a, q_a_norm.reshape(1, -1), kv_a_norm.reshape(1, -1), q_lora, kv_lora)
    q, k, v2 = mla_b(lay, cq, ckv, kr, krp, wq_h, wk_h, wv_h, nrm, cos_t, sin_t)
    past = cache_ckv.shape[1]
    ckr = jnp.pad(cache_krope.reshape(-1, QK_ROPE), ((0, 0), (QK_NOPE, LANES - QK_DIM)))
    kc, vc2 = ctx_kv(cache_ckv.reshape(-1, kv_lora), ckr, wk_h, wv_h, nrm, past)
    o_ctx = attention(q, k, v2, 0, lay.n_ctx_b, lay.ctx_len, min(lay.ctx_len, 256))
    o_lat = attention(q, k, v2, lay.t_ctx, lay.n_lat_b, lay.lat_len, min(lay.lat_len, 256), kc, vc2, past)
    o = jnp.concatenate([o_ctx, o_lat], axis=0)
    y = proj_residual(lay, y, mod, o, w_o.astype(BF16), 2)
    new_ckv = ckv[:lay.t_ctx].reshape(lay.n_ctx_b, lay.ctx_len, kv_lora)
    new_kr = kr[:lay.t_ctx, QK_NOPE:QK_DIM].reshape(lay.n_ctx_b, lay.ctx_len, QK_ROPE)
    return y, new_ckv, new_kr


def kernel(x_prompt, x_sample, c, cache_ckv, cache_krope, state_ssm_re, state_ssm_im, c_ctx, w_mod, b_mod, norm1_w, norm2_w, mla_w_q_a, mla_q_a_norm, mla_w_q_b, mla_w_kv_a, mla_kv_a_norm, mla_w_kv_b, mla_q_norm, mla_k_norm, mla_w_o, ssm_w_in, ssm_a_re, ssm_a_im, ssm_log_dt, ssm_b_re, ssm_b_im, ssm_c_re, ssm_c_im, ssm_d, ssm_w_glu, w_router, b_router, moe_w_gate, moe_w_up, moe_w_down):
    n_ctx_b, ctx_len, d = x_prompt.shape
    n_lat_b, lat_len, _ = x_sample.shape
    depth = w_mod.shape[0]
    lay = Layout(n_ctx_b, ctx_len, n_lat_b, lat_len, tm=256)
    assert n_lat_b + 1 <= 8
    y = (x_prompt.reshape(-1, d), x_sample.reshape(-1, d))
    cond8 = jnp.pad(jnp.concatenate([c_ctx[None, :], c], axis=0), ((0, 7 - n_lat_b), (0, 0)))
    mods = adaln_all(cond8, w_mod, b_mod)
    mods = jnp.pad(mods.reshape(depth, 8, 6, d), ((0, 0), (0, 0), (0, 2), (0, 0)))
    cos_t, sin_t = _rope_tables(lat_len, lay.tm)
    n_exp = b_router.shape[0]
    wr_hi = w_router.astype(BF16)
    wr_lo = (w_router - wr_hi.astype(F32)).astype(BF16)
    wr_t = jnp.concatenate([wr_hi.T, wr_lo.T], axis=0)
    br = b_router.reshape(n_exp, 1)
    lay_s = Layout(n_ctx_b, ctx_len, n_lat_b, lat_len, tm=512)
    tri = jnp.asarray(np.triu(np.ones((lay_s.tm, lay_s.tm), np.float32), k=1), BF16)
    ckv_out, kr_out, sre_out, sim_out = [], [], [], []
    for i in range(depth):
        j = i // 2
        mod = mods[i]
        nw1 = norm1_w[i].reshape(1, d)
        if i % 2 == 0:
            y, ckv_p, kr_p = mla_layer(lay, y, mod, nw1, cache_ckv[:, j], cache_krope[:, j], cos_t, sin_t,
                                       mla_w_q_a[j], mla_q_a_norm[j], mla_w_q_b[j], mla_w_kv_a[j], mla_kv_a_norm[j],
                                       mla_w_kv_b[j], mla_q_norm[j], mla_k_norm[j], mla_w_o[j])
            ckv_out.append(ckv_p)
            kr_out.append(kr_p)
        else:
            y, s_re, s_im = s5_layer(lay, y, mod, nw1, state_ssm_re[:, j], state_ssm_im[:, j], ssm_w_in[j],
                                     ssm_a_re[j], ssm_a_im[j], ssm_log_dt[j], ssm_b_re[j], ssm_b_im[j],
                                     ssm_c_re[j], ssm_c_im[j], ssm_d[j], ssm_w_glu[j])
            sre_out.append(s_re)
            sim_out.append(s_im)
        y = moe_layer(lay_s, y, mod, norm2_w[i].reshape(1, d), wr_t, br, tri, moe_w_gate[i], moe_w_up[i],
                      moe_w_down[i], tmm=512, pair_out=(i == depth - 1))
    yp = y[0].reshape(n_ctx_b, ctx_len, d)
    ys = y[1].reshape(n_lat_b, lat_len, d)
    return (yp, ys, jnp.stack(ckv_out, axis=1), jnp.stack(kr_out, axis=1),
            jnp.stack(sre_out, axis=1), jnp.stack(sim_out, axis=1))
```

```python
import functools
import math

import jax
import jax.numpy as jnp
import numpy as np
from jax import lax
from jax.experimental import pallas as pl
from jax.experimental.pallas import tpu as pltpu

F32 = jnp.float32
BF16 = jnp.bfloat16
EPS = 1e-6

GRID_W = 64
N_HEADS = 8
QK_NOPE = 64
QK_ROPE = 32
QK_DIM = QK_NOPE + QK_ROPE
V_DIM = 64
ROPE_BASE = 10000.0
SSM_GROUP = 16
SSM_STATE = 64
N_EXPERT_GROUPS = 4
EXPERTS_PER_GROUP = 4

LANES = 128
HEAD_PAD = LANES
SSM_CHUNK = 16
SEG_ALIGN = 16
VMEM_LIMIT = 48 * 1024 * 1024


def _cparams(sem, vmem=VMEM_LIMIT):
    return pltpu.CompilerParams(dimension_semantics=sem, vmem_limit_bytes=vmem)


def _dot(a, b):
    return jnp.dot(a, b, preferred_element_type=F32)


def _dot_nt(a, b):
    return lax.dot_general(a, b, (((1,), (1,)), ((), ())), preferred_element_type=F32)


def _dot_hi(a, b):
    return jnp.dot(a, b, preferred_element_type=F32, precision=lax.Precision.HIGHEST)


def _norm_mod(x, nw, shift, scale):
    ms = jnp.mean(x * x, axis=-1, keepdims=True)
    return (x * lax.rsqrt(ms + EPS) * nw) * (1.0 + scale) + shift


class Layout:
    def __init__(self, n_ctx_b, ctx_len, n_lat_b, lat_len, tm):
        self.n_ctx_b, self.ctx_len, self.n_lat_b, self.lat_len = n_ctx_b, ctx_len, n_lat_b, lat_len
        self.t_ctx = n_ctx_b * ctx_len
        self.t_lat = n_lat_b * lat_len
        self.t = self.t_ctx + self.t_lat
        self.tm = tm
        assert self.t_ctx % tm == 0 and lat_len % tm == 0
        self.ctx_tiles = self.t_ctx // tm
        self.lat_tiles_per_b = lat_len // tm
        self.n_tiles = self.t // tm

    def seg(self, i):
        return jnp.where(i < self.ctx_tiles, 0, 1 + (i - self.ctx_tiles) // self.lat_tiles_per_b)

    def rope_blk(self, i):
        return jnp.where(i < self.ctx_tiles, 0, 1 + (i - self.ctx_tiles) % self.lat_tiles_per_b)

    def stream_specs(self, d, pair):
        tm = self.tm
        if not pair:
            return [pl.BlockSpec((tm, d), lambda i, *_: (i, 0))]
        return [pl.BlockSpec((tm, d), lambda i, *_: (jnp.minimum(i, self.ctx_tiles - 1), 0)),
                pl.BlockSpec((tm, d), lambda i, *_: (jnp.maximum(i - self.ctx_tiles, 0), 0))]

    def stream_load(self, refs):
        if len(refs) == 1:
            return refs[0][...]
        return jnp.where(pl.program_id(0) < self.ctx_tiles, refs[0][...], refs[1][...])

    def stream_store(self, refs, val):
        if len(refs) == 1:
            refs[0][...] = val
            return
        i = pl.program_id(0)

        @pl.when(i < self.ctx_tiles)
        def _():
            refs[0][...] = val

        @pl.when(i >= self.ctx_tiles)
        def _():
            refs[1][...] = val


def _as_list(y):
    return list(y) if isinstance(y, (tuple, list)) else [y]


def _adaln_kernel(c_ref, w_ref, b_ref, o_ref):
    c = c_ref[...]
    s = c * jax.nn.sigmoid(c)
    o_ref[...] = _dot_hi(s, w_ref[...]) + b_ref[...]


def adaln_all(cond8, w_mod, b_mod, tn=1536):
    depth, d, n6 = w_mod.shape
    return pl.pallas_call(
        _adaln_kernel,
        out_shape=jax.ShapeDtypeStruct((depth, 8, n6), F32),
        grid=(depth, n6 // tn),
        in_specs=[pl.BlockSpec((8, d), lambda l, j: (0, 0)),
                  pl.BlockSpec((None, d, tn), lambda l, j: (l, 0, j)),
                  pl.BlockSpec((None, 1, tn), lambda l, j: (l, 0, j))],
        out_specs=pl.BlockSpec((None, 8, tn), lambda l, j: (l, 0, j)),
        compiler_params=_cparams(("parallel", "parallel")),
        name="adaln",
    )(cond8, w_mod, b_mod.reshape(depth, 1, n6))


def _mla_a_kernel(*refs, lay, n_y, q_lora, kv_lora):
    y_refs = refs[:n_y]
    mod_ref, nw_ref, w_ref, qan_ref, kvan_ref, cq_ref, ckv_ref, kr_ref, krp_ref = refs[n_y:]
    m = mod_ref[...]
    h = _norm_mod(lay.stream_load(y_refs), nw_ref[...], m[0:1], m[1:2])
    z = _dot(h.astype(BF16), w_ref[...])
    cq = z[:, :q_lora]
    cq = cq * lax.rsqrt(jnp.mean(cq * cq, axis=-1, keepdims=True) + EPS) * qan_ref[...]
    cq_ref[...] = cq.astype(BF16)
    ckv = z[:, q_lora:q_lora + kv_lora]
    ckv_ref[...] = ckv * lax.rsqrt(jnp.mean(ckv * ckv, axis=-1, keepdims=True) + EPS) * kvan_ref[...]
    kr_ref[...] = z[:, q_lora + kv_lora:q_lora + kv_lora + LANES]
    krp_ref[...] = z[:, q_lora + kv_lora + LANES:]


def mla_a(lay, y, mod, nw, w_a, qan, kvan, q_lora, kv_lora):
    ys = _as_list(y)
    t, d = lay.t, ys[0].shape[1]
    tm = lay.tm
    na = w_a.shape[1]
    row = lambda i: (i, 0)
    full = lambda i: (0, 0)
    return pl.pallas_call(
        functools.partial(_mla_a_kernel, lay=lay, n_y=len(ys), q_lora=q_lora, kv_lora=kv_lora),
        out_shape=(jax.ShapeDtypeStruct((t, q_lora), BF16), jax.ShapeDtypeStruct((t, kv_lora), F32),
                   jax.ShapeDtypeStruct((t, LANES), F32), jax.ShapeDtypeStruct((t, LANES), F32)),
        grid=(lay.n_tiles,),
        in_specs=lay.stream_specs(d, len(ys) == 2) + [
            pl.BlockSpec((None, 8, d), lambda i: (lay.seg(i), 0, 0)),
            pl.BlockSpec((1, d), full), pl.BlockSpec((d, na), full),
            pl.BlockSpec((1, q_lora), full), pl.BlockSpec((1, kv_lora), full)],
        out_specs=(pl.BlockSpec((tm, q_lora), row), pl.BlockSpec((tm, kv_lora), row),
                   pl.BlockSpec((tm, LANES), row), pl.BlockSpec((tm, LANES), row)),
        compiler_params=_cparams(("parallel",)),
        name="mla_a",
    )(*ys, mod, nw, w_a, qan, kvan)


def _kv_heads(ckv_b, kr, krp, wk_ref, wv_ref, knw, knwp, cos, sin, k_ref, v_ref):
    kc = knw * cos
    ks = knwp * sin
    for h in range(N_HEADS):
        kz = _dot(ckv_b, wk_ref[h]) + kr
        r = lax.rsqrt(jnp.sum(kz * kz, axis=-1, keepdims=True) * (1.0 / QK_DIM) + EPS)
        k_ref[h] = (r * (kz * kc + krp * ks)).astype(BF16)
    for hp in range(N_HEADS // 2):
        v_ref[hp] = _dot(ckv_b, wv_ref[hp]).astype(BF16)


def _mla_b_kernel(cq_ref, ckv_ref, kr_ref, krp_ref, wq_ref, wk_ref, wv_ref, nrm_ref, cos_ref, sin_ref,
                  q_ref, k_ref, v_ref):
    nrm = nrm_ref[...]
    cos = cos_ref[...]
    sin = sin_ref[...]
    cq = cq_ref[...]
    qc = nrm[0:1] * cos * (QK_DIM ** -0.5)
    qs = nrm[1:2] * sin * (QK_DIM ** -0.5)
    for h in range(N_HEADS):
        z = _dot(cq, wq_ref[h])
        qm = z[:, :HEAD_PAD]
        qp = z[:, HEAD_PAD:]
        r = lax.rsqrt(jnp.sum(qm * qm, axis=-1, keepdims=True) * (1.0 / QK_DIM) + EPS)
        q_ref[h] = (r * (qm * qc + qp * qs)).astype(BF16)
    _kv_heads(ckv_ref[...].astype(BF16), kr_ref[...], krp_ref[...], wk_ref, wv_ref, nrm[2:3], nrm[3:4],
              cos, sin, k_ref, v_ref)


def mla_b(lay, cq, ckv, kr, krp, wq, wk, wv, nrm, cos_t, sin_t):
    t = cq.shape[0]
    tm = lay.tm
    row = lambda i: (i, 0)
    hrow = lambda i: (0, i, 0)
    full3 = lambda i: (0, 0, 0)
    rope = lambda i: (lay.rope_blk(i), 0)
    nh = N_HEADS
    return pl.pallas_call(
        _mla_b_kernel,
        out_shape=(jax.ShapeDtypeStruct((nh, t, HEAD_PAD), BF16), jax.ShapeDtypeStruct((nh, t, HEAD_PAD), BF16),
                   jax.ShapeDtypeStruct((nh // 2, t, LANES), BF16)),
        grid=(lay.n_tiles,),
        in_specs=[pl.BlockSpec((tm, cq.shape[1]), row), pl.BlockSpec((tm, ckv.shape[1]), row),
                  pl.BlockSpec((tm, LANES), row), pl.BlockSpec((tm, LANES), row),
                  pl.BlockSpec(wq.shape, full3), pl.BlockSpec(wk.shape, full3), pl.BlockSpec(wv.shape, full3),
                  pl.BlockSpec((8, LANES), lambda i: (0, 0)),
                  pl.BlockSpec((tm, LANES), rope), pl.BlockSpec((tm, LANES), rope)],
        out_specs=(pl.BlockSpec((nh, tm, HEAD_PAD), hrow), pl.BlockSpec((nh, tm, HEAD_PAD), hrow),
                   pl.BlockSpec((nh // 2, tm, LANES), hrow)),
        compiler_params=_cparams(("parallel",)),
        name="mla_b",
    )(cq, ckv, kr, krp, wq, wk, wv, nrm, cos_t, sin_t)


def _ctx_kv_kernel(ckv_ref, kr_ref, wk_ref, wv_ref, nrm_ref, k_ref, v_ref):
    nrm = nrm_ref[...]
    kr = kr_ref[...]
    one = jnp.ones((1, LANES), F32)
    _kv_heads(ckv_ref[...].astype(BF16), kr, kr, wk_ref, wv_ref, nrm[2:3], nrm[3:4],
              one, jnp.zeros((1, LANES), F32), k_ref, v_ref)


def ctx_kv(ckv, kr, wk, wv, nrm, tm):
    t = ckv.shape[0]
    row = lambda i: (i, 0)
    hrow = lambda i: (0, i, 0)
    full3 = lambda i: (0, 0, 0)
    nh = N_HEADS
    return pl.pallas_call(
        _ctx_kv_kernel,
        out_shape=(jax.ShapeDtypeStruct((nh, t, HEAD_PAD), BF16), jax.ShapeDtypeStruct((nh // 2, t, LANES), BF16)),
        grid=(t // tm,),
        in_specs=[pl.BlockSpec((tm, ckv.shape[1]), row), pl.BlockSpec((tm, LANES), row),
                  pl.BlockSpec(wk.shape, full3), pl.BlockSpec(wv.shape, full3),
                  pl.BlockSpec((8, LANES), lambda i: (0, 0))],
        out_specs=(pl.BlockSpec((nh, tm, HEAD_PAD), hrow), pl.BlockSpec((nh // 2, tm, LANES), hrow)),
        compiler_params=_cparams(("parallel",)),
        name="ctx_kv",
    )(ckv, kr, wk, wv, nrm)


def _attn_kernel(*refs, with_ctx, kblk):
    if with_ctx:
        q_ref, k_ref, v_ref, kc_ref, vc_ref, o_ref = refs
    else:
        q_ref, k_ref, v_ref, o_ref = refs
    seq = k_ref.shape[1]
    blocks = [(kc_ref, vc_ref, 0, kc_ref.shape[1])] if with_ctx else []
    blocks += [(k_ref, v_ref, b0, kblk) for b0 in range(0, seq, kblk)]
    items = [(j, blk) for j in range(2) for blk in blocks]

    def scores(item):
        j, (kr, _, b0, n) = item
        return _dot_nt(q_ref[j], kr[j, b0:b0 + n, :])

    outs = []
    m = l = acc = None
    s_next = scores(items[0])
    for idx, (j, (_, vr, b0, n)) in enumerate(items):
        s = s_next
        if idx + 1 < len(items):
            s_next = scores(items[idx + 1])
        mb = jnp.max(s, axis=-1, keepdims=True)
        m_new = mb if m is None else jnp.maximum(m, mb)
        p = jnp.exp(s - m_new)
        pv = _dot(p.astype(BF16), vr[0, b0:b0 + n, :])
        ps = jnp.sum(p, axis=-1, keepdims=True)
        if m is None:
            l, acc = ps, pv
        else:
            a = jnp.exp(m - m_new)
            l, acc = a * l + ps, a * acc + pv
        m = m_new
        if idx + 1 == len(items) or items[idx + 1][0] != j:
            outs.append(acc / l)
            m = l = acc = None
    lane = lax.broadcasted_iota(jnp.int32, outs[0].shape, 1)
    o_ref[...] = jnp.where(lane < V_DIM, outs[0], outs[1]).astype(BF16)


def attention(q, k, v2, row0, n_b, seq, tq, kc=None, vc2=None, ctx_len=0, kblk=1024):
    assert row0 % seq == 0 and seq % tq == 0
    nq = seq // tq
    qb0 = row0 // tq
    kb0 = row0 // seq
    with_ctx = kc is not None
    in_specs = [pl.BlockSpec((2, tq, HEAD_PAD), lambda b, hp, qi: (hp, qb0 + b * nq + qi, 0)),
                pl.BlockSpec((2, seq, HEAD_PAD), lambda b, hp, qi: (hp, kb0 + b, 0)),
                pl.BlockSpec((1, seq, LANES), lambda b, hp, qi: (hp, kb0 + b, 0))]
    args = [q, k, v2]
    if with_ctx:
        in_specs += [pl.BlockSpec((2, ctx_len, HEAD_PAD), lambda b, hp, qi: (hp, b, 0)),
                     pl.BlockSpec((1, ctx_len, LANES), lambda b, hp, qi: (hp, b, 0))]
        args += [kc, vc2]
    return pl.pallas_call(
        functools.partial(_attn_kernel, with_ctx=with_ctx, kblk=min(seq, kblk)),
        out_shape=jax.ShapeDtypeStruct((n_b * seq, (N_HEADS // 2) * LANES), BF16),
        grid=(n_b, N_HEADS // 2, nq),
        in_specs=in_specs,
        out_specs=pl.BlockSpec((tq, LANES), lambda b, hp, qi: (b * nq + qi, hp)),
        compiler_params=_cparams(("parallel", "parallel", "parallel")),
        name="attn_ctx" if with_ctx else "attn",
    )(*args)


def _proj_res_kernel(*refs, lay, n_y, gate_row):
    y_refs = refs[:n_y]
    mod_ref, o_ref, w_ref, out_ref = refs[n_y:]
    g = mod_ref[...][gate_row:gate_row + 1]
    out_ref[...] = lay.stream_load(y_refs) + g * _dot(o_ref[...], w_ref[...])


def proj_residual(lay, y, mod, o, w, gate_row):
    ys = _as_list(y)
    t, d = lay.t, ys[0].shape[1]
    tm = lay.tm
    row = lambda i: (i, 0)
    return pl.pallas_call(
        functools.partial(_proj_res_kernel, lay=lay, n_y=len(ys), gate_row=gate_row),
        out_shape=jax.ShapeDtypeStruct((t, d), F32),
        grid=(lay.n_tiles,),
        in_specs=lay.stream_specs(d, len(ys) == 2) + [
            pl.BlockSpec((None, 8, d), lambda i: (lay.seg(i), 0, 0)),
            pl.BlockSpec((tm, o.shape[1]), row), pl.BlockSpec(w.shape, lambda i: (0, 0))],
        out_specs=pl.BlockSpec((tm, d), row),
        input_output_aliases={0: 0} if len(ys) == 1 else {},
        compiler_params=_cparams(("parallel",)),
        name="proj_res",
    )(*ys, mod, o, w)


def _ssm_in_kernel(y_ref, mod_ref, nw_ref, w_ref, u_ref, us_ref, ug_ref):
    m = mod_ref[...]
    h = _norm_mod(y_ref[...], nw_ref[...], m[0:1], m[1:2])
    z = _dot(h.astype(BF16), w_ref[...])
    n_lt = us_ref.shape[0]
    for j in range(n_lt):
        us_ref[j] = z[:, j * LANES:(j + 1) * LANES]
    n_ch = us_ref.shape[1] // SSM_CHUNK
    kk = SSM_GROUP
    gpt = LANES // kk
    for t in range(SSM_CHUNK):
        for j in range(n_lt):
            piece = us_ref[j, pl.ds(t, n_ch, stride=SSM_CHUNK), :]
            for gl in range(gpt):
                ug_ref[j * gpt + gl, :, t * kk:(t + 1) * kk] = piece[:, gl * kk:(gl + 1) * kk]
    u_ref[...] = ug_ref[...].astype(BF16)


def ssm_in(lay, y, mod, nw, w_in):
    t, d = y.shape
    tm = lay.tm
    n = w_in.shape[1]
    g = n // SSM_GROUP
    n_ch = tm // SSM_CHUNK
    qk = SSM_CHUNK * SSM_GROUP
    row = lambda i: (i, 0)
    return pl.pallas_call(
        _ssm_in_kernel,
        out_shape=jax.ShapeDtypeStruct((g, t // SSM_CHUNK, qk), BF16),
        grid=(lay.n_tiles,),
        in_specs=[pl.BlockSpec((tm, d), row), pl.BlockSpec((None, 8, d), lambda i: (lay.seg(i), 0, 0)),
                  pl.BlockSpec((1, d), lambda i: (0, 0)), pl.BlockSpec((d, n), lambda i: (0, 0))],
        out_specs=pl.BlockSpec((g, n_ch, qk), lambda i: (0, i, 0)),
        scratch_shapes=[pltpu.VMEM((n // LANES, tm, LANES), F32), pltpu.VMEM((g, n_ch, qk), F32)],
        compiler_params=_cparams(("parallel",)),
        name="ssm_in",
    )(y, mod, nw, w_in)


def _ssm_prep_kernel(ar_ref, ai_ref, ld_ref, arc_ref, aic_ref, ldc_ref, btr_ref, bti_ref, ctr_ref, cti_ref, dsk_ref,
                     mt_ref, ett_ref, ft_ref, coef_ref):
    q = SSM_CHUNK
    k = SSM_GROUP
    p = SSM_STATE
    qk = q * k
    fwd = pl.program_id(0) == 0
    dt = jnp.exp(ld_ref[...])
    are = ar_ref[...]
    aim = ai_ref[...]
    mag = jnp.exp(dt * are)
    abr = mag * jnp.cos(dt * aim)
    abi = mag * jnp.sin(dt * aim)
    den = are * are + aim * aim
    nr = abr - 1.0
    cf_re = (nr * are + abi * aim) / den
    cf_im = (abi * are - nr * aim) / den
    btr = btr_ref[...]
    bti = bti_ref[...]
    bbt_re = cf_re * btr - cf_im * bti
    bbt_im = cf_re * bti + cf_im * btr
    bbt_re_t = jnp.concatenate([bbt_re] * q, axis=0)
    bbt_im_t = jnp.concatenate([bbt_im] * q, axis=0)
    s_idx = (lax.broadcasted_iota(jnp.int32, (qk, 1), 0) // k).astype(F32)
    pw = jnp.where(fwd, (q - 1.0) - s_idx, s_idx)
    pm = jnp.exp(pw * dt * are)
    pr = pm * jnp.cos(pw * dt * aim)
    pi = pm * jnp.sin(pw * dt * aim)
    et_re = pr * bbt_re_t - pi * bbt_im_t
    et_im = pr * bbt_im_t + pi * bbt_re_t
    ett_ref[...] = jnp.concatenate([et_re, et_im, et_im, et_re], axis=1).astype(BF16)
    qm = jnp.exp(q * dt * are)
    aq_re = qm * jnp.cos(q * dt * aim)
    aq_im = qm * jnp.sin(q * dt * aim)
    c1 = jnp.concatenate([aq_re, aq_re], axis=1)
    c2 = jnp.concatenate([-aq_im, aq_im], axis=1)
    coef_ref[...] = jnp.concatenate([c1, c2, jnp.zeros((6, 2 * p), F32)], axis=0)
    dtc = jnp.exp(ldc_ref[...])
    arec = arc_ref[...]
    aimc = aic_ref[...]
    tau = (lax.broadcasted_iota(jnp.int32, (1, qk), 1) // k).astype(F32)
    ctr = ctr_ref[...]
    cti = cti_ref[...]
    tau1 = jnp.where(fwd, tau + 1.0, q - tau)
    m1 = jnp.exp(tau1 * dtc * arec)
    p1r = m1 * jnp.cos(tau1 * dtc * aimc)
    p1i = m1 * jnp.sin(tau1 * dtc * aimc)
    ft_ref[...] = jnp.concatenate([ctr * p1r - cti * p1i, -(ctr * p1i + cti * p1r)], axis=0).astype(BF16)
    tau0 = jnp.where(fwd, tau, (q - 1.0) - tau)
    m0 = jnp.exp(tau0 * dtc * arec)
    p0r = m0 * jnp.cos(tau0 * dtc * aimc)
    p0i = m0 * jnp.sin(tau0 * dtc * aimc)
    left_re = ctr * p0r - cti * p0i
    left_im = ctr * p0i + cti * p0r
    kt = _dot_hi(bbt_re, left_re) - _dot_hi(bbt_im, left_im)
    lane = lax.broadcasted_iota(jnp.int32, (k, qk), 1)
    rowi = lax.broadcasted_iota(jnp.int32, (k, qk), 0)
    kt_f = kt + jnp.where(lane == rowi, dsk_ref[...], 0.0)
    rows_f, rows_b = [], []
    for s in range(q):
        rows_f.append(kt_f if s == 0 else jnp.where(lane >= s * k, pltpu.roll(kt_f, s * k, 1), 0.0))
        sh = ((s + 1) * k) % qk
        rows_b.append(kt if sh == 0 else jnp.where(lane < (s + 1) * k, pltpu.roll(kt, sh, 1), 0.0))
    mt = jnp.where(fwd, jnp.concatenate(rows_f, axis=0), jnp.concatenate(rows_b, axis=0))
    mt_ref[...] = mt.astype(BF16)


def ssm_prep(a_re, a_im, log_dt, b_re, b_im, c_re, c_im, d_skip):
    nd, g, p = a_re.shape
    k = b_re.shape[-1]
    qk = SSM_CHUNK * k
    ld = jnp.broadcast_to(log_dt[..., None], (nd, g, p))
    dsk = jnp.broadcast_to(jnp.tile(d_skip.reshape(g, 1, k), (1, 1, SSM_CHUNK))[None], (nd, g, 1, qk))
    rowv = lambda x: x.reshape(nd, g, 1, p)
    colv = lambda x: x.reshape(nd, g, p, 1)
    bt = lambda x: jnp.swapaxes(x, -1, -2)
    ct = lambda x: jnp.tile(jnp.swapaxes(x, -1, -2), (1, 1, 1, SSM_CHUNK))
    spec = lambda r, c: pl.BlockSpec((None, None, r, c), lambda d, j: (d, j, 0, 0))
    return pl.pallas_call(
        _ssm_prep_kernel,
        out_shape=(jax.ShapeDtypeStruct((nd, g, qk, qk), BF16), jax.ShapeDtypeStruct((nd, g, qk, 4 * p), BF16),
                   jax.ShapeDtypeStruct((nd, g, 2 * p, qk), BF16), jax.ShapeDtypeStruct((nd, g, 8, 2 * p), F32)),
        grid=(nd, g),
        in_specs=[spec(1, p), spec(1, p), spec(1, p), spec(p, 1), spec(p, 1), spec(p, 1),
                  spec(k, p), spec(k, p), spec(p, qk), spec(p, qk), spec(1, qk)],
        out_specs=(spec(qk, qk), spec(qk, 4 * p), spec(2 * p, qk), spec(8, 2 * p)),
        compiler_params=_cparams(("parallel", "parallel")),
        name="ssm_prep",
    )(rowv(a_re), rowv(a_im), rowv(ld), colv(a_re), colv(a_im), colv(ld), bt(b_re), bt(b_im), ct(c_re), ct(c_im), dsk)


def _ssm_scan_kernel(u_ref, mt_ref, ett_ref, ft_ref, coef_ref, x0_ref, x0s_ref, y_ref, hfin_ref,
                     sx_ref, sxs_ref, hp_ref, *, gpb, ctx_b, ctx_chunks, lat_b, lat_chunks, rchunk):
    w = 2 * SSM_STATE
    n_rows = u_ref.shape[1]
    ctx_rows = ctx_b * ctx_chunks
    for d in range(2):
        for g in range(gpb):
            for r0 in range(0, n_rows, rchunk):
                ss = _dot(u_ref[g, r0:r0 + rchunk, :], ett_ref[d, g])
                sx_ref[g, r0:r0 + rchunk, :] = ss[:, :w]
                sxs_ref[g, r0:r0 + rchunk, :] = ss[:, w:]
        c1 = [coef_ref[d, g][0:1] for g in range(gpb)]
        c2 = [coef_ref[d, g][1:2] for g in range(gpb)]

        def make_body(base, nb, nc):
            def body(i, carry):
                c = i if d == 0 else nc - 1 - i
                rows = pl.ds(base + c, nb, stride=nc)
                out = []
                for g in range(gpb):
                    x, xs = carry[g]
                    hp_ref[g, rows, :] = x
                    xn = c1[g] * x + c2[g] * xs + sx_ref[g, rows, :]
                    xsn = c1[g] * xs - c2[g] * x + sxs_ref[g, rows, :]
                    out.append((xn, xsn))
                return tuple(out)
            return body

        z = jnp.zeros((ctx_b, w), F32)
        fin = lax.fori_loop(0, ctx_chunks, make_body(0, ctx_b, ctx_chunks), tuple((z, z) for _ in range(gpb)))
        hfin_ref[d] = jnp.concatenate([fin[g][0] for g in range(gpb)], axis=1)
        init = tuple((x0_ref[d][:, g * w:(g + 1) * w], x0s_ref[d][:, g * w:(g + 1) * w]) for g in range(gpb))
        lax.fori_loop(0, lat_chunks, make_body(ctx_rows, lat_b, lat_chunks), init, unroll=4)
        for g in range(gpb):
            for r0 in range(0, n_rows, rchunk):
                y = _dot(u_ref[g, r0:r0 + rchunk, :], mt_ref[d, g])
                y = y + _dot(hp_ref[g, r0:r0 + rchunk, :].astype(BF16), ft_ref[d, g])
                if d == 0:
                    y_ref[g, r0:r0 + rchunk, :] = y
                else:
                    y_ref[g, r0:r0 + rchunk, :] += y


def ssm_scan(u, mt, ett, ft, coef, x0, x0s, ctx_b, ctx_chunks, lat_b, lat_chunks, gpb=4):
    g, n_rows, qk = u.shape
    w = 2 * SSM_STATE
    rchunk = math.gcd(n_rows, 512)
    assert n_rows == ctx_b * ctx_chunks + lat_b * lat_chunks
    ublk = pl.BlockSpec((gpb, n_rows, qk), lambda j: (j, 0, 0))
    blk = lambda r, c: pl.BlockSpec((2, gpb, r, c), lambda j: (0, j, 0, 0))
    vec = lambda r: pl.BlockSpec((2, r, gpb * w), lambda j: (0, 0, j))
    return pl.pallas_call(
        functools.partial(_ssm_scan_kernel, gpb=gpb, ctx_b=ctx_b, ctx_chunks=ctx_chunks, lat_b=lat_b,
                          lat_chunks=lat_chunks, rchunk=rchunk),
        out_shape=(jax.ShapeDtypeStruct((g, n_rows, qk), F32), jax.ShapeDtypeStruct((2, ctx_b, g * w), F32)),
        grid=(g // gpb,),
        in_specs=[ublk, blk(qk, qk), blk(qk, 2 * w), blk(w, qk), blk(8, w), vec(lat_b), vec(lat_b)],
        out_specs=(ublk, vec(ctx_b)),
        scratch_shapes=[pltpu.VMEM((gpb, n_rows, w), F32), pltpu.VMEM((gpb, n_rows, w), F32),
                        pltpu.VMEM((gpb, n_rows, w), F32)],
        compiler_params=_cparams(("parallel",)),
        name="ssm_scan",
    )(u, mt, ett, ft, coef, x0, x0s)


def _ssm_out_kernel(y_ref, mod_ref, yc_ref, w_ref, out_ref, s_ref, *, d_model):
    kk = SSM_GROUP
    n_ch = yc_ref.shape[1]
    gpt = LANES // kk
    n_lt = s_ref.shape[0]
    for t in range(SSM_CHUNK):
        for j in range(n_lt):
            piece = jnp.concatenate([yc_ref[j * gpt + gl, :, t * kk:(t + 1) * kk] for gl in range(gpt)], axis=1)
            s_ref[j, pl.ds(t, n_ch, stride=SSM_CHUNK), :] = piece
    s = jnp.concatenate([s_ref[j] for j in range(n_lt)], axis=1)
    a = jax.nn.gelu(s, approximate=True).astype(BF16)
    z = _dot(a, w_ref[...])
    gate = mod_ref[...][2:3]
    out_ref[...] = y_ref[...] + gate * (z[:, :d_model] * jax.nn.sigmoid(z[:, d_model:]))


def ssm_out(lay, y, mod, ych, w_glu):
    t, d = y.shape
    tm = lay.tm
    g, _, qk = ych.shape
    n_ch = tm // SSM_CHUNK
    row = lambda i: (i, 0)
    return pl.pallas_call(
        functools.partial(_ssm_out_kernel, d_model=d),
        out_shape=jax.ShapeDtypeStruct((t, d), F32),
        grid=(lay.n_tiles,),
        in_specs=[pl.BlockSpec((tm, d), row), pl.BlockSpec((None, 8, d), lambda i: (lay.seg(i), 0, 0)),
                  pl.BlockSpec((g, n_ch, qk), lambda i: (0, i, 0)),
                  pl.BlockSpec(w_glu.shape, lambda i: (0, 0))],
        out_specs=pl.BlockSpec((tm, d), row),
        scratch_shapes=[pltpu.VMEM((g * SSM_GROUP // LANES, tm, LANES), F32)],
        input_output_aliases={0: 0},
        compiler_params=_cparams(("parallel",)),
        name="ssm_out",
    )(y, mod, ych, w_glu)


def s5_layer(lay, y, mod, nw, state_re, state_im, w_in, a_re, a_im, log_dt, b_re, b_im, c_re, c_im, d_skip, w_glu):
    uch = ssm_in(lay, y, mod, nw, w_in.astype(BF16))
    g = a_re.shape[1]
    p = SSM_STATE
    mt, ett, ft, coef = ssm_prep(a_re, a_im, log_dt, b_re, b_im, c_re, c_im, d_skip)
    st = jnp.concatenate([state_re, state_im], axis=-1)
    sts = jnp.concatenate([state_im, state_re], axis=-1)
    x0 = jnp.transpose(st, (1, 0, 2, 3)).reshape(2, lay.n_lat_b, g * 2 * p)
    x0s = jnp.transpose(sts, (1, 0, 2, 3)).reshape(2, lay.n_lat_b, g * 2 * p)
    ych, hfin = ssm_scan(uch, mt, ett, ft, coef, x0, x0s, lay.n_ctx_b, lay.ctx_len // SSM_CHUNK,
                         lay.n_lat_b, lay.lat_len // SSM_CHUNK)
    y = ssm_out(lay, y, mod, ych, w_glu.astype(BF16))
    hf = hfin.reshape(2, lay.n_ctx_b, g, 2, p)
    new_re = jnp.transpose(hf[:, :, :, 0], (1, 0, 2, 3))
    new_im = jnp.transpose(hf[:, :, :, 1], (1, 0, 2, 3))
    return y, new_re, new_im


def _route_kernel(y_ref, mod_ref, nw_ref, wr_ref, br_ref, tri_ref, h_ref, ri_ref, pc_ref, *, n_exp):
    m = mod_ref[...]
    h = _norm_mod(y_ref[...], nw_ref[...], m[3:4], m[4:5])
    h_hi = h.astype(BF16)
    h_ref[...] = h_hi
    h_lo = (h - h_hi.astype(F32)).astype(BF16)
    wr = wr_ref[...]
    lt = _dot_nt(wr, h_hi)
    logits = lt[:n_exp] + lt[n_exp:] + _dot_nt(wr[:n_exp], h_lo)
    scores = jax.nn.sigmoid(logits)
    sel = scores + br_ref[...]
    epg = EXPERTS_PER_GROUP
    row = lambda x, e: x[e:e + 1, :]
    gscore = []
    for g in range(N_EXPERT_GROUPS):
        a, b, c, d = (row(sel, g * epg + j) for j in range(epg))
        m1, n1, m2, n2 = jnp.maximum(a, b), jnp.minimum(a, b), jnp.maximum(c, d), jnp.minimum(c, d)
        gscore.append(jnp.maximum(m1, m2) + jnp.maximum(jnp.minimum(m1, m2), jnp.maximum(n1, n2)))
    best = gscore[0]
    gi = jnp.zeros_like(best, dtype=jnp.int32)
    for g in range(1, N_EXPERT_GROUPS):
        better = gscore[g] > best
        gi = jnp.where(better, g, gi)
        best = jnp.where(better, gscore[g], best)

    def pick(x, j):
        out = row(x, j)
        for g in range(1, N_EXPERT_GROUPS):
            out = jnp.where(gi == g, row(x, g * epg + j), out)
        return out

    sv = [pick(sel, j) for j in range(epg)]
    cv = [pick(scores, j) for j in range(epg)]
    b1, i1, w1 = sv[0], jnp.zeros_like(gi), cv[0]
    for j in range(1, epg):
        better = sv[j] > b1
        i1 = jnp.where(better, j, i1)
        w1 = jnp.where(better, cv[j], w1)
        b1 = jnp.where(better, sv[j], b1)
    neg = jnp.full_like(b1, -jnp.inf)
    b2, i2, w2 = neg, jnp.zeros_like(gi), jnp.zeros_like(w1)
    for j in range(epg):
        better = (i1 != j) & (sv[j] > b2)
        i2 = jnp.where(better, j, i2)
        w2 = jnp.where(better, cv[j], w2)
        b2 = jnp.where(better, sv[j], b2)
    tot = w1 + w2
    e1 = gi * epg + i1
    e2 = gi * epg + i2
    eid = lax.broadcasted_iota(jnp.int32, logits.shape, 0)
    m1h = eid == e1
    m2h = eid == e2
    mc = jnp.where(m1h | m2h, 1.0, 0.0)
    pref = _dot(mc.astype(BF16), tri_ref[...])
    cnt = jnp.sum(mc, axis=1, keepdims=True)
    pc_al = jnp.ceil(cnt * (1.0 / SEG_ALIGN))
    pcb = jnp.broadcast_to(pc_al, (n_exp, LANES))
    er = lax.broadcasted_iota(jnp.int32, (n_exp, n_exp), 0)
    ec = lax.broadcasted_iota(jnp.int32, (n_exp, n_exp), 1)
    lower = jnp.where(ec < er, 1.0, 0.0).astype(BF16)
    seg = _dot(lower, pcb.astype(BF16))[:, 0:1] * SEG_ALIGN
    slot = seg + pref
    pos1 = jnp.sum(jnp.where(m1h, slot, 0.0), axis=0, keepdims=True)
    pos2 = jnp.sum(jnp.where(m2h, slot, 0.0), axis=0, keepdims=True)
    zero = jnp.zeros_like(pos1)
    ri_ref[...] = jnp.concatenate([pos1, pos2, w1 / tot, w2 / tot, zero, zero, zero, zero], axis=0)
    pc_ref[...] = pcb * SEG_ALIGN


def moe_route(lay, y, mod, nw, wr_t, br, tri):
    t, d = y.shape
    tm = lay.tm
    n_exp = br.shape[0]
    return pl.pallas_call(
        functools.partial(_route_kernel, n_exp=n_exp),
        out_shape=(jax.ShapeDtypeStruct((t, d), BF16), jax.ShapeDtypeStruct((8, t), F32),
                   jax.ShapeDtypeStruct((lay.n_tiles, n_exp, LANES), F32)),
        grid=(lay.n_tiles,),
        in_specs=[pl.BlockSpec((tm, d), lambda i: (i, 0)),
                  pl.BlockSpec((None, 8, d), lambda i: (lay.seg(i), 0, 0)),
                  pl.BlockSpec((1, d), lambda i: (0, 0)), pl.BlockSpec(wr_t.shape, lambda i: (0, 0)),
                  pl.BlockSpec((n_exp, 1), lambda i: (0, 0)), pl.BlockSpec((tm, tm), lambda i: (0, 0))],
        out_specs=(pl.BlockSpec((tm, d), lambda i: (i, 0)), pl.BlockSpec((8, tm), lambda i: (0, i)),
                   pl.BlockSpec((None, n_exp, LANES), lambda i: (i, 0, 0))),
        compiler_params=_cparams(("parallel",)),
        name="moe_route",
    )(y, mod, nw, wr_t, br, tri)


def _segment_pieces(i, off_ref, pc_ref, n_exp, sizes):
    local = 0
    for e in range(n_exp):
        n = pc_ref[i * n_exp + e]
        g0 = off_ref[i * n_exp + e]
        for size in sizes:
            above = n & (-2 * size)
            yield (e, (n & size) != 0, pl.multiple_of(local + above, SEG_ALIGN),
                   pl.multiple_of(g0 + above, SEG_ALIGN), size)
        local = local + n


def _pieces(n, sizes):
    for size in sizes:
        yield (n & size) != 0, n & (-2 * size), size


def _moe_permute_kernel(off_ref, pc_ref, goff_ref, gn_ref, h_ref, ri_ref, xs_ref, ys_ref, zp_ref, sem, gsem,
                        *, n_exp, sizes, gap_sizes, rchunk):
    i = pl.program_id(0)
    last = pl.num_programs(0) - 1
    slot = i % 2
    ri = ri_ref[...]
    pos1 = ri[0:1]
    pos2 = ri[1:2]
    h = h_ref[...]
    n_loc = ys_ref.shape[1]
    for r0 in range(0, n_loc, rchunk):
        r = (lax.broadcasted_iota(jnp.int32, (rchunk, h.shape[0]), 0) + r0).astype(F32)
        p = jnp.where(pos1 == r, 1.0, 0.0) + jnp.where(pos2 == r, 1.0, 0.0)
        ys_ref[slot, r0:r0 + rchunk, :] = _dot(p.astype(BF16), h).astype(BF16)

    def copies(tile, sl):
        for e, present, lrow, grow, size in _segment_pieces(tile, off_ref, pc_ref, n_exp, sizes):
            yield present, pltpu.make_async_copy(ys_ref.at[sl, pl.ds(lrow, size)], xs_ref.at[pl.ds(grow, size)],
                                                 sem.at[sl, e])

    for present, cp in copies(i, slot):
        pl.when(present)(cp.start)

    @pl.when(i > 0)
    def _():
        for present, cp in copies(i - 1, 1 - slot):
            pl.when(present)(cp.wait)

    @pl.when(i == last)
    def _():
        for present, cp in copies(i, slot):
            pl.when(present)(cp.wait)
        zp_ref[...] = jnp.zeros_like(zp_ref)

        def gaps():
            for e in range(n_exp):
                g0 = goff_ref[e]
                for present, above, size in _pieces(gn_ref[e], gap_sizes):
                    yield present, pltpu.make_async_copy(
                        zp_ref.at[pl.ds(0, size)], xs_ref.at[pl.ds(pl.multiple_of(g0 + above, SEG_ALIGN), size)],
                        gsem.at[e])

        for present, cp in gaps():
            pl.when(present)(cp.start)
        for present, cp in gaps():
            pl.when(present)(cp.wait)

        zrows = zp_ref.shape[0]
        tail0 = goff_ref[n_exp]
        tail = lambda c: pltpu.make_async_copy(
            zp_ref, xs_ref.at[pl.ds(pl.multiple_of(tail0 + c * zrows, zrows), zrows)], gsem.at[0])

        @pl.loop(0, gn_ref[n_exp])
        def _(c):
            tail(c).start()

        @pl.loop(0, gn_ref[n_exp])
        def _(c):
            tail(c).wait()


def moe_permute(lay, h, rinfo, off, pcs, gap_off, gap_n, n_sorted, n_loc, n_exp, tmm):
    t, d = h.shape
    tm = lay.tm
    sizes = [s for s in (1 << b for b in range(12, 3, -1)) if s <= tm]
    gap_sizes = [s for s in (1 << b for b in range(12, 3, -1)) if s < tmm]
    return pl.pallas_call(
        functools.partial(_moe_permute_kernel, n_exp=n_exp, sizes=sizes, gap_sizes=gap_sizes, rchunk=256),
        out_shape=jax.ShapeDtypeStruct((n_sorted, d), BF16),
        grid_spec=pltpu.PrefetchScalarGridSpec(
            num_scalar_prefetch=4, grid=(lay.n_tiles,),
            in_specs=[pl.BlockSpec((tm, d), lambda i, *_: (i, 0)), pl.BlockSpec((8, tm), lambda i, *_: (0, i))],
            out_specs=pl.BlockSpec(memory_space=pl.ANY),
            scratch_shapes=[pltpu.VMEM((2, n_loc, d), BF16), pltpu.VMEM((gap_sizes[0], d), BF16),
                            pltpu.SemaphoreType.DMA((2, n_exp)), pltpu.SemaphoreType.DMA((n_exp,))]),
        compiler_params=_cparams(("arbitrary",)),
        name="moe_permute",
    )(off, pcs, gap_off, gap_n, h, rinfo)


def _moe_expert_kernel(te_ref, tv_ref, tf_ref, xb_ref, x_ref, wg_ref, wu_ref, wd_ref, z_ref, wgb_ref, wub_ref, wdb_ref):
    del te_ref, xb_ref
    r = pl.program_id(0)

    @pl.when(tf_ref[r] == 1)
    def _():
        wgb_ref[...] = wg_ref[...].astype(BF16)
        wub_ref[...] = wu_ref[...].astype(BF16)
        wdb_ref[...] = wd_ref[...].astype(BF16)

    @pl.when(tv_ref[r] > 0)
    def _():
        x = x_ref[...]
        a = _dot(x, wgb_ref[...])
        b = _dot(x, wub_ref[...])
        hid = (a * jax.nn.sigmoid(a)) * b
        z_ref[...] = _dot(hid.astype(BF16), wdb_ref[...]).astype(BF16)

    @pl.when(tv_ref[r] == 0)
    def _():
        z_ref[...] = jnp.zeros_like(z_ref)


def moe_experts(xs, tile_expert, tile_valid, tile_first, x_block, layer, wg, wu, wd, tmm):
    n_sorted, d = xs.shape
    f = wg.shape[-1]
    wmap = lambda r, te, tv, tf, xb: (layer, te[r], 0, 0)
    return pl.pallas_call(
        _moe_expert_kernel,
        out_shape=jax.ShapeDtypeStruct((n_sorted, d), BF16),
        grid_spec=pltpu.PrefetchScalarGridSpec(
            num_scalar_prefetch=4, grid=(n_sorted // tmm,),
            in_specs=[pl.BlockSpec((tmm, d), lambda r, te, tv, tf, xb: (xb[r], 0)),
                      pl.BlockSpec((None, None, d, f), wmap), pl.BlockSpec((None, None, d, f), wmap),
                      pl.BlockSpec((None, None, f, d), wmap)],
            out_specs=pl.BlockSpec((tmm, d), lambda r, te, tv, tf, xb: (r, 0)),
            scratch_shapes=[pltpu.VMEM((d, f), BF16), pltpu.VMEM((d, f), BF16), pltpu.VMEM((f, d), BF16)]),
        compiler_params=_cparams(("arbitrary",)),
        name="moe_experts",
    )(tile_expert, tile_valid, tile_first, x_block, xs, wg, wu, wd)


def _moe_combine_kernel(off_ref, pc_ref, y_ref, mod_ref, rit_ref, zs_ref, *rest, lay, n_exp, sizes):
    out_refs, (zt_ref, sem) = rest[:-2], rest[-2:]
    i = pl.program_id(0)
    slot = i % 2

    def copies(tile, sl):
        for e, present, lrow, grow, size in _segment_pieces(tile, off_ref, pc_ref, n_exp, sizes):
            yield present, pltpu.make_async_copy(zs_ref.at[pl.ds(grow, size)], zt_ref.at[sl, pl.ds(lrow, size)],
                                                 sem.at[sl, e])

    def fetch(tile, sl):
        zt_ref[sl] = jnp.zeros(zt_ref.shape[1:], BF16)
        for present, cp in copies(tile, sl):
            pl.when(present)(cp.start)

    @pl.when(i == 0)
    def _():
        fetch(i, slot)

    @pl.when(i + 1 < pl.num_programs(0))
    def _():
        fetch(i + 1, 1 - slot)

    rit = rit_ref[...]
    n_loc = zt_ref.shape[1]
    r = lax.broadcasted_iota(jnp.int32, (rit.shape[0], n_loc), 1).astype(F32)
    pw = jnp.where(rit[:, 0:1] == r, rit[:, 2:3], 0.0) + jnp.where(rit[:, 1:2] == r, rit[:, 3:4], 0.0)
    pw = pw.astype(BF16)
    for present, cp in copies(i, slot):
        pl.when(present)(cp.wait)
    lay.stream_store(out_refs, y_ref[...] + mod_ref[...][5:6] * _dot(pw, zt_ref[slot]))


def moe_combine(lay, y, mod, rinfo_t, zs, off, pcs, n_loc, n_exp, pair_out):
    t, d = y.shape
    tm = lay.tm
    sizes = [s for s in (1 << b for b in range(12, 3, -1)) if s <= tm]
    if pair_out:
        out_shape = (jax.ShapeDtypeStruct((lay.t_ctx, d), F32), jax.ShapeDtypeStruct((lay.t_lat, d), F32))
    else:
        out_shape = (jax.ShapeDtypeStruct((t, d), F32),)
    out = pl.pallas_call(
        functools.partial(_moe_combine_kernel, lay=lay, n_exp=n_exp, sizes=sizes),
        out_shape=out_shape,
        grid_spec=pltpu.PrefetchScalarGridSpec(
            num_scalar_prefetch=2, grid=(lay.n_tiles,),
            in_specs=[pl.BlockSpec((tm, d), lambda i, o, p: (i, 0)),
                      pl.BlockSpec((None, 8, d), lambda i, o, p: (lay.seg(i), 0, 0)),
                      pl.BlockSpec((tm, 8), lambda i, o, p: (i, 0)),
                      pl.BlockSpec(memory_space=pl.ANY)],
            out_specs=tuple(lay.stream_specs(d, pair_out)),
            scratch_shapes=[pltpu.VMEM((2, n_loc, d), BF16), pltpu.SemaphoreType.DMA((2, n_exp))]),
        input_output_aliases={} if pair_out else {2: 0},
        compiler_params=_cparams(("arbitrary",)),
        name="moe_combine",
    )(off, pcs, y, mod, rinfo_t, zs)
    return out if pair_out else out[0]


def _round_up(x, m):
    return (x + m - 1) // m * m


def moe_layer(lay, y, mod, nw, wr_t, br, tri, layer, wg, wu, wd, tmm, pair_out=False):
    t, d = y.shape
    n_exp = br.shape[0]
    h, rinfo, pc = moe_route(lay, y, mod, nw, wr_t, br, tri)
    pc = pc[:, :, 0].astype(jnp.int32)
    n_loc = 2 * lay.tm + SEG_ALIGN * n_exp
    n_sorted = _round_up(2 * t + lay.n_tiles * n_exp * (SEG_ALIGN - 1) + n_exp * (tmm - 1), tmm)
    tot = jnp.sum(pc, axis=0)
    region = _round_up(tot, tmm)
    ends = jnp.cumsum(region)
    base = ends - region
    off = (base[None, :] + jnp.cumsum(pc, axis=0) - pc).reshape(-1)
    starts = jnp.arange(n_sorted // tmm, dtype=jnp.int32) * tmm
    te = jnp.minimum(jnp.sum(starts[:, None] >= ends[None, :], axis=1), n_exp - 1).astype(jnp.int32)
    tv = jnp.clip(tot[te] - (starts - base[te]), 0, tmm).astype(jnp.int32)
    tf = jnp.concatenate([jnp.ones((1,), jnp.int32), (te[1:] != te[:-1]).astype(jnp.int32)])
    xb = jnp.minimum(starts // tmm, jnp.maximum(ends[-1] // tmm - 1, 0)).astype(jnp.int32)
    pcs = pc.reshape(-1)
    assert tmm & (tmm - 1) == 0
    zrows = tmm // 2
    gap_off = jnp.concatenate([base + tot, ends[-1:]]).astype(jnp.int32)
    gap_n = jnp.concatenate([region - tot, (n_sorted - ends[-1:]) // zrows]).astype(jnp.int32)
    xs = moe_permute(lay, h, rinfo, off, pcs, gap_off, gap_n, n_sorted, n_loc, n_exp, tmm)
    zs = moe_experts(xs, te, tv, tf, xb, layer, wg, wu, wd, tmm)
    return moe_combine(lay, y, mod, rinfo.T, zs, off, pcs, n_loc, n_exp, pair_out)


def _rope_perm():
    d = np.arange(QK_ROPE)
    return QK_NOPE + (d ^ (QK_ROPE // 4))


def _pad_cols(x, n):
    return jnp.pad(x, ((0, 0),) * (x.ndim - 1) + ((0, n - x.shape[-1]),))


def _mla_weights(w_q_a, w_q_b, w_kv_a, w_kv_b, q_norm, k_norm, kv_lora):
    perm = _rope_perm()
    nh = N_HEADS
    wkr = w_kv_a[:, kv_lora:]
    d = w_kv_a.shape[0]
    zl = jnp.zeros((d, QK_NOPE), F32)
    kr_blk = _pad_cols(jnp.concatenate([zl, wkr], axis=1), LANES)
    krp_blk = _pad_cols(jnp.concatenate([zl, wkr[:, perm - QK_NOPE]], axis=1), LANES)
    w_a = jnp.concatenate([w_q_a, w_kv_a[:, :kv_lora], kr_blk, krp_blk], axis=1).astype(BF16)
    wq = w_q_b.reshape(-1, nh, QK_DIM).transpose(1, 0, 2)
    wq_main = _pad_cols(wq, HEAD_PAD)
    wq_part = _pad_cols(jnp.concatenate([jnp.zeros_like(wq[..., :QK_NOPE]), wq[..., perm]], axis=-1), HEAD_PAD)
    wq_h = jnp.concatenate([wq_main, wq_part], axis=-1).astype(BF16)
    wkv = w_kv_b.reshape(-1, nh, QK_NOPE + V_DIM).transpose(1, 0, 2)
    wk_h = _pad_cols(wkv[..., :QK_NOPE], HEAD_PAD).astype(BF16)
    wv = wkv[..., QK_NOPE:]
    wv_h = jnp.concatenate([wv[0::2], wv[1::2]], axis=-1).astype(BF16)
    nrm = jnp.stack([_pad_cols(q_norm, LANES), _pad_cols(jnp.concatenate([jnp.zeros((QK_NOPE,), F32), q_norm[perm]]), LANES),
                     _pad_cols(k_norm, LANES), _pad_cols(jnp.concatenate([jnp.zeros((QK_NOPE,), F32), k_norm[perm]]), LANES)])
    nrm = jnp.pad(nrm, ((0, 4), (0, 0)))
    return w_a, wq_h, wk_h, wv_h, nrm


def _rope_tables(lat_len, tm):
    rows = lat_len // GRID_W
    t_row = np.repeat(np.arange(rows, dtype=np.float32), GRID_W)
    t_col = np.tile(np.arange(GRID_W, dtype=np.float32), rows)
    half = QK_ROPE // 2
    inv = (ROPE_BASE ** (-np.arange(0, half, 2, dtype=np.float32) / half)).astype(np.float32)
    ang = jnp.concatenate([jnp.asarray(t_row)[:, None] * inv, jnp.asarray(t_col)[:, None] * inv], axis=-1)
    cos, sin = jnp.cos(ang), jnp.sin(ang)
    nf = QK_ROPE // 4
    cos_l = jnp.concatenate([cos[:, :nf], cos[:, :nf], cos[:, nf:], cos[:, nf:]], axis=-1)
    sin_l = jnp.concatenate([-sin[:, :nf], sin[:, :nf], -sin[:, nf:], sin[:, nf:]], axis=-1)
    one = jnp.ones((lat_len, QK_NOPE), F32)
    cos_t = _pad_cols(jnp.concatenate([one, cos_l], axis=-1), LANES)
    cos_t = cos_t.at[:, QK_DIM:].set(1.0)
    sin_t = _pad_cols(jnp.concatenate([jnp.zeros((lat_len, QK_NOPE), F32), sin_l], axis=-1), LANES)
    cos_t = jnp.concatenate([jnp.ones((tm, LANES), F32), cos_t], axis=0)
    sin_t = jnp.concatenate([jnp.zeros((tm, LANES), F32), sin_t], axis=0)
    return cos_t, sin_t


def mla_layer(lay, y, mod, nw, cache_ckv, cache_krope, cos_t, sin_t, w_q_a, q_a_norm, w_q_b, w_kv_a, kv_a_norm,
              w_kv_b, q_norm, k_norm, w_o):
    q_lora = w_q_a.shape[1]
    kv_lora = kv_a_norm.shape[0]
    w_a, wq_h, wk_h, wv_h, nrm = _mla_weights(w_q_a, w_q_b, w_kv_a, w_kv_b, q_norm, k_norm, kv_lora)
    cq, ckv, kr, krp = mla_a(lay, y, mod, nw, w_a, q_a_norm.reshape(1, -1), kv_a_norm.reshape(1, -1), q_lora, kv_lora)
    q, k, v2 = mla_b(lay, cq, ckv, kr, krp, wq_h, wk_h, wv_h, nrm, cos_t, sin_t)
    past = cache_ckv.shape[1]
    ckr = jnp.pad(cache_krope.reshape(-1, QK_ROPE), ((0, 0), (QK_NOPE, LANES - QK_DIM)))
    kc, vc2 = ctx_kv(cache_ckv.reshape(-1, kv_lora), ckr, wk_h, wv_h, nrm, past)
    o_ctx = attention(q, k, v2, 0, lay.n_ctx_b, lay.ctx_len, min(lay.ctx_len, 256))
    o_lat = attention(q, k, v2, lay.t_ctx, lay.n_lat_b, lay.lat_len, min(lay.lat_len, 512), kc, vc2, past)
    o = jnp.concatenate([o_ctx, o_lat], axis=0)
    y = proj_residual(lay, y, mod, o, w_o.astype(BF16), 2)
    new_ckv = ckv[:lay.t_ctx].reshape(lay.n_ctx_b, lay.ctx_len, kv_lora)
    new_kr = kr[:lay.t_ctx, QK_NOPE:QK_DIM].reshape(lay.n_ctx_b, lay.ctx_len, QK_ROPE)
    return y, new_ckv, new_kr


def kernel(x_prompt, x_sample, c, cache_ckv, cache_krope, state_ssm_re, state_ssm_im, c_ctx, w_mod, b_mod, norm1_w, norm2_w, mla_w_q_a, mla_q_a_norm, mla_w_q_b, mla_w_kv_a, mla_kv_a_norm, mla_w_kv_b, mla_q_norm, mla_k_norm, mla_w_o, ssm_w_in, ssm_a_re, ssm_a_im, ssm_log_dt, ssm_b_re, ssm_b_im, ssm_c_re, ssm_c_im, ssm_d, ssm_w_glu, w_router, b_router, moe_w_gate, moe_w_up, moe_w_down):
    n_ctx_b, ctx_len, d = x_prompt.shape
    n_lat_b, lat_len, _ = x_sample.shape
    depth = w_mod.shape[0]
    lay = Layout(n_ctx_b, ctx_len, n_lat_b, lat_len, tm=256)
    assert n_lat_b + 1 <= 8
    y = (x_prompt.reshape(-1, d), x_sample.reshape(-1, d))
    cond8 = jnp.pad(jnp.concatenate([c_ctx[None, :], c], axis=0), ((0, 7 - n_lat_b), (0, 0)))
    mods = adaln_all(cond8, w_mod, b_mod)
    mods = jnp.pad(mods.reshape(depth, 8, 6, d), ((0, 0), (0, 0), (0, 2), (0, 0)))
    cos_t, sin_t = _rope_tables(lat_len, lay.tm)
    n_exp = b_router.shape[0]
    wr_hi = w_router.astype(BF16)
    wr_lo = (w_router - wr_hi.astype(F32)).astype(BF16)
    wr_t = jnp.concatenate([wr_hi.T, wr_lo.T], axis=0)
    br = b_router.reshape(n_exp, 1)
    lay_s = Layout(n_ctx_b, ctx_len, n_lat_b, lat_len, tm=512)
    tri = jnp.asarray(np.triu(np.ones((lay_s.tm, lay_s.tm), np.float32), k=1), BF16)
    ckv_out, kr_out, sre_out, sim_out = [], [], [], []
    for i in range(depth):
        j = i // 2
        mod = mods[i]
        nw1 = norm1_w[i].reshape(1, d)
        if i % 2 == 0:
            y, ckv_p, kr_p = mla_layer(lay, y, mod, nw1, cache_ckv[:, j], cache_krope[:, j], cos_t, sin_t,
                                       mla_w_q_a[j], mla_q_a_norm[j], mla_w_q_b[j], mla_w_kv_a[j], mla_kv_a_norm[j],
                                       mla_w_kv_b[j], mla_q_norm[j], mla_k_norm[j], mla_w_o[j])
            ckv_out.append(ckv_p)
            kr_out.append(kr_p)
        else:
            y, s_re, s_im = s5_layer(lay, y, mod, nw1, state_ssm_re[:, j], state_ssm_im[:, j], ssm_w_in[j],
                                     ssm_a_re[j], ssm_a_im[j], ssm_log_dt[j], ssm_b_re[j], ssm_b_im[j],
                                     ssm_c_re[j], ssm_c_im[j], ssm_d[j], ssm_w_glu[j])
            sre_out.append(s_re)
            sim_out.append(s_im)
        y = moe_layer(lay_s, y, mod, norm2_w[i].reshape(1, d), wr_t, br, tri, i, moe_w_gate, moe_w_up,
                      moe_w_down, tmm=512, pair_out=(i == depth - 1))
    yp = y[0].reshape(n_ctx_b, ctx_len, d)
    ys = y[1].reshape(n_lat_b, lat_len, d)
    return (yp, ys, jnp.stack(ckv_out, axis=1), jnp.stack(kr_out, axis=1),
            jnp.stack(sre_out, axis=1), jnp.stack(sim_out, axis=1))
```

```python
import functools
import math

import jax
import jax.numpy as jnp
import numpy as np
from jax import lax
from jax.experimental import pallas as pl
from jax.experimental.pallas import tpu as pltpu

F32 = jnp.float32
BF16 = jnp.bfloat16
EPS = 1e-6

GRID_W = 64
N_HEADS = 8
QK_NOPE = 64
QK_ROPE = 32
QK_DIM = QK_NOPE + QK_ROPE
V_DIM = 64
ROPE_BASE = 10000.0
SSM_GROUP = 16
SSM_STATE = 64
N_EXPERT_GROUPS = 4
EXPERTS_PER_GROUP = 4

LANES = 128
HEAD_PAD = LANES
SSM_CHUNK = 16
SEG_ALIGN = 16
VMEM_LIMIT = 48 * 1024 * 1024


def _cparams(sem, vmem=VMEM_LIMIT):
    return pltpu.CompilerParams(dimension_semantics=sem, vmem_limit_bytes=vmem)


def _dot(a, b):
    return jnp.dot(a, b, preferred_element_type=F32)


def _dot_nt(a, b):
    return lax.dot_general(a, b, (((1,), (1,)), ((), ())), preferred_element_type=F32)


def _dot_hi(a, b):
    return jnp.dot(a, b, preferred_element_type=F32, precision=lax.Precision.HIGHEST)


def _norm_mod(x, nw, shift, scale):
    ms = jnp.mean(x * x, axis=-1, keepdims=True)
    return (x * lax.rsqrt(ms + EPS) * nw) * (1.0 + scale) + shift


class Layout:
    def __init__(self, n_ctx_b, ctx_len, n_lat_b, lat_len, tm):
        self.n_ctx_b, self.ctx_len, self.n_lat_b, self.lat_len = n_ctx_b, ctx_len, n_lat_b, lat_len
        self.t_ctx = n_ctx_b * ctx_len
        self.t_lat = n_lat_b * lat_len
        self.t = self.t_ctx + self.t_lat
        self.tm = tm
        assert self.t_ctx % tm == 0 and lat_len % tm == 0
        self.ctx_tiles = self.t_ctx // tm
        self.lat_tiles_per_b = lat_len // tm
        self.n_tiles = self.t // tm

    def seg(self, i):
        return jnp.where(i < self.ctx_tiles, 0, 1 + (i - self.ctx_tiles) // self.lat_tiles_per_b)

    def rope_blk(self, i):
        return jnp.where(i < self.ctx_tiles, 0, 1 + (i - self.ctx_tiles) % self.lat_tiles_per_b)

    def stream_specs(self, d, pair):
        tm = self.tm
        if not pair:
            return [pl.BlockSpec((tm, d), lambda i, *_: (i, 0))]
        return [pl.BlockSpec((tm, d), lambda i, *_: (jnp.minimum(i, self.ctx_tiles - 1), 0)),
                pl.BlockSpec((tm, d), lambda i, *_: (jnp.maximum(i - self.ctx_tiles, 0), 0))]

    def stream_load(self, refs):
        if len(refs) == 1:
            return refs[0][...]
        return jnp.where(pl.program_id(0) < self.ctx_tiles, refs[0][...], refs[1][...])

    def stream_store(self, refs, val):
        if len(refs) == 1:
            refs[0][...] = val
            return
        i = pl.program_id(0)

        @pl.when(i < self.ctx_tiles)
        def _():
            refs[0][...] = val

        @pl.when(i >= self.ctx_tiles)
        def _():
            refs[1][...] = val


def _as_list(y):
    return list(y) if isinstance(y, (tuple, list)) else [y]


def _adaln_kernel(c_ref, w_ref, b_ref, o_ref):
    c = c_ref[...]
    s = c * jax.nn.sigmoid(c)
    o_ref[...] = _dot_hi(s, w_ref[...]) + b_ref[...]


def adaln_all(cond8, w_mod, b_mod, tn=1536):
    depth, d, n6 = w_mod.shape
    return pl.pallas_call(
        _adaln_kernel,
        out_shape=jax.ShapeDtypeStruct((depth, 8, n6), F32),
        grid=(depth, n6 // tn),
        in_specs=[pl.BlockSpec((8, d), lambda l, j: (0, 0)),
                  pl.BlockSpec((None, d, tn), lambda l, j: (l, 0, j)),
                  pl.BlockSpec((None, 1, tn), lambda l, j: (l, 0, j))],
        out_specs=pl.BlockSpec((None, 8, tn), lambda l, j: (l, 0, j)),
        compiler_params=_cparams(("parallel", "parallel")),
        name="adaln",
    )(cond8, w_mod, b_mod.reshape(depth, 1, n6))


def _mla_a_kernel(*refs, lay, n_y, q_lora, kv_lora):
    y_refs = refs[:n_y]
    mod_ref, nw_ref, w_ref, qan_ref, kvan_ref, cq_ref, ckv_ref, kr_ref, krp_ref = refs[n_y:]
    m = mod_ref[...]
    h = _norm_mod(lay.stream_load(y_refs), nw_ref[...], m[0:1], m[1:2])
    z = _dot(h.astype(BF16), w_ref[...])
    cq = z[:, :q_lora]
    cq = cq * lax.rsqrt(jnp.mean(cq * cq, axis=-1, keepdims=True) + EPS) * qan_ref[...]
    cq_ref[...] = cq.astype(BF16)
    ckv = z[:, q_lora:q_lora + kv_lora]
    ckv_ref[...] = ckv * lax.rsqrt(jnp.mean(ckv * ckv, axis=-1, keepdims=True) + EPS) * kvan_ref[...]
    kr_ref[...] = z[:, q_lora + kv_lora:q_lora + kv_lora + LANES]
    krp_ref[...] = z[:, q_lora + kv_lora + LANES:]


def mla_a(lay, y, mod, nw, w_a, qan, kvan, q_lora, kv_lora):
    ys = _as_list(y)
    t, d = lay.t, ys[0].shape[1]
    tm = lay.tm
    na = w_a.shape[1]
    row = lambda i: (i, 0)
    full = lambda i: (0, 0)
    return pl.pallas_call(
        functools.partial(_mla_a_kernel, lay=lay, n_y=len(ys), q_lora=q_lora, kv_lora=kv_lora),
        out_shape=(jax.ShapeDtypeStruct((t, q_lora), BF16), jax.ShapeDtypeStruct((t, kv_lora), F32),
                   jax.ShapeDtypeStruct((t, LANES), F32), jax.ShapeDtypeStruct((t, LANES), F32)),
        grid=(lay.n_tiles,),
        in_specs=lay.stream_specs(d, len(ys) == 2) + [
            pl.BlockSpec((None, 8, d), lambda i: (lay.seg(i), 0, 0)),
            pl.BlockSpec((1, d), full), pl.BlockSpec((d, na), full),
            pl.BlockSpec((1, q_lora), full), pl.BlockSpec((1, kv_lora), full)],
        out_specs=(pl.BlockSpec((tm, q_lora), row), pl.BlockSpec((tm, kv_lora), row),
                   pl.BlockSpec((tm, LANES), row), pl.BlockSpec((tm, LANES), row)),
        compiler_params=_cparams(("parallel",)),
        name="mla_a",
    )(*ys, mod, nw, w_a, qan, kvan)


def _kv_heads(ckv_b, kr, krp, wk_ref, wv_ref, knw, knwp, cos, sin, k_ref, v_ref):
    kc = knw * cos
    ks = knwp * sin
    for h in range(N_HEADS):
        kz = _dot(ckv_b, wk_ref[h]) + kr
        r = lax.rsqrt(jnp.sum(kz * kz, axis=-1, keepdims=True) * (1.0 / QK_DIM) + EPS)
        k_ref[h] = (r * (kz * kc + krp * ks)).astype(BF16)
    for hp in range(N_HEADS // 2):
        v_ref[hp] = _dot(ckv_b, wv_ref[hp]).astype(BF16)


def _mla_b_kernel(cq_ref, ckv_ref, kr_ref, krp_ref, wq_ref, wk_ref, wv_ref, nrm_ref, cos_ref, sin_ref,
                  q_ref, k_ref, v_ref):
    nrm = nrm_ref[...]
    cos = cos_ref[...]
    sin = sin_ref[...]
    cq = cq_ref[...]
    qc = nrm[0:1] * cos * (QK_DIM ** -0.5)
    qs = nrm[1:2] * sin * (QK_DIM ** -0.5)
    for h in range(N_HEADS):
        z = _dot(cq, wq_ref[h])
        qm = z[:, :HEAD_PAD]
        qp = z[:, HEAD_PAD:]
        r = lax.rsqrt(jnp.sum(qm * qm, axis=-1, keepdims=True) * (1.0 / QK_DIM) + EPS)
        q_ref[h] = (r * (qm * qc + qp * qs)).astype(BF16)
    _kv_heads(ckv_ref[...].astype(BF16), kr_ref[...], krp_ref[...], wk_ref, wv_ref, nrm[2:3], nrm[3:4],
              cos, sin, k_ref, v_ref)


def mla_b(lay, cq, ckv, kr, krp, wq, wk, wv, nrm, cos_t, sin_t):
    t = cq.shape[0]
    tm = lay.tm
    row = lambda i: (i, 0)
    hrow = lambda i: (0, i, 0)
    full3 = lambda i: (0, 0, 0)
    rope = lambda i: (lay.rope_blk(i), 0)
    nh = N_HEADS
    return pl.pallas_call(
        _mla_b_kernel,
        out_shape=(jax.ShapeDtypeStruct((nh, t, HEAD_PAD), BF16), jax.ShapeDtypeStruct((nh, t, HEAD_PAD), BF16),
                   jax.ShapeDtypeStruct((nh // 2, t, LANES), BF16)),
        grid=(lay.n_tiles,),
        in_specs=[pl.BlockSpec((tm, cq.shape[1]), row), pl.BlockSpec((tm, ckv.shape[1]), row),
                  pl.BlockSpec((tm, LANES), row), pl.BlockSpec((tm, LANES), row),
                  pl.BlockSpec(wq.shape, full3), pl.BlockSpec(wk.shape, full3), pl.BlockSpec(wv.shape, full3),
                  pl.BlockSpec((8, LANES), lambda i: (0, 0)),
                  pl.BlockSpec((tm, LANES), rope), pl.BlockSpec((tm, LANES), rope)],
        out_specs=(pl.BlockSpec((nh, tm, HEAD_PAD), hrow), pl.BlockSpec((nh, tm, HEAD_PAD), hrow),
                   pl.BlockSpec((nh // 2, tm, LANES), hrow)),
        compiler_params=_cparams(("parallel",)),
        name="mla_b",
    )(cq, ckv, kr, krp, wq, wk, wv, nrm, cos_t, sin_t)


def _ctx_kv_kernel(ckv_ref, kr_ref, wk_ref, wv_ref, nrm_ref, k_ref, v_ref):
    nrm = nrm_ref[...]
    kr = kr_ref[...]
    one = jnp.ones((1, LANES), F32)
    _kv_heads(ckv_ref[...].astype(BF16), kr, kr, wk_ref, wv_ref, nrm[2:3], nrm[3:4],
              one, jnp.zeros((1, LANES), F32), k_ref, v_ref)


def ctx_kv(ckv, kr, wk, wv, nrm, tm):
    t = ckv.shape[0]
    row = lambda i: (i, 0)
    hrow = lambda i: (0, i, 0)
    full3 = lambda i: (0, 0, 0)
    nh = N_HEADS
    return pl.pallas_call(
        _ctx_kv_kernel,
        out_shape=(jax.ShapeDtypeStruct((nh, t, HEAD_PAD), BF16), jax.ShapeDtypeStruct((nh // 2, t, LANES), BF16)),
        grid=(t // tm,),
        in_specs=[pl.BlockSpec((tm, ckv.shape[1]), row), pl.BlockSpec((tm, LANES), row),
                  pl.BlockSpec(wk.shape, full3), pl.BlockSpec(wv.shape, full3),
                  pl.BlockSpec((8, LANES), lambda i: (0, 0))],
        out_specs=(pl.BlockSpec((nh, tm, HEAD_PAD), hrow), pl.BlockSpec((nh // 2, tm, LANES), hrow)),
        compiler_params=_cparams(("parallel",)),
        name="ctx_kv",
    )(ckv, kr, wk, wv, nrm)


def _attn_kernel(*refs, with_ctx, kblk):
    if with_ctx:
        q_ref, k_ref, v_ref, kc_ref, vc_ref, o_ref = refs
    else:
        q_ref, k_ref, v_ref, o_ref = refs
    seq = k_ref.shape[1]
    blocks = [(kc_ref, vc_ref, 0, kc_ref.shape[1])] if with_ctx else []
    blocks += [(k_ref, v_ref, b0, kblk) for b0 in range(0, seq, kblk)]
    items = [(j, blk) for j in range(2) for blk in blocks]

    def scores(item):
        j, (kr, _, b0, n) = item
        return _dot_nt(q_ref[j], kr[j, b0:b0 + n, :])

    outs = []
    m = l = acc = None
    s_next = scores(items[0])
    for idx, (j, (_, vr, b0, n)) in enumerate(items):
        s = s_next
        if idx + 1 < len(items):
            s_next = scores(items[idx + 1])
        mb = jnp.max(s, axis=-1, keepdims=True)
        m_new = mb if m is None else jnp.maximum(m, mb)
        p = jnp.exp(s - m_new)
        pv = _dot(p.astype(BF16), vr[0, b0:b0 + n, :])
        ps = jnp.sum(p, axis=-1, keepdims=True)
        if m is None:
            l, acc = ps, pv
        else:
            a = jnp.exp(m - m_new)
            l, acc = a * l + ps, a * acc + pv
        m = m_new
        if idx + 1 == len(items) or items[idx + 1][0] != j:
            outs.append(acc / l)
            m = l = acc = None
    lane = lax.broadcasted_iota(jnp.int32, outs[0].shape, 1)
    o_ref[...] = jnp.where(lane < V_DIM, outs[0], outs[1]).astype(BF16)


def attention(q, k, v2, row0, n_b, seq, tq, kc=None, vc2=None, ctx_len=0, kblk=1024):
    assert row0 % seq == 0 and seq % tq == 0
    nq = seq // tq
    qb0 = row0 // tq
    kb0 = row0 // seq
    with_ctx = kc is not None
    in_specs = [pl.BlockSpec((2, tq, HEAD_PAD), lambda b, hp, qi: (hp, qb0 + b * nq + qi, 0)),
                pl.BlockSpec((2, seq, HEAD_PAD), lambda b, hp, qi: (hp, kb0 + b, 0)),
                pl.BlockSpec((1, seq, LANES), lambda b, hp, qi: (hp, kb0 + b, 0))]
    args = [q, k, v2]
    if with_ctx:
        in_specs += [pl.BlockSpec((2, ctx_len, HEAD_PAD), lambda b, hp, qi: (hp, b, 0)),
                     pl.BlockSpec((1, ctx_len, LANES), lambda b, hp, qi: (hp, b, 0))]
        args += [kc, vc2]
    return pl.pallas_call(
        functools.partial(_attn_kernel, with_ctx=with_ctx, kblk=min(seq, kblk)),
        out_shape=jax.ShapeDtypeStruct((n_b * seq, (N_HEADS // 2) * LANES), BF16),
        grid=(n_b, N_HEADS // 2, nq),
        in_specs=in_specs,
        out_specs=pl.BlockSpec((tq, LANES), lambda b, hp, qi: (b * nq + qi, hp)),
        compiler_params=_cparams(("parallel", "parallel", "parallel")),
        name="attn_ctx" if with_ctx else "attn",
    )(*args)


def _proj_res_kernel(*refs, lay, n_y, gate_row):
    y_refs = refs[:n_y]
    mod_ref, o_ref, w_ref, out_ref = refs[n_y:]
    g = mod_ref[...][gate_row:gate_row + 1]
    out_ref[...] = lay.stream_load(y_refs) + g * _dot(o_ref[...], w_ref[...])


def proj_residual(lay, y, mod, o, w, gate_row):
    ys = _as_list(y)
    t, d = lay.t, ys[0].shape[1]
    tm = lay.tm
    row = lambda i: (i, 0)
    return pl.pallas_call(
        functools.partial(_proj_res_kernel, lay=lay, n_y=len(ys), gate_row=gate_row),
        out_shape=jax.ShapeDtypeStruct((t, d), F32),
        grid=(lay.n_tiles,),
        in_specs=lay.stream_specs(d, len(ys) == 2) + [
            pl.BlockSpec((None, 8, d), lambda i: (lay.seg(i), 0, 0)),
            pl.BlockSpec((tm, o.shape[1]), row), pl.BlockSpec(w.shape, lambda i: (0, 0))],
        out_specs=pl.BlockSpec((tm, d), row),
        input_output_aliases={0: 0} if len(ys) == 1 else {},
        compiler_params=_cparams(("parallel",)),
        name="proj_res",
    )(*ys, mod, o, w)


def _ssm_in_kernel(y_ref, mod_ref, nw_ref, w_ref, u_ref, us_ref, ug_ref):
    m = mod_ref[...]
    h = _norm_mod(y_ref[...], nw_ref[...], m[0:1], m[1:2])
    z = _dot(h.astype(BF16), w_ref[...])
    n_lt = us_ref.shape[0]
    for j in range(n_lt):
        us_ref[j] = z[:, j * LANES:(j + 1) * LANES]
    n_ch = us_ref.shape[1] // SSM_CHUNK
    kk = SSM_GROUP
    gpt = LANES // kk
    for t in range(SSM_CHUNK):
        for j in range(n_lt):
            piece = us_ref[j, pl.ds(t, n_ch, stride=SSM_CHUNK), :]
            for gl in range(gpt):
                ug_ref[j * gpt + gl, :, t * kk:(t + 1) * kk] = piece[:, gl * kk:(gl + 1) * kk]
    u_ref[...] = ug_ref[...].astype(BF16)


def ssm_in(lay, y, mod, nw, w_in):
    t, d = y.shape
    tm = lay.tm
    n = w_in.shape[1]
    g = n // SSM_GROUP
    n_ch = tm // SSM_CHUNK
    qk = SSM_CHUNK * SSM_GROUP
    row = lambda i: (i, 0)
    return pl.pallas_call(
        _ssm_in_kernel,
        out_shape=jax.ShapeDtypeStruct((g, t // SSM_CHUNK, qk), BF16),
        grid=(lay.n_tiles,),
        in_specs=[pl.BlockSpec((tm, d), row), pl.BlockSpec((None, 8, d), lambda i: (lay.seg(i), 0, 0)),
                  pl.BlockSpec((1, d), lambda i: (0, 0)), pl.BlockSpec((d, n), lambda i: (0, 0))],
        out_specs=pl.BlockSpec((g, n_ch, qk), lambda i: (0, i, 0)),
        scratch_shapes=[pltpu.VMEM((n // LANES, tm, LANES), F32), pltpu.VMEM((g, n_ch, qk), F32)],
        compiler_params=_cparams(("parallel",)),
        name="ssm_in",
    )(y, mod, nw, w_in)


def _ssm_prep_kernel(ar_ref, ai_ref, ld_ref, arc_ref, aic_ref, ldc_ref, btr_ref, bti_ref, ctr_ref, cti_ref, dsk_ref,
                     mt_ref, ett_ref, ft_ref, coef_ref):
    q = SSM_CHUNK
    k = SSM_GROUP
    p = SSM_STATE
    qk = q * k
    fwd = pl.program_id(0) == 0
    dt = jnp.exp(ld_ref[...])
    are = ar_ref[...]
    aim = ai_ref[...]
    mag = jnp.exp(dt * are)
    abr = mag * jnp.cos(dt * aim)
    abi = mag * jnp.sin(dt * aim)
    den = are * are + aim * aim
    nr = abr - 1.0
    cf_re = (nr * are + abi * aim) / den
    cf_im = (abi * are - nr * aim) / den
    btr = btr_ref[...]
    bti = bti_ref[...]
    bbt_re = cf_re * btr - cf_im * bti
    bbt_im = cf_re * bti + cf_im * btr
    bbt_re_t = jnp.concatenate([bbt_re] * q, axis=0)
    bbt_im_t = jnp.concatenate([bbt_im] * q, axis=0)
    s_idx = (lax.broadcasted_iota(jnp.int32, (qk, 1), 0) // k).astype(F32)
    pw = jnp.where(fwd, (q - 1.0) - s_idx, s_idx)
    pm = jnp.exp(pw * dt * are)
    pr = pm * jnp.cos(pw * dt * aim)
    pi = pm * jnp.sin(pw * dt * aim)
    et_re = pr * bbt_re_t - pi * bbt_im_t
    et_im = pr * bbt_im_t + pi * bbt_re_t
    ett_ref[...] = jnp.concatenate([et_re, et_im, et_im, et_re], axis=1).astype(BF16)
    qm = jnp.exp(q * dt * are)
    aq_re = qm * jnp.cos(q * dt * aim)
    aq_im = qm * jnp.sin(q * dt * aim)
    c1 = jnp.concatenate([aq_re, aq_re], axis=1)
    c2 = jnp.concatenate([-aq_im, aq_im], axis=1)
    coef_ref[...] = jnp.concatenate([c1, c2, jnp.zeros((6, 2 * p), F32)], axis=0)
    dtc = jnp.exp(ldc_ref[...])
    arec = arc_ref[...]
    aimc = aic_ref[...]
    tau = (lax.broadcasted_iota(jnp.int32, (1, qk), 1) // k).astype(F32)
    ctr = ctr_ref[...]
    cti = cti_ref[...]
    tau1 = jnp.where(fwd, tau + 1.0, q - tau)
    m1 = jnp.exp(tau1 * dtc * arec)
    p1r = m1 * jnp.cos(tau1 * dtc * aimc)
    p1i = m1 * jnp.sin(tau1 * dtc * aimc)
    ft_ref[...] = jnp.concatenate([ctr * p1r - cti * p1i, -(ctr * p1i + cti * p1r)], axis=0).astype(BF16)
    tau0 = jnp.where(fwd, tau, (q - 1.0) - tau)
    m0 = jnp.exp(tau0 * dtc * arec)
    p0r = m0 * jnp.cos(tau0 * dtc * aimc)
    p0i = m0 * jnp.sin(tau0 * dtc * aimc)
    left_re = ctr * p0r - cti * p0i
    left_im = ctr * p0i + cti * p0r
    kt = _dot_hi(bbt_re, left_re) - _dot_hi(bbt_im, left_im)
    lane = lax.broadcasted_iota(jnp.int32, (k, qk), 1)
    rowi = lax.broadcasted_iota(jnp.int32, (k, qk), 0)
    kt_f = kt + jnp.where(lane == rowi, dsk_ref[...], 0.0)
    rows_f, rows_b = [], []
    for s in range(q):
        rows_f.append(kt_f if s == 0 else jnp.where(lane >= s * k, pltpu.roll(kt_f, s * k, 1), 0.0))
        sh = ((s + 1) * k) % qk
        rows_b.append(kt if sh == 0 else jnp.where(lane < (s + 1) * k, pltpu.roll(kt, sh, 1), 0.0))
    mt = jnp.where(fwd, jnp.concatenate(rows_f, axis=0), jnp.concatenate(rows_b, axis=0))
    mt_ref[...] = mt.astype(BF16)


def ssm_prep(a_re, a_im, log_dt, b_re, b_im, c_re, c_im, d_skip):
    nd, g, p = a_re.shape
    k = b_re.shape[-1]
    qk = SSM_CHUNK * k
    ld = jnp.broadcast_to(log_dt[..., None], (nd, g, p))
    dsk = jnp.broadcast_to(jnp.tile(d_skip.reshape(g, 1, k), (1, 1, SSM_CHUNK))[None], (nd, g, 1, qk))
    rowv = lambda x: x.reshape(nd, g, 1, p)
    colv = lambda x: x.reshape(nd, g, p, 1)
    bt = lambda x: jnp.swapaxes(x, -1, -2)
    ct = lambda x: jnp.tile(jnp.swapaxes(x, -1, -2), (1, 1, 1, SSM_CHUNK))
    spec = lambda r, c: pl.BlockSpec((None, None, r, c), lambda d, j: (d, j, 0, 0))
    return pl.pallas_call(
        _ssm_prep_kernel,
        out_shape=(jax.ShapeDtypeStruct((nd, g, qk, qk), BF16), jax.ShapeDtypeStruct((nd, g, qk, 4 * p), BF16),
                   jax.ShapeDtypeStruct((nd, g, 2 * p, qk), BF16), jax.ShapeDtypeStruct((nd, g, 8, 2 * p), F32)),
        grid=(nd, g),
        in_specs=[spec(1, p), spec(1, p), spec(1, p), spec(p, 1), spec(p, 1), spec(p, 1),
                  spec(k, p), spec(k, p), spec(p, qk), spec(p, qk), spec(1, qk)],
        out_specs=(spec(qk, qk), spec(qk, 4 * p), spec(2 * p, qk), spec(8, 2 * p)),
        compiler_params=_cparams(("parallel", "parallel")),
        name="ssm_prep",
    )(rowv(a_re), rowv(a_im), rowv(ld), colv(a_re), colv(a_im), colv(ld), bt(b_re), bt(b_im), ct(c_re), ct(c_im), dsk)


def _ssm_scan_kernel(u_ref, mt_ref, ett_ref, ft_ref, coef_ref, x0_ref, x0s_ref, y_ref, hfin_ref,
                     sx_ref, sxs_ref, hp_ref, *, gpb, ctx_b, ctx_chunks, lat_b, lat_chunks, rchunk):
    w = 2 * SSM_STATE
    n_rows = u_ref.shape[1]
    ctx_rows = ctx_b * ctx_chunks
    for d in range(2):
        for g in range(gpb):
            for r0 in range(0, n_rows, rchunk):
                ss = _dot(u_ref[g, r0:r0 + rchunk, :], ett_ref[d, g])
                sx_ref[g, r0:r0 + rchunk, :] = ss[:, :w]
                sxs_ref[g, r0:r0 + rchunk, :] = ss[:, w:]
        c1 = [coef_ref[d, g][0:1] for g in range(gpb)]
        c2 = [coef_ref[d, g][1:2] for g in range(gpb)]

        def make_body(base, nb, nc):
            def body(i, carry):
                c = i if d == 0 else nc - 1 - i
                rows = pl.ds(base + c, nb, stride=nc)
                out = []
                for g in range(gpb):
                    x, xs = carry[g]
                    hp_ref[g, rows, :] = x
                    xn = c1[g] * x + c2[g] * xs + sx_ref[g, rows, :]
                    xsn = c1[g] * xs - c2[g] * x + sxs_ref[g, rows, :]
                    out.append((xn, xsn))
                return tuple(out)
            return body

        z = jnp.zeros((ctx_b, w), F32)
        fin = lax.fori_loop(0, ctx_chunks, make_body(0, ctx_b, ctx_chunks), tuple((z, z) for _ in range(gpb)))
        hfin_ref[d] = jnp.concatenate([fin[g][0] for g in range(gpb)], axis=1)
        init = tuple((x0_ref[d][:, g * w:(g + 1) * w], x0s_ref[d][:, g * w:(g + 1) * w]) for g in range(gpb))
        lax.fori_loop(0, lat_chunks, make_body(ctx_rows, lat_b, lat_chunks), init, unroll=4)
        for g in range(gpb):
            for r0 in range(0, n_rows, rchunk):
                y = _dot(u_ref[g, r0:r0 + rchunk, :], mt_ref[d, g])
                y = y + _dot(hp_ref[g, r0:r0 + rchunk, :].astype(BF16), ft_ref[d, g])
                if d == 0:
                    y_ref[g, r0:r0 + rchunk, :] = y
                else:
                    y_ref[g, r0:r0 + rchunk, :] += y


def ssm_scan(u, mt, ett, ft, coef, x0, x0s, ctx_b, ctx_chunks, lat_b, lat_chunks, gpb=4):
    g, n_rows, qk = u.shape
    w = 2 * SSM_STATE
    rchunk = math.gcd(n_rows, 512)
    assert n_rows == ctx_b * ctx_chunks + lat_b * lat_chunks
    ublk = pl.BlockSpec((gpb, n_rows, qk), lambda j: (j, 0, 0))
    blk = lambda r, c: pl.BlockSpec((2, gpb, r, c), lambda j: (0, j, 0, 0))
    vec = lambda r: pl.BlockSpec((2, r, gpb * w), lambda j: (0, 0, j))
    return pl.pallas_call(
        functools.partial(_ssm_scan_kernel, gpb=gpb, ctx_b=ctx_b, ctx_chunks=ctx_chunks, lat_b=lat_b,
                          lat_chunks=lat_chunks, rchunk=rchunk),
        out_shape=(jax.ShapeDtypeStruct((g, n_rows, qk), F32), jax.ShapeDtypeStruct((2, ctx_b, g * w), F32)),
        grid=(g // gpb,),
        in_specs=[ublk, blk(qk, qk), blk(qk, 2 * w), blk(w, qk), blk(8, w), vec(lat_b), vec(lat_b)],
        out_specs=(ublk, vec(ctx_b)),
        scratch_shapes=[pltpu.VMEM((gpb, n_rows, w), F32), pltpu.VMEM((gpb, n_rows, w), F32),
                        pltpu.VMEM((gpb, n_rows, w), F32)],
        compiler_params=_cparams(("parallel",)),
        name="ssm_scan",
    )(u, mt, ett, ft, coef, x0, x0s)


def _ssm_out_kernel(y_ref, mod_ref, yc_ref, w_ref, out_ref, s_ref, *, d_model):
    kk = SSM_GROUP
    n_ch = yc_ref.shape[1]
    gpt = LANES // kk
    n_lt = s_ref.shape[0]
    for t in range(SSM_CHUNK):
        for j in range(n_lt):
            piece = jnp.concatenate([yc_ref[j * gpt + gl, :, t * kk:(t + 1) * kk] for gl in range(gpt)], axis=1)
            s_ref[j, pl.ds(t, n_ch, stride=SSM_CHUNK), :] = piece
    s = jnp.concatenate([s_ref[j] for j in range(n_lt)], axis=1)
    a = jax.nn.gelu(s, approximate=True).astype(BF16)
    z = _dot(a, w_ref[...])
    gate = mod_ref[...][2:3]
    out_ref[...] = y_ref[...] + gate * (z[:, :d_model] * jax.nn.sigmoid(z[:, d_model:]))


def ssm_out(lay, y, mod, ych, w_glu):
    t, d = y.shape
    tm = lay.tm
    g, _, qk = ych.shape
    n_ch = tm // SSM_CHUNK
    row = lambda i: (i, 0)
    return pl.pallas_call(
        functools.partial(_ssm_out_kernel, d_model=d),
        out_shape=jax.ShapeDtypeStruct((t, d), F32),
        grid=(lay.n_tiles,),
        in_specs=[pl.BlockSpec((tm, d), row), pl.BlockSpec((None, 8, d), lambda i: (lay.seg(i), 0, 0)),
                  pl.BlockSpec((g, n_ch, qk), lambda i: (0, i, 0)),
                  pl.BlockSpec(w_glu.shape, lambda i: (0, 0))],
        out_specs=pl.BlockSpec((tm, d), row),
        scratch_shapes=[pltpu.VMEM((g * SSM_GROUP // LANES, tm, LANES), F32)],
        input_output_aliases={0: 0},
        compiler_params=_cparams(("parallel",)),
        name="ssm_out",
    )(y, mod, ych, w_glu)


def s5_layer(lay, y, mod, nw, state_re, state_im, w_in, a_re, a_im, log_dt, b_re, b_im, c_re, c_im, d_skip, w_glu):
    uch = ssm_in(lay, y, mod, nw, w_in.astype(BF16))
    g = a_re.shape[1]
    p = SSM_STATE
    mt, ett, ft, coef = ssm_prep(a_re, a_im, log_dt, b_re, b_im, c_re, c_im, d_skip)
    st = jnp.concatenate([state_re, state_im], axis=-1)
    sts = jnp.concatenate([state_im, state_re], axis=-1)
    x0 = jnp.transpose(st, (1, 0, 2, 3)).reshape(2, lay.n_lat_b, g * 2 * p)
    x0s = jnp.transpose(sts, (1, 0, 2, 3)).reshape(2, lay.n_lat_b, g * 2 * p)
    ych, hfin = ssm_scan(uch, mt, ett, ft, coef, x0, x0s, lay.n_ctx_b, lay.ctx_len // SSM_CHUNK,
                         lay.n_lat_b, lay.lat_len // SSM_CHUNK)
    y = ssm_out(lay, y, mod, ych, w_glu.astype(BF16))
    hf = hfin.reshape(2, lay.n_ctx_b, g, 2, p)
    new_re = jnp.transpose(hf[:, :, :, 0], (1, 0, 2, 3))
    new_im = jnp.transpose(hf[:, :, :, 1], (1, 0, 2, 3))
    return y, new_re, new_im


def _route_kernel(y_ref, mod_ref, nw_ref, wr_ref, br_ref, tri_ref, h_ref, ri_ref, pc_ref, *, n_exp):
    m = mod_ref[...]
    h = _norm_mod(y_ref[...], nw_ref[...], m[3:4], m[4:5])
    h_hi = h.astype(BF16)
    h_ref[...] = h_hi
    h_lo = (h - h_hi.astype(F32)).astype(BF16)
    wr = wr_ref[...]
    lt = _dot_nt(wr, h_hi)
    logits = lt[:n_exp] + lt[n_exp:] + _dot_nt(wr[:n_exp], h_lo)
    scores = jax.nn.sigmoid(logits)
    sel = scores + br_ref[...]
    epg = EXPERTS_PER_GROUP
    row = lambda x, e: x[e:e + 1, :]
    gscore = []
    for g in range(N_EXPERT_GROUPS):
        a, b, c, d = (row(sel, g * epg + j) for j in range(epg))
        m1, n1, m2, n2 = jnp.maximum(a, b), jnp.minimum(a, b), jnp.maximum(c, d), jnp.minimum(c, d)
        gscore.append(jnp.maximum(m1, m2) + jnp.maximum(jnp.minimum(m1, m2), jnp.maximum(n1, n2)))
    best = gscore[0]
    gi = jnp.zeros_like(best, dtype=jnp.int32)
    for g in range(1, N_EXPERT_GROUPS):
        better = gscore[g] > best
        gi = jnp.where(better, g, gi)
        best = jnp.where(better, gscore[g], best)

    def pick(x, j):
        out = row(x, j)
        for g in range(1, N_EXPERT_GROUPS):
            out = jnp.where(gi == g, row(x, g * epg + j), out)
        return out

    sv = [pick(sel, j) for j in range(epg)]
    cv = [pick(scores, j) for j in range(epg)]
    b1, i1, w1 = sv[0], jnp.zeros_like(gi), cv[0]
    for j in range(1, epg):
        better = sv[j] > b1
        i1 = jnp.where(better, j, i1)
        w1 = jnp.where(better, cv[j], w1)
        b1 = jnp.where(better, sv[j], b1)
    neg = jnp.full_like(b1, -jnp.inf)
    b2, i2, w2 = neg, jnp.zeros_like(gi), jnp.zeros_like(w1)
    for j in range(epg):
        better = (i1 != j) & (sv[j] > b2)
        i2 = jnp.where(better, j, i2)
        w2 = jnp.where(better, cv[j], w2)
        b2 = jnp.where(better, sv[j], b2)
    tot = w1 + w2
    e1 = gi * epg + i1
    e2 = gi * epg + i2
    eid = lax.broadcasted_iota(jnp.int32, logits.shape, 0)
    m1h = eid == e1
    m2h = eid == e2
    mc = jnp.where(m1h | m2h, 1.0, 0.0)
    pref = _dot(mc.astype(BF16), tri_ref[...])
    cnt = jnp.sum(mc, axis=1, keepdims=True)
    pc_al = jnp.ceil(cnt * (1.0 / SEG_ALIGN))
    pcb = jnp.broadcast_to(pc_al, (n_exp, LANES))
    er = lax.broadcasted_iota(jnp.int32, (n_exp, n_exp), 0)
    ec = lax.broadcasted_iota(jnp.int32, (n_exp, n_exp), 1)
    lower = jnp.where(ec < er, 1.0, 0.0).astype(BF16)
    seg = _dot(lower, pcb.astype(BF16))[:, 0:1] * SEG_ALIGN
    slot = seg + pref
    pos1 = jnp.sum(jnp.where(m1h, slot, 0.0), axis=0, keepdims=True)
    pos2 = jnp.sum(jnp.where(m2h, slot, 0.0), axis=0, keepdims=True)
    zero = jnp.zeros_like(pos1)
    ri_ref[...] = jnp.concatenate([pos1, pos2, w1 / tot, w2 / tot, zero, zero, zero, zero], axis=0)
    pc_ref[...] = pcb * SEG_ALIGN


def moe_route(lay, y, mod, nw, wr_t, br, tri):
    t, d = y.shape
    tm = lay.tm
    n_exp = br.shape[0]
    return pl.pallas_call(
        functools.partial(_route_kernel, n_exp=n_exp),
        out_shape=(jax.ShapeDtypeStruct((t, d), BF16), jax.ShapeDtypeStruct((8, t), F32),
                   jax.ShapeDtypeStruct((lay.n_tiles, n_exp, LANES), F32)),
        grid=(lay.n_tiles,),
        in_specs=[pl.BlockSpec((tm, d), lambda i: (i, 0)),
                  pl.BlockSpec((None, 8, d), lambda i: (lay.seg(i), 0, 0)),
                  pl.BlockSpec((1, d), lambda i: (0, 0)), pl.BlockSpec(wr_t.shape, lambda i: (0, 0)),
                  pl.BlockSpec((n_exp, 1), lambda i: (0, 0)), pl.BlockSpec((tm, tm), lambda i: (0, 0))],
        out_specs=(pl.BlockSpec((tm, d), lambda i: (i, 0)), pl.BlockSpec((8, tm), lambda i: (0, i)),
                   pl.BlockSpec((None, n_exp, LANES), lambda i: (i, 0, 0))),
        compiler_params=_cparams(("parallel",)),
        name="moe_route",
    )(y, mod, nw, wr_t, br, tri)


def _chunk_copy(gtab_ref, tile, c, local_ref, sl, global_ref, sem, to_global):
    max_ch = local_ref.shape[1] // SEG_ALIGN
    g = pl.multiple_of(gtab_ref[tile * max_ch + c], SEG_ALIGN)
    loc = local_ref.at[sl, pl.ds(pl.multiple_of(c * SEG_ALIGN, SEG_ALIGN), SEG_ALIGN)]
    glob = global_ref.at[pl.ds(g, SEG_ALIGN)]
    return pltpu.make_async_copy(loc, glob, sem.at[sl]) if to_global else pltpu.make_async_copy(glob, loc, sem.at[sl])


def _pieces(n, sizes):
    for size in sizes:
        yield (n & size) != 0, n & (-2 * size), size


def _moe_permute_kernel(gtab_ref, nch_ref, goff_ref, gn_ref, h_ref, ri_ref, xs_ref, ys_ref, zp_ref, sem, gsem,
                        *, n_exp, gap_sizes, rchunk):
    i = pl.program_id(0)
    last = pl.num_programs(0) - 1
    slot = i % 2
    ri = ri_ref[...]
    pos1 = ri[0:1]
    pos2 = ri[1:2]
    h = h_ref[...]
    n_loc = ys_ref.shape[1]
    for r0 in range(0, n_loc, rchunk):
        r = (lax.broadcasted_iota(jnp.int32, (rchunk, h.shape[0]), 0) + r0).astype(F32)
        p = jnp.where(pos1 == r, 1.0, 0.0) + jnp.where(pos2 == r, 1.0, 0.0)
        ys_ref[slot, r0:r0 + rchunk, :] = _dot(p.astype(BF16), h).astype(BF16)

    def copies(tile, sl, act):
        @pl.loop(0, nch_ref[tile])
        def _(c):
            act(_chunk_copy(gtab_ref, tile, c, ys_ref, sl, xs_ref, sem, True))

    copies(i, slot, lambda cp: cp.start())

    @pl.when(i > 0)
    def _():
        copies(i - 1, 1 - slot, lambda cp: cp.wait())

    @pl.when(i == last)
    def _():
        copies(i, slot, lambda cp: cp.wait())
        zp_ref[...] = jnp.zeros_like(zp_ref)

        def gaps():
            for e in range(n_exp):
                g0 = goff_ref[e]
                for present, above, size in _pieces(gn_ref[e], gap_sizes):
                    yield present, pltpu.make_async_copy(
                        zp_ref.at[pl.ds(0, size)], xs_ref.at[pl.ds(pl.multiple_of(g0 + above, SEG_ALIGN), size)],
                        gsem.at[e])

        for present, cp in gaps():
            pl.when(present)(cp.start)
        for present, cp in gaps():
            pl.when(present)(cp.wait)

        zrows = zp_ref.shape[0]
        tail0 = goff_ref[n_exp]
        tail = lambda c: pltpu.make_async_copy(
            zp_ref, xs_ref.at[pl.ds(pl.multiple_of(tail0 + c * zrows, zrows), zrows)], gsem.at[0])

        @pl.loop(0, gn_ref[n_exp])
        def _(c):
            tail(c).start()

        @pl.loop(0, gn_ref[n_exp])
        def _(c):
            tail(c).wait()


def moe_permute(lay, h, rinfo, gtab, nch, gap_off, gap_n, n_sorted, n_loc, n_exp, tmm):
    t, d = h.shape
    tm = lay.tm
    gap_sizes = [s for s in (1 << b for b in range(12, 3, -1)) if s < tmm]
    return pl.pallas_call(
        functools.partial(_moe_permute_kernel, n_exp=n_exp, gap_sizes=gap_sizes, rchunk=256),
        out_shape=jax.ShapeDtypeStruct((n_sorted, d), BF16),
        grid_spec=pltpu.PrefetchScalarGridSpec(
            num_scalar_prefetch=4, grid=(lay.n_tiles,),
            in_specs=[pl.BlockSpec((tm, d), lambda i, *_: (i, 0)), pl.BlockSpec((8, tm), lambda i, *_: (0, i))],
            out_specs=pl.BlockSpec(memory_space=pl.ANY),
            scratch_shapes=[pltpu.VMEM((2, n_loc, d), BF16), pltpu.VMEM((gap_sizes[0], d), BF16),
                            pltpu.SemaphoreType.DMA((2,)), pltpu.SemaphoreType.DMA((n_exp,))]),
        compiler_params=_cparams(("arbitrary",)),
        name="moe_permute",
    )(gtab, nch, gap_off, gap_n, h, rinfo)


def _moe_expert_kernel(te_ref, tv_ref, tf_ref, xb_ref, x_ref, wg_ref, wu_ref, wd_ref, z_ref, wgb_ref, wub_ref, wdb_ref):
    del te_ref, xb_ref
    r = pl.program_id(0)

    @pl.when(tf_ref[r] == 1)
    def _():
        wgb_ref[...] = wg_ref[...].astype(BF16)
        wub_ref[...] = wu_ref[...].astype(BF16)
        wdb_ref[...] = wd_ref[...].astype(BF16)

    @pl.when(tv_ref[r] > 0)
    def _():
        x = x_ref[...]
        a = _dot(x, wgb_ref[...])
        b = _dot(x, wub_ref[...])
        hid = (a * jax.nn.sigmoid(a)) * b
        z_ref[...] = _dot(hid.astype(BF16), wdb_ref[...]).astype(BF16)

    @pl.when(tv_ref[r] == 0)
    def _():
        z_ref[...] = jnp.zeros_like(z_ref)


def moe_experts(xs, tile_expert, tile_valid, tile_first, x_block, layer, wg, wu, wd, tmm):
    n_sorted, d = xs.shape
    f = wg.shape[-1]
    wmap = lambda r, te, tv, tf, xb: (layer, te[r], 0, 0)
    return pl.pallas_call(
        _moe_expert_kernel,
        out_shape=jax.ShapeDtypeStruct((n_sorted, d), BF16),
        grid_spec=pltpu.PrefetchScalarGridSpec(
            num_scalar_prefetch=4, grid=(n_sorted // tmm,),
            in_specs=[pl.BlockSpec((tmm, d), lambda r, te, tv, tf, xb: (xb[r], 0)),
                      pl.BlockSpec((None, None, d, f), wmap), pl.BlockSpec((None, None, d, f), wmap),
                      pl.BlockSpec((None, None, f, d), wmap)],
            out_specs=pl.BlockSpec((tmm, d), lambda r, te, tv, tf, xb: (r, 0)),
            scratch_shapes=[pltpu.VMEM((d, f), BF16), pltpu.VMEM((d, f), BF16), pltpu.VMEM((f, d), BF16)]),
        compiler_params=_cparams(("arbitrary",)),
        name="moe_experts",
    )(tile_expert, tile_valid, tile_first, x_block, xs, wg, wu, wd)


def _moe_combine_kernel(gtab_ref, nch_ref, y_ref, mod_ref, rit_ref, zs_ref, *rest, lay):
    out_refs, (zt_ref, sem) = rest[:-2], rest[-2:]
    i = pl.program_id(0)
    slot = i % 2
    max_ch = zt_ref.shape[1] // SEG_ALIGN

    def copies(tile, sl, act):
        @pl.loop(0, nch_ref[tile])
        def _(c):
            act(_chunk_copy(gtab_ref, tile, c, zt_ref, sl, zs_ref, sem, False))

    def fetch(tile, sl):
        @pl.loop(nch_ref[tile], max_ch)
        def _(c):
            zt_ref[sl, pl.ds(pl.multiple_of(c * SEG_ALIGN, SEG_ALIGN), SEG_ALIGN), :] = jnp.zeros(
                (SEG_ALIGN, zt_ref.shape[2]), BF16)
        copies(tile, sl, lambda cp: cp.start())

    @pl.when(i == 0)
    def _():
        fetch(i, slot)

    @pl.when(i + 1 < pl.num_programs(0))
    def _():
        fetch(i + 1, 1 - slot)

    rit = rit_ref[...]
    n_loc = zt_ref.shape[1]
    r = lax.broadcasted_iota(jnp.int32, (rit.shape[0], n_loc), 1).astype(F32)
    pw = jnp.where(rit[:, 0:1] == r, rit[:, 2:3], 0.0) + jnp.where(rit[:, 1:2] == r, rit[:, 3:4], 0.0)
    pw = pw.astype(BF16)
    copies(i, slot, lambda cp: cp.wait())
    lay.stream_store(out_refs, y_ref[...] + mod_ref[...][5:6] * _dot(pw, zt_ref[slot]))


def moe_combine(lay, y, mod, rinfo_t, zs, gtab, nch, n_loc, pair_out):
    t, d = y.shape
    tm = lay.tm
    if pair_out:
        out_shape = (jax.ShapeDtypeStruct((lay.t_ctx, d), F32), jax.ShapeDtypeStruct((lay.t_lat, d), F32))
    else:
        out_shape = (jax.ShapeDtypeStruct((t, d), F32),)
    out = pl.pallas_call(
        functools.partial(_moe_combine_kernel, lay=lay),
        out_shape=out_shape,
        grid_spec=pltpu.PrefetchScalarGridSpec(
            num_scalar_prefetch=2, grid=(lay.n_tiles,),
            in_specs=[pl.BlockSpec((tm, d), lambda i, o, p: (i, 0)),
                      pl.BlockSpec((None, 8, d), lambda i, o, p: (lay.seg(i), 0, 0)),
                      pl.BlockSpec((tm, 8), lambda i, o, p: (i, 0)),
                      pl.BlockSpec(memory_space=pl.ANY)],
            out_specs=tuple(lay.stream_specs(d, pair_out)),
            scratch_shapes=[pltpu.VMEM((2, n_loc, d), BF16), pltpu.SemaphoreType.DMA((2,))]),
        input_output_aliases={} if pair_out else {2: 0},
        compiler_params=_cparams(("arbitrary",)),
        name="moe_combine",
    )(gtab, nch, y, mod, rinfo_t, zs)
    return out if pair_out else out[0]


def _round_up(x, m):
    return (x + m - 1) // m * m


def moe_layer(lay, y, mod, nw, wr_t, br, tri, layer, wg, wu, wd, tmm, pair_out=False):
    t, d = y.shape
    n_exp = br.shape[0]
    h, rinfo, pc = moe_route(lay, y, mod, nw, wr_t, br, tri)
    pc = pc[:, :, 0].astype(jnp.int32)
    n_loc = 2 * lay.tm + SEG_ALIGN * n_exp
    n_sorted = _round_up(2 * t + lay.n_tiles * n_exp * (SEG_ALIGN - 1) + n_exp * (tmm - 1), tmm)
    tot = jnp.sum(pc, axis=0)
    region = _round_up(tot, tmm)
    ends = jnp.cumsum(region)
    base = ends - region
    off = (base[None, :] + jnp.cumsum(pc, axis=0) - pc).reshape(-1)
    starts = jnp.arange(n_sorted // tmm, dtype=jnp.int32) * tmm
    te = jnp.minimum(jnp.sum(starts[:, None] >= ends[None, :], axis=1), n_exp - 1).astype(jnp.int32)
    tv = jnp.clip(tot[te] - (starts - base[te]), 0, tmm).astype(jnp.int32)
    tf = jnp.concatenate([jnp.ones((1,), jnp.int32), (te[1:] != te[:-1]).astype(jnp.int32)])
    xb = jnp.minimum(starts // tmm, jnp.maximum(ends[-1] // tmm - 1, 0)).astype(jnp.int32)
    seg_end = jnp.cumsum(pc, axis=1)
    lrow = jnp.arange(n_loc // SEG_ALIGN, dtype=jnp.int32) * SEG_ALIGN
    owner = jnp.minimum(jnp.sum(lrow[None, :, None] >= seg_end[:, None, :], axis=2), n_exp - 1)
    goff2 = off.reshape(lay.n_tiles, n_exp)
    gtab = (jnp.take_along_axis(goff2, owner, axis=1)
            + lrow[None, :] - jnp.take_along_axis(seg_end - pc, owner, axis=1)).astype(jnp.int32).reshape(-1)
    nch = (seg_end[:, -1] // SEG_ALIGN).astype(jnp.int32)
    assert tmm & (tmm - 1) == 0
    zrows = tmm // 2
    gap_off = jnp.concatenate([base + tot, ends[-1:]]).astype(jnp.int32)
    gap_n = jnp.concatenate([region - tot, (n_sorted - ends[-1:]) // zrows]).astype(jnp.int32)
    xs = moe_permute(lay, h, rinfo, gtab, nch, gap_off, gap_n, n_sorted, n_loc, n_exp, tmm)
    zs = moe_experts(xs, te, tv, tf, xb, layer, wg, wu, wd, tmm)
    return moe_combine(lay, y, mod, rinfo.T, zs, gtab, nch, n_loc, pair_out)


def _rope_perm():
    d = np.arange(QK_ROPE)
    return QK_NOPE + (d ^ (QK_ROPE // 4))


def _pad_cols(x, n):
    return jnp.pad(x, ((0, 0),) * (x.ndim - 1) + ((0, n - x.shape[-1]),))


def _mla_weights(w_q_a, w_q_b, w_kv_a, w_kv_b, q_norm, k_norm, kv_lora):
    perm = _rope_perm()
    nh = N_HEADS
    wkr = w_kv_a[:, kv_lora:]
    d = w_kv_a.shape[0]
    zl = jnp.zeros((d, QK_NOPE), F32)
    kr_blk = _pad_cols(jnp.concatenate([zl, wkr], axis=1), LANES)
    krp_blk = _pad_cols(jnp.concatenate([zl, wkr[:, perm - QK_NOPE]], axis=1), LANES)
    w_a = jnp.concatenate([w_q_a, w_kv_a[:, :kv_lora], kr_blk, krp_blk], axis=1).astype(BF16)
    wq = w_q_b.reshape(-1, nh, QK_DIM).transpose(1, 0, 2)
    wq_main = _pad_cols(wq, HEAD_PAD)
    wq_part = _pad_cols(jnp.concatenate([jnp.zeros_like(wq[..., :QK_NOPE]), wq[..., perm]], axis=-1), HEAD_PAD)
    wq_h = jnp.concatenate([wq_main, wq_part], axis=-1).astype(BF16)
    wkv = w_kv_b.reshape(-1, nh, QK_NOPE + V_DIM).transpose(1, 0, 2)
    wk_h = _pad_cols(wkv[..., :QK_NOPE], HEAD_PAD).astype(BF16)
    wv = wkv[..., QK_NOPE:]
    wv_h = jnp.concatenate([wv[0::2], wv[1::2]], axis=-1).astype(BF16)
    nrm = jnp.stack([_pad_cols(q_norm, LANES), _pad_cols(jnp.concatenate([jnp.zeros((QK_NOPE,), F32), q_norm[perm]]), LANES),
                     _pad_cols(k_norm, LANES), _pad_cols(jnp.concatenate([jnp.zeros((QK_NOPE,), F32), k_norm[perm]]), LANES)])
    nrm = jnp.pad(nrm, ((0, 4), (0, 0)))
    return w_a, wq_h, wk_h, wv_h, nrm


def _rope_tables(lat_len, tm):
    rows = lat_len // GRID_W
    t_row = np.repeat(np.arange(rows, dtype=np.float32), GRID_W)
    t_col = np.tile(np.arange(GRID_W, dtype=np.float32), rows)
    half = QK_ROPE // 2
    inv = (ROPE_BASE ** (-np.arange(0, half, 2, dtype=np.float32) / half)).astype(np.float32)
    ang = jnp.concatenate([jnp.asarray(t_row)[:, None] * inv, jnp.asarray(t_col)[:, None] * inv], axis=-1)
    cos, sin = jnp.cos(ang), jnp.sin(ang)
    nf = QK_ROPE // 4
    cos_l = jnp.concatenate([cos[:, :nf], cos[:, :nf], cos[:, nf:], cos[:, nf:]], axis=-1)
    sin_l = jnp.concatenate([-sin[:, :nf], sin[:, :nf], -sin[:, nf:], sin[:, nf:]], axis=-1)
    one = jnp.ones((lat_len, QK_NOPE), F32)
    cos_t = _pad_cols(jnp.concatenate([one, cos_l], axis=-1), LANES)
    cos_t = cos_t.at[:, QK_DIM:].set(1.0)
    sin_t = _pad_cols(jnp.concatenate([jnp.zeros((lat_len, QK_NOPE), F32), sin_l], axis=-1), LANES)
    cos_t = jnp.concatenate([jnp.ones((tm, LANES), F32), cos_t], axis=0)
    sin_t = jnp.concatenate([jnp.zeros((tm, LANES), F32), sin_t], axis=0)
    return cos_t, sin_t


def mla_layer(lay, y, mod, nw, cache_ckv, cache_krope, cos_t, sin_t, w_q_a, q_a_norm, w_q_b, w_kv_a, kv_a_norm,
              w_kv_b, q_norm, k_norm, w_o):
    q_lora = w_q_a.shape[1]
    kv_lora = kv_a_norm.shape[0]
    w_a, wq_h, wk_h, wv_h, nrm = _mla_weights(w_q_a, w_q_b, w_kv_a, w_kv_b, q_norm, k_norm, kv_lora)
    cq, ckv, kr, krp = mla_a(lay, y, mod, nw, w_a, q_a_norm.reshape(1, -1), kv_a_norm.reshape(1, -1), q_lora, kv_lora)
    q, k, v2 = mla_b(lay, cq, ckv, kr, krp, wq_h, wk_h, wv_h, nrm, cos_t, sin_t)
    past = cache_ckv.shape[1]
    ckr = jnp.pad(cache_krope.reshape(-1, QK_ROPE), ((0, 0), (QK_NOPE, LANES - QK_DIM)))
    kc, vc2 = ctx_kv(cache_ckv.reshape(-1, kv_lora), ckr, wk_h, wv_h, nrm, past)
    o_ctx = attention(q, k, v2, 0, lay.n_ctx_b, lay.ctx_len, min(lay.ctx_len, 256))
    o_lat = attention(q, k, v2, lay.t_ctx, lay.n_lat_b, lay.lat_len, min(lay.lat_len, 512), kc, vc2, past)
    o = jnp.concatenate([o_ctx, o_lat], axis=0)
    y = proj_residual(lay, y, mod, o, w_o.astype(BF16), 2)
    new_ckv = ckv[:lay.t_ctx].reshape(lay.n_ctx_b, lay.ctx_len, kv_lora)
    new_kr = kr[:lay.t_ctx, QK_NOPE:QK_DIM].reshape(lay.n_ctx_b, lay.ctx_len, QK_ROPE)
    return y, new_ckv, new_kr


def kernel(x_prompt, x_sample, c, cache_ckv, cache_krope, state_ssm_re, state_ssm_im, c_ctx, w_mod, b_mod, norm1_w, norm2_w, mla_w_q_a, mla_q_a_norm, mla_w_q_b, mla_w_kv_a, mla_kv_a_norm, mla_w_kv_b, mla_q_norm, mla_k_norm, mla_w_o, ssm_w_in, ssm_a_re, ssm_a_im, ssm_log_dt, ssm_b_re, ssm_b_im, ssm_c_re, ssm_c_im, ssm_d, ssm_w_glu, w_router, b_router, moe_w_gate, moe_w_up, moe_w_down):
    n_ctx_b, ctx_len, d = x_prompt.shape
    n_lat_b, lat_len, _ = x_sample.shape
    depth = w_mod.shape[0]
    lay = Layout(n_ctx_b, ctx_len, n_lat_b, lat_len, tm=256)
    assert n_lat_b + 1 <= 8
    y = (x_prompt.reshape(-1, d), x_sample.reshape(-1, d))
    cond8 = jnp.pad(jnp.concatenate([c_ctx[None, :], c], axis=0), ((0, 7 - n_lat_b), (0, 0)))
    mods = adaln_all(cond8, w_mod, b_mod)
    mods = jnp.pad(mods.reshape(depth, 8, 6, d), ((0, 0), (0, 0), (0, 2), (0, 0)))
    cos_t, sin_t = _rope_tables(lat_len, lay.tm)
    n_exp = b_router.shape[0]
    wr_hi = w_router.astype(BF16)
    wr_lo = (w_router - wr_hi.astype(F32)).astype(BF16)
    wr_t = jnp.concatenate([wr_hi.T, wr_lo.T], axis=0)
    br = b_router.reshape(n_exp, 1)
    lay_s = Layout(n_ctx_b, ctx_len, n_lat_b, lat_len, tm=512)
    tri = jnp.asarray(np.triu(np.ones((lay_s.tm, lay_s.tm), np.float32), k=1), BF16)
    ckv_out, kr_out, sre_out, sim_out = [], [], [], []
    for i in range(depth):
        j = i // 2
        mod = mods[i]
        nw1 = norm1_w[i].reshape(1, d)
        if i % 2 == 0:
            y, ckv_p, kr_p = mla_layer(lay, y, mod, nw1, cache_ckv[:, j], cache_krope[:, j], cos_t, sin_t,
                                       mla_w_q_a[j], mla_q_a_norm[j], mla_w_q_b[j], mla_w_kv_a[j], mla_kv_a_norm[j],
                                       mla_w_kv_b[j], mla_q_norm[j], mla_k_norm[j], mla_w_o[j])
            ckv_out.append(ckv_p)
            kr_out.append(kr_p)
        else:
            y, s_re, s_im = s5_layer(lay, y, mod, nw1, state_ssm_re[:, j], state_ssm_im[:, j], ssm_w_in[j],
                                     ssm_a_re[j], ssm_a_im[j], ssm_log_dt[j], ssm_b_re[j], ssm_b_im[j],
                                     ssm_c_re[j], ssm_c_im[j], ssm_d[j], ssm_w_glu[j])
            sre_out.append(s_re)
            sim_out.append(s_im)
        y = moe_layer(lay_s, y, mod, norm2_w[i].reshape(1, d), wr_t, br, tri, i, moe_w_gate, moe_w_up,
                      moe_w_down, tmm=512, pair_out=(i == depth - 1))
    yp = y[0].reshape(n_ctx_b, ctx_len, d)
    ys = y[1].reshape(n_lat_b, lat_len, d)
    return (yp, ys, jnp.stack(ckv_out, axis=1), jnp.stack(kr_out, axis=1),
            jnp.stack(sre_out, axis=1), jnp.stack(sim_out, axis=1))
```

```python
import functools
import math

import jax
import jax.numpy as jnp
import numpy as np
from jax import lax
from jax.experimental import pallas as pl
from jax.experimental.pallas import tpu as pltpu

F32 = jnp.float32
BF16 = jnp.bfloat16
EPS = 1e-6

GRID_W = 64
N_HEADS = 8
QK_NOPE = 64
QK_ROPE = 32
QK_DIM = QK_NOPE + QK_ROPE
V_DIM = 64
ROPE_BASE = 10000.0
SSM_GROUP = 16
SSM_STATE = 64
N_EXPERT_GROUPS = 4
EXPERTS_PER_GROUP = 4

LANES = 128
HEAD_PAD = LANES
SSM_CHUNK = 16
SEG_ALIGN = 16
VMEM_LIMIT = 48 * 1024 * 1024


def _cparams(sem, vmem=VMEM_LIMIT):
    return pltpu.CompilerParams(dimension_semantics=sem, vmem_limit_bytes=vmem)


def _dot(a, b):
    return jnp.dot(a, b, preferred_element_type=F32)


def _dot_nt(a, b):
    return lax.dot_general(a, b, (((1,), (1,)), ((), ())), preferred_element_type=F32)


def _dot_hi(a, b):
    return jnp.dot(a, b, preferred_element_type=F32, precision=lax.Precision.HIGHEST)


def _norm_mod(x, nw, shift, scale):
    ms = jnp.mean(x * x, axis=-1, keepdims=True)
    return (x * lax.rsqrt(ms + EPS) * nw) * (1.0 + scale) + shift


class Layout:
    def __init__(self, n_ctx_b, ctx_len, n_lat_b, lat_len, tm):
        self.n_ctx_b, self.ctx_len, self.n_lat_b, self.lat_len = n_ctx_b, ctx_len, n_lat_b, lat_len
        self.t_ctx = n_ctx_b * ctx_len
        self.t_lat = n_lat_b * lat_len
        self.t = self.t_ctx + self.t_lat
        self.tm = tm
        assert self.t_ctx % tm == 0 and lat_len % tm == 0
        self.ctx_tiles = self.t_ctx // tm
        self.lat_tiles_per_b = lat_len // tm
        self.n_tiles = self.t // tm

    def seg(self, i):
        return jnp.where(i < self.ctx_tiles, 0, 1 + (i - self.ctx_tiles) // self.lat_tiles_per_b)

    def rope_blk(self, i):
        return jnp.where(i < self.ctx_tiles, 0, 1 + (i - self.ctx_tiles) % self.lat_tiles_per_b)

    def stream_specs(self, d, pair):
        tm = self.tm
        if not pair:
            return [pl.BlockSpec((tm, d), lambda i, *_: (i, 0))]
        return [pl.BlockSpec((tm, d), lambda i, *_: (jnp.minimum(i, self.ctx_tiles - 1), 0)),
                pl.BlockSpec((tm, d), lambda i, *_: (jnp.maximum(i - self.ctx_tiles, 0), 0))]

    def stream_load(self, refs):
        if len(refs) == 1:
            return refs[0][...]
        return jnp.where(pl.program_id(0) < self.ctx_tiles, refs[0][...], refs[1][...])

    def stream_store(self, refs, val):
        if len(refs) == 1:
            refs[0][...] = val
            return
        i = pl.program_id(0)

        @pl.when(i < self.ctx_tiles)
        def _():
            refs[0][...] = val

        @pl.when(i >= self.ctx_tiles)
        def _():
            refs[1][...] = val


def _as_list(y):
    return list(y) if isinstance(y, (tuple, list)) else [y]


def _adaln_kernel(c_ref, w_ref, b_ref, o_ref):
    c = c_ref[...]
    s = c * jax.nn.sigmoid(c)
    o_ref[...] = _dot_hi(s, w_ref[...]) + b_ref[...]


def adaln_all(cond8, w_mod, b_mod, tn=1536):
    depth, d, n6 = w_mod.shape
    return pl.pallas_call(
        _adaln_kernel,
        out_shape=jax.ShapeDtypeStruct((depth, 8, n6), F32),
        grid=(depth, n6 // tn),
        in_specs=[pl.BlockSpec((8, d), lambda l, j: (0, 0)),
                  pl.BlockSpec((None, d, tn), lambda l, j: (l, 0, j)),
                  pl.BlockSpec((None, 1, tn), lambda l, j: (l, 0, j))],
        out_specs=pl.BlockSpec((None, 8, tn), lambda l, j: (l, 0, j)),
        compiler_params=_cparams(("parallel", "parallel")),
        name="adaln",
    )(cond8, w_mod, b_mod.reshape(depth, 1, n6))


def _kv_heads(ckv_b, kr, krp, wk_ref, wv_ref, knw, knwp, cos, sin, k_ref, v_ref):
    kc = knw * cos
    ks = knwp * sin
    for h in range(N_HEADS):
        kz = _dot(ckv_b, wk_ref[h]) + kr
        r = lax.rsqrt(jnp.sum(kz * kz, axis=-1, keepdims=True) * (1.0 / QK_DIM) + EPS)
        k_ref[h] = (r * (kz * kc + krp * ks)).astype(BF16)
    for hp in range(N_HEADS // 2):
        v_ref[hp] = _dot(ckv_b, wv_ref[hp]).astype(BF16)


def _mla_qkv_kernel(*refs, lay, n_y, q_lora, kv_lora):
    y_refs = refs[:n_y]
    (mod_ref, nw_ref, w_ref, qan_ref, kvan_ref, wq_ref, wk_ref, wv_ref, nrm_ref, cos_ref, sin_ref,
     ckv_ref, kr_ref, q_ref, k_ref, v_ref) = refs[n_y:]
    m = mod_ref[...]
    h = _norm_mod(lay.stream_load(y_refs), nw_ref[...], m[0:1], m[1:2])
    z = _dot(h.astype(BF16), w_ref[...])
    cq = z[:, :q_lora]
    cq = (cq * lax.rsqrt(jnp.mean(cq * cq, axis=-1, keepdims=True) + EPS) * qan_ref[...]).astype(BF16)
    ckv = z[:, q_lora:q_lora + kv_lora]
    ckv = ckv * lax.rsqrt(jnp.mean(ckv * ckv, axis=-1, keepdims=True) + EPS) * kvan_ref[...]
    ckv_ref[...] = ckv
    kr = z[:, q_lora + kv_lora:q_lora + kv_lora + LANES]
    kr_ref[...] = kr
    krp = z[:, q_lora + kv_lora + LANES:]
    nrm = nrm_ref[...]
    cos = cos_ref[...]
    sin = sin_ref[...]
    qc = nrm[0:1] * cos * (QK_DIM ** -0.5)
    qs = nrm[1:2] * sin * (QK_DIM ** -0.5)
    for hd in range(N_HEADS):
        zq = _dot(cq, wq_ref[hd])
        qm = zq[:, :HEAD_PAD]
        qp = zq[:, HEAD_PAD:]
        r = lax.rsqrt(jnp.sum(qm * qm, axis=-1, keepdims=True) * (1.0 / QK_DIM) + EPS)
        q_ref[hd] = (r * (qm * qc + qp * qs)).astype(BF16)
    _kv_heads(ckv.astype(BF16), kr, krp, wk_ref, wv_ref, nrm[2:3], nrm[3:4], cos, sin, k_ref, v_ref)


def mla_qkv(lay, y, mod, nw, w_a, qan, kvan, wq, wk, wv, nrm, cos_t, sin_t, q_lora, kv_lora):
    ys = _as_list(y)
    t, d = lay.t, ys[0].shape[1]
    tm = lay.tm
    row = lambda i: (i, 0)
    full = lambda i: (0, 0)
    hrow = lambda i: (0, i, 0)
    full3 = lambda i: (0, 0, 0)
    rope = lambda i: (lay.rope_blk(i), 0)
    nh = N_HEADS
    return pl.pallas_call(
        functools.partial(_mla_qkv_kernel, lay=lay, n_y=len(ys), q_lora=q_lora, kv_lora=kv_lora),
        out_shape=(jax.ShapeDtypeStruct((t, kv_lora), F32), jax.ShapeDtypeStruct((t, LANES), F32),
                   jax.ShapeDtypeStruct((nh, t, HEAD_PAD), BF16), jax.ShapeDtypeStruct((nh, t, HEAD_PAD), BF16),
                   jax.ShapeDtypeStruct((nh // 2, t, LANES), BF16)),
        grid=(lay.n_tiles,),
        in_specs=lay.stream_specs(d, len(ys) == 2) + [
            pl.BlockSpec((None, 8, d), lambda i: (lay.seg(i), 0, 0)),
            pl.BlockSpec((1, d), full), pl.BlockSpec(w_a.shape, full),
            pl.BlockSpec((1, q_lora), full), pl.BlockSpec((1, kv_lora), full),
            pl.BlockSpec(wq.shape, full3), pl.BlockSpec(wk.shape, full3), pl.BlockSpec(wv.shape, full3),
            pl.BlockSpec((8, LANES), full),
            pl.BlockSpec((tm, LANES), rope), pl.BlockSpec((tm, LANES), rope)],
        out_specs=(pl.BlockSpec((tm, kv_lora), row), pl.BlockSpec((tm, LANES), row),
                   pl.BlockSpec((nh, tm, HEAD_PAD), hrow), pl.BlockSpec((nh, tm, HEAD_PAD), hrow),
                   pl.BlockSpec((nh // 2, tm, LANES), hrow)),
        compiler_params=_cparams(("parallel",)),
        name="mla_qkv",
    )(*ys, mod, nw, w_a, qan, kvan, wq, wk, wv, nrm, cos_t, sin_t)


def _ctx_kv_kernel(ckv_ref, kr_ref, wk_ref, wv_ref, nrm_ref, k_ref, v_ref):
    nrm = nrm_ref[...]
    kr = kr_ref[...]
    one = jnp.ones((1, LANES), F32)
    _kv_heads(ckv_ref[...].astype(BF16), kr, kr, wk_ref, wv_ref, nrm[2:3], nrm[3:4],
              one, jnp.zeros((1, LANES), F32), k_ref, v_ref)


def ctx_kv(ckv, kr, wk, wv, nrm, tm):
    t = ckv.shape[0]
    row = lambda i: (i, 0)
    hrow = lambda i: (0, i, 0)
    full3 = lambda i: (0, 0, 0)
    nh = N_HEADS
    return pl.pallas_call(
        _ctx_kv_kernel,
        out_shape=(jax.ShapeDtypeStruct((nh, t, HEAD_PAD), BF16), jax.ShapeDtypeStruct((nh // 2, t, LANES), BF16)),
        grid=(t // tm,),
        in_specs=[pl.BlockSpec((tm, ckv.shape[1]), row), pl.BlockSpec((tm, LANES), row),
                  pl.BlockSpec(wk.shape, full3), pl.BlockSpec(wv.shape, full3),
                  pl.BlockSpec((8, LANES), lambda i: (0, 0))],
        out_specs=(pl.BlockSpec((nh, tm, HEAD_PAD), hrow), pl.BlockSpec((nh // 2, tm, LANES), hrow)),
        compiler_params=_cparams(("parallel",)),
        name="ctx_kv",
    )(ckv, kr, wk, wv, nrm)


def _attn_kernel(*refs, with_ctx, kblk):
    if with_ctx:
        q_ref, k_ref, v_ref, kc_ref, vc_ref, o_ref = refs
    else:
        q_ref, k_ref, v_ref, o_ref = refs
    seq = k_ref.shape[1]
    n_heads = q_ref.shape[0]
    blocks = [(kc_ref, vc_ref, 0, kc_ref.shape[1])] if with_ctx else []
    blocks += [(k_ref, v_ref, b0, kblk) for b0 in range(0, seq, kblk)]
    items = [(j, blk) for j in range(n_heads) for blk in blocks]

    def scores(item):
        j, (kr, _, b0, n) = item
        return _dot_nt(q_ref[j], kr[j, b0:b0 + n, :])

    outs = []
    m = l = acc = None
    s_next = scores(items[0])
    for idx, (j, (_, vr, b0, n)) in enumerate(items):
        s = s_next
        if idx + 1 < len(items):
            s_next = scores(items[idx + 1])
        mb = jnp.max(s, axis=-1, keepdims=True)
        m_new = mb if m is None else jnp.maximum(m, mb)
        p = jnp.exp(s - m_new)
        pv = _dot(p.astype(BF16), vr[j // 2, b0:b0 + n, :])
        ps = jnp.sum(p, axis=-1, keepdims=True)
        if m is None:
            l, acc = ps, pv
        else:
            a = jnp.exp(m - m_new)
            l, acc = a * l + ps, a * acc + pv
        m = m_new
        if idx + 1 == len(items) or items[idx + 1][0] != j:
            outs.append(acc / l)
            m = l = acc = None
    lane = lax.broadcasted_iota(jnp.int32, outs[0].shape, 1)
    for pr in range(n_heads // 2):
        o_ref[:, pr * LANES:(pr + 1) * LANES] = jnp.where(lane < V_DIM, outs[2 * pr], outs[2 * pr + 1]).astype(BF16)


def attention(q, k, v2, row0, n_b, seq, tq, kc=None, vc2=None, ctx_len=0, kblk=1024, pairs=1):
    assert row0 % seq == 0 and seq % tq == 0
    nq = seq // tq
    qb0 = row0 // tq
    kb0 = row0 // seq
    with_ctx = kc is not None
    hp_n = 2 * pairs
    in_specs = [pl.BlockSpec((hp_n, tq, HEAD_PAD), lambda b, hp, qi: (hp, qb0 + b * nq + qi, 0)),
                pl.BlockSpec((hp_n, seq, HEAD_PAD), lambda b, hp, qi: (hp, kb0 + b, 0)),
                pl.BlockSpec((pairs, seq, LANES), lambda b, hp, qi: (hp, kb0 + b, 0))]
    args = [q, k, v2]
    if with_ctx:
        in_specs += [pl.BlockSpec((hp_n, ctx_len, HEAD_PAD), lambda b, hp, qi: (hp, b, 0)),
                     pl.BlockSpec((pairs, ctx_len, LANES), lambda b, hp, qi: (hp, b, 0))]
        args += [kc, vc2]
    return pl.pallas_call(
        functools.partial(_attn_kernel, with_ctx=with_ctx, kblk=min(seq, kblk)),
        out_shape=jax.ShapeDtypeStruct((n_b * seq, (N_HEADS // 2) * LANES), BF16),
        grid=(n_b, N_HEADS // (2 * pairs), nq),
        in_specs=in_specs,
        out_specs=pl.BlockSpec((tq, pairs * LANES), lambda b, hp, qi: (b * nq + qi, hp)),
        compiler_params=_cparams(("parallel", "parallel", "parallel")),
        name="attn_ctx" if with_ctx else "attn",
    )(*args)


def _ssm_in_kernel(y_ref, mod_ref, nw_ref, w_ref, u_ref, us_ref, ug_ref):
    m = mod_ref[...]
    h = _norm_mod(y_ref[...], nw_ref[...], m[0:1], m[1:2])
    z = _dot(h.astype(BF16), w_ref[...])
    n_lt = us_ref.shape[0]
    for j in range(n_lt):
        us_ref[j] = z[:, j * LANES:(j + 1) * LANES]
    n_ch = us_ref.shape[1] // SSM_CHUNK
    kk = SSM_GROUP
    gpt = LANES // kk
    for t in range(SSM_CHUNK):
        for j in range(n_lt):
            piece = us_ref[j, pl.ds(t, n_ch, stride=SSM_CHUNK), :]
            for gl in range(gpt):
                ug_ref[j * gpt + gl, :, t * kk:(t + 1) * kk] = piece[:, gl * kk:(gl + 1) * kk]
    u_ref[...] = ug_ref[...].astype(BF16)


def ssm_in(lay, y, mod, nw, w_in):
    t, d = y.shape
    tm = lay.tm
    n = w_in.shape[1]
    g = n // SSM_GROUP
    n_ch = tm // SSM_CHUNK
    qk = SSM_CHUNK * SSM_GROUP
    row = lambda i: (i, 0)
    return pl.pallas_call(
        _ssm_in_kernel,
        out_shape=jax.ShapeDtypeStruct((g, t // SSM_CHUNK, qk), BF16),
        grid=(lay.n_tiles,),
        in_specs=[pl.BlockSpec((tm, d), row), pl.BlockSpec((None, 8, d), lambda i: (lay.seg(i), 0, 0)),
                  pl.BlockSpec((1, d), lambda i: (0, 0)), pl.BlockSpec((d, n), lambda i: (0, 0))],
        out_specs=pl.BlockSpec((g, n_ch, qk), lambda i: (0, i, 0)),
        scratch_shapes=[pltpu.VMEM((n // LANES, tm, LANES), F32), pltpu.VMEM((g, n_ch, qk), F32)],
        compiler_params=_cparams(("parallel",)),
        name="ssm_in",
    )(y, mod, nw, w_in)


def _ssm_prep_kernel(ar_ref, ai_ref, ld_ref, arc_ref, aic_ref, ldc_ref, btr_ref, bti_ref, ctr_ref, cti_ref, dsk_ref,
                     mt_ref, ett_ref, ft_ref, coef_ref):
    q = SSM_CHUNK
    k = SSM_GROUP
    p = SSM_STATE
    qk = q * k
    fwd = pl.program_id(0) == 0
    dt = jnp.exp(ld_ref[...])
    are = ar_ref[...]
    aim = ai_ref[...]
    mag = jnp.exp(dt * are)
    abr = mag * jnp.cos(dt * aim)
    abi = mag * jnp.sin(dt * aim)
    den = are * are + aim * aim
    nr = abr - 1.0
    cf_re = (nr * are + abi * aim) / den
    cf_im = (abi * are - nr * aim) / den
    btr = btr_ref[...]
    bti = bti_ref[...]
    bbt_re = cf_re * btr - cf_im * bti
    bbt_im = cf_re * bti + cf_im * btr
    bbt_re_t = jnp.concatenate([bbt_re] * q, axis=0)
    bbt_im_t = jnp.concatenate([bbt_im] * q, axis=0)
    s_idx = (lax.broadcasted_iota(jnp.int32, (qk, 1), 0) // k).astype(F32)
    pw = jnp.where(fwd, (q - 1.0) - s_idx, s_idx)
    pm = jnp.exp(pw * dt * are)
    pr = pm * jnp.cos(pw * dt * aim)
    pi = pm * jnp.sin(pw * dt * aim)
    et_re = pr * bbt_re_t - pi * bbt_im_t
    et_im = pr * bbt_im_t + pi * bbt_re_t
    ett_ref[...] = jnp.concatenate([et_re, et_im, et_im, et_re], axis=1).astype(BF16)
    qm = jnp.exp(q * dt * are)
    aq_re = qm * jnp.cos(q * dt * aim)
    aq_im = qm * jnp.sin(q * dt * aim)
    c1 = jnp.concatenate([aq_re, aq_re], axis=1)
    c2 = jnp.concatenate([-aq_im, aq_im], axis=1)
    coef_ref[...] = jnp.concatenate([c1, c2, jnp.zeros((6, 2 * p), F32)], axis=0)
    dtc = jnp.exp(ldc_ref[...])
    arec = arc_ref[...]
    aimc = aic_ref[...]
    tau = (lax.broadcasted_iota(jnp.int32, (1, qk), 1) // k).astype(F32)
    ctr = ctr_ref[...]
    cti = cti_ref[...]
    tau1 = jnp.where(fwd, tau + 1.0, q - tau)
    m1 = jnp.exp(tau1 * dtc * arec)
    p1r = m1 * jnp.cos(tau1 * dtc * aimc)
    p1i = m1 * jnp.sin(tau1 * dtc * aimc)
    ft_ref[...] = jnp.concatenate([ctr * p1r - cti * p1i, -(ctr * p1i + cti * p1r)], axis=0).astype(BF16)
    tau0 = jnp.where(fwd, tau, (q - 1.0) - tau)
    m0 = jnp.exp(tau0 * dtc * arec)
    p0r = m0 * jnp.cos(tau0 * dtc * aimc)
    p0i = m0 * jnp.sin(tau0 * dtc * aimc)
    left_re = ctr * p0r - cti * p0i
    left_im = ctr * p0i + cti * p0r
    kt = _dot_hi(bbt_re, left_re) - _dot_hi(bbt_im, left_im)
    lane = lax.broadcasted_iota(jnp.int32, (k, qk), 1)
    rowi = lax.broadcasted_iota(jnp.int32, (k, qk), 0)
    kt_f = kt + jnp.where(lane == rowi, dsk_ref[...], 0.0)
    rows_f, rows_b = [], []
    for s in range(q):
        rows_f.append(kt_f if s == 0 else jnp.where(lane >= s * k, pltpu.roll(kt_f, s * k, 1), 0.0))
        sh = ((s + 1) * k) % qk
        rows_b.append(kt if sh == 0 else jnp.where(lane < (s + 1) * k, pltpu.roll(kt, sh, 1), 0.0))
    mt = jnp.where(fwd, jnp.concatenate(rows_f, axis=0), jnp.concatenate(rows_b, axis=0))
    mt_ref[...] = mt.astype(BF16)


def ssm_prep(a_re, a_im, log_dt, b_re, b_im, c_re, c_im, d_skip):
    nd, g, p = a_re.shape
    k = b_re.shape[-1]
    qk = SSM_CHUNK * k
    ld = jnp.broadcast_to(log_dt[..., None], (nd, g, p))
    dsk = jnp.broadcast_to(jnp.tile(d_skip.reshape(g, 1, k), (1, 1, SSM_CHUNK))[None], (nd, g, 1, qk))
    rowv = lambda x: x.reshape(nd, g, 1, p)
    colv = lambda x: x.reshape(nd, g, p, 1)
    bt = lambda x: jnp.swapaxes(x, -1, -2)
    ct = lambda x: jnp.tile(jnp.swapaxes(x, -1, -2), (1, 1, 1, SSM_CHUNK))
    spec = lambda r, c: pl.BlockSpec((None, None, r, c), lambda d, j: (d, j, 0, 0))
    return pl.pallas_call(
        _ssm_prep_kernel,
        out_shape=(jax.ShapeDtypeStruct((nd, g, qk, qk), BF16), jax.ShapeDtypeStruct((nd, g, qk, 4 * p), BF16),
                   jax.ShapeDtypeStruct((nd, g, 2 * p, qk), BF16), jax.ShapeDtypeStruct((nd, g, 8, 2 * p), F32)),
        grid=(nd, g),
        in_specs=[spec(1, p), spec(1, p), spec(1, p), spec(p, 1), spec(p, 1), spec(p, 1),
                  spec(k, p), spec(k, p), spec(p, qk), spec(p, qk), spec(1, qk)],
        out_specs=(spec(qk, qk), spec(qk, 4 * p), spec(2 * p, qk), spec(8, 2 * p)),
        compiler_params=_cparams(("parallel", "parallel")),
        name="ssm_prep",
    )(rowv(a_re), rowv(a_im), rowv(ld), colv(a_re), colv(a_im), colv(ld), bt(b_re), bt(b_im), ct(c_re), ct(c_im), dsk)


def _ssm_scan_kernel(u_ref, mt_ref, ett_ref, ft_ref, coef_ref, x0_ref, x0s_ref, y_ref, hfin_ref,
                     sx_ref, sxs_ref, hp_ref, *, gpb, ctx_b, ctx_chunks, lat_b, lat_chunks, rchunk):
    w = 2 * SSM_STATE
    n_rows = u_ref.shape[1]
    ctx_rows = ctx_b * ctx_chunks
    for d in range(2):
        for g in range(gpb):
            for r0 in range(0, n_rows, rchunk):
                ss = _dot(u_ref[g, r0:r0 + rchunk, :], ett_ref[d, g])
                sx_ref[g, r0:r0 + rchunk, :] = ss[:, :w]
                sxs_ref[g, r0:r0 + rchunk, :] = ss[:, w:]
        c1 = [coef_ref[d, g][0:1] for g in range(gpb)]
        c2 = [coef_ref[d, g][1:2] for g in range(gpb)]

        def make_body(base, nb, nc):
            def body(i, carry):
                c = i if d == 0 else nc - 1 - i
                rows = pl.ds(base + c, nb, stride=nc)
                out = []
                for g in range(gpb):
                    x, xs = carry[g]
                    hp_ref[g, rows, :] = x
                    xn = c1[g] * x + c2[g] * xs + sx_ref[g, rows, :]
                    xsn = c1[g] * xs - c2[g] * x + sxs_ref[g, rows, :]
                    out.append((xn, xsn))
                return tuple(out)
            return body

        z = jnp.zeros((ctx_b, w), F32)
        fin = lax.fori_loop(0, ctx_chunks, make_body(0, ctx_b, ctx_chunks), tuple((z, z) for _ in range(gpb)))
        hfin_ref[d] = jnp.concatenate([fin[g][0] for g in range(gpb)], axis=1)
        init = tuple((x0_ref[d][:, g * w:(g + 1) * w], x0s_ref[d][:, g * w:(g + 1) * w]) for g in range(gpb))
        lax.fori_loop(0, lat_chunks, make_body(ctx_rows, lat_b, lat_chunks), init, unroll=4)
        for g in range(gpb):
            for r0 in range(0, n_rows, rchunk):
                y = _dot(u_ref[g, r0:r0 + rchunk, :], mt_ref[d, g])
                y = y + _dot(hp_ref[g, r0:r0 + rchunk, :].astype(BF16), ft_ref[d, g])
                if d == 0:
                    y_ref[g, r0:r0 + rchunk, :] = y
                else:
                    y_ref[g, r0:r0 + rchunk, :] += y


def ssm_scan(u, mt, ett, ft, coef, x0, x0s, ctx_b, ctx_chunks, lat_b, lat_chunks, gpb=4):
    g, n_rows, qk = u.shape
    w = 2 * SSM_STATE
    rchunk = math.gcd(n_rows, 512)
    assert n_rows == ctx_b * ctx_chunks + lat_b * lat_chunks
    ublk = pl.BlockSpec((gpb, n_rows, qk), lambda j: (j, 0, 0))
    blk = lambda r, c: pl.BlockSpec((2, gpb, r, c), lambda j: (0, j, 0, 0))
    vec = lambda r: pl.BlockSpec((2, r, gpb * w), lambda j: (0, 0, j))
    return pl.pallas_call(
        functools.partial(_ssm_scan_kernel, gpb=gpb, ctx_b=ctx_b, ctx_chunks=ctx_chunks, lat_b=lat_b,
                          lat_chunks=lat_chunks, rchunk=rchunk),
        out_shape=(jax.ShapeDtypeStruct((g, n_rows, qk), F32), jax.ShapeDtypeStruct((2, ctx_b, g * w), F32)),
        grid=(g // gpb,),
        in_specs=[ublk, blk(qk, qk), blk(qk, 2 * w), blk(w, qk), blk(8, w), vec(lat_b), vec(lat_b)],
        out_specs=(ublk, vec(ctx_b)),
        scratch_shapes=[pltpu.VMEM((gpb, n_rows, w), F32), pltpu.VMEM((gpb, n_rows, w), F32),
                        pltpu.VMEM((gpb, n_rows, w), F32)],
        compiler_params=_cparams(("parallel",)),
        name="ssm_scan",
    )(u, mt, ett, ft, coef, x0, x0s)


def _ssm_out_kernel(y_ref, mod_ref, yc_ref, w_ref, nw_ref, wr_ref, br_ref, tri_ref, out_ref, h_ref, ri_ref, pc_ref,
                    s_ref, *, d_model):
    kk = SSM_GROUP
    n_ch = yc_ref.shape[1]
    gpt = LANES // kk
    n_lt = s_ref.shape[0]
    for t in range(SSM_CHUNK):
        for j in range(n_lt):
            piece = jnp.concatenate([yc_ref[j * gpt + gl, :, t * kk:(t + 1) * kk] for gl in range(gpt)], axis=1)
            s_ref[j, pl.ds(t, n_ch, stride=SSM_CHUNK), :] = piece
    s = jnp.concatenate([s_ref[j] for j in range(n_lt)], axis=1)
    a = jax.nn.gelu(s, approximate=True).astype(BF16)
    z = _dot(a, w_ref[...])
    m = mod_ref[...]
    y = y_ref[...] + m[2:3] * (z[:, :d_model] * jax.nn.sigmoid(z[:, d_model:]))
    out_ref[...] = y
    _route_core(y, m, nw_ref, wr_ref, br_ref, tri_ref, h_ref, ri_ref, pc_ref)


def ssm_out_route(lay, y, mod, ych, w_glu, route):
    t, d = y.shape
    tm = lay.tm
    g, _, qk = ych.shape
    n_ch = tm // SSM_CHUNK
    row = lambda i: (i, 0)
    r_args, r_in, r_shape, r_out = _route_io(lay, d, *route)
    return pl.pallas_call(
        functools.partial(_ssm_out_kernel, d_model=d),
        out_shape=[jax.ShapeDtypeStruct((t, d), F32)] + r_shape,
        grid=(lay.n_tiles,),
        in_specs=[pl.BlockSpec((tm, d), row), pl.BlockSpec((None, 8, d), lambda i: (lay.seg(i), 0, 0)),
                  pl.BlockSpec((g, n_ch, qk), lambda i: (0, i, 0)),
                  pl.BlockSpec(w_glu.shape, lambda i: (0, 0))] + r_in,
        out_specs=[pl.BlockSpec((tm, d), row)] + r_out,
        scratch_shapes=[pltpu.VMEM((g * SSM_GROUP // LANES, tm, LANES), F32)],
        input_output_aliases={0: 0},
        compiler_params=_cparams(("parallel",)),
        name="ssm_out_route",
    )(y, mod, ych, w_glu, *r_args)


def s5_layer(lay, y, mod, nw, route, state_re, state_im, w_in, a_re, a_im, log_dt, b_re, b_im, c_re, c_im, d_skip,
             w_glu):
    uch = ssm_in(lay, y, mod, nw, w_in.astype(BF16))
    g = a_re.shape[1]
    p = SSM_STATE
    mt, ett, ft, coef = ssm_prep(a_re, a_im, log_dt, b_re, b_im, c_re, c_im, d_skip)
    st = jnp.concatenate([state_re, state_im], axis=-1)
    sts = jnp.concatenate([state_im, state_re], axis=-1)
    x0 = jnp.transpose(st, (1, 0, 2, 3)).reshape(2, lay.n_lat_b, g * 2 * p)
    x0s = jnp.transpose(sts, (1, 0, 2, 3)).reshape(2, lay.n_lat_b, g * 2 * p)
    ych, hfin = ssm_scan(uch, mt, ett, ft, coef, x0, x0s, lay.n_ctx_b, lay.ctx_len // SSM_CHUNK,
                         lay.n_lat_b, lay.lat_len // SSM_CHUNK)
    routed = ssm_out_route(lay, y, mod, ych, w_glu.astype(BF16), route)
    hf = hfin.reshape(2, lay.n_ctx_b, g, 2, p)
    new_re = jnp.transpose(hf[:, :, :, 0], (1, 0, 2, 3))
    new_im = jnp.transpose(hf[:, :, :, 1], (1, 0, 2, 3))
    return routed, new_re, new_im


def _route_core(y, m, nw_ref, wr_ref, br_ref, tri_ref, h_ref, ri_ref, pc_ref):
    n_exp = br_ref.shape[0]
    h = _norm_mod(y, nw_ref[...], m[3:4], m[4:5])
    h_hi = h.astype(BF16)
    h_ref[...] = h_hi
    h_lo = (h - h_hi.astype(F32)).astype(BF16)
    wr = wr_ref[...]
    lt = _dot_nt(wr, h_hi)
    logits = lt[:n_exp] + lt[n_exp:] + _dot_nt(wr[:n_exp], h_lo)
    scores = jax.nn.sigmoid(logits)
    sel = scores + br_ref[...]
    epg = EXPERTS_PER_GROUP
    row = lambda x, e: x[e:e + 1, :]
    gscore = []
    for g in range(N_EXPERT_GROUPS):
        a, b, c, d = (row(sel, g * epg + j) for j in range(epg))
        m1, n1, m2, n2 = jnp.maximum(a, b), jnp.minimum(a, b), jnp.maximum(c, d), jnp.minimum(c, d)
        gscore.append(jnp.maximum(m1, m2) + jnp.maximum(jnp.minimum(m1, m2), jnp.maximum(n1, n2)))
    best = gscore[0]
    gi = jnp.zeros_like(best, dtype=jnp.int32)
    for g in range(1, N_EXPERT_GROUPS):
        better = gscore[g] > best
        gi = jnp.where(better, g, gi)
        best = jnp.where(better, gscore[g], best)

    def pick(x, j):
        out = row(x, j)
        for g in range(1, N_EXPERT_GROUPS):
            out = jnp.where(gi == g, row(x, g * epg + j), out)
        return out

    sv = [pick(sel, j) for j in range(epg)]
    cv = [pick(scores, j) for j in range(epg)]
    b1, i1, w1 = sv[0], jnp.zeros_like(gi), cv[0]
    for j in range(1, epg):
        better = sv[j] > b1
        i1 = jnp.where(better, j, i1)
        w1 = jnp.where(better, cv[j], w1)
        b1 = jnp.where(better, sv[j], b1)
    neg = jnp.full_like(b1, -jnp.inf)
    b2, i2, w2 = neg, jnp.zeros_like(gi), jnp.zeros_like(w1)
    for j in range(epg):
        better = (i1 != j) & (sv[j] > b2)
        i2 = jnp.where(better, j, i2)
        w2 = jnp.where(better, cv[j], w2)
        b2 = jnp.where(better, sv[j], b2)
    tot = w1 + w2
    e1 = gi * epg + i1
    e2 = gi * epg + i2
    eid = lax.broadcasted_iota(jnp.int32, logits.shape, 0)
    m1h = eid == e1
    m2h = eid == e2
    mc = jnp.where(m1h | m2h, 1.0, 0.0)
    pref = _dot(mc.astype(BF16), tri_ref[...])
    cnt = jnp.sum(mc, axis=1, keepdims=True)
    pc_al = jnp.ceil(cnt * (1.0 / SEG_ALIGN))
    pcb = jnp.broadcast_to(pc_al, (n_exp, LANES))
    er = lax.broadcasted_iota(jnp.int32, (n_exp, n_exp), 0)
    ec = lax.broadcasted_iota(jnp.int32, (n_exp, n_exp), 1)
    lower = jnp.where(ec < er, 1.0, 0.0).astype(BF16)
    seg = _dot(lower, pcb.astype(BF16))[:, 0:1] * SEG_ALIGN
    slot = seg + pref
    pos1 = jnp.sum(jnp.where(m1h, slot, 0.0), axis=0, keepdims=True)
    pos2 = jnp.sum(jnp.where(m2h, slot, 0.0), axis=0, keepdims=True)
    zero = jnp.zeros_like(pos1)
    ri_ref[...] = jnp.concatenate([pos1, pos2, w1 / tot, w2 / tot, zero, zero, zero, zero], axis=0)
    pc_ref[...] = pcb * SEG_ALIGN


def _route_io(lay, d, nw, wr_t, br, tri):
    tm = lay.tm
    n_exp = br.shape[0]
    full = lambda i: (0, 0)
    args = [nw, wr_t, br, tri]
    in_specs = [pl.BlockSpec((1, d), full), pl.BlockSpec(wr_t.shape, full), pl.BlockSpec((n_exp, 1), full),
                pl.BlockSpec((tm, tm), full)]
    out_shape = [jax.ShapeDtypeStruct((lay.t, d), BF16), jax.ShapeDtypeStruct((8, lay.t), F32),
                 jax.ShapeDtypeStruct((lay.n_tiles, n_exp, LANES), F32)]
    out_specs = [pl.BlockSpec((tm, d), lambda i: (i, 0)), pl.BlockSpec((8, tm), lambda i: (0, i)),
                 pl.BlockSpec((None, n_exp, LANES), lambda i: (i, 0, 0))]
    return args, in_specs, out_shape, out_specs


def _proj_route_kernel(*refs, lay, n_y):
    y_refs = refs[:n_y]
    mod_ref, o_ref, w_ref, nw_ref, wr_ref, br_ref, tri_ref, out_ref, h_ref, ri_ref, pc_ref = refs[n_y:]
    m = mod_ref[...]
    y = lay.stream_load(y_refs) + m[2:3] * _dot(o_ref[...], w_ref[...])
    out_ref[...] = y
    _route_core(y, m, nw_ref, wr_ref, br_ref, tri_ref, h_ref, ri_ref, pc_ref)


def proj_route(lay, y, mod, o, w, route):
    ys = _as_list(y)
    d = ys[0].shape[1]
    tm = lay.tm
    row = lambda i: (i, 0)
    r_args, r_in, r_shape, r_out = _route_io(lay, d, *route)
    return pl.pallas_call(
        functools.partial(_proj_route_kernel, lay=lay, n_y=len(ys)),
        out_shape=[jax.ShapeDtypeStruct((lay.t, d), F32)] + r_shape,
        grid=(lay.n_tiles,),
        in_specs=lay.stream_specs(d, len(ys) == 2) + [
            pl.BlockSpec((None, 8, d), lambda i: (lay.seg(i), 0, 0)),
            pl.BlockSpec((tm, o.shape[1]), row), pl.BlockSpec(w.shape, lambda i: (0, 0))] + r_in,
        out_specs=[pl.BlockSpec((tm, d), row)] + r_out,
        input_output_aliases={0: 0} if len(ys) == 1 else {},
        compiler_params=_cparams(("parallel",)),
        name="proj_route",
    )(*ys, mod, o, w, *r_args)


def _segment_pieces(i, off_ref, pc_ref, n_exp, sizes):
    local = 0
    for e in range(n_exp):
        n = pc_ref[i * n_exp + e]
        g0 = off_ref[i * n_exp + e]
        for size in sizes:
            above = n & (-2 * size)
            yield (e, (n & size) != 0, pl.multiple_of(local + above, SEG_ALIGN),
                   pl.multiple_of(g0 + above, SEG_ALIGN), size)
        local = local + n


def _pieces(n, sizes):
    for size in sizes:
        yield (n & size) != 0, n & (-2 * size), size


def _moe_permute_kernel(off_ref, pc_ref, goff_ref, gn_ref, h_ref, ri_ref, xs_ref, ys_ref, zp_ref, sem, gsem,
                        *, n_exp, sizes, gap_sizes, rchunk):
    i = pl.program_id(0)
    last = pl.num_programs(0) - 1
    slot = i % 2
    ri = ri_ref[...]
    pos1 = ri[0:1]
    pos2 = ri[1:2]
    h = h_ref[...]
    n_loc = ys_ref.shape[1]
    for r0 in range(0, n_loc, rchunk):
        r = (lax.broadcasted_iota(jnp.int32, (rchunk, h.shape[0]), 0) + r0).astype(F32)
        p = jnp.where(pos1 == r, 1.0, 0.0) + jnp.where(pos2 == r, 1.0, 0.0)
        ys_ref[slot, r0:r0 + rchunk, :] = _dot(p.astype(BF16), h).astype(BF16)

    def copies(tile, sl):
        for e, present, lrow, grow, size in _segment_pieces(tile, off_ref, pc_ref, n_exp, sizes):
            yield present, pltpu.make_async_copy(ys_ref.at[sl, pl.ds(lrow, size)], xs_ref.at[pl.ds(grow, size)],
                                                 sem.at[sl, e])

    for present, cp in copies(i, slot):
        pl.when(present)(cp.start)

    @pl.when(i > 0)
    def _():
        for present, cp in copies(i - 1, 1 - slot):
            pl.when(present)(cp.wait)

    @pl.when(i == last)
    def _():
        for present, cp in copies(i, slot):
            pl.when(present)(cp.wait)
        zp_ref[...] = jnp.zeros_like(zp_ref)

        def gaps():
            for e in range(n_exp):
                g0 = goff_ref[e]
                for present, above, size in _pieces(gn_ref[e], gap_sizes):
                    yield present, pltpu.make_async_copy(
                        zp_ref.at[pl.ds(0, size)], xs_ref.at[pl.ds(pl.multiple_of(g0 + above, SEG_ALIGN), size)],
                        gsem.at[e])

        for present, cp in gaps():
            pl.when(present)(cp.start)
        for present, cp in gaps():
            pl.when(present)(cp.wait)

        zrows = zp_ref.shape[0]
        tail0 = goff_ref[n_exp]
        tail = lambda c: pltpu.make_async_copy(
            zp_ref, xs_ref.at[pl.ds(pl.multiple_of(tail0 + c * zrows, zrows), zrows)], gsem.at[0])

        @pl.loop(0, gn_ref[n_exp])
        def _(c):
            tail(c).start()

        @pl.loop(0, gn_ref[n_exp])
        def _(c):
            tail(c).wait()


def moe_permute(lay, h, rinfo, off, pcs, gap_off, gap_n, n_sorted, n_loc, n_exp, tmm):
    t, d = h.shape
    tm = lay.tm
    sizes = [s for s in (1 << b for b in range(12, 3, -1)) if s <= tm]
    gap_sizes = [s for s in (1 << b for b in range(12, 3, -1)) if s < tmm]
    return pl.pallas_call(
        functools.partial(_moe_permute_kernel, n_exp=n_exp, sizes=sizes, gap_sizes=gap_sizes, rchunk=256),
        out_shape=jax.ShapeDtypeStruct((n_sorted, d), BF16),
        grid_spec=pltpu.PrefetchScalarGridSpec(
            num_scalar_prefetch=4, grid=(lay.n_tiles,),
            in_specs=[pl.BlockSpec((tm, d), lambda i, *_: (i, 0)), pl.BlockSpec((8, tm), lambda i, *_: (0, i))],
            out_specs=pl.BlockSpec(memory_space=pl.ANY),
            scratch_shapes=[pltpu.VMEM((2, n_loc, d), BF16), pltpu.VMEM((gap_sizes[0], d), BF16),
                            pltpu.SemaphoreType.DMA((2, n_exp)), pltpu.SemaphoreType.DMA((n_exp,))]),
        compiler_params=_cparams(("arbitrary",)),
        name="moe_permute",
    )(off, pcs, gap_off, gap_n, h, rinfo)


def _moe_expert_kernel(te_ref, tv_ref, tf_ref, xb_ref, x_ref, wg_ref, wu_ref, wd_ref, z_ref, wgb_ref, wub_ref, wdb_ref):
    del te_ref, xb_ref
    r = pl.program_id(0)

    @pl.when(tf_ref[r] == 1)
    def _():
        wgb_ref[...] = wg_ref[...].astype(BF16)
        wub_ref[...] = wu_ref[...].astype(BF16)
        wdb_ref[...] = wd_ref[...].astype(BF16)

    @pl.when(tv_ref[r] > 0)
    def _():
        x = x_ref[...]
        a = _dot(x, wgb_ref[...])
        b = _dot(x, wub_ref[...])
        hid = (a * jax.nn.sigmoid(a)) * b
        z_ref[...] = _dot(hid.astype(BF16), wdb_ref[...]).astype(BF16)

    @pl.when(tv_ref[r] == 0)
    def _():
        z_ref[...] = jnp.zeros_like(z_ref)


def moe_experts(xs, tile_expert, tile_valid, tile_first, x_block, layer, wg, wu, wd, tmm):
    n_sorted, d = xs.shape
    f = wg.shape[-1]
    wmap = lambda r, te, tv, tf, xb: (layer, te[r], 0, 0)
    return pl.pallas_call(
        _moe_expert_kernel,
        out_shape=jax.ShapeDtypeStruct((n_sorted, d), BF16),
        grid_spec=pltpu.PrefetchScalarGridSpec(
            num_scalar_prefetch=4, grid=(n_sorted // tmm,),
            in_specs=[pl.BlockSpec((tmm, d), lambda r, te, tv, tf, xb: (xb[r], 0)),
                      pl.BlockSpec((None, None, d, f), wmap), pl.BlockSpec((None, None, d, f), wmap),
                      pl.BlockSpec((None, None, f, d), wmap)],
            out_specs=pl.BlockSpec((tmm, d), lambda r, te, tv, tf, xb: (r, 0)),
            scratch_shapes=[pltpu.VMEM((d, f), BF16), pltpu.VMEM((d, f), BF16), pltpu.VMEM((f, d), BF16)]),
        compiler_params=_cparams(("arbitrary",)),
        name="moe_experts",
    )(tile_expert, tile_valid, tile_first, x_block, xs, wg, wu, wd)


def _moe_combine_kernel(off_ref, pc_ref, y_ref, mod_ref, rit_ref, zs_ref, *rest, lay, n_exp, sizes):
    out_refs, (zt_ref, sem) = rest[:-2], rest[-2:]
    i = pl.program_id(0)
    slot = i % 2

    def copies(tile, sl):
        for e, present, lrow, grow, size in _segment_pieces(tile, off_ref, pc_ref, n_exp, sizes):
            yield present, pltpu.make_async_copy(zs_ref.at[pl.ds(grow, size)], zt_ref.at[sl, pl.ds(lrow, size)],
                                                 sem.at[sl, e])

    def fetch(tile, sl):
        zt_ref[sl] = jnp.zeros(zt_ref.shape[1:], BF16)
        for present, cp in copies(tile, sl):
            pl.when(present)(cp.start)

    @pl.when(i == 0)
    def _():
        fetch(i, slot)

    @pl.when(i + 1 < pl.num_programs(0))
    def _():
        fetch(i + 1, 1 - slot)

    rit = rit_ref[...]
    n_loc = zt_ref.shape[1]
    r = lax.broadcasted_iota(jnp.int32, (rit.shape[0], n_loc), 1).astype(F32)
    pw = jnp.where(rit[:, 0:1] == r, rit[:, 2:3], 0.0) + jnp.where(rit[:, 1:2] == r, rit[:, 3:4], 0.0)
    pw = pw.astype(BF16)
    for present, cp in copies(i, slot):
        pl.when(present)(cp.wait)
    lay.stream_store(out_refs, y_ref[...] + mod_ref[...][5:6] * _dot(pw, zt_ref[slot]))


def moe_combine(lay, y, mod, rinfo_t, zs, off, pcs, n_loc, n_exp, pair_out):
    t, d = y.shape
    tm = lay.tm
    sizes = [s for s in (1 << b for b in range(12, 3, -1)) if s <= tm]
    if pair_out:
        out_shape = (jax.ShapeDtypeStruct((lay.t_ctx, d), F32), jax.ShapeDtypeStruct((lay.t_lat, d), F32))
    else:
        out_shape = (jax.ShapeDtypeStruct((t, d), F32),)
    out = pl.pallas_call(
        functools.partial(_moe_combine_kernel, lay=lay, n_exp=n_exp, sizes=sizes),
        out_shape=out_shape,
        grid_spec=pltpu.PrefetchScalarGridSpec(
            num_scalar_prefetch=2, grid=(lay.n_tiles,),
            in_specs=[pl.BlockSpec((tm, d), lambda i, o, p: (i, 0)),
                      pl.BlockSpec((None, 8, d), lambda i, o, p: (lay.seg(i), 0, 0)),
                      pl.BlockSpec((tm, 8), lambda i, o, p: (i, 0)),
                      pl.BlockSpec(memory_space=pl.ANY)],
            out_specs=tuple(lay.stream_specs(d, pair_out)),
            scratch_shapes=[pltpu.VMEM((2, n_loc, d), BF16), pltpu.SemaphoreType.DMA((2, n_exp))]),
        input_output_aliases={} if pair_out else {2: 0},
        compiler_params=_cparams(("arbitrary",)),
        name="moe_combine",
    )(off, pcs, y, mod, rinfo_t, zs)
    return out if pair_out else out[0]


def _round_up(x, m):
    return (x + m - 1) // m * m


def moe_layer(lay, routed, mod, layer, wg, wu, wd, tmm, pair_out=False):
    y, h, rinfo, pc = routed
    t, d = y.shape
    n_exp = pc.shape[1]
    pc = pc[:, :, 0].astype(jnp.int32)
    n_loc = 2 * lay.tm + SEG_ALIGN * n_exp
    n_sorted = _round_up(2 * t + lay.n_tiles * n_exp * (SEG_ALIGN - 1) + n_exp * (tmm - 1), tmm)
    tot = jnp.sum(pc, axis=0)
    region = _round_up(tot, tmm)
    ends = jnp.cumsum(region)
    base = ends - region
    off = (base[None, :] + jnp.cumsum(pc, axis=0) - pc).reshape(-1)
    starts = jnp.arange(n_sorted // tmm, dtype=jnp.int32) * tmm
    te = jnp.minimum(jnp.sum(starts[:, None] >= ends[None, :], axis=1), n_exp - 1).astype(jnp.int32)
    tv = jnp.clip(tot[te] - (starts - base[te]), 0, tmm).astype(jnp.int32)
    tf = jnp.concatenate([jnp.ones((1,), jnp.int32), (te[1:] != te[:-1]).astype(jnp.int32)])
    xb = jnp.minimum(starts // tmm, jnp.maximum(ends[-1] // tmm - 1, 0)).astype(jnp.int32)
    pcs = pc.reshape(-1)
    assert tmm & (tmm - 1) == 0
    zrows = tmm // 2
    gap_off = jnp.concatenate([base + tot, ends[-1:]]).astype(jnp.int32)
    gap_n = jnp.concatenate([region - tot, (n_sorted - ends[-1:]) // zrows]).astype(jnp.int32)
    xs = moe_permute(lay, h, rinfo, off, pcs, gap_off, gap_n, n_sorted, n_loc, n_exp, tmm)
    zs = moe_experts(xs, te, tv, tf, xb, layer, wg, wu, wd, tmm)
    return moe_combine(lay, y, mod, rinfo.T, zs, off, pcs, n_loc, n_exp, pair_out)


def _rope_perm():
    d = np.arange(QK_ROPE)
    return QK_NOPE + (d ^ (QK_ROPE // 4))


def _pad_cols(x, n):
    return jnp.pad(x, ((0, 0),) * (x.ndim - 1) + ((0, n - x.shape[-1]),))


def _mla_weights(w_q_a, w_q_b, w_kv_a, w_kv_b, q_norm, k_norm, kv_lora):
    perm = _rope_perm()
    nh = N_HEADS
    wkr = w_kv_a[:, kv_lora:]
    d = w_kv_a.shape[0]
    zl = jnp.zeros((d, QK_NOPE), F32)
    kr_blk = _pad_cols(jnp.concatenate([zl, wkr], axis=1), LANES)
    krp_blk = _pad_cols(jnp.concatenate([zl, wkr[:, perm - QK_NOPE]], axis=1), LANES)
    w_a = jnp.concatenate([w_q_a, w_kv_a[:, :kv_lora], kr_blk, krp_blk], axis=1).astype(BF16)
    wq = w_q_b.reshape(-1, nh, QK_DIM).transpose(1, 0, 2)
    wq_main = _pad_cols(wq, HEAD_PAD)
    wq_part = _pad_cols(jnp.concatenate([jnp.zeros_like(wq[..., :QK_NOPE]), wq[..., perm]], axis=-1), HEAD_PAD)
    wq_h = jnp.concatenate([wq_main, wq_part], axis=-1).astype(BF16)
    wkv = w_kv_b.reshape(-1, nh, QK_NOPE + V_DIM).transpose(1, 0, 2)
    wk_h = _pad_cols(wkv[..., :QK_NOPE], HEAD_PAD).astype(BF16)
    wv = wkv[..., QK_NOPE:]
    wv_h = jnp.concatenate([wv[0::2], wv[1::2]], axis=-1).astype(BF16)
    nrm = jnp.stack([_pad_cols(q_norm, LANES), _pad_cols(jnp.concatenate([jnp.zeros((QK_NOPE,), F32), q_norm[perm]]), LANES),
                     _pad_cols(k_norm, LANES), _pad_cols(jnp.concatenate([jnp.zeros((QK_NOPE,), F32), k_norm[perm]]), LANES)])
    nrm = jnp.pad(nrm, ((0, 4), (0, 0)))
    return w_a, wq_h, wk_h, wv_h, nrm


def _rope_tables(lat_len, tm):
    rows = lat_len // GRID_W
    t_row = np.repeat(np.arange(rows, dtype=np.float32), GRID_W)
    t_col = np.tile(np.arange(GRID_W, dtype=np.float32), rows)
    half = QK_ROPE // 2
    inv = (ROPE_BASE ** (-np.arange(0, half, 2, dtype=np.float32) / half)).astype(np.float32)
    ang = jnp.concatenate([jnp.asarray(t_row)[:, None] * inv, jnp.asarray(t_col)[:, None] * inv], axis=-1)
    cos, sin = jnp.cos(ang), jnp.sin(ang)
    nf = QK_ROPE // 4
    cos_l = jnp.concatenate([cos[:, :nf], cos[:, :nf], cos[:, nf:], cos[:, nf:]], axis=-1)
    sin_l = jnp.concatenate([-sin[:, :nf], sin[:, :nf], -sin[:, nf:], sin[:, nf:]], axis=-1)
    one = jnp.ones((lat_len, QK_NOPE), F32)
    cos_t = _pad_cols(jnp.concatenate([one, cos_l], axis=-1), LANES)
    cos_t = cos_t.at[:, QK_DIM:].set(1.0)
    sin_t = _pad_cols(jnp.concatenate([jnp.zeros((lat_len, QK_NOPE), F32), sin_l], axis=-1), LANES)
    cos_t = jnp.concatenate([jnp.ones((tm, LANES), F32), cos_t], axis=0)
    sin_t = jnp.concatenate([jnp.zeros((tm, LANES), F32), sin_t], axis=0)
    return cos_t, sin_t


def mla_layer(lay, y, mod, nw, route, cache_ckv, cache_krope, cos_t, sin_t, w_q_a, q_a_norm, w_q_b, w_kv_a,
              kv_a_norm, w_kv_b, q_norm, k_norm, w_o):
    q_lora = w_q_a.shape[1]
    kv_lora = kv_a_norm.shape[0]
    w_a, wq_h, wk_h, wv_h, nrm = _mla_weights(w_q_a, w_q_b, w_kv_a, w_kv_b, q_norm, k_norm, kv_lora)
    ckv, kr, q, k, v2 = mla_qkv(lay, y, mod, nw, w_a, q_a_norm.reshape(1, -1), kv_a_norm.reshape(1, -1),
                                wq_h, wk_h, wv_h, nrm, cos_t, sin_t, q_lora, kv_lora)
    past = cache_ckv.shape[1]
    ckr = jnp.pad(cache_krope.reshape(-1, QK_ROPE), ((0, 0), (QK_NOPE, LANES - QK_DIM)))
    kc, vc2 = ctx_kv(cache_ckv.reshape(-1, kv_lora), ckr, wk_h, wv_h, nrm, past)
    o_ctx = attention(q, k, v2, 0, lay.n_ctx_b, lay.ctx_len, min(lay.ctx_len, 256), pairs=N_HEADS // 2)
    o_lat = attention(q, k, v2, lay.t_ctx, lay.n_lat_b, lay.lat_len, min(lay.lat_len, 512), kc, vc2, past)
    o = jnp.concatenate([o_ctx, o_lat], axis=0)
    routed = proj_route(lay, y, mod, o, w_o.astype(BF16), route)
    new_ckv = ckv[:lay.t_ctx].reshape(lay.n_ctx_b, lay.ctx_len, kv_lora)
    new_kr = kr[:lay.t_ctx, QK_NOPE:QK_DIM].reshape(lay.n_ctx_b, lay.ctx_len, QK_ROPE)
    return routed, new_ckv, new_kr


def kernel(x_prompt, x_sample, c, cache_ckv, cache_krope, state_ssm_re, state_ssm_im, c_ctx, w_mod, b_mod, norm1_w, norm2_w, mla_w_q_a, mla_q_a_norm, mla_w_q_b, mla_w_kv_a, mla_kv_a_norm, mla_w_kv_b, mla_q_norm, mla_k_norm, mla_w_o, ssm_w_in, ssm_a_re, ssm_a_im, ssm_log_dt, ssm_b_re, ssm_b_im, ssm_c_re, ssm_c_im, ssm_d, ssm_w_glu, w_router, b_router, moe_w_gate, moe_w_up, moe_w_down):
    n_ctx_b, ctx_len, d = x_prompt.shape
    n_lat_b, lat_len, _ = x_sample.shape
    depth = w_mod.shape[0]
    lay = Layout(n_ctx_b, ctx_len, n_lat_b, lat_len, tm=512)
    assert n_lat_b + 1 <= 8
    y = (x_prompt.reshape(-1, d), x_sample.reshape(-1, d))
    cond8 = jnp.pad(jnp.concatenate([c_ctx[None, :], c], axis=0), ((0, 7 - n_lat_b), (0, 0)))
    mods = adaln_all(cond8, w_mod, b_mod)
    mods = jnp.pad(mods.reshape(depth, 8, 6, d), ((0, 0), (0, 0), (0, 2), (0, 0)))
    cos_t, sin_t = _rope_tables(lat_len, lay.tm)
    n_exp = b_router.shape[0]
    wr_hi = w_router.astype(BF16)
    wr_lo = (w_router - wr_hi.astype(F32)).astype(BF16)
    wr_t = jnp.concatenate([wr_hi.T, wr_lo.T], axis=0)
    br = b_router.reshape(n_exp, 1)
    tri = jnp.asarray(np.triu(np.ones((lay.tm, lay.tm), np.float32), k=1), BF16)
    ckv_out, kr_out, sre_out, sim_out = [], [], [], []
    for i in range(depth):
        j = i // 2
        mod = mods[i]
        nw1 = norm1_w[i].reshape(1, d)
        route = (norm2_w[i].reshape(1, d), wr_t, br, tri)
        if i % 2 == 0:
            routed, ckv_p, kr_p = mla_layer(lay, y, mod, nw1, route, cache_ckv[:, j], cache_krope[:, j], cos_t, sin_t,
                                            mla_w_q_a[j], mla_q_a_norm[j], mla_w_q_b[j], mla_w_kv_a[j],
                                            mla_kv_a_norm[j], mla_w_kv_b[j], mla_q_norm[j], mla_k_norm[j], mla_w_o[j])
            ckv_out.append(ckv_p)
            kr_out.append(kr_p)
        else:
            routed, s_re, s_im = s5_layer(lay, y, mod, nw1, route, state_ssm_re[:, j], state_ssm_im[:, j], ssm_w_in[j],
                                          ssm_a_re[j], ssm_a_im[j], ssm_log_dt[j], ssm_b_re[j], ssm_b_im[j],
                                          ssm_c_re[j], ssm_c_im[j], ssm_d[j], ssm_w_glu[j])
            sre_out.append(s_re)
            sim_out.append(s_im)
        y = moe_layer(lay, routed, mod, i, moe_w_gate, moe_w_up, moe_w_down, tmm=512, pair_out=(i == depth - 1))
    yp = y[0].reshape(n_ctx_b, ctx_len, d)
    ys = y[1].reshape(n_lat_b, lat_len, d)
    return (yp, ys, jnp.stack(ckv_out, axis=1), jnp.stack(kr_out, axis=1),
            jnp.stack(sre_out, axis=1), jnp.stack(sim_out, axis=1))
```

```python
import functools
import math

import jax
import jax.numpy as jnp
import numpy as np
from jax import lax
from jax.experimental import pallas as pl
from jax.experimental.pallas import tpu as pltpu

F32 = jnp.float32
BF16 = jnp.bfloat16
EPS = 1e-6

GRID_W = 64
N_HEADS = 8
QK_NOPE = 64
QK_ROPE = 32
QK_DIM = QK_NOPE + QK_ROPE
V_DIM = 64
ROPE_BASE = 10000.0
SSM_GROUP = 16
SSM_STATE = 64
N_EXPERT_GROUPS = 4
EXPERTS_PER_GROUP = 4

LANES = 128
HEAD_PAD = LANES
SSM_CHUNK = 16
SEG_ALIGN = 16
VMEM_LIMIT = 48 * 1024 * 1024


def _cparams(sem, vmem=VMEM_LIMIT):
    return pltpu.CompilerParams(dimension_semantics=sem, vmem_limit_bytes=vmem)


def _dot(a, b):
    return jnp.dot(a, b, preferred_element_type=F32)


def _dot_nt(a, b):
    return lax.dot_general(a, b, (((1,), (1,)), ((), ())), preferred_element_type=F32)


def _dot_hi(a, b):
    return jnp.dot(a, b, preferred_element_type=F32, precision=lax.Precision.HIGHEST)


def _norm_mod(x, nw, shift, scale):
    ms = jnp.mean(x * x, axis=-1, keepdims=True)
    return (x * lax.rsqrt(ms + EPS) * nw) * (1.0 + scale) + shift


class Layout:
    def __init__(self, n_ctx_b, ctx_len, n_lat_b, lat_len, tm):
        self.n_ctx_b, self.ctx_len, self.n_lat_b, self.lat_len = n_ctx_b, ctx_len, n_lat_b, lat_len
        self.t_ctx = n_ctx_b * ctx_len
        self.t_lat = n_lat_b * lat_len
        self.t = self.t_ctx + self.t_lat
        self.tm = tm
        assert self.t_ctx % tm == 0 and lat_len % tm == 0
        self.ctx_tiles = self.t_ctx // tm
        self.lat_tiles_per_b = lat_len // tm
        self.n_tiles = self.t // tm

    def seg(self, i):
        return jnp.where(i < self.ctx_tiles, 0, 1 + (i - self.ctx_tiles) // self.lat_tiles_per_b)

    def rope_blk(self, i):
        return jnp.where(i < self.ctx_tiles, 0, 1 + (i - self.ctx_tiles) % self.lat_tiles_per_b)

    def stream_specs(self, d, pair):
        tm = self.tm
        if not pair:
            return [pl.BlockSpec((tm, d), lambda i, *_: (i, 0))]
        return [pl.BlockSpec((tm, d), lambda i, *_: (jnp.minimum(i, self.ctx_tiles - 1), 0)),
                pl.BlockSpec((tm, d), lambda i, *_: (jnp.maximum(i - self.ctx_tiles, 0), 0))]

    def stream_load(self, refs):
        if len(refs) == 1:
            return refs[0][...]
        return jnp.where(pl.program_id(0) < self.ctx_tiles, refs[0][...], refs[1][...])

    def stream_store(self, refs, val):
        if len(refs) == 1:
            refs[0][...] = val
            return
        i = pl.program_id(0)

        @pl.when(i < self.ctx_tiles)
        def _():
            refs[0][...] = val

        @pl.when(i >= self.ctx_tiles)
        def _():
            refs[1][...] = val


def _as_list(y):
    return list(y) if isinstance(y, (tuple, list)) else [y]


def _adaln_kernel(c_ref, w_ref, b_ref, o_ref):
    c = c_ref[...]
    s = c * jax.nn.sigmoid(c)
    o_ref[...] = _dot_hi(s, w_ref[...]) + b_ref[...]


def adaln_all(cond8, w_mod, b_mod, tn=1536):
    depth, d, n6 = w_mod.shape
    return pl.pallas_call(
        _adaln_kernel,
        out_shape=jax.ShapeDtypeStruct((depth, 8, n6), F32),
        grid=(depth, n6 // tn),
        in_specs=[pl.BlockSpec((8, d), lambda l, j: (0, 0)),
                  pl.BlockSpec((None, d, tn), lambda l, j: (l, 0, j)),
                  pl.BlockSpec((None, 1, tn), lambda l, j: (l, 0, j))],
        out_specs=pl.BlockSpec((None, 8, tn), lambda l, j: (l, 0, j)),
        compiler_params=_cparams(("parallel", "parallel")),
        name="adaln",
    )(cond8, w_mod, b_mod.reshape(depth, 1, n6))


def _kv_heads(ckv_b, kr, krp, wk_ref, wv_ref, knw, knwp, cos, sin, k_ref, v_ref):
    kc = knw * cos
    ks = knwp * sin
    for h in range(N_HEADS):
        kz = _dot(ckv_b, wk_ref[h]) + kr
        r = lax.rsqrt(jnp.sum(kz * kz, axis=-1, keepdims=True) * (1.0 / QK_DIM) + EPS)
        k_ref[h] = (r * (kz * kc + krp * ks)).astype(BF16)
    for hp in range(N_HEADS // 2):
        v_ref[hp] = _dot(ckv_b, wv_ref[hp]).astype(BF16)


def _mla_qkv_kernel(*refs, lay, n_y, q_lora, kv_lora):
    y_refs = refs[:n_y]
    (mod_ref, nw_ref, w_ref, qan_ref, kvan_ref, wq_ref, wk_ref, wv_ref, nrm_ref, cos_ref, sin_ref,
     ckv_ref, kr_ref, q_ref, k_ref, v_ref) = refs[n_y:]
    m = mod_ref[...]
    h = _norm_mod(lay.stream_load(y_refs), nw_ref[...], m[0:1], m[1:2])
    z = _dot(h.astype(BF16), w_ref[...])
    cq = z[:, :q_lora]
    cq = (cq * lax.rsqrt(jnp.mean(cq * cq, axis=-1, keepdims=True) + EPS) * qan_ref[...]).astype(BF16)
    ckv = z[:, q_lora:q_lora + kv_lora]
    ckv = ckv * lax.rsqrt(jnp.mean(ckv * ckv, axis=-1, keepdims=True) + EPS) * kvan_ref[...]
    ckv_ref[...] = ckv
    kr = z[:, q_lora + kv_lora:q_lora + kv_lora + LANES]
    kr_ref[...] = kr
    krp = z[:, q_lora + kv_lora + LANES:]
    nrm = nrm_ref[...]
    cos = cos_ref[...]
    sin = sin_ref[...]
    qc = nrm[0:1] * cos * (QK_DIM ** -0.5)
    qs = nrm[1:2] * sin * (QK_DIM ** -0.5)
    for hd in range(N_HEADS):
        zq = _dot(cq, wq_ref[hd])
        qm = zq[:, :HEAD_PAD]
        qp = zq[:, HEAD_PAD:]
        r = lax.rsqrt(jnp.sum(qm * qm, axis=-1, keepdims=True) * (1.0 / QK_DIM) + EPS)
        q_ref[hd] = (r * (qm * qc + qp * qs)).astype(BF16)
    _kv_heads(ckv.astype(BF16), kr, krp, wk_ref, wv_ref, nrm[2:3], nrm[3:4], cos, sin, k_ref, v_ref)


def mla_qkv(lay, y, mod, nw, w_a, qan, kvan, wq, wk, wv, nrm, cos_t, sin_t, q_lora, kv_lora):
    ys = _as_list(y)
    t, d = lay.t, ys[0].shape[1]
    tm = lay.tm
    row = lambda i: (i, 0)
    full = lambda i: (0, 0)
    hrow = lambda i: (0, i, 0)
    full3 = lambda i: (0, 0, 0)
    rope = lambda i: (lay.rope_blk(i), 0)
    nh = N_HEADS
    return pl.pallas_call(
        functools.partial(_mla_qkv_kernel, lay=lay, n_y=len(ys), q_lora=q_lora, kv_lora=kv_lora),
        out_shape=(jax.ShapeDtypeStruct((t, kv_lora), F32), jax.ShapeDtypeStruct((t, LANES), F32),
                   jax.ShapeDtypeStruct((nh, t, HEAD_PAD), BF16), jax.ShapeDtypeStruct((nh, t, HEAD_PAD), BF16),
                   jax.ShapeDtypeStruct((nh // 2, t, LANES), BF16)),
        grid=(lay.n_tiles,),
        in_specs=lay.stream_specs(d, len(ys) == 2) + [
            pl.BlockSpec((None, 8, d), lambda i: (lay.seg(i), 0, 0)),
            pl.BlockSpec((1, d), full), pl.BlockSpec(w_a.shape, full),
            pl.BlockSpec((1, q_lora), full), pl.BlockSpec((1, kv_lora), full),
            pl.BlockSpec(wq.shape, full3), pl.BlockSpec(wk.shape, full3), pl.BlockSpec(wv.shape, full3),
            pl.BlockSpec((8, LANES), full),
            pl.BlockSpec((tm, LANES), rope), pl.BlockSpec((tm, LANES), rope)],
        out_specs=(pl.BlockSpec((tm, kv_lora), row), pl.BlockSpec((tm, LANES), row),
                   pl.BlockSpec((nh, tm, HEAD_PAD), hrow), pl.BlockSpec((nh, tm, HEAD_PAD), hrow),
                   pl.BlockSpec((nh // 2, tm, LANES), hrow)),
        compiler_params=_cparams(("parallel",)),
        name="mla_qkv",
    )(*ys, mod, nw, w_a, qan, kvan, wq, wk, wv, nrm, cos_t, sin_t)


def _ctx_kv_kernel(ckv_ref, kr_ref, wk_ref, wv_ref, nrm_ref, k_ref, v_ref):
    nrm = nrm_ref[...]
    kr = kr_ref[...]
    one = jnp.ones((1, LANES), F32)
    _kv_heads(ckv_ref[...].astype(BF16), kr, kr, wk_ref, wv_ref, nrm[2:3], nrm[3:4],
              one, jnp.zeros((1, LANES), F32), k_ref, v_ref)


def ctx_kv(ckv, kr, wk, wv, nrm, tm):
    t = ckv.shape[0]
    row = lambda i: (i, 0)
    hrow = lambda i: (0, i, 0)
    full3 = lambda i: (0, 0, 0)
    nh = N_HEADS
    return pl.pallas_call(
        _ctx_kv_kernel,
        out_shape=(jax.ShapeDtypeStruct((nh, t, HEAD_PAD), BF16), jax.ShapeDtypeStruct((nh // 2, t, LANES), BF16)),
        grid=(t // tm,),
        in_specs=[pl.BlockSpec((tm, ckv.shape[1]), row), pl.BlockSpec((tm, LANES), row),
                  pl.BlockSpec(wk.shape, full3), pl.BlockSpec(wv.shape, full3),
                  pl.BlockSpec((8, LANES), lambda i: (0, 0))],
        out_specs=(pl.BlockSpec((nh, tm, HEAD_PAD), hrow), pl.BlockSpec((nh // 2, tm, LANES), hrow)),
        compiler_params=_cparams(("parallel",)),
        name="ctx_kv",
    )(ckv, kr, wk, wv, nrm)


def _attn_kernel(*refs, with_ctx, kblk):
    if with_ctx:
        q_ref, k_ref, v_ref, kc_ref, vc_ref, o_ref = refs
    else:
        q_ref, k_ref, v_ref, o_ref = refs
    seq = k_ref.shape[1]
    n_heads = q_ref.shape[0]
    blocks = [(kc_ref, vc_ref, 0, kc_ref.shape[1])] if with_ctx else []
    blocks += [(k_ref, v_ref, b0, kblk) for b0 in range(0, seq, kblk)]
    items = [(j, blk) for j in range(n_heads) for blk in blocks]

    def scores(item):
        j, (kr, _, b0, n) = item
        return _dot_nt(q_ref[j], kr[j, b0:b0 + n, :])

    outs = []
    m = l = acc = None
    s_next = scores(items[0])
    for idx, (j, (_, vr, b0, n)) in enumerate(items):
        s = s_next
        if idx + 1 < len(items):
            s_next = scores(items[idx + 1])
        mb = jnp.max(s, axis=-1, keepdims=True)
        m_new = mb if m is None else jnp.maximum(m, mb)
        p = jnp.exp(s - m_new)
        pv = _dot(p.astype(BF16), vr[j // 2, b0:b0 + n, :])
        ps = jnp.sum(p, axis=-1, keepdims=True)
        if m is None:
            l, acc = ps, pv
        else:
            a = jnp.exp(m - m_new)
            l, acc = a * l + ps, a * acc + pv
        m = m_new
        if idx + 1 == len(items) or items[idx + 1][0] != j:
            outs.append(acc / l)
            m = l = acc = None
    lane = lax.broadcasted_iota(jnp.int32, outs[0].shape, 1)
    for pr in range(n_heads // 2):
        o_ref[:, pr * LANES:(pr + 1) * LANES] = jnp.where(lane < V_DIM, outs[2 * pr], outs[2 * pr + 1]).astype(BF16)


def attention(q, k, v2, row0, n_b, seq, tq, kc=None, vc2=None, ctx_len=0, kblk=1024, pairs=1):
    assert row0 % seq == 0 and seq % tq == 0
    nq = seq // tq
    qb0 = row0 // tq
    kb0 = row0 // seq
    with_ctx = kc is not None
    hp_n = 2 * pairs
    in_specs = [pl.BlockSpec((hp_n, tq, HEAD_PAD), lambda b, hp, qi: (hp, qb0 + b * nq + qi, 0)),
                pl.BlockSpec((hp_n, seq, HEAD_PAD), lambda b, hp, qi: (hp, kb0 + b, 0)),
                pl.BlockSpec((pairs, seq, LANES), lambda b, hp, qi: (hp, kb0 + b, 0))]
    args = [q, k, v2]
    if with_ctx:
        in_specs += [pl.BlockSpec((hp_n, ctx_len, HEAD_PAD), lambda b, hp, qi: (hp, b, 0)),
                     pl.BlockSpec((pairs, ctx_len, LANES), lambda b, hp, qi: (hp, b, 0))]
        args += [kc, vc2]
    return pl.pallas_call(
        functools.partial(_attn_kernel, with_ctx=with_ctx, kblk=min(seq, kblk)),
        out_shape=jax.ShapeDtypeStruct((n_b * seq, (N_HEADS // 2) * LANES), BF16),
        grid=(n_b, N_HEADS // (2 * pairs), nq),
        in_specs=in_specs,
        out_specs=pl.BlockSpec((tq, pairs * LANES), lambda b, hp, qi: (b * nq + qi, hp)),
        compiler_params=_cparams(("parallel", "parallel", "parallel")),
        name="attn_ctx" if with_ctx else "attn",
    )(*args)


SSM_GPT = LANES // SSM_GROUP
SUBLANES = 8


def _lane_block_transpose(vs):
    n = len(vs)
    blk = lax.broadcasted_iota(jnp.int32, vs[0].shape, 1) // SSM_GROUP
    d = n // 2
    while d >= 1:
        keep = (blk & d) == 0
        new = list(vs)
        for a in range(n):
            if a & d == 0:
                b = a + d
                new[a] = jnp.where(keep, vs[a], pltpu.roll(vs[b], d * SSM_GROUP, 1))
                new[b] = jnp.where(keep, pltpu.roll(vs[a], LANES - d * SSM_GROUP, 1), vs[b])
        vs = new
        d //= 2
    return vs


def _ssm_in_kernel(y_ref, mod_ref, nw_ref, w_ref, u_ref, us_ref, ug_ref):
    m = mod_ref[...]
    h = _norm_mod(y_ref[...], nw_ref[...], m[0:1], m[1:2])
    z = _dot(h.astype(BF16), w_ref[...])
    n_lt = us_ref.shape[0]
    for j in range(n_lt):
        us_ref[j] = z[:, j * LANES:(j + 1) * LANES]
    n_ch = us_ref.shape[1] // SSM_CHUNK
    for j in range(n_lt):
        for rb in range(n_ch // SUBLANES):
            for hf in range(SSM_CHUNK // SSM_GPT):
                src = [us_ref[j, pl.ds(rb * SUBLANES * SSM_CHUNK + hf * SSM_GPT + tt, SUBLANES, stride=SSM_CHUNK), :]
                       for tt in range(SSM_GPT)]
                for gl, v in enumerate(_lane_block_transpose(src)):
                    ug_ref[j * SSM_GPT + gl, rb * SUBLANES:(rb + 1) * SUBLANES, hf * LANES:(hf + 1) * LANES] = v
    u_ref[...] = ug_ref[...].astype(BF16)


def ssm_in(lay, y, mod, nw, w_in):
    t, d = y.shape
    tm = lay.tm
    n = w_in.shape[1]
    g = n // SSM_GROUP
    n_ch = tm // SSM_CHUNK
    qk = SSM_CHUNK * SSM_GROUP
    row = lambda i: (i, 0)
    return pl.pallas_call(
        _ssm_in_kernel,
        out_shape=jax.ShapeDtypeStruct((g, t // SSM_CHUNK, qk), BF16),
        grid=(lay.n_tiles,),
        in_specs=[pl.BlockSpec((tm, d), row), pl.BlockSpec((None, 8, d), lambda i: (lay.seg(i), 0, 0)),
                  pl.BlockSpec((1, d), lambda i: (0, 0)), pl.BlockSpec((d, n), lambda i: (0, 0))],
        out_specs=pl.BlockSpec((g, n_ch, qk), lambda i: (0, i, 0)),
        scratch_shapes=[pltpu.VMEM((n // LANES, tm, LANES), F32), pltpu.VMEM((g, n_ch, qk), F32)],
        compiler_params=_cparams(("parallel",)),
        name="ssm_in",
    )(y, mod, nw, w_in)


def _ssm_prep_kernel(ar_ref, ai_ref, ld_ref, arc_ref, aic_ref, ldc_ref, btr_ref, bti_ref, ctr_ref, cti_ref, dsk_ref,
                     mt_ref, ett_ref, ft_ref, coef_ref):
    q = SSM_CHUNK
    k = SSM_GROUP
    p = SSM_STATE
    qk = q * k
    fwd = pl.program_id(0) == 0
    dt = jnp.exp(ld_ref[...])
    are = ar_ref[...]
    aim = ai_ref[...]
    mag = jnp.exp(dt * are)
    abr = mag * jnp.cos(dt * aim)
    abi = mag * jnp.sin(dt * aim)
    den = are * are + aim * aim
    nr = abr - 1.0
    cf_re = (nr * are + abi * aim) / den
    cf_im = (abi * are - nr * aim) / den
    btr = btr_ref[...]
    bti = bti_ref[...]
    bbt_re = cf_re * btr - cf_im * bti
    bbt_im = cf_re * bti + cf_im * btr
    bbt_re_t = jnp.concatenate([bbt_re] * q, axis=0)
    bbt_im_t = jnp.concatenate([bbt_im] * q, axis=0)
    s_idx = lax.broadcasted_iota(jnp.int32, (q, 1), 0).astype(F32)
    pw = jnp.where(fwd, (q - 1.0) - s_idx, s_idx)
    pm = jnp.exp(pw * dt * are)
    rep_rows = lambda x: jnp.concatenate([jnp.broadcast_to(x[s:s + 1], (k, p)) for s in range(q)], axis=0)
    pr = rep_rows(pm * jnp.cos(pw * dt * aim))
    pi = rep_rows(pm * jnp.sin(pw * dt * aim))
    et_re = pr * bbt_re_t - pi * bbt_im_t
    et_im = pr * bbt_im_t + pi * bbt_re_t
    ett_ref[...] = jnp.concatenate([et_re, et_im, et_im, et_re], axis=1).astype(BF16)
    qm = jnp.exp(q * dt * are)
    aq_re = qm * jnp.cos(q * dt * aim)
    aq_im = qm * jnp.sin(q * dt * aim)
    c1 = jnp.concatenate([aq_re, aq_re], axis=1)
    c2 = jnp.concatenate([-aq_im, aq_im], axis=1)
    coef_ref[...] = jnp.concatenate([c1, c2, jnp.zeros((6, 2 * p), F32)], axis=0)
    dtc = jnp.exp(ldc_ref[...])
    arec = arc_ref[...]
    aimc = aic_ref[...]
    ctr = ctr_ref[...]
    cti = cti_ref[...]
    tau = lax.broadcasted_iota(jnp.int32, (1, q), 1).astype(F32)
    tau0 = jnp.where(fwd, tau, (q - 1.0) - tau)
    m0 = jnp.exp(tau0 * dtc * arec)
    p0r_q = m0 * jnp.cos(tau0 * dtc * aimc)
    p0i_q = m0 * jnp.sin(tau0 * dtc * aimc)
    magc = jnp.exp(dtc * arec)
    abr_c = magc * jnp.cos(dtc * aimc)
    abi_c = magc * jnp.sin(dtc * aimc)
    spread = jnp.where(lax.broadcasted_iota(jnp.int32, (q, qk), 1) // k == lax.broadcasted_iota(jnp.int32, (q, qk), 0),
                       1.0, 0.0)
    p0r = _dot_hi(p0r_q, spread)
    p0i = _dot_hi(p0i_q, spread)
    p1r = _dot_hi(p0r_q * abr_c - p0i_q * abi_c, spread)
    p1i = _dot_hi(p0r_q * abi_c + p0i_q * abr_c, spread)
    ft_ref[...] = jnp.concatenate([ctr * p1r - cti * p1i, -(ctr * p1i + cti * p1r)], axis=0).astype(BF16)
    left_re = ctr * p0r - cti * p0i
    left_im = ctr * p0i + cti * p0r
    kt = _dot_hi(bbt_re, left_re) - _dot_hi(bbt_im, left_im)
    lane = lax.broadcasted_iota(jnp.int32, (k, qk), 1)
    rowi = lax.broadcasted_iota(jnp.int32, (k, qk), 0)
    kt_f = kt + jnp.where(lane == rowi, dsk_ref[...], 0.0)
    rows_f, rows_b = [], []
    for s in range(q):
        rows_f.append(kt_f if s == 0 else jnp.where(lane >= s * k, pltpu.roll(kt_f, s * k, 1), 0.0))
        sh = ((s + 1) * k) % qk
        rows_b.append(kt if sh == 0 else jnp.where(lane < (s + 1) * k, pltpu.roll(kt, sh, 1), 0.0))
    mt = jnp.where(fwd, jnp.concatenate(rows_f, axis=0), jnp.concatenate(rows_b, axis=0))
    mt_ref[...] = mt.astype(BF16)


def ssm_prep(a_re, a_im, log_dt, b_re, b_im, c_re, c_im, d_skip):
    nd, g, p = a_re.shape
    k = b_re.shape[-1]
    qk = SSM_CHUNK * k
    ld = jnp.broadcast_to(log_dt[..., None], (nd, g, p))
    dsk = jnp.broadcast_to(jnp.tile(d_skip.reshape(g, 1, k), (1, 1, SSM_CHUNK))[None], (nd, g, 1, qk))
    rowv = lambda x: x.reshape(nd, g, 1, p)
    colv = lambda x: x.reshape(nd, g, p, 1)
    bt = lambda x: jnp.swapaxes(x, -1, -2)
    ct = lambda x: jnp.tile(jnp.swapaxes(x, -1, -2), (1, 1, 1, SSM_CHUNK))
    spec = lambda r, c: pl.BlockSpec((None, None, r, c), lambda d, j: (d, j, 0, 0))
    return pl.pallas_call(
        _ssm_prep_kernel,
        out_shape=(jax.ShapeDtypeStruct((nd, g, qk, qk), BF16), jax.ShapeDtypeStruct((nd, g, qk, 4 * p), BF16),
                   jax.ShapeDtypeStruct((nd, g, 2 * p, qk), BF16), jax.ShapeDtypeStruct((nd, g, 8, 2 * p), F32)),
        grid=(nd, g),
        in_specs=[spec(1, p), spec(1, p), spec(1, p), spec(p, 1), spec(p, 1), spec(p, 1),
                  spec(k, p), spec(k, p), spec(p, qk), spec(p, qk), spec(1, qk)],
        out_specs=(spec(qk, qk), spec(qk, 4 * p), spec(2 * p, qk), spec(8, 2 * p)),
        compiler_params=_cparams(("parallel", "parallel")),
        name="ssm_prep",
    )(rowv(a_re), rowv(a_im), rowv(ld), colv(a_re), colv(a_im), colv(ld), bt(b_re), bt(b_im), ct(c_re), ct(c_im), dsk)


def _ssm_scan_kernel(u_ref, mt_ref, ett_ref, ft_ref, coef_ref, x0_ref, x0s_ref, y_ref, hfin_ref,
                     sx_ref, sxs_ref, hp_ref, *, gpb, ctx_b, ctx_chunks, lat_b, lat_chunks, rchunk):
    w = 2 * SSM_STATE
    n_rows = u_ref.shape[1]
    ctx_rows = ctx_b * ctx_chunks
    for d in range(2):
        for g in range(gpb):
            for r0 in range(0, n_rows, rchunk):
                ss = _dot(u_ref[g, r0:r0 + rchunk, :], ett_ref[d, g])
                sx_ref[g, r0:r0 + rchunk, :] = ss[:, :w]
                sxs_ref[g, r0:r0 + rchunk, :] = ss[:, w:]
        c1 = [coef_ref[d, g][0:1] for g in range(gpb)]
        c2 = [coef_ref[d, g][1:2] for g in range(gpb)]

        def make_body(base, nb, nc):
            def body(i, carry):
                c = i if d == 0 else nc - 1 - i
                rows = pl.ds(base + c, nb, stride=nc)
                out = []
                for g in range(gpb):
                    x, xs = carry[g]
                    hp_ref[g, rows, :] = x
                    xn = c1[g] * x + c2[g] * xs + sx_ref[g, rows, :]
                    xsn = c1[g] * xs - c2[g] * x + sxs_ref[g, rows, :]
                    out.append((xn, xsn))
                return tuple(out)
            return body

        z = jnp.zeros((ctx_b, w), F32)
        fin = lax.fori_loop(0, ctx_chunks, make_body(0, ctx_b, ctx_chunks), tuple((z, z) for _ in range(gpb)))
        hfin_ref[d] = jnp.concatenate([fin[g][0] for g in range(gpb)], axis=1)
        init = tuple((x0_ref[d][:, g * w:(g + 1) * w], x0s_ref[d][:, g * w:(g + 1) * w]) for g in range(gpb))
        lax.fori_loop(0, lat_chunks, make_body(ctx_rows, lat_b, lat_chunks), init, unroll=4)
        for g in range(gpb):
            for r0 in range(0, n_rows, rchunk):
                y = _dot(u_ref[g, r0:r0 + rchunk, :], mt_ref[d, g])
                y = y + _dot(hp_ref[g, r0:r0 + rchunk, :].astype(BF16), ft_ref[d, g])
                if d == 0:
                    y_ref[g, r0:r0 + rchunk, :] = y
                else:
                    y_ref[g, r0:r0 + rchunk, :] += y


def ssm_scan(u, mt, ett, ft, coef, x0, x0s, ctx_b, ctx_chunks, lat_b, lat_chunks, gpb=4):
    g, n_rows, qk = u.shape
    w = 2 * SSM_STATE
    rchunk = math.gcd(n_rows, 512)
    assert n_rows == ctx_b * ctx_chunks + lat_b * lat_chunks
    ublk = pl.BlockSpec((gpb, n_rows, qk), lambda j: (j, 0, 0))
    blk = lambda r, c: pl.BlockSpec((2, gpb, r, c), lambda j: (0, j, 0, 0))
    vec = lambda r: pl.BlockSpec((2, r, gpb * w), lambda j: (0, 0, j))
    return pl.pallas_call(
        functools.partial(_ssm_scan_kernel, gpb=gpb, ctx_b=ctx_b, ctx_chunks=ctx_chunks, lat_b=lat_b,
                          lat_chunks=lat_chunks, rchunk=rchunk),
        out_shape=(jax.ShapeDtypeStruct((g, n_rows, qk), F32), jax.ShapeDtypeStruct((2, ctx_b, g * w), F32)),
        grid=(g // gpb,),
        in_specs=[ublk, blk(qk, qk), blk(qk, 2 * w), blk(w, qk), blk(8, w), vec(lat_b), vec(lat_b)],
        out_specs=(ublk, vec(ctx_b)),
        scratch_shapes=[pltpu.VMEM((gpb, n_rows, w), F32), pltpu.VMEM((gpb, n_rows, w), F32),
                        pltpu.VMEM((gpb, n_rows, w), F32)],
        compiler_params=_cparams(("parallel",)),
        name="ssm_scan",
    )(u, mt, ett, ft, coef, x0, x0s)


def _ssm_out_kernel(y_ref, mod_ref, yc_ref, w_ref, nw_ref, wr_ref, br_ref, tri_ref, out_ref, h_ref, ri_ref, pc_ref,
                    s_ref, *, d_model):
    n_ch = yc_ref.shape[1]
    n_lt = s_ref.shape[0]
    for j in range(n_lt):
        for rb in range(n_ch // SUBLANES):
            for hf in range(SSM_CHUNK // SSM_GPT):
                src = [yc_ref[j * SSM_GPT + gl, rb * SUBLANES:(rb + 1) * SUBLANES, hf * LANES:(hf + 1) * LANES]
                       for gl in range(SSM_GPT)]
                for tt, v in enumerate(_lane_block_transpose(src)):
                    s_ref[j, pl.ds(rb * SUBLANES * SSM_CHUNK + hf * SSM_GPT + tt, SUBLANES, stride=SSM_CHUNK), :] = v
    s = jnp.concatenate([s_ref[j] for j in range(n_lt)], axis=1)
    a = jax.nn.gelu(s, approximate=True).astype(BF16)
    z = _dot(a, w_ref[...])
    m = mod_ref[...]
    y = y_ref[...] + m[2:3] * (z[:, :d_model] * jax.nn.sigmoid(z[:, d_model:]))
    out_ref[...] = y
    _route_core(y, m, nw_ref, wr_ref, br_ref, tri_ref, h_ref, ri_ref, pc_ref)


def ssm_out_route(lay, y, mod, ych, w_glu, route):
    t, d = y.shape
    tm = lay.tm
    g, _, qk = ych.shape
    n_ch = tm // SSM_CHUNK
    row = lambda i: (i, 0)
    r_args, r_in, r_shape, r_out = _route_io(lay, d, *route)
    return pl.pallas_call(
        functools.partial(_ssm_out_kernel, d_model=d),
        out_shape=[jax.ShapeDtypeStruct((t, d), F32)] + r_shape,
        grid=(lay.n_tiles,),
        in_specs=[pl.BlockSpec((tm, d), row), pl.BlockSpec((None, 8, d), lambda i: (lay.seg(i), 0, 0)),
                  pl.BlockSpec((g, n_ch, qk), lambda i: (0, i, 0)),
                  pl.BlockSpec(w_glu.shape, lambda i: (0, 0))] + r_in,
        out_specs=[pl.BlockSpec((tm, d), row)] + r_out,
        scratch_shapes=[pltpu.VMEM((g * SSM_GROUP // LANES, tm, LANES), F32)],
        input_output_aliases={0: 0},
        compiler_params=_cparams(("parallel",)),
        name="ssm_out_route",
    )(y, mod, ych, w_glu, *r_args)


def s5_layer(lay, y, mod, nw, route, state_re, state_im, w_in, a_re, a_im, log_dt, b_re, b_im, c_re, c_im, d_skip,
             w_glu):
    uch = ssm_in(lay, y, mod, nw, w_in.astype(BF16))
    g = a_re.shape[1]
    p = SSM_STATE
    mt, ett, ft, coef = ssm_prep(a_re, a_im, log_dt, b_re, b_im, c_re, c_im, d_skip)
    st = jnp.concatenate([state_re, state_im], axis=-1)
    sts = jnp.concatenate([state_im, state_re], axis=-1)
    x0 = jnp.transpose(st, (1, 0, 2, 3)).reshape(2, lay.n_lat_b, g * 2 * p)
    x0s = jnp.transpose(sts, (1, 0, 2, 3)).reshape(2, lay.n_lat_b, g * 2 * p)
    ych, hfin = ssm_scan(uch, mt, ett, ft, coef, x0, x0s, lay.n_ctx_b, lay.ctx_len // SSM_CHUNK,
                         lay.n_lat_b, lay.lat_len // SSM_CHUNK)
    routed = ssm_out_route(lay, y, mod, ych, w_glu.astype(BF16), route)
    hf = hfin.reshape(2, lay.n_ctx_b, g, 2, p)
    new_re = jnp.transpose(hf[:, :, :, 0], (1, 0, 2, 3))
    new_im = jnp.transpose(hf[:, :, :, 1], (1, 0, 2, 3))
    return routed, new_re, new_im


def _route_core(y, m, nw_ref, wr_ref, br_ref, tri_ref, h_ref, ri_ref, pc_ref):
    n_exp = br_ref.shape[0]
    h = _norm_mod(y, nw_ref[...], m[3:4], m[4:5])
    h_hi = h.astype(BF16)
    h_ref[...] = h_hi
    h_lo = (h - h_hi.astype(F32)).astype(BF16)
    wr = wr_ref[...]
    lt = _dot_nt(wr, h_hi)
    logits = lt[:n_exp] + lt[n_exp:] + _dot_nt(wr[:n_exp], h_lo)
    scores = jax.nn.sigmoid(logits)
    sel = scores + br_ref[...]
    epg = EXPERTS_PER_GROUP
    row = lambda x, e: x[e:e + 1, :]
    gscore = []
    for g in range(N_EXPERT_GROUPS):
        a, b, c, d = (row(sel, g * epg + j) for j in range(epg))
        m1, n1, m2, n2 = jnp.maximum(a, b), jnp.minimum(a, b), jnp.maximum(c, d), jnp.minimum(c, d)
        gscore.append(jnp.maximum(m1, m2) + jnp.maximum(jnp.minimum(m1, m2), jnp.maximum(n1, n2)))
    best = gscore[0]
    gi = jnp.zeros_like(best, dtype=jnp.int32)
    for g in range(1, N_EXPERT_GROUPS):
        better = gscore[g] > best
        gi = jnp.where(better, g, gi)
        best = jnp.where(better, gscore[g], best)

    def pick(x, j):
        out = row(x, j)
        for g in range(1, N_EXPERT_GROUPS):
            out = jnp.where(gi == g, row(x, g * epg + j), out)
        return out

    sv = [pick(sel, j) for j in range(epg)]
    cv = [pick(scores, j) for j in range(epg)]
    b1, i1, w1 = sv[0], jnp.zeros_like(gi), cv[0]
    for j in range(1, epg):
        better = sv[j] > b1
        i1 = jnp.where(better, j, i1)
        w1 = jnp.where(better, cv[j], w1)
        b1 = jnp.where(better, sv[j], b1)
    neg = jnp.full_like(b1, -jnp.inf)
    b2, i2, w2 = neg, jnp.zeros_like(gi), jnp.zeros_like(w1)
    for j in range(epg):
        better = (i1 != j) & (sv[j] > b2)
        i2 = jnp.where(better, j, i2)
        w2 = jnp.where(better, cv[j], w2)
        b2 = jnp.where(better, sv[j], b2)
    tot = w1 + w2
    e1 = gi * epg + i1
    e2 = gi * epg + i2
    eid = lax.broadcasted_iota(jnp.int32, logits.shape, 0)
    m1h = eid == e1
    m2h = eid == e2
    mc = jnp.where(m1h | m2h, 1.0, 0.0)
    pref = _dot(mc.astype(BF16), tri_ref[...])
    cnt = jnp.sum(mc, axis=1, keepdims=True)
    pc_al = jnp.ceil(cnt * (1.0 / SEG_ALIGN))
    pcb = jnp.broadcast_to(pc_al, (n_exp, LANES))
    er = lax.broadcasted_iota(jnp.int32, (n_exp, n_exp), 0)
    ec = lax.broadcasted_iota(jnp.int32, (n_exp, n_exp), 1)
    lower = jnp.where(ec < er, 1.0, 0.0).astype(BF16)
    seg = _dot(lower, pcb.astype(BF16))[:, 0:1] * SEG_ALIGN
    slot = seg + pref
    pos1 = jnp.sum(jnp.where(m1h, slot, 0.0), axis=0, keepdims=True)
    pos2 = jnp.sum(jnp.where(m2h, slot, 0.0), axis=0, keepdims=True)
    zero = jnp.zeros_like(pos1)
    ri_ref[...] = jnp.concatenate([pos1, pos2, w1 / tot, w2 / tot, zero, zero, zero, zero], axis=0)
    pc_ref[...] = pcb * SEG_ALIGN


def _route_io(lay, d, nw, wr_t, br, tri):
    tm = lay.tm
    n_exp = br.shape[0]
    full = lambda i: (0, 0)
    args = [nw, wr_t, br, tri]
    in_specs = [pl.BlockSpec((1, d), full), pl.BlockSpec(wr_t.shape, full), pl.BlockSpec((n_exp, 1), full),
                pl.BlockSpec((tm, tm), full)]
    out_shape = [jax.ShapeDtypeStruct((lay.t, d), BF16), jax.ShapeDtypeStruct((8, lay.t), F32),
                 jax.ShapeDtypeStruct((lay.n_tiles, n_exp, LANES), F32)]
    out_specs = [pl.BlockSpec((tm, d), lambda i: (i, 0)), pl.BlockSpec((8, tm), lambda i: (0, i)),
                 pl.BlockSpec((None, n_exp, LANES), lambda i: (i, 0, 0))]
    return args, in_specs, out_shape, out_specs


def _proj_route_kernel(*refs, lay, n_y):
    y_refs = refs[:n_y]
    mod_ref, o_ref, w_ref, nw_ref, wr_ref, br_ref, tri_ref, out_ref, h_ref, ri_ref, pc_ref = refs[n_y:]
    m = mod_ref[...]
    y = lay.stream_load(y_refs) + m[2:3] * _dot(o_ref[...], w_ref[...])
    out_ref[...] = y
    _route_core(y, m, nw_ref, wr_ref, br_ref, tri_ref, h_ref, ri_ref, pc_ref)


def proj_route(lay, y, mod, o, w, route):
    ys = _as_list(y)
    d = ys[0].shape[1]
    tm = lay.tm
    row = lambda i: (i, 0)
    r_args, r_in, r_shape, r_out = _route_io(lay, d, *route)
    return pl.pallas_call(
        functools.partial(_proj_route_kernel, lay=lay, n_y=len(ys)),
        out_shape=[jax.ShapeDtypeStruct((lay.t, d), F32)] + r_shape,
        grid=(lay.n_tiles,),
        in_specs=lay.stream_specs(d, len(ys) == 2) + [
            pl.BlockSpec((None, 8, d), lambda i: (lay.seg(i), 0, 0)),
            pl.BlockSpec((tm, o.shape[1]), row), pl.BlockSpec(w.shape, lambda i: (0, 0))] + r_in,
        out_specs=[pl.BlockSpec((tm, d), row)] + r_out,
        input_output_aliases={0: 0} if len(ys) == 1 else {},
        compiler_params=_cparams(("parallel",)),
        name="proj_route",
    )(*ys, mod, o, w, *r_args)


def _segment_pieces(i, off_ref, pc_ref, n_exp, sizes):
    local = 0
    for e in range(n_exp):
        n = pc_ref[i * n_exp + e]
        g0 = off_ref[i * n_exp + e]
        for size in sizes:
            above = n & (-2 * size)
            yield (e, (n & size) != 0, pl.multiple_of(local + above, SEG_ALIGN),
                   pl.multiple_of(g0 + above, SEG_ALIGN), size)
        local = local + n


def _pieces(n, sizes):
    for size in sizes:
        yield (n & size) != 0, n & (-2 * size), size


def _moe_permute_kernel(off_ref, pc_ref, goff_ref, gn_ref, h_ref, ri_ref, xs_ref, ys_ref, zp_ref, sem, gsem,
                        *, n_exp, sizes, gap_sizes, rchunk):
    i = pl.program_id(0)
    last = pl.num_programs(0) - 1
    slot = i % 2
    ri = ri_ref[...]
    pos1 = ri[0:1]
    pos2 = ri[1:2]
    h = h_ref[...]
    n_loc = ys_ref.shape[1]
    for r0 in range(0, n_loc, rchunk):
        r = (lax.broadcasted_iota(jnp.int32, (rchunk, h.shape[0]), 0) + r0).astype(F32)
        p = jnp.where(pos1 == r, 1.0, 0.0) + jnp.where(pos2 == r, 1.0, 0.0)
        ys_ref[slot, r0:r0 + rchunk, :] = _dot(p.astype(BF16), h).astype(BF16)

    def copies(tile, sl):
        for e, present, lrow, grow, size in _segment_pieces(tile, off_ref, pc_ref, n_exp, sizes):
            yield present, pltpu.make_async_copy(ys_ref.at[sl, pl.ds(lrow, size)], xs_ref.at[pl.ds(grow, size)],
                                                 sem.at[sl, e])

    for present, cp in copies(i, slot):
        pl.when(present)(cp.start)

    @pl.when(i > 0)
    def _():
        for present, cp in copies(i - 1, 1 - slot):
            pl.when(present)(cp.wait)

    @pl.when(i == last)
    def _():
        for present, cp in copies(i, slot):
            pl.when(present)(cp.wait)
        zp_ref[...] = jnp.zeros_like(zp_ref)

        def gaps():
            for e in range(n_exp):
                g0 = goff_ref[e]
                for present, above, size in _pieces(gn_ref[e], gap_sizes):
                    yield present, pltpu.make_async_copy(
                        zp_ref.at[pl.ds(0, size)], xs_ref.at[pl.ds(pl.multiple_of(g0 + above, SEG_ALIGN), size)],
                        gsem.at[e])

        for present, cp in gaps():
            pl.when(present)(cp.start)
        for present, cp in gaps():
            pl.when(present)(cp.wait)

        zrows = zp_ref.shape[0]
        tail0 = goff_ref[n_exp]
        tail = lambda c: pltpu.make_async_copy(
            zp_ref, xs_ref.at[pl.ds(pl.multiple_of(tail0 + c * zrows, zrows), zrows)], gsem.at[0])

        @pl.loop(0, gn_ref[n_exp])
        def _(c):
            tail(c).start()

        @pl.loop(0, gn_ref[n_exp])
        def _(c):
            tail(c).wait()


def moe_permute(lay, h, rinfo, off, pcs, gap_off, gap_n, n_sorted, n_loc, n_exp, tmm):
    t, d = h.shape
    tm = lay.tm
    sizes = [s for s in (1 << b for b in range(12, 3, -1)) if s <= tm]
    gap_sizes = [s for s in (1 << b for b in range(12, 3, -1)) if s < tmm]
    return pl.pallas_call(
        functools.partial(_moe_permute_kernel, n_exp=n_exp, sizes=sizes, gap_sizes=gap_sizes, rchunk=256),
        out_shape=jax.ShapeDtypeStruct((n_sorted, d), BF16),
        grid_spec=pltpu.PrefetchScalarGridSpec(
            num_scalar_prefetch=4, grid=(lay.n_tiles,),
            in_specs=[pl.BlockSpec((tm, d), lambda i, *_: (i, 0)), pl.BlockSpec((8, tm), lambda i, *_: (0, i))],
            out_specs=pl.BlockSpec(memory_space=pl.ANY),
            scratch_shapes=[pltpu.VMEM((2, n_loc, d), BF16), pltpu.VMEM((gap_sizes[0], d), BF16),
                            pltpu.SemaphoreType.DMA((2, n_exp)), pltpu.SemaphoreType.DMA((n_exp,))]),
        compiler_params=_cparams(("arbitrary",)),
        name="moe_permute",
    )(off, pcs, gap_off, gap_n, h, rinfo)


def _moe_expert_kernel(te_ref, tv_ref, tf_ref, xb_ref, x_ref, wg_ref, wu_ref, wd_ref, z_ref, wgb_ref, wub_ref, wdb_ref):
    del te_ref, xb_ref
    r = pl.program_id(0)

    @pl.when(tf_ref[r] == 1)
    def _():
        wgb_ref[...] = wg_ref[...].astype(BF16)
        wub_ref[...] = wu_ref[...].astype(BF16)
        wdb_ref[...] = wd_ref[...].astype(BF16)

    @pl.when(tv_ref[r] > 0)
    def _():
        x = x_ref[...]
        a = _dot(x, wgb_ref[...])
        b = _dot(x, wub_ref[...])
        hid = (a * jax.nn.sigmoid(a)) * b
        z_ref[...] = _dot(hid.astype(BF16), wdb_ref[...]).astype(BF16)

    @pl.when(tv_ref[r] == 0)
    def _():
        z_ref[...] = jnp.zeros_like(z_ref)


def moe_experts(xs, tile_expert, tile_valid, tile_first, x_block, layer, wg, wu, wd, tmm):
    n_sorted, d = xs.shape
    f = wg.shape[-1]
    wmap = lambda r, te, tv, tf, xb: (layer, te[r], 0, 0)
    return pl.pallas_call(
        _moe_expert_kernel,
        out_shape=jax.ShapeDtypeStruct((n_sorted, d), BF16),
        grid_spec=pltpu.PrefetchScalarGridSpec(
            num_scalar_prefetch=4, grid=(n_sorted // tmm,),
            in_specs=[pl.BlockSpec((tmm, d), lambda r, te, tv, tf, xb: (xb[r], 0)),
                      pl.BlockSpec((None, None, d, f), wmap), pl.BlockSpec((None, None, d, f), wmap),
                      pl.BlockSpec((None, None, f, d), wmap)],
            out_specs=pl.BlockSpec((tmm, d), lambda r, te, tv, tf, xb: (r, 0)),
            scratch_shapes=[pltpu.VMEM((d, f), BF16), pltpu.VMEM((d, f), BF16), pltpu.VMEM((f, d), BF16)]),
        compiler_params=_cparams(("arbitrary",)),
        name="moe_experts",
    )(tile_expert, tile_valid, tile_first, x_block, xs, wg, wu, wd)


def _moe_combine_kernel(off_ref, pc_ref, y_ref, mod_ref, rit_ref, zs_ref, *rest, lay, n_exp, sizes):
    out_refs, (zt_ref, sem) = rest[:-2], rest[-2:]
    i = pl.program_id(0)
    slot = i % 2

    def copies(tile, sl):
        for e, present, lrow, grow, size in _segment_pieces(tile, off_ref, pc_ref, n_exp, sizes):
            yield present, pltpu.make_async_copy(zs_ref.at[pl.ds(grow, size)], zt_ref.at[sl, pl.ds(lrow, size)],
                                                 sem.at[sl, e])

    def fetch(tile, sl):
        zt_ref[sl] = jnp.zeros(zt_ref.shape[1:], BF16)
        for present, cp in copies(tile, sl):
            pl.when(present)(cp.start)

    @pl.when(i == 0)
    def _():
        fetch(i, slot)

    @pl.when(i + 1 < pl.num_programs(0))
    def _():
        fetch(i + 1, 1 - slot)

    rit = rit_ref[...]
    n_loc = zt_ref.shape[1]
    r = lax.broadcasted_iota(jnp.int32, (rit.shape[0], n_loc), 1).astype(F32)
    pw = jnp.where(rit[:, 0:1] == r, rit[:, 2:3], 0.0) + jnp.where(rit[:, 1:2] == r, rit[:, 3:4], 0.0)
    pw = pw.astype(BF16)
    for present, cp in copies(i, slot):
        pl.when(present)(cp.wait)
    lay.stream_store(out_refs, y_ref[...] + mod_ref[...][5:6] * _dot(pw, zt_ref[slot]))


def moe_combine(lay, y, mod, rinfo_t, zs, off, pcs, n_loc, n_exp, pair_out):
    t, d = y.shape
    tm = lay.tm
    sizes = [s for s in (1 << b for b in range(12, 3, -1)) if s <= tm]
    if pair_out:
        out_shape = (jax.ShapeDtypeStruct((lay.t_ctx, d), F32), jax.ShapeDtypeStruct((lay.t_lat, d), F32))
    else:
        out_shape = (jax.ShapeDtypeStruct((t, d), F32),)
    out = pl.pallas_call(
        functools.partial(_moe_combine_kernel, lay=lay, n_exp=n_exp, sizes=sizes),
        out_shape=out_shape,
        grid_spec=pltpu.PrefetchScalarGridSpec(
            num_scalar_prefetch=2, grid=(lay.n_tiles,),
            in_specs=[pl.BlockSpec((tm, d), lambda i, o, p: (i, 0)),
                      pl.BlockSpec((None, 8, d), lambda i, o, p: (lay.seg(i), 0, 0)),
                      pl.BlockSpec((tm, 8), lambda i, o, p: (i, 0)),
                      pl.BlockSpec(memory_space=pl.ANY)],
            out_specs=tuple(lay.stream_specs(d, pair_out)),
            scratch_shapes=[pltpu.VMEM((2, n_loc, d), BF16), pltpu.SemaphoreType.DMA((2, n_exp))]),
        input_output_aliases={} if pair_out else {2: 0},
        compiler_params=_cparams(("arbitrary",)),
        name="moe_combine",
    )(off, pcs, y, mod, rinfo_t, zs)
    return out if pair_out else out[0]


def _round_up(x, m):
    return (x + m - 1) // m * m


def moe_layer(lay, routed, mod, layer, wg, wu, wd, tmm, pair_out=False):
    y, h, rinfo, pc = routed
    t, d = y.shape
    n_exp = pc.shape[1]
    pc = pc[:, :, 0].astype(jnp.int32)
    n_loc = 2 * lay.tm + SEG_ALIGN * n_exp
    n_sorted = _round_up(2 * t + lay.n_tiles * n_exp * (SEG_ALIGN - 1) + n_exp * (tmm - 1), tmm)
    tot = jnp.sum(pc, axis=0)
    region = _round_up(tot, tmm)
    ends = jnp.cumsum(region)
    base = ends - region
    off = (base[None, :] + jnp.cumsum(pc, axis=0) - pc).reshape(-1)
    starts = jnp.arange(n_sorted // tmm, dtype=jnp.int32) * tmm
    te = jnp.minimum(jnp.sum(starts[:, None] >= ends[None, :], axis=1), n_exp - 1).astype(jnp.int32)
    tv = jnp.clip(tot[te] - (starts - base[te]), 0, tmm).astype(jnp.int32)
    tf = jnp.concatenate([jnp.ones((1,), jnp.int32), (te[1:] != te[:-1]).astype(jnp.int32)])
    xb = jnp.minimum(starts // tmm, jnp.maximum(ends[-1] // tmm - 1, 0)).astype(jnp.int32)
    pcs = pc.reshape(-1)
    assert tmm & (tmm - 1) == 0
    zrows = tmm // 2
    gap_off = jnp.concatenate([base + tot, ends[-1:]]).astype(jnp.int32)
    gap_n = jnp.concatenate([region - tot, (n_sorted - ends[-1:]) // zrows]).astype(jnp.int32)
    xs = moe_permute(lay, h, rinfo, off, pcs, gap_off, gap_n, n_sorted, n_loc, n_exp, tmm)
    zs = moe_experts(xs, te, tv, tf, xb, layer, wg, wu, wd, tmm)
    return moe_combine(lay, y, mod, rinfo.T, zs, off, pcs, n_loc, n_exp, pair_out)


def _rope_perm():
    d = np.arange(QK_ROPE)
    return QK_NOPE + (d ^ (QK_ROPE // 4))


def _pad_cols(x, n):
    return jnp.pad(x, ((0, 0),) * (x.ndim - 1) + ((0, n - x.shape[-1]),))


def _mla_weights(w_q_a, w_q_b, w_kv_a, w_kv_b, q_norm, k_norm, kv_lora):
    perm = _rope_perm()
    nh = N_HEADS
    wkr = w_kv_a[:, kv_lora:]
    d = w_kv_a.shape[0]
    zl = jnp.zeros((d, QK_NOPE), F32)
    kr_blk = _pad_cols(jnp.concatenate([zl, wkr], axis=1), LANES)
    krp_blk = _pad_cols(jnp.concatenate([zl, wkr[:, perm - QK_NOPE]], axis=1), LANES)
    w_a = jnp.concatenate([w_q_a, w_kv_a[:, :kv_lora], kr_blk, krp_blk], axis=1).astype(BF16)
    wq = w_q_b.reshape(-1, nh, QK_DIM).transpose(1, 0, 2)
    wq_main = _pad_cols(wq, HEAD_PAD)
    wq_part = _pad_cols(jnp.concatenate([jnp.zeros_like(wq[..., :QK_NOPE]), wq[..., perm]], axis=-1), HEAD_PAD)
    wq_h = jnp.concatenate([wq_main, wq_part], axis=-1).astype(BF16)
    wkv = w_kv_b.reshape(-1, nh, QK_NOPE + V_DIM).transpose(1, 0, 2)
    wk_h = _pad_cols(wkv[..., :QK_NOPE], HEAD_PAD).astype(BF16)
    wv = wkv[..., QK_NOPE:]
    wv_h = jnp.concatenate([wv[0::2], wv[1::2]], axis=-1).astype(BF16)
    nrm = jnp.stack([_pad_cols(q_norm, LANES), _pad_cols(jnp.concatenate([jnp.zeros((QK_NOPE,), F32), q_norm[perm]]), LANES),
                     _pad_cols(k_norm, LANES), _pad_cols(jnp.concatenate([jnp.zeros((QK_NOPE,), F32), k_norm[perm]]), LANES)])
    nrm = jnp.pad(nrm, ((0, 4), (0, 0)))
    return w_a, wq_h, wk_h, wv_h, nrm


def _rope_tables(lat_len, tm):
    rows = lat_len // GRID_W
    t_row = np.repeat(np.arange(rows, dtype=np.float32), GRID_W)
    t_col = np.tile(np.arange(GRID_W, dtype=np.float32), rows)
    half = QK_ROPE // 2
    inv = (ROPE_BASE ** (-np.arange(0, half, 2, dtype=np.float32) / half)).astype(np.float32)
    ang = jnp.concatenate([jnp.asarray(t_row)[:, None] * inv, jnp.asarray(t_col)[:, None] * inv], axis=-1)
    cos, sin = jnp.cos(ang), jnp.sin(ang)
    nf = QK_ROPE // 4
    cos_l = jnp.concatenate([cos[:, :nf], cos[:, :nf], cos[:, nf:], cos[:, nf:]], axis=-1)
    sin_l = jnp.concatenate([-sin[:, :nf], sin[:, :nf], -sin[:, nf:], sin[:, nf:]], axis=-1)
    one = jnp.ones((lat_len, QK_NOPE), F32)
    cos_t = _pad_cols(jnp.concatenate([one, cos_l], axis=-1), LANES)
    cos_t = cos_t.at[:, QK_DIM:].set(1.0)
    sin_t = _pad_cols(jnp.concatenate([jnp.zeros((lat_len, QK_NOPE), F32), sin_l], axis=-1), LANES)
    cos_t = jnp.concatenate([jnp.ones((tm, LANES), F32), cos_t], axis=0)
    sin_t = jnp.concatenate([jnp.zeros((tm, LANES), F32), sin_t], axis=0)
    return cos_t, sin_t


def mla_layer(lay, y, mod, nw, route, cache_ckv, cache_krope, cos_t, sin_t, w_q_a, q_a_norm, w_q_b, w_kv_a,
              kv_a_norm, w_kv_b, q_norm, k_norm, w_o):
    q_lora = w_q_a.shape[1]
    kv_lora = kv_a_norm.shape[0]
    w_a, wq_h, wk_h, wv_h, nrm = _mla_weights(w_q_a, w_q_b, w_kv_a, w_kv_b, q_norm, k_norm, kv_lora)
    ckv, kr, q, k, v2 = mla_qkv(lay, y, mod, nw, w_a, q_a_norm.reshape(1, -1), kv_a_norm.reshape(1, -1),
                                wq_h, wk_h, wv_h, nrm, cos_t, sin_t, q_lora, kv_lora)
    past = cache_ckv.shape[1]
    ckr = jnp.pad(cache_krope.reshape(-1, QK_ROPE), ((0, 0), (QK_NOPE, LANES - QK_DIM)))
    kc, vc2 = ctx_kv(cache_ckv.reshape(-1, kv_lora), ckr, wk_h, wv_h, nrm, past)
    o_ctx = attention(q, k, v2, 0, lay.n_ctx_b, lay.ctx_len, min(lay.ctx_len, 256), pairs=N_HEADS // 2)
    o_lat = attention(q, k, v2, lay.t_ctx, lay.n_lat_b, lay.lat_len, min(lay.lat_len, 512), kc, vc2, past)
    o = jnp.concatenate([o_ctx, o_lat], axis=0)
    routed = proj_route(lay, y, mod, o, w_o.astype(BF16), route)
    new_ckv = ckv[:lay.t_ctx].reshape(lay.n_ctx_b, lay.ctx_len, kv_lora)
    new_kr = kr[:lay.t_ctx, QK_NOPE:QK_DIM].reshape(lay.n_ctx_b, lay.ctx_len, QK_ROPE)
    return routed, new_ckv, new_kr


def kernel(x_prompt, x_sample, c, cache_ckv, cache_krope, state_ssm_re, state_ssm_im, c_ctx, w_mod, b_mod, norm1_w, norm2_w, mla_w_q_a, mla_q_a_norm, mla_w_q_b, mla_w_kv_a, mla_kv_a_norm, mla_w_kv_b, mla_q_norm, mla_k_norm, mla_w_o, ssm_w_in, ssm_a_re, ssm_a_im, ssm_log_dt, ssm_b_re, ssm_b_im, ssm_c_re, ssm_c_im, ssm_d, ssm_w_glu, w_router, b_router, moe_w_gate, moe_w_up, moe_w_down):
    n_ctx_b, ctx_len, d = x_prompt.shape
    n_lat_b, lat_len, _ = x_sample.shape
    depth = w_mod.shape[0]
    lay = Layout(n_ctx_b, ctx_len, n_lat_b, lat_len, tm=512)
    assert n_lat_b + 1 <= 8
    y = (x_prompt.reshape(-1, d), x_sample.reshape(-1, d))
    cond8 = jnp.pad(jnp.concatenate([c_ctx[None, :], c], axis=0), ((0, 7 - n_lat_b), (0, 0)))
    mods = adaln_all(cond8, w_mod, b_mod)
    mods = jnp.pad(mods.reshape(depth, 8, 6, d), ((0, 0), (0, 0), (0, 2), (0, 0)))
    cos_t, sin_t = _rope_tables(lat_len, lay.tm)
    n_exp = b_router.shape[0]
    wr_hi = w_router.astype(BF16)
    wr_lo = (w_router - wr_hi.astype(F32)).astype(BF16)
    wr_t = jnp.concatenate([wr_hi.T, wr_lo.T], axis=0)
    br = b_router.reshape(n_exp, 1)
    tri = jnp.asarray(np.triu(np.ones((lay.tm, lay.tm), np.float32), k=1), BF16)
    ckv_out, kr_out, sre_out, sim_out = [], [], [], []
    for i in range(depth):
        j = i // 2
        mod = mods[i]
        nw1 = norm1_w[i].reshape(1, d)
        route = (norm2_w[i].reshape(1, d), wr_t, br, tri)
        if i % 2 == 0:
            routed, ckv_p, kr_p = mla_layer(lay, y, mod, nw1, route, cache_ckv[:, j], cache_krope[:, j], cos_t, sin_t,
                                            mla_w_q_a[j], mla_q_a_norm[j], mla_w_q_b[j], mla_w_kv_a[j],
                                            mla_kv_a_norm[j], mla_w_kv_b[j], mla_q_norm[j], mla_k_norm[j], mla_w_o[j])
            ckv_out.append(ckv_p)
            kr_out.append(kr_p)
        else:
            routed, s_re, s_im = s5_layer(lay, y, mod, nw1, route, state_ssm_re[:, j], state_ssm_im[:, j], ssm_w_in[j],
                                          ssm_a_re[j], ssm_a_im[j], ssm_log_dt[j], ssm_b_re[j], ssm_b_im[j],
                                          ssm_c_re[j], ssm_c_im[j], ssm_d[j], ssm_w_glu[j])
            sre_out.append(s_re)
            sim_out.append(s_im)
        y = moe_layer(lay, routed, mod, i, moe_w_gate, moe_w_up, moe_w_down, tmm=512, pair_out=(i == depth - 1))
    yp = y[0].reshape(n_ctx_b, ctx_len, d)
    ys = y[1].reshape(n_lat_b, lat_len, d)
    return (yp, ys, jnp.stack(ckv_out, axis=1), jnp.stack(kr_out, axis=1),
            jnp.stack(sre_out, axis=1), jnp.stack(sim_out, axis=1))
```

```python
import functools
import math

import jax
import jax.numpy as jnp
import numpy as np
from jax import lax
from jax.experimental import pallas as pl
from jax.experimental.pallas import tpu as pltpu

F32 = jnp.float32
BF16 = jnp.bfloat16
EPS = 1e-6

GRID_W = 64
N_HEADS = 8
QK_NOPE = 64
QK_ROPE = 32
QK_DIM = QK_NOPE + QK_ROPE
V_DIM = 64
ROPE_BASE = 10000.0
SSM_GROUP = 16
SSM_STATE = 64
N_EXPERT_GROUPS = 4
EXPERTS_PER_GROUP = 4

LANES = 128
HEAD_PAD = LANES
SSM_CHUNK = 16
SEG_ALIGN = 16
VMEM_LIMIT = 48 * 1024 * 1024


def _cparams(sem, vmem=VMEM_LIMIT):
    return pltpu.CompilerParams(dimension_semantics=sem, vmem_limit_bytes=vmem)


def _dot(a, b):
    return jnp.dot(a, b, preferred_element_type=F32)


def _dot_nt(a, b):
    return lax.dot_general(a, b, (((1,), (1,)), ((), ())), preferred_element_type=F32)


def _dot_hi(a, b):
    return jnp.dot(a, b, preferred_element_type=F32, precision=lax.Precision.HIGHEST)


def _norm_mod(x, nw, shift, scale):
    ms = jnp.mean(x * x, axis=-1, keepdims=True)
    return (x * lax.rsqrt(ms + EPS) * nw) * (1.0 + scale) + shift


class Layout:
    def __init__(self, n_ctx_b, ctx_len, n_lat_b, lat_len, tm):
        self.n_ctx_b, self.ctx_len, self.n_lat_b, self.lat_len = n_ctx_b, ctx_len, n_lat_b, lat_len
        self.t_ctx = n_ctx_b * ctx_len
        self.t_lat = n_lat_b * lat_len
        self.t = self.t_ctx + self.t_lat
        self.tm = tm
        assert self.t_ctx % tm == 0 and lat_len % tm == 0
        self.ctx_tiles = self.t_ctx // tm
        self.lat_tiles_per_b = lat_len // tm
        self.n_tiles = self.t // tm

    def seg(self, i):
        return jnp.where(i < self.ctx_tiles, 0, 1 + (i - self.ctx_tiles) // self.lat_tiles_per_b)

    def rope_blk(self, i):
        return jnp.where(i < self.ctx_tiles, 0, 1 + (i - self.ctx_tiles) % self.lat_tiles_per_b)

    def stream_specs(self, d, pair):
        tm = self.tm
        if not pair:
            return [pl.BlockSpec((tm, d), lambda i, *_: (i, 0))]
        return [pl.BlockSpec((tm, d), lambda i, *_: (jnp.minimum(i, self.ctx_tiles - 1), 0)),
                pl.BlockSpec((tm, d), lambda i, *_: (jnp.maximum(i - self.ctx_tiles, 0), 0))]

    def stream_load(self, refs):
        if len(refs) == 1:
            return refs[0][...]
        return jnp.where(pl.program_id(0) < self.ctx_tiles, refs[0][...], refs[1][...])

    def stream_store(self, refs, val):
        if len(refs) == 1:
            refs[0][...] = val
            return
        i = pl.program_id(0)

        @pl.when(i < self.ctx_tiles)
        def _():
            refs[0][...] = val

        @pl.when(i >= self.ctx_tiles)
        def _():
            refs[1][...] = val


def _as_list(y):
    return list(y) if isinstance(y, (tuple, list)) else [y]


def _adaln_kernel(c_ref, w_ref, b_ref, o_ref):
    c = c_ref[...]
    s = c * jax.nn.sigmoid(c)
    o_ref[...] = _dot_hi(s, w_ref[...]) + b_ref[...]


def adaln_all(cond8, w_mod, b_mod, tn=1536):
    depth, d, n6 = w_mod.shape
    return pl.pallas_call(
        _adaln_kernel,
        out_shape=jax.ShapeDtypeStruct((depth, 8, n6), F32),
        grid=(depth, n6 // tn),
        in_specs=[pl.BlockSpec((8, d), lambda l, j: (0, 0)),
                  pl.BlockSpec((None, d, tn), lambda l, j: (l, 0, j)),
                  pl.BlockSpec((None, 1, tn), lambda l, j: (l, 0, j))],
        out_specs=pl.BlockSpec((None, 8, tn), lambda l, j: (l, 0, j)),
        compiler_params=_cparams(("parallel", "parallel")),
        name="adaln",
    )(cond8, w_mod, b_mod.reshape(depth, 1, n6))


def _kv_heads(ckv_b, kr, krp, wk_ref, wv_ref, knw, knwp, cos, sin, k_ref, v_ref):
    kc = knw * cos
    ks = knwp * sin
    for h in range(N_HEADS):
        kz = _dot(ckv_b, wk_ref[h]) + kr
        r = lax.rsqrt(jnp.sum(kz * kz, axis=-1, keepdims=True) * (1.0 / QK_DIM) + EPS)
        k_ref[h] = (r * (kz * kc + krp * ks)).astype(BF16)
    for hp in range(N_HEADS // 2):
        v_ref[hp] = _dot(ckv_b, wv_ref[hp]).astype(BF16)


def _mla_qkv_kernel(*refs, lay, n_y, q_lora, kv_lora):
    y_refs = refs[:n_y]
    (mod_ref, nw_ref, w_ref, qan_ref, kvan_ref, wq_ref, wk_ref, wv_ref, nrm_ref, cos_ref, sin_ref,
     ckv_ref, kr_ref, q_ref, k_ref, v_ref) = refs[n_y:]
    m = mod_ref[...]
    h = _norm_mod(lay.stream_load(y_refs), nw_ref[...], m[0:1], m[1:2])
    z = _dot(h.astype(BF16), w_ref[...])
    cq = z[:, :q_lora]
    cq = (cq * lax.rsqrt(jnp.mean(cq * cq, axis=-1, keepdims=True) + EPS) * qan_ref[...]).astype(BF16)
    ckv = z[:, q_lora:q_lora + kv_lora]
    ckv = ckv * lax.rsqrt(jnp.mean(ckv * ckv, axis=-1, keepdims=True) + EPS) * kvan_ref[...]
    ckv_ref[...] = ckv
    kr = z[:, q_lora + kv_lora:q_lora + kv_lora + LANES]
    kr_ref[...] = kr
    krp = z[:, q_lora + kv_lora + LANES:]
    nrm = nrm_ref[...]
    cos = cos_ref[...]
    sin = sin_ref[...]
    qc = nrm[0:1] * cos * (QK_DIM ** -0.5)
    qs = nrm[1:2] * sin * (QK_DIM ** -0.5)
    for hd in range(N_HEADS):
        zq = _dot(cq, wq_ref[hd])
        qm = zq[:, :HEAD_PAD]
        qp = zq[:, HEAD_PAD:]
        r = lax.rsqrt(jnp.sum(qm * qm, axis=-1, keepdims=True) * (1.0 / QK_DIM) + EPS)
        q_ref[hd] = (r * (qm * qc + qp * qs)).astype(BF16)
    _kv_heads(ckv.astype(BF16), kr, krp, wk_ref, wv_ref, nrm[2:3], nrm[3:4], cos, sin, k_ref, v_ref)


def mla_qkv(lay, y, mod, nw, w_a, qan, kvan, wq, wk, wv, nrm, cos_t, sin_t, q_lora, kv_lora):
    ys = _as_list(y)
    t, d = lay.t, ys[0].shape[1]
    tm = lay.tm
    row = lambda i: (i, 0)
    full = lambda i: (0, 0)
    hrow = lambda i: (0, i, 0)
    full3 = lambda i: (0, 0, 0)
    rope = lambda i: (lay.rope_blk(i), 0)
    nh = N_HEADS
    return pl.pallas_call(
        functools.partial(_mla_qkv_kernel, lay=lay, n_y=len(ys), q_lora=q_lora, kv_lora=kv_lora),
        out_shape=(jax.ShapeDtypeStruct((t, kv_lora), F32), jax.ShapeDtypeStruct((t, LANES), F32),
                   jax.ShapeDtypeStruct((nh, t, HEAD_PAD), BF16), jax.ShapeDtypeStruct((nh, t, HEAD_PAD), BF16),
                   jax.ShapeDtypeStruct((nh // 2, t, LANES), BF16)),
        grid=(lay.n_tiles,),
        in_specs=lay.stream_specs(d, len(ys) == 2) + [
            pl.BlockSpec((None, 8, d), lambda i: (lay.seg(i), 0, 0)),
            pl.BlockSpec((1, d), full), pl.BlockSpec(w_a.shape, full),
            pl.BlockSpec((1, q_lora), full), pl.BlockSpec((1, kv_lora), full),
            pl.BlockSpec(wq.shape, full3), pl.BlockSpec(wk.shape, full3), pl.BlockSpec(wv.shape, full3),
            pl.BlockSpec((8, LANES), full),
            pl.BlockSpec((tm, LANES), rope), pl.BlockSpec((tm, LANES), rope)],
        out_specs=(pl.BlockSpec((tm, kv_lora), row), pl.BlockSpec((tm, LANES), row),
                   pl.BlockSpec((nh, tm, HEAD_PAD), hrow), pl.BlockSpec((nh, tm, HEAD_PAD), hrow),
                   pl.BlockSpec((nh // 2, tm, LANES), hrow)),
        compiler_params=_cparams(("parallel",)),
        name="mla_qkv",
    )(*ys, mod, nw, w_a, qan, kvan, wq, wk, wv, nrm, cos_t, sin_t)


def _ctx_kv_kernel(ckv_ref, kr_ref, wk_ref, wv_ref, nrm_ref, k_ref, v_ref):
    nrm = nrm_ref[...]
    kr = kr_ref[...]
    one = jnp.ones((1, LANES), F32)
    _kv_heads(ckv_ref[...].astype(BF16), kr, kr, wk_ref, wv_ref, nrm[2:3], nrm[3:4],
              one, jnp.zeros((1, LANES), F32), k_ref, v_ref)


def ctx_kv(ckv, kr, wk, wv, nrm, tm):
    t = ckv.shape[0]
    row = lambda i: (i, 0)
    hrow = lambda i: (0, i, 0)
    full3 = lambda i: (0, 0, 0)
    nh = N_HEADS
    return pl.pallas_call(
        _ctx_kv_kernel,
        out_shape=(jax.ShapeDtypeStruct((nh, t, HEAD_PAD), BF16), jax.ShapeDtypeStruct((nh // 2, t, LANES), BF16)),
        grid=(t // tm,),
        in_specs=[pl.BlockSpec((tm, ckv.shape[1]), row), pl.BlockSpec((tm, LANES), row),
                  pl.BlockSpec(wk.shape, full3), pl.BlockSpec(wv.shape, full3),
                  pl.BlockSpec((8, LANES), lambda i: (0, 0))],
        out_specs=(pl.BlockSpec((nh, tm, HEAD_PAD), hrow), pl.BlockSpec((nh // 2, tm, LANES), hrow)),
        compiler_params=_cparams(("parallel",)),
        name="ctx_kv",
    )(ckv, kr, wk, wv, nrm)


def _attn_kernel(*refs, with_ctx, kblk):
    if with_ctx:
        q_ref, k_ref, v_ref, kc_ref, vc_ref, o_ref = refs
    else:
        q_ref, k_ref, v_ref, o_ref = refs
    seq = k_ref.shape[1]
    n_heads = q_ref.shape[0]
    blocks = [(kc_ref, vc_ref, 0, kc_ref.shape[1])] if with_ctx else []
    blocks += [(k_ref, v_ref, b0, kblk) for b0 in range(0, seq, kblk)]
    items = [(j, blk) for j in range(n_heads) for blk in blocks]

    def scores(item):
        j, (kr, _, b0, n) = item
        return _dot_nt(q_ref[j], kr[j, b0:b0 + n, :])

    outs = []
    m = l = acc = None
    s_next = scores(items[0])
    for idx, (j, (_, vr, b0, n)) in enumerate(items):
        s = s_next
        if idx + 1 < len(items):
            s_next = scores(items[idx + 1])
        mb = jnp.max(s, axis=-1, keepdims=True)
        m_new = mb if m is None else jnp.maximum(m, mb)
        p = jnp.exp(s - m_new)
        pv = _dot(p.astype(BF16), vr[j // 2, b0:b0 + n, :])
        ps = jnp.sum(p, axis=-1, keepdims=True)
        if m is None:
            l, acc = ps, pv
        else:
            a = jnp.exp(m - m_new)
            l, acc = a * l + ps, a * acc + pv
        m = m_new
        if idx + 1 == len(items) or items[idx + 1][0] != j:
            outs.append(acc / l)
            m = l = acc = None
    lane = lax.broadcasted_iota(jnp.int32, outs[0].shape, 1)
    for pr in range(n_heads // 2):
        o_ref[:, pr * LANES:(pr + 1) * LANES] = jnp.where(lane < V_DIM, outs[2 * pr], outs[2 * pr + 1]).astype(BF16)


def attention(q, k, v2, row0, n_b, seq, tq, kc=None, vc2=None, ctx_len=0, kblk=1024, pairs=1):
    assert row0 % seq == 0 and seq % tq == 0
    nq = seq // tq
    qb0 = row0 // tq
    kb0 = row0 // seq
    with_ctx = kc is not None
    hp_n = 2 * pairs
    in_specs = [pl.BlockSpec((hp_n, tq, HEAD_PAD), lambda b, hp, qi: (hp, qb0 + b * nq + qi, 0)),
                pl.BlockSpec((hp_n, seq, HEAD_PAD), lambda b, hp, qi: (hp, kb0 + b, 0)),
                pl.BlockSpec((pairs, seq, LANES), lambda b, hp, qi: (hp, kb0 + b, 0))]
    args = [q, k, v2]
    if with_ctx:
        in_specs += [pl.BlockSpec((hp_n, ctx_len, HEAD_PAD), lambda b, hp, qi: (hp, b, 0)),
                     pl.BlockSpec((pairs, ctx_len, LANES), lambda b, hp, qi: (hp, b, 0))]
        args += [kc, vc2]
    return pl.pallas_call(
        functools.partial(_attn_kernel, with_ctx=with_ctx, kblk=min(seq, kblk)),
        out_shape=jax.ShapeDtypeStruct((n_b * seq, (N_HEADS // 2) * LANES), BF16),
        grid=(n_b, N_HEADS // (2 * pairs), nq),
        in_specs=in_specs,
        out_specs=pl.BlockSpec((tq, pairs * LANES), lambda b, hp, qi: (b * nq + qi, hp)),
        compiler_params=_cparams(("parallel", "parallel", "parallel")),
        name="attn_ctx" if with_ctx else "attn",
    )(*args)


SSM_GPT = LANES // SSM_GROUP
SUBLANES = 8


def _lane_block_transpose(vs):
    n = len(vs)
    blk = lax.broadcasted_iota(jnp.int32, vs[0].shape, 1) // SSM_GROUP
    d = n // 2
    while d >= 1:
        keep = (blk & d) == 0
        new = list(vs)
        for a in range(n):
            if a & d == 0:
                b = a + d
                new[a] = jnp.where(keep, vs[a], pltpu.roll(vs[b], d * SSM_GROUP, 1))
                new[b] = jnp.where(keep, pltpu.roll(vs[a], LANES - d * SSM_GROUP, 1), vs[b])
        vs = new
        d //= 2
    return vs


def _ssm_in_kernel(y_ref, mod_ref, nw_ref, w_ref, u_ref, us_ref, ug_ref):
    m = mod_ref[...]
    h = _norm_mod(y_ref[...], nw_ref[...], m[0:1], m[1:2])
    z = _dot(h.astype(BF16), w_ref[...])
    n_lt = us_ref.shape[0]
    for j in range(n_lt):
        us_ref[j] = z[:, j * LANES:(j + 1) * LANES]
    n_ch = us_ref.shape[1] // SSM_CHUNK
    for j in range(n_lt):
        for rb in range(n_ch // SUBLANES):
            for hf in range(SSM_CHUNK // SSM_GPT):
                src = [us_ref[j, pl.ds(rb * SUBLANES * SSM_CHUNK + hf * SSM_GPT + tt, SUBLANES, stride=SSM_CHUNK), :]
                       for tt in range(SSM_GPT)]
                for gl, v in enumerate(_lane_block_transpose(src)):
                    ug_ref[j * SSM_GPT + gl, rb * SUBLANES:(rb + 1) * SUBLANES, hf * LANES:(hf + 1) * LANES] = v
    u_ref[...] = ug_ref[...].astype(BF16)


def ssm_in(lay, y, mod, nw, w_in):
    t, d = y.shape
    tm = lay.tm
    n = w_in.shape[1]
    g = n // SSM_GROUP
    n_ch = tm // SSM_CHUNK
    qk = SSM_CHUNK * SSM_GROUP
    row = lambda i: (i, 0)
    return pl.pallas_call(
        _ssm_in_kernel,
        out_shape=jax.ShapeDtypeStruct((g, t // SSM_CHUNK, qk), BF16),
        grid=(lay.n_tiles,),
        in_specs=[pl.BlockSpec((tm, d), row), pl.BlockSpec((None, 8, d), lambda i: (lay.seg(i), 0, 0)),
                  pl.BlockSpec((1, d), lambda i: (0, 0)), pl.BlockSpec((d, n), lambda i: (0, 0))],
        out_specs=pl.BlockSpec((g, n_ch, qk), lambda i: (0, i, 0)),
        scratch_shapes=[pltpu.VMEM((n // LANES, tm, LANES), F32), pltpu.VMEM((g, n_ch, qk), F32)],
        compiler_params=_cparams(("parallel",)),
        name="ssm_in",
    )(y, mod, nw, w_in)


def _ssm_prep_kernel(ar_ref, ai_ref, ld_ref, arc_ref, aic_ref, ldc_ref, btr_ref, bti_ref, ctr_ref, cti_ref, dsk_ref,
                     mt_ref, ett_ref, ft_ref, coef_ref):
    q = SSM_CHUNK
    k = SSM_GROUP
    p = SSM_STATE
    qk = q * k
    fwd = pl.program_id(0) == 0
    dt = jnp.exp(ld_ref[...])
    are = ar_ref[...]
    aim = ai_ref[...]
    mag = jnp.exp(dt * are)
    abr = mag * jnp.cos(dt * aim)
    abi = mag * jnp.sin(dt * aim)
    den = are * are + aim * aim
    nr = abr - 1.0
    cf_re = (nr * are + abi * aim) / den
    cf_im = (abi * are - nr * aim) / den
    btr = btr_ref[...]
    bti = bti_ref[...]
    bbt_re = cf_re * btr - cf_im * bti
    bbt_im = cf_re * bti + cf_im * btr
    bbt_re_t = jnp.concatenate([bbt_re] * q, axis=0)
    bbt_im_t = jnp.concatenate([bbt_im] * q, axis=0)
    s_idx = lax.broadcasted_iota(jnp.int32, (q, 1), 0).astype(F32)
    pw = jnp.where(fwd, (q - 1.0) - s_idx, s_idx)
    pm = jnp.exp(pw * dt * are)
    rep_rows = lambda x: jnp.concatenate([jnp.broadcast_to(x[s:s + 1], (k, p)) for s in range(q)], axis=0)
    pr = rep_rows(pm * jnp.cos(pw * dt * aim))
    pi = rep_rows(pm * jnp.sin(pw * dt * aim))
    et_re = pr * bbt_re_t - pi * bbt_im_t
    et_im = pr * bbt_im_t + pi * bbt_re_t
    ett_ref[...] = jnp.concatenate([et_re, et_im, et_im, et_re], axis=1).astype(BF16)
    qm = jnp.exp(q * dt * are)
    aq_re = qm * jnp.cos(q * dt * aim)
    aq_im = qm * jnp.sin(q * dt * aim)
    c1 = jnp.concatenate([aq_re, aq_re], axis=1)
    c2 = jnp.concatenate([-aq_im, aq_im], axis=1)
    coef_ref[...] = jnp.concatenate([c1, c2, jnp.zeros((6, 2 * p), F32)], axis=0)
    dtc = jnp.exp(ldc_ref[...])
    arec = arc_ref[...]
    aimc = aic_ref[...]
    ctr = ctr_ref[...]
    cti = cti_ref[...]
    tau = lax.broadcasted_iota(jnp.int32, (1, q), 1).astype(F32)
    tau0 = jnp.where(fwd, tau, (q - 1.0) - tau)
    m0 = jnp.exp(tau0 * dtc * arec)
    p0r_q = m0 * jnp.cos(tau0 * dtc * aimc)
    p0i_q = m0 * jnp.sin(tau0 * dtc * aimc)
    magc = jnp.exp(dtc * arec)
    abr_c = magc * jnp.cos(dtc * aimc)
    abi_c = magc * jnp.sin(dtc * aimc)
    spread = jnp.where(lax.broadcasted_iota(jnp.int32, (q, qk), 1) // k == lax.broadcasted_iota(jnp.int32, (q, qk), 0),
                       1.0, 0.0)
    p0r = _dot_hi(p0r_q, spread)
    p0i = _dot_hi(p0i_q, spread)
    p1r = _dot_hi(p0r_q * abr_c - p0i_q * abi_c, spread)
    p1i = _dot_hi(p0r_q * abi_c + p0i_q * abr_c, spread)
    ft_ref[...] = jnp.concatenate([ctr * p1r - cti * p1i, -(ctr * p1i + cti * p1r)], axis=0).astype(BF16)
    left_re = ctr * p0r - cti * p0i
    left_im = ctr * p0i + cti * p0r
    kt = _dot_hi(bbt_re, left_re) - _dot_hi(bbt_im, left_im)
    lane = lax.broadcasted_iota(jnp.int32, (k, qk), 1)
    rowi = lax.broadcasted_iota(jnp.int32, (k, qk), 0)
    kt_f = kt + jnp.where(lane == rowi, dsk_ref[...], 0.0)
    rows_f, rows_b = [], []
    for s in range(q):
        rows_f.append(kt_f if s == 0 else jnp.where(lane >= s * k, pltpu.roll(kt_f, s * k, 1), 0.0))
        sh = ((s + 1) * k) % qk
        rows_b.append(kt if sh == 0 else jnp.where(lane < (s + 1) * k, pltpu.roll(kt, sh, 1), 0.0))
    mt = jnp.where(fwd, jnp.concatenate(rows_f, axis=0), jnp.concatenate(rows_b, axis=0))
    mt_ref[...] = mt.astype(BF16)


def ssm_prep(a_re, a_im, log_dt, b_re, b_im, c_re, c_im, d_skip):
    nd, g, p = a_re.shape
    k = b_re.shape[-1]
    qk = SSM_CHUNK * k
    ld = jnp.broadcast_to(log_dt[..., None], (nd, g, p))
    dsk = jnp.broadcast_to(jnp.tile(d_skip.reshape(g, 1, k), (1, 1, SSM_CHUNK))[None], (nd, g, 1, qk))
    rowv = lambda x: x.reshape(nd, g, 1, p)
    colv = lambda x: x.reshape(nd, g, p, 1)
    bt = lambda x: jnp.swapaxes(x, -1, -2)
    ct = lambda x: jnp.tile(jnp.swapaxes(x, -1, -2), (1, 1, 1, SSM_CHUNK))
    spec = lambda r, c: pl.BlockSpec((None, None, r, c), lambda d, j: (d, j, 0, 0))
    return pl.pallas_call(
        _ssm_prep_kernel,
        out_shape=(jax.ShapeDtypeStruct((nd, g, qk, qk), BF16), jax.ShapeDtypeStruct((nd, g, qk, 4 * p), BF16),
                   jax.ShapeDtypeStruct((nd, g, 2 * p, qk), BF16), jax.ShapeDtypeStruct((nd, g, 8, 2 * p), F32)),
        grid=(nd, g),
        in_specs=[spec(1, p), spec(1, p), spec(1, p), spec(p, 1), spec(p, 1), spec(p, 1),
                  spec(k, p), spec(k, p), spec(p, qk), spec(p, qk), spec(1, qk)],
        out_specs=(spec(qk, qk), spec(qk, 4 * p), spec(2 * p, qk), spec(8, 2 * p)),
        compiler_params=_cparams(("parallel", "parallel")),
        name="ssm_prep",
    )(rowv(a_re), rowv(a_im), rowv(ld), colv(a_re), colv(a_im), colv(ld), bt(b_re), bt(b_im), ct(c_re), ct(c_im), dsk)


def _ssm_scan_kernel(u_ref, mt_ref, ett_ref, ft_ref, coef_ref, x0_ref, x0s_ref, y_ref, hfin_ref,
                     sx_ref, sxs_ref, hp_ref, *, gpb, ctx_b, ctx_chunks, lat_b, lat_chunks, rchunk):
    w = 2 * SSM_STATE
    n_rows = u_ref.shape[1]
    ctx_rows = ctx_b * ctx_chunks
    for d in range(2):
        for g in range(gpb):
            for r0 in range(0, n_rows, rchunk):
                ss = _dot(u_ref[g, r0:r0 + rchunk, :], ett_ref[d, g])
                sx_ref[g, r0:r0 + rchunk, :] = ss[:, :w]
                sxs_ref[g, r0:r0 + rchunk, :] = ss[:, w:]
        c1 = [coef_ref[d, g][0:1] for g in range(gpb)]
        c2 = [coef_ref[d, g][1:2] for g in range(gpb)]

        def make_body(base, nb, nc):
            def body(i, carry):
                c = i if d == 0 else nc - 1 - i
                rows = pl.ds(base + c, nb, stride=nc)
                out = []
                for g in range(gpb):
                    x, xs = carry[g]
                    hp_ref[g, rows, :] = x
                    xn = c1[g] * x + c2[g] * xs + sx_ref[g, rows, :]
                    xsn = c1[g] * xs - c2[g] * x + sxs_ref[g, rows, :]
                    out.append((xn, xsn))
                return tuple(out)
            return body

        z = jnp.zeros((ctx_b, w), F32)
        fin = lax.fori_loop(0, ctx_chunks, make_body(0, ctx_b, ctx_chunks), tuple((z, z) for _ in range(gpb)))
        hfin_ref[d] = jnp.concatenate([fin[g][0] for g in range(gpb)], axis=1)
        init = tuple((x0_ref[d][:, g * w:(g + 1) * w], x0s_ref[d][:, g * w:(g + 1) * w]) for g in range(gpb))
        lax.fori_loop(0, lat_chunks, make_body(ctx_rows, lat_b, lat_chunks), init, unroll=4)
        for g in range(gpb):
            for r0 in range(0, n_rows, rchunk):
                y = _dot(u_ref[g, r0:r0 + rchunk, :], mt_ref[d, g])
                y = y + _dot(hp_ref[g, r0:r0 + rchunk, :].astype(BF16), ft_ref[d, g])
                if d == 0:
                    y_ref[g, r0:r0 + rchunk, :] = y
                else:
                    y_ref[g, r0:r0 + rchunk, :] += y


def ssm_scan(u, mt, ett, ft, coef, x0, x0s, ctx_b, ctx_chunks, lat_b, lat_chunks, gpb=4):
    g, n_rows, qk = u.shape
    w = 2 * SSM_STATE
    rchunk = math.gcd(n_rows, 512)
    assert n_rows == ctx_b * ctx_chunks + lat_b * lat_chunks
    ublk = pl.BlockSpec((gpb, n_rows, qk), lambda j: (j, 0, 0))
    blk = lambda r, c: pl.BlockSpec((2, gpb, r, c), lambda j: (0, j, 0, 0))
    vec = lambda r: pl.BlockSpec((2, r, gpb * w), lambda j: (0, 0, j))
    return pl.pallas_call(
        functools.partial(_ssm_scan_kernel, gpb=gpb, ctx_b=ctx_b, ctx_chunks=ctx_chunks, lat_b=lat_b,
                          lat_chunks=lat_chunks, rchunk=rchunk),
        out_shape=(jax.ShapeDtypeStruct((g, n_rows, qk), F32), jax.ShapeDtypeStruct((2, ctx_b, g * w), F32)),
        grid=(g // gpb,),
        in_specs=[ublk, blk(qk, qk), blk(qk, 2 * w), blk(w, qk), blk(8, w), vec(lat_b), vec(lat_b)],
        out_specs=(ublk, vec(ctx_b)),
        scratch_shapes=[pltpu.VMEM((gpb, n_rows, w), F32), pltpu.VMEM((gpb, n_rows, w), F32),
                        pltpu.VMEM((gpb, n_rows, w), F32)],
        compiler_params=_cparams(("parallel",)),
        name="ssm_scan",
    )(u, mt, ett, ft, coef, x0, x0s)


def _ssm_out_kernel(y_ref, mod_ref, yc_ref, w_ref, nw_ref, wr_ref, br_ref, tri_ref, out_ref, h_ref, ri_ref, pc_ref,
                    s_ref, *, d_model):
    n_ch = yc_ref.shape[1]
    n_lt = s_ref.shape[0]
    for j in range(n_lt):
        for rb in range(n_ch // SUBLANES):
            for hf in range(SSM_CHUNK // SSM_GPT):
                src = [yc_ref[j * SSM_GPT + gl, rb * SUBLANES:(rb + 1) * SUBLANES, hf * LANES:(hf + 1) * LANES]
                       for gl in range(SSM_GPT)]
                for tt, v in enumerate(_lane_block_transpose(src)):
                    s_ref[j, pl.ds(rb * SUBLANES * SSM_CHUNK + hf * SSM_GPT + tt, SUBLANES, stride=SSM_CHUNK), :] = v
    s = jnp.concatenate([s_ref[j] for j in range(n_lt)], axis=1)
    a = jax.nn.gelu(s, approximate=True).astype(BF16)
    z = _dot(a, w_ref[...])
    m = mod_ref[...]
    y = y_ref[...] + m[2:3] * (z[:, :d_model] * jax.nn.sigmoid(z[:, d_model:]))
    out_ref[...] = y
    _route_core(y, m, nw_ref, wr_ref, br_ref, tri_ref, h_ref, ri_ref, pc_ref)


def ssm_out_route(lay, y, mod, ych, w_glu, route):
    t, d = y.shape
    tm = lay.tm
    g, _, qk = ych.shape
    n_ch = tm // SSM_CHUNK
    row = lambda i: (i, 0)
    r_args, r_in, r_shape, r_out = _route_io(lay, d, *route)
    return pl.pallas_call(
        functools.partial(_ssm_out_kernel, d_model=d),
        out_shape=[jax.ShapeDtypeStruct((t, d), F32)] + r_shape,
        grid=(lay.n_tiles,),
        in_specs=[pl.BlockSpec((tm, d), row), pl.BlockSpec((None, 8, d), lambda i: (lay.seg(i), 0, 0)),
                  pl.BlockSpec((g, n_ch, qk), lambda i: (0, i, 0)),
                  pl.BlockSpec(w_glu.shape, lambda i: (0, 0))] + r_in,
        out_specs=[pl.BlockSpec((tm, d), row)] + r_out,
        scratch_shapes=[pltpu.VMEM((g * SSM_GROUP // LANES, tm, LANES), F32)],
        input_output_aliases={0: 0},
        compiler_params=_cparams(("parallel",)),
        name="ssm_out_route",
    )(y, mod, ych, w_glu, *r_args)


def s5_layer(lay, y, mod, nw, route, state_re, state_im, w_in, a_re, a_im, log_dt, b_re, b_im, c_re, c_im, d_skip,
             w_glu):
    uch = ssm_in(lay, y, mod, nw, w_in.astype(BF16))
    g = a_re.shape[1]
    p = SSM_STATE
    mt, ett, ft, coef = ssm_prep(a_re, a_im, log_dt, b_re, b_im, c_re, c_im, d_skip)
    st = jnp.concatenate([state_re, state_im], axis=-1)
    sts = jnp.concatenate([state_im, state_re], axis=-1)
    x0 = jnp.transpose(st, (1, 0, 2, 3)).reshape(2, lay.n_lat_b, g * 2 * p)
    x0s = jnp.transpose(sts, (1, 0, 2, 3)).reshape(2, lay.n_lat_b, g * 2 * p)
    ych, hfin = ssm_scan(uch, mt, ett, ft, coef, x0, x0s, lay.n_ctx_b, lay.ctx_len // SSM_CHUNK,
                         lay.n_lat_b, lay.lat_len // SSM_CHUNK)
    routed = ssm_out_route(lay, y, mod, ych, w_glu.astype(BF16), route)
    hf = hfin.reshape(2, lay.n_ctx_b, g, 2, p)
    new_re = jnp.transpose(hf[:, :, :, 0], (1, 0, 2, 3))
    new_im = jnp.transpose(hf[:, :, :, 1], (1, 0, 2, 3))
    return routed, new_re, new_im


def _route_core(y, m, nw_ref, wr_ref, br_ref, tri_ref, h_ref, ri_ref, pc_ref):
    n_exp = br_ref.shape[0]
    h = _norm_mod(y, nw_ref[...], m[3:4], m[4:5])
    h_hi = h.astype(BF16)
    h_ref[...] = h_hi
    h_lo = (h - h_hi.astype(F32)).astype(BF16)
    wr = wr_ref[...]
    lt = _dot_nt(wr, h_hi)
    logits = lt[:n_exp] + lt[n_exp:] + _dot_nt(wr[:n_exp], h_lo)
    scores = jax.nn.sigmoid(logits)
    sel = scores + br_ref[...]
    epg = EXPERTS_PER_GROUP
    row = lambda x, e: x[e:e + 1, :]
    gscore = []
    for g in range(N_EXPERT_GROUPS):
        a, b, c, d = (row(sel, g * epg + j) for j in range(epg))
        m1, n1, m2, n2 = jnp.maximum(a, b), jnp.minimum(a, b), jnp.maximum(c, d), jnp.minimum(c, d)
        gscore.append(jnp.maximum(m1, m2) + jnp.maximum(jnp.minimum(m1, m2), jnp.maximum(n1, n2)))
    best = gscore[0]
    gi = jnp.zeros_like(best, dtype=jnp.int32)
    for g in range(1, N_EXPERT_GROUPS):
        better = gscore[g] > best
        gi = jnp.where(better, g, gi)
        best = jnp.where(better, gscore[g], best)

    def pick(x, j):
        out = row(x, j)
        for g in range(1, N_EXPERT_GROUPS):
            out = jnp.where(gi == g, row(x, g * epg + j), out)
        return out

    sv = [pick(sel, j) for j in range(epg)]
    cv = [pick(scores, j) for j in range(epg)]
    b1, i1, w1 = sv[0], jnp.zeros_like(gi), cv[0]
    for j in range(1, epg):
        better = sv[j] > b1
        i1 = jnp.where(better, j, i1)
        w1 = jnp.where(better, cv[j], w1)
        b1 = jnp.where(better, sv[j], b1)
    neg = jnp.full_like(b1, -jnp.inf)
    b2, i2, w2 = neg, jnp.zeros_like(gi), jnp.zeros_like(w1)
    for j in range(epg):
        better = (i1 != j) & (sv[j] > b2)
        i2 = jnp.where(better, j, i2)
        w2 = jnp.where(better, cv[j], w2)
        b2 = jnp.where(better, sv[j], b2)
    tot = w1 + w2
    e1 = gi * epg + i1
    e2 = gi * epg + i2
    eid = lax.broadcasted_iota(jnp.int32, logits.shape, 0)
    m1h = eid == e1
    m2h = eid == e2
    mc = jnp.where(m1h | m2h, 1.0, 0.0)
    pref = _dot(mc.astype(BF16), tri_ref[...])
    cnt = jnp.sum(mc, axis=1, keepdims=True)
    pc_al = jnp.ceil(cnt * (1.0 / SEG_ALIGN))
    pcb = jnp.broadcast_to(pc_al, (n_exp, LANES))
    er = lax.broadcasted_iota(jnp.int32, (n_exp, n_exp), 0)
    ec = lax.broadcasted_iota(jnp.int32, (n_exp, n_exp), 1)
    lower = jnp.where(ec < er, 1.0, 0.0).astype(BF16)
    seg = _dot(lower, pcb.astype(BF16))[:, 0:1] * SEG_ALIGN
    slot = seg + pref
    pos1 = jnp.sum(jnp.where(m1h, slot, 0.0), axis=0, keepdims=True)
    pos2 = jnp.sum(jnp.where(m2h, slot, 0.0), axis=0, keepdims=True)
    zero = jnp.zeros_like(pos1)
    ri_ref[...] = jnp.concatenate([pos1, pos2, w1 / tot, w2 / tot, zero, zero, zero, zero], axis=0)
    pc_ref[...] = pcb * SEG_ALIGN


def _route_io(lay, d, nw, wr_t, br, tri):
    tm = lay.tm
    n_exp = br.shape[0]
    full = lambda i: (0, 0)
    args = [nw, wr_t, br, tri]
    in_specs = [pl.BlockSpec((1, d), full), pl.BlockSpec(wr_t.shape, full), pl.BlockSpec((n_exp, 1), full),
                pl.BlockSpec((tm, tm), full)]
    out_shape = [jax.ShapeDtypeStruct((lay.t, d), BF16), jax.ShapeDtypeStruct((8, lay.t), F32),
                 jax.ShapeDtypeStruct((lay.n_tiles, n_exp, LANES), F32)]
    out_specs = [pl.BlockSpec((tm, d), lambda i: (i, 0)), pl.BlockSpec((8, tm), lambda i: (0, i)),
                 pl.BlockSpec((None, n_exp, LANES), lambda i: (i, 0, 0))]
    return args, in_specs, out_shape, out_specs


def _proj_route_kernel(*refs, lay, n_y):
    y_refs = refs[:n_y]
    mod_ref, o_ref, w_ref, nw_ref, wr_ref, br_ref, tri_ref, out_ref, h_ref, ri_ref, pc_ref = refs[n_y:]
    m = mod_ref[...]
    y = lay.stream_load(y_refs) + m[2:3] * _dot(o_ref[...], w_ref[...])
    out_ref[...] = y
    _route_core(y, m, nw_ref, wr_ref, br_ref, tri_ref, h_ref, ri_ref, pc_ref)


def proj_route(lay, y, mod, o, w, route):
    ys = _as_list(y)
    d = ys[0].shape[1]
    tm = lay.tm
    row = lambda i: (i, 0)
    r_args, r_in, r_shape, r_out = _route_io(lay, d, *route)
    return pl.pallas_call(
        functools.partial(_proj_route_kernel, lay=lay, n_y=len(ys)),
        out_shape=[jax.ShapeDtypeStruct((lay.t, d), F32)] + r_shape,
        grid=(lay.n_tiles,),
        in_specs=lay.stream_specs(d, len(ys) == 2) + [
            pl.BlockSpec((None, 8, d), lambda i: (lay.seg(i), 0, 0)),
            pl.BlockSpec((tm, o.shape[1]), row), pl.BlockSpec(w.shape, lambda i: (0, 0))] + r_in,
        out_specs=[pl.BlockSpec((tm, d), row)] + r_out,
        input_output_aliases={0: 0} if len(ys) == 1 else {},
        compiler_params=_cparams(("parallel",)),
        name="proj_route",
    )(*ys, mod, o, w, *r_args)


RARE_PIECE = 128


def _segment_pieces(i, off_ref, pc_ref, n_exp, sizes, fn):
    local = 0
    for e in range(n_exp):
        n = pc_ref[i * n_exp + e]
        g0 = off_ref[i * n_exp + e]

        def emit(size, e=e, n=n, g0=g0, local=local):
            above = n & (-2 * size)

            @pl.when((n & size) != 0)
            def _():
                fn(e, pl.multiple_of(local + above, SEG_ALIGN), pl.multiple_of(g0 + above, SEG_ALIGN), size)

        rare = [s for s in sizes if s >= RARE_PIECE]
        if rare:
            @pl.when(n >= RARE_PIECE)
            def _():
                for size in rare:
                    emit(size)
        for size in sizes:
            if size < RARE_PIECE:
                emit(size)
        local = local + n


def _pieces(n, sizes):
    for size in sizes:
        yield (n & size) != 0, n & (-2 * size), size


def _moe_permute_kernel(off_ref, pc_ref, goff_ref, gn_ref, h_ref, ri_ref, xs_ref, ys_ref, zp_ref, sem, gsem,
                        *, n_exp, sizes, gap_sizes, rchunk):
    i = pl.program_id(0)
    last = pl.num_programs(0) - 1
    slot = i % 2
    ri = ri_ref[...]
    pos1 = ri[0:1]
    pos2 = ri[1:2]
    h = h_ref[...]
    n_loc = ys_ref.shape[1]
    for r0 in range(0, n_loc, rchunk):
        r = (lax.broadcasted_iota(jnp.int32, (rchunk, h.shape[0]), 0) + r0).astype(F32)
        p = jnp.where(pos1 == r, 1.0, 0.0) + jnp.where(pos2 == r, 1.0, 0.0)
        ys_ref[slot, r0:r0 + rchunk, :] = _dot(p.astype(BF16), h).astype(BF16)

    def copies(tile, sl, act):
        def piece(e, lrow, grow, size):
            act(pltpu.make_async_copy(ys_ref.at[sl, pl.ds(lrow, size)], xs_ref.at[pl.ds(grow, size)], sem.at[sl, e]))
        _segment_pieces(tile, off_ref, pc_ref, n_exp, sizes, piece)

    start = lambda cp: cp.start()
    wait = lambda cp: cp.wait()
    copies(i, slot, start)

    @pl.when(i > 0)
    def _():
        copies(i - 1, 1 - slot, wait)

    @pl.when(i == last)
    def _():
        copies(i, slot, wait)
        zp_ref[...] = jnp.zeros_like(zp_ref)

        def gaps():
            for e in range(n_exp):
                g0 = goff_ref[e]
                for present, above, size in _pieces(gn_ref[e], gap_sizes):
                    yield present, pltpu.make_async_copy(
                        zp_ref.at[pl.ds(0, size)], xs_ref.at[pl.ds(pl.multiple_of(g0 + above, SEG_ALIGN), size)],
                        gsem.at[e])

        for present, cp in gaps():
            pl.when(present)(cp.start)
        for present, cp in gaps():
            pl.when(present)(cp.wait)

        zrows = zp_ref.shape[0]
        tail0 = goff_ref[n_exp]
        tail = lambda c: pltpu.make_async_copy(
            zp_ref, xs_ref.at[pl.ds(pl.multiple_of(tail0 + c * zrows, zrows), zrows)], gsem.at[0])

        @pl.loop(0, gn_ref[n_exp])
        def _(c):
            tail(c).start()

        @pl.loop(0, gn_ref[n_exp])
        def _(c):
            tail(c).wait()


def moe_permute(lay, h, rinfo, off, pcs, gap_off, gap_n, n_sorted, n_loc, n_exp, tmm):
    t, d = h.shape
    tm = lay.tm
    sizes = [s for s in (1 << b for b in range(12, 3, -1)) if s <= tm]
    gap_sizes = [s for s in (1 << b for b in range(12, 3, -1)) if s < tmm]
    return pl.pallas_call(
        functools.partial(_moe_permute_kernel, n_exp=n_exp, sizes=sizes, gap_sizes=gap_sizes, rchunk=256),
        out_shape=jax.ShapeDtypeStruct((n_sorted, d), BF16),
        grid_spec=pltpu.PrefetchScalarGridSpec(
            num_scalar_prefetch=4, grid=(lay.n_tiles,),
            in_specs=[pl.BlockSpec((tm, d), lambda i, *_: (i, 0)), pl.BlockSpec((8, tm), lambda i, *_: (0, i))],
            out_specs=pl.BlockSpec(memory_space=pl.ANY),
            scratch_shapes=[pltpu.VMEM((2, n_loc, d), BF16), pltpu.VMEM((gap_sizes[0], d), BF16),
                            pltpu.SemaphoreType.DMA((2, n_exp)), pltpu.SemaphoreType.DMA((n_exp,))]),
        compiler_params=_cparams(("arbitrary",)),
        name="moe_permute",
    )(off, pcs, gap_off, gap_n, h, rinfo)


def _moe_expert_kernel(te_ref, tv_ref, tf_ref, xb_ref, nx_ref, x_ref, wg_hbm, wu_hbm, wd_hbm, z_ref,
                       wgs_ref, wus_ref, wds_ref, wgb_ref, wub_ref, wdb_ref, sem, *, layer):
    del xb_ref
    r = pl.program_id(0)

    def fetch(e):
        return (pltpu.make_async_copy(wg_hbm.at[layer, e], wgs_ref, sem.at[0]),
                pltpu.make_async_copy(wu_hbm.at[layer, e], wus_ref, sem.at[1]),
                pltpu.make_async_copy(wd_hbm.at[layer, e], wds_ref, sem.at[2]))

    @pl.when(r == 0)
    def _():
        for cp in fetch(te_ref[0]):
            cp.start()

    @pl.when(tf_ref[r] == 1)
    def _():
        for cp in fetch(te_ref[r]):
            cp.wait()
        wgb_ref[...] = wgs_ref[...].astype(BF16)
        wub_ref[...] = wus_ref[...].astype(BF16)
        wdb_ref[...] = wds_ref[...].astype(BF16)

        @pl.when(nx_ref[r] >= 0)
        def _():
            for cp in fetch(nx_ref[r]):
                cp.start()

    @pl.when(tv_ref[r] > 0)
    def _():
        x = x_ref[...]
        a = _dot(x, wgb_ref[...])
        b = _dot(x, wub_ref[...])
        hid = (a * jax.nn.sigmoid(a)) * b
        z_ref[...] = _dot(hid.astype(BF16), wdb_ref[...]).astype(BF16)

    @pl.when(tv_ref[r] == 0)
    def _():
        z_ref[...] = jnp.zeros_like(z_ref)


def moe_experts(xs, tile_expert, tile_valid, tile_first, x_block, next_expert, layer, wg, wu, wd, tmm):
    n_sorted, d = xs.shape
    f = wg.shape[-1]
    hbm = pl.BlockSpec(memory_space=pl.ANY)
    return pl.pallas_call(
        functools.partial(_moe_expert_kernel, layer=layer),
        out_shape=jax.ShapeDtypeStruct((n_sorted, d), BF16),
        grid_spec=pltpu.PrefetchScalarGridSpec(
            num_scalar_prefetch=5, grid=(n_sorted // tmm,),
            in_specs=[pl.BlockSpec((tmm, d), lambda r, te, tv, tf, xb, nx: (xb[r], 0)), hbm, hbm, hbm],
            out_specs=pl.BlockSpec((tmm, d), lambda r, te, tv, tf, xb, nx: (r, 0)),
            scratch_shapes=[pltpu.VMEM((d, f), F32), pltpu.VMEM((d, f), F32), pltpu.VMEM((f, d), F32),
                            pltpu.VMEM((d, f), BF16), pltpu.VMEM((d, f), BF16), pltpu.VMEM((f, d), BF16),
                            pltpu.SemaphoreType.DMA((3,))]),
        compiler_params=_cparams(("arbitrary",)),
        name="moe_experts",
    )(tile_expert, tile_valid, tile_first, x_block, next_expert, xs, wg, wu, wd)


def _moe_combine_kernel(off_ref, pc_ref, y_ref, mod_ref, rit_ref, zs_ref, *rest, lay, n_exp, sizes):
    out_refs, (zt_ref, sem) = rest[:-2], rest[-2:]
    i = pl.program_id(0)
    slot = i % 2

    def copies(tile, sl, act):
        def piece(e, lrow, grow, size):
            act(pltpu.make_async_copy(zs_ref.at[pl.ds(grow, size)], zt_ref.at[sl, pl.ds(lrow, size)], sem.at[sl, e]))
        _segment_pieces(tile, off_ref, pc_ref, n_exp, sizes, piece)

    def fetch(tile, sl):
        zt_ref[sl] = jnp.zeros(zt_ref.shape[1:], BF16)
        copies(tile, sl, lambda cp: cp.start())

    @pl.when(i == 0)
    def _():
        fetch(i, slot)

    @pl.when(i + 1 < pl.num_programs(0))
    def _():
        fetch(i + 1, 1 - slot)

    rit = rit_ref[...]
    n_loc = zt_ref.shape[1]
    r = lax.broadcasted_iota(jnp.int32, (rit.shape[0], n_loc), 1).astype(F32)
    pw = jnp.where(rit[:, 0:1] == r, rit[:, 2:3], 0.0) + jnp.where(rit[:, 1:2] == r, rit[:, 3:4], 0.0)
    pw = pw.astype(BF16)
    copies(i, slot, lambda cp: cp.wait())
    lay.stream_store(out_refs, y_ref[...] + mod_ref[...][5:6] * _dot(pw, zt_ref[slot]))


def moe_combine(lay, y, mod, rinfo_t, zs, off, pcs, n_loc, n_exp, pair_out):
    t, d = y.shape
    tm = lay.tm
    sizes = [s for s in (1 << b for b in range(12, 3, -1)) if s <= tm]
    if pair_out:
        out_shape = (jax.ShapeDtypeStruct((lay.t_ctx, d), F32), jax.ShapeDtypeStruct((lay.t_lat, d), F32))
    else:
        out_shape = (jax.ShapeDtypeStruct((t, d), F32),)
    out = pl.pallas_call(
        functools.partial(_moe_combine_kernel, lay=lay, n_exp=n_exp, sizes=sizes),
        out_shape=out_shape,
        grid_spec=pltpu.PrefetchScalarGridSpec(
            num_scalar_prefetch=2, grid=(lay.n_tiles,),
            in_specs=[pl.BlockSpec((tm, d), lambda i, o, p: (i, 0)),
                      pl.BlockSpec((None, 8, d), lambda i, o, p: (lay.seg(i), 0, 0)),
                      pl.BlockSpec((tm, 8), lambda i, o, p: (i, 0)),
                      pl.BlockSpec(memory_space=pl.ANY)],
            out_specs=tuple(lay.stream_specs(d, pair_out)),
            scratch_shapes=[pltpu.VMEM((2, n_loc, d), BF16), pltpu.SemaphoreType.DMA((2, n_exp))]),
        input_output_aliases={} if pair_out else {2: 0},
        compiler_params=_cparams(("arbitrary",)),
        name="moe_combine",
    )(off, pcs, y, mod, rinfo_t, zs)
    return out if pair_out else out[0]


def _round_up(x, m):
    return (x + m - 1) // m * m


def moe_layer(lay, routed, mod, layer, wg, wu, wd, tmm, pair_out=False):
    y, h, rinfo, pc = routed
    t, d = y.shape
    n_exp = pc.shape[1]
    pc = pc[:, :, 0].astype(jnp.int32)
    n_loc = 2 * lay.tm + SEG_ALIGN * n_exp
    n_sorted = _round_up(2 * t + lay.n_tiles * n_exp * (SEG_ALIGN - 1) + n_exp * (tmm - 1), tmm)
    tot = jnp.sum(pc, axis=0)
    region = _round_up(tot, tmm)
    ends = jnp.cumsum(region)
    base = ends - region
    off = (base[None, :] + jnp.cumsum(pc, axis=0) - pc).reshape(-1)
    starts = jnp.arange(n_sorted // tmm, dtype=jnp.int32) * tmm
    te = jnp.minimum(jnp.sum(starts[:, None] >= ends[None, :], axis=1), n_exp - 1).astype(jnp.int32)
    tv = jnp.clip(tot[te] - (starts - base[te]), 0, tmm).astype(jnp.int32)
    tf = jnp.concatenate([jnp.ones((1,), jnp.int32), (te[1:] != te[:-1]).astype(jnp.int32)])
    xb = jnp.minimum(starts // tmm, jnp.maximum(ends[-1] // tmm - 1, 0)).astype(jnp.int32)
    n_rt = te.shape[0]
    first_at = jnp.where(tf == 1, jnp.arange(n_rt, dtype=jnp.int32), n_rt)
    next_first = jnp.concatenate([lax.cummin(first_at[::-1])[::-1][1:], jnp.full((1,), n_rt, jnp.int32)])
    nx = jnp.where(next_first < n_rt, te[jnp.minimum(next_first, n_rt - 1)], -1).astype(jnp.int32)
    pcs = pc.reshape(-1)
    assert tmm & (tmm - 1) == 0
    zrows = tmm // 2
    gap_off = jnp.concatenate([base + tot, ends[-1:]]).astype(jnp.int32)
    gap_n = jnp.concatenate([region - tot, (n_sorted - ends[-1:]) // zrows]).astype(jnp.int32)
    xs = moe_permute(lay, h, rinfo, off, pcs, gap_off, gap_n, n_sorted, n_loc, n_exp, tmm)
    zs = moe_experts(xs, te, tv, tf, xb, nx, layer, wg, wu, wd, tmm)
    return moe_combine(lay, y, mod, rinfo.T, zs, off, pcs, n_loc, n_exp, pair_out)


def _rope_perm():
    d = np.arange(QK_ROPE)
    return QK_NOPE + (d ^ (QK_ROPE // 4))


def _pad_cols(x, n):
    return jnp.pad(x, ((0, 0),) * (x.ndim - 1) + ((0, n - x.shape[-1]),))


def _mla_weights(w_q_a, w_q_b, w_kv_a, w_kv_b, q_norm, k_norm, kv_lora):
    perm = _rope_perm()
    nh = N_HEADS
    wkr = w_kv_a[:, kv_lora:]
    d = w_kv_a.shape[0]
    zl = jnp.zeros((d, QK_NOPE), F32)
    kr_blk = _pad_cols(jnp.concatenate([zl, wkr], axis=1), LANES)
    krp_blk = _pad_cols(jnp.concatenate([zl, wkr[:, perm - QK_NOPE]], axis=1), LANES)
    w_a = jnp.concatenate([w_q_a, w_kv_a[:, :kv_lora], kr_blk, krp_blk], axis=1).astype(BF16)
    wq = w_q_b.reshape(-1, nh, QK_DIM).transpose(1, 0, 2)
    wq_main = _pad_cols(wq, HEAD_PAD)
    wq_part = _pad_cols(jnp.concatenate([jnp.zeros_like(wq[..., :QK_NOPE]), wq[..., perm]], axis=-1), HEAD_PAD)
    wq_h = jnp.concatenate([wq_main, wq_part], axis=-1).astype(BF16)
    wkv = w_kv_b.reshape(-1, nh, QK_NOPE + V_DIM).transpose(1, 0, 2)
    wk_h = _pad_cols(wkv[..., :QK_NOPE], HEAD_PAD).astype(BF16)
    wv = wkv[..., QK_NOPE:]
    wv_h = jnp.concatenate([wv[0::2], wv[1::2]], axis=-1).astype(BF16)
    nrm = jnp.stack([_pad_cols(q_norm, LANES), _pad_cols(jnp.concatenate([jnp.zeros((QK_NOPE,), F32), q_norm[perm]]), LANES),
                     _pad_cols(k_norm, LANES), _pad_cols(jnp.concatenate([jnp.zeros((QK_NOPE,), F32), k_norm[perm]]), LANES)])
    nrm = jnp.pad(nrm, ((0, 4), (0, 0)))
    return w_a, wq_h, wk_h, wv_h, nrm


def _rope_tables(lat_len, tm):
    rows = lat_len // GRID_W
    t_row = np.repeat(np.arange(rows, dtype=np.float32), GRID_W)
    t_col = np.tile(np.arange(GRID_W, dtype=np.float32), rows)
    half = QK_ROPE // 2
    inv = (ROPE_BASE ** (-np.arange(0, half, 2, dtype=np.float32) / half)).astype(np.float32)
    ang = jnp.concatenate([jnp.asarray(t_row)[:, None] * inv, jnp.asarray(t_col)[:, None] * inv], axis=-1)
    cos, sin = jnp.cos(ang), jnp.sin(ang)
    nf = QK_ROPE // 4
    cos_l = jnp.concatenate([cos[:, :nf], cos[:, :nf], cos[:, nf:], cos[:, nf:]], axis=-1)
    sin_l = jnp.concatenate([-sin[:, :nf], sin[:, :nf], -sin[:, nf:], sin[:, nf:]], axis=-1)
    one = jnp.ones((lat_len, QK_NOPE), F32)
    cos_t = _pad_cols(jnp.concatenate([one, cos_l], axis=-1), LANES)
    cos_t = cos_t.at[:, QK_DIM:].set(1.0)
    sin_t = _pad_cols(jnp.concatenate([jnp.zeros((lat_len, QK_NOPE), F32), sin_l], axis=-1), LANES)
    cos_t = jnp.concatenate([jnp.ones((tm, LANES), F32), cos_t], axis=0)
    sin_t = jnp.concatenate([jnp.zeros((tm, LANES), F32), sin_t], axis=0)
    return cos_t, sin_t


def mla_layer(lay, y, mod, nw, route, cache_ckv, cache_krope, cos_t, sin_t, w_q_a, q_a_norm, w_q_b, w_kv_a,
              kv_a_norm, w_kv_b, q_norm, k_norm, w_o):
    q_lora = w_q_a.shape[1]
    kv_lora = kv_a_norm.shape[0]
    w_a, wq_h, wk_h, wv_h, nrm = _mla_weights(w_q_a, w_q_b, w_kv_a, w_kv_b, q_norm, k_norm, kv_lora)
    ckv, kr, q, k, v2 = mla_qkv(lay, y, mod, nw, w_a, q_a_norm.reshape(1, -1), kv_a_norm.reshape(1, -1),
                                wq_h, wk_h, wv_h, nrm, cos_t, sin_t, q_lora, kv_lora)
    past = cache_ckv.shape[1]
    ckr = jnp.pad(cache_krope.reshape(-1, QK_ROPE), ((0, 0), (QK_NOPE, LANES - QK_DIM)))
    kc, vc2 = ctx_kv(cache_ckv.reshape(-1, kv_lora), ckr, wk_h, wv_h, nrm, past)
    o_ctx = attention(q, k, v2, 0, lay.n_ctx_b, lay.ctx_len, min(lay.ctx_len, 256), pairs=N_HEADS // 2)
    o_lat = attention(q, k, v2, lay.t_ctx, lay.n_lat_b, lay.lat_len, min(lay.lat_len, 512), kc, vc2, past)
    o = jnp.concatenate([o_ctx, o_lat], axis=0)
    routed = proj_route(lay, y, mod, o, w_o.astype(BF16), route)
    new_ckv = ckv[:lay.t_ctx].reshape(lay.n_ctx_b, lay.ctx_len, kv_lora)
    new_kr = kr[:lay.t_ctx, QK_NOPE:QK_DIM].reshape(lay.n_ctx_b, lay.ctx_len, QK_ROPE)
    return routed, new_ckv, new_kr


def kernel(x_prompt, x_sample, c, cache_ckv, cache_krope, state_ssm_re, state_ssm_im, c_ctx, w_mod, b_mod, norm1_w, norm2_w, mla_w_q_a, mla_q_a_norm, mla_w_q_b, mla_w_kv_a, mla_kv_a_norm, mla_w_kv_b, mla_q_norm, mla_k_norm, mla_w_o, ssm_w_in, ssm_a_re, ssm_a_im, ssm_log_dt, ssm_b_re, ssm_b_im, ssm_c_re, ssm_c_im, ssm_d, ssm_w_glu, w_router, b_router, moe_w_gate, moe_w_up, moe_w_down):
    n_ctx_b, ctx_len, d = x_prompt.shape
    n_lat_b, lat_len, _ = x_sample.shape
    depth = w_mod.shape[0]
    lay = Layout(n_ctx_b, ctx_len, n_lat_b, lat_len, tm=512)
    assert n_lat_b + 1 <= 8
    y = (x_prompt.reshape(-1, d), x_sample.reshape(-1, d))
    cond8 = jnp.pad(jnp.concatenate([c_ctx[None, :], c], axis=0), ((0, 7 - n_lat_b), (0, 0)))
    mods = adaln_all(cond8, w_mod, b_mod)
    mods = jnp.pad(mods.reshape(depth, 8, 6, d), ((0, 0), (0, 0), (0, 2), (0, 0)))
    cos_t, sin_t = _rope_tables(lat_len, lay.tm)
    n_exp = b_router.shape[0]
    wr_hi = w_router.astype(BF16)
    wr_lo = (w_router - wr_hi.astype(F32)).astype(BF16)
    wr_t = jnp.concatenate([wr_hi.T, wr_lo.T], axis=0)
    br = b_router.reshape(n_exp, 1)
    tri = jnp.asarray(np.triu(np.ones((lay.tm, lay.tm), np.float32), k=1), BF16)
    ckv_out, kr_out, sre_out, sim_out = [], [], [], []
    for i in range(depth):
        j = i // 2
        mod = mods[i]
        nw1 = norm1_w[i].reshape(1, d)
        route = (norm2_w[i].reshape(1, d), wr_t, br, tri)
        if i % 2 == 0:
            routed, ckv_p, kr_p = mla_layer(lay, y, mod, nw1, route, cache_ckv[:, j], cache_krope[:, j], cos_t, sin_t,
                                            mla_w_q_a[j], mla_q_a_norm[j], mla_w_q_b[j], mla_w_kv_a[j],
                                            mla_kv_a_norm[j], mla_w_kv_b[j], mla_q_norm[j], mla_k_norm[j], mla_w_o[j])
            ckv_out.append(ckv_p)
            kr_out.append(kr_p)
        else:
            routed, s_re, s_im = s5_layer(lay, y, mod, nw1, route, state_ssm_re[:, j], state_ssm_im[:, j], ssm_w_in[j],
                                          ssm_a_re[j], ssm_a_im[j], ssm_log_dt[j], ssm_b_re[j], ssm_b_im[j],
                                          ssm_c_re[j], ssm_c_im[j], ssm_d[j], ssm_w_glu[j])
            sre_out.append(s_re)
            sim_out.append(s_im)
        y = moe_layer(lay, routed, mod, i, moe_w_gate, moe_w_up, moe_w_down, tmm=512, pair_out=(i == depth - 1))
    yp = y[0].reshape(n_ctx_b, ctx_len, d)
    ys = y[1].reshape(n_lat_b, lat_len, d)
    return (yp, ys, jnp.stack(ckv_out, axis=1), jnp.stack(kr_out, axis=1),
            jnp.stack(sre_out, axis=1), jnp.stack(sim_out, axis=1))
```

```python
import functools
import math

import jax
import jax.numpy as jnp
import numpy as np
from jax import lax
from jax.experimental import pallas as pl
from jax.experimental.pallas import tpu as pltpu

F32 = jnp.float32
BF16 = jnp.bfloat16
EPS = 1e-6

GRID_W = 64
N_HEADS = 8
QK_NOPE = 64
QK_ROPE = 32
QK_DIM = QK_NOPE + QK_ROPE
V_DIM = 64
ROPE_BASE = 10000.0
SSM_GROUP = 16
SSM_STATE = 64
N_EXPERT_GROUPS = 4
EXPERTS_PER_GROUP = 4

LANES = 128
HEAD_PAD = LANES
SSM_CHUNK = 16
SEG_ALIGN = 16
VMEM_LIMIT = 48 * 1024 * 1024


def _cparams(sem, vmem=VMEM_LIMIT):
    return pltpu.CompilerParams(dimension_semantics=sem, vmem_limit_bytes=vmem)


def _dot(a, b):
    return jnp.dot(a, b, preferred_element_type=F32)


def _dot_nt(a, b):
    return lax.dot_general(a, b, (((1,), (1,)), ((), ())), preferred_element_type=F32)


def _dot_hi(a, b):
    return jnp.dot(a, b, preferred_element_type=F32, precision=lax.Precision.HIGHEST)


def _norm_mod(x, nw, shift, scale):
    ms = jnp.mean(x * x, axis=-1, keepdims=True)
    return (x * lax.rsqrt(ms + EPS) * nw) * (1.0 + scale) + shift


class Layout:
    def __init__(self, n_ctx_b, ctx_len, n_lat_b, lat_len, tm):
        self.n_ctx_b, self.ctx_len, self.n_lat_b, self.lat_len = n_ctx_b, ctx_len, n_lat_b, lat_len
        self.t_ctx = n_ctx_b * ctx_len
        self.t_lat = n_lat_b * lat_len
        self.t = self.t_ctx + self.t_lat
        self.tm = tm
        assert self.t_ctx % tm == 0 and lat_len % tm == 0
        self.ctx_tiles = self.t_ctx // tm
        self.lat_tiles_per_b = lat_len // tm
        self.n_tiles = self.t // tm

    def seg(self, i):
        return jnp.where(i < self.ctx_tiles, 0, 1 + (i - self.ctx_tiles) // self.lat_tiles_per_b)

    def rope_blk(self, i):
        return jnp.where(i < self.ctx_tiles, 0, 1 + (i - self.ctx_tiles) % self.lat_tiles_per_b)

    def stream_specs(self, d, pair):
        tm = self.tm
        if not pair:
            return [pl.BlockSpec((tm, d), lambda i, *_: (i, 0))]
        return [pl.BlockSpec((tm, d), lambda i, *_: (jnp.minimum(i, self.ctx_tiles - 1), 0)),
                pl.BlockSpec((tm, d), lambda i, *_: (jnp.maximum(i - self.ctx_tiles, 0), 0))]

    def stream_load(self, refs):
        if len(refs) == 1:
            return refs[0][...]
        return jnp.where(pl.program_id(0) < self.ctx_tiles, refs[0][...], refs[1][...])

    def stream_store(self, refs, val):
        if len(refs) == 1:
            refs[0][...] = val
            return
        i = pl.program_id(0)

        @pl.when(i < self.ctx_tiles)
        def _():
            refs[0][...] = val

        @pl.when(i >= self.ctx_tiles)
        def _():
            refs[1][...] = val


def _as_list(y):
    return list(y) if isinstance(y, (tuple, list)) else [y]


def _adaln_kernel(c_ref, w_ref, b_ref, o_ref):
    c = c_ref[...]
    s = c * jax.nn.sigmoid(c)
    o_ref[...] = _dot_hi(s, w_ref[...]) + b_ref[...]


def adaln_all(cond8, w_mod, b_mod, tn=1536):
    depth, d, n6 = w_mod.shape
    return pl.pallas_call(
        _adaln_kernel,
        out_shape=jax.ShapeDtypeStruct((depth, 8, n6), F32),
        grid=(depth, n6 // tn),
        in_specs=[pl.BlockSpec((8, d), lambda l, j: (0, 0)),
                  pl.BlockSpec((None, d, tn), lambda l, j: (l, 0, j)),
                  pl.BlockSpec((None, 1, tn), lambda l, j: (l, 0, j))],
        out_specs=pl.BlockSpec((None, 8, tn), lambda l, j: (l, 0, j)),
        compiler_params=_cparams(("parallel", "parallel")),
        name="adaln",
    )(cond8, w_mod, b_mod.reshape(depth, 1, n6))


def _kv_heads(ckv_b, kr, krp, wk_ref, wv_ref, knw, knwp, cos, sin, k_ref, v_ref):
    kc = knw * cos
    ks = knwp * sin
    for h in range(N_HEADS):
        kz = _dot(ckv_b, wk_ref[h]) + kr
        r = lax.rsqrt(jnp.sum(kz * kz, axis=-1, keepdims=True) * (1.0 / QK_DIM) + EPS)
        k_ref[h] = (r * (kz * kc + krp * ks)).astype(BF16)
    for hp in range(N_HEADS // 2):
        v_ref[hp] = _dot(ckv_b, wv_ref[hp]).astype(BF16)


def _mla_qkv_kernel(*refs, lay, n_y, q_lora, kv_lora):
    y_refs = refs[:n_y]
    (mod_ref, nw_ref, w_ref, qan_ref, kvan_ref, wq_ref, wk_ref, wv_ref, nrm_ref, cos_ref, sin_ref,
     ckv_ref, kr_ref, q_ref, k_ref, v_ref) = refs[n_y:]
    m = mod_ref[...]
    h = _norm_mod(lay.stream_load(y_refs), nw_ref[...], m[0:1], m[1:2])
    z = _dot(h.astype(BF16), w_ref[...])
    cq = z[:, :q_lora]
    cq = (cq * lax.rsqrt(jnp.mean(cq * cq, axis=-1, keepdims=True) + EPS) * qan_ref[...]).astype(BF16)
    ckv = z[:, q_lora:q_lora + kv_lora]
    ckv = ckv * lax.rsqrt(jnp.mean(ckv * ckv, axis=-1, keepdims=True) + EPS) * kvan_ref[...]
    ckv_ref[...] = ckv
    kr = z[:, q_lora + kv_lora:q_lora + kv_lora + LANES]
    kr_ref[...] = kr
    krp = z[:, q_lora + kv_lora + LANES:]
    nrm = nrm_ref[...]
    cos = cos_ref[...]
    sin = sin_ref[...]
    qc = nrm[0:1] * cos * (QK_DIM ** -0.5)
    qs = nrm[1:2] * sin * (QK_DIM ** -0.5)
    for hd in range(N_HEADS):
        zq = _dot(cq, wq_ref[hd])
        qm = zq[:, :HEAD_PAD]
        qp = zq[:, HEAD_PAD:]
        r = lax.rsqrt(jnp.sum(qm * qm, axis=-1, keepdims=True) * (1.0 / QK_DIM) + EPS)
        q_ref[hd] = (r * (qm * qc + qp * qs)).astype(BF16)
    _kv_heads(ckv.astype(BF16), kr, krp, wk_ref, wv_ref, nrm[2:3], nrm[3:4], cos, sin, k_ref, v_ref)


def mla_qkv(lay, y, mod, nw, w_a, qan, kvan, wq, wk, wv, nrm, cos_t, sin_t, q_lora, kv_lora):
    ys = _as_list(y)
    t, d = lay.t, ys[0].shape[1]
    tm = lay.tm
    row = lambda i: (i, 0)
    full = lambda i: (0, 0)
    hrow = lambda i: (0, i, 0)
    full3 = lambda i: (0, 0, 0)
    rope = lambda i: (lay.rope_blk(i), 0)
    nh = N_HEADS
    return pl.pallas_call(
        functools.partial(_mla_qkv_kernel, lay=lay, n_y=len(ys), q_lora=q_lora, kv_lora=kv_lora),
        out_shape=(jax.ShapeDtypeStruct((t, kv_lora), F32), jax.ShapeDtypeStruct((t, LANES), F32),
                   jax.ShapeDtypeStruct((nh, t, HEAD_PAD), BF16), jax.ShapeDtypeStruct((nh, t, HEAD_PAD), BF16),
                   jax.ShapeDtypeStruct((nh // 2, t, LANES), BF16)),
        grid=(lay.n_tiles,),
        in_specs=lay.stream_specs(d, len(ys) == 2) + [
            pl.BlockSpec((None, 8, d), lambda i: (lay.seg(i), 0, 0)),
            pl.BlockSpec((1, d), full), pl.BlockSpec(w_a.shape, full),
            pl.BlockSpec((1, q_lora), full), pl.BlockSpec((1, kv_lora), full),
            pl.BlockSpec(wq.shape, full3), pl.BlockSpec(wk.shape, full3), pl.BlockSpec(wv.shape, full3),
            pl.BlockSpec((8, LANES), full),
            pl.BlockSpec((tm, LANES), rope), pl.BlockSpec((tm, LANES), rope)],
        out_specs=(pl.BlockSpec((tm, kv_lora), row), pl.BlockSpec((tm, LANES), row),
                   pl.BlockSpec((nh, tm, HEAD_PAD), hrow), pl.BlockSpec((nh, tm, HEAD_PAD), hrow),
                   pl.BlockSpec((nh // 2, tm, LANES), hrow)),
        compiler_params=_cparams(("parallel",)),
        name="mla_qkv",
    )(*ys, mod, nw, w_a, qan, kvan, wq, wk, wv, nrm, cos_t, sin_t)


def _ctx_kv_kernel(ckv_ref, kr_ref, wk_ref, wv_ref, nrm_ref, k_ref, v_ref):
    nrm = nrm_ref[...]
    kr = kr_ref[...]
    one = jnp.ones((1, LANES), F32)
    _kv_heads(ckv_ref[...].astype(BF16), kr, kr, wk_ref, wv_ref, nrm[2:3], nrm[3:4],
              one, jnp.zeros((1, LANES), F32), k_ref, v_ref)


def ctx_kv(ckv, kr, wk, wv, nrm, tm):
    t = ckv.shape[0]
    row = lambda i: (i, 0)
    hrow = lambda i: (0, i, 0)
    full3 = lambda i: (0, 0, 0)
    nh = N_HEADS
    return pl.pallas_call(
        _ctx_kv_kernel,
        out_shape=(jax.ShapeDtypeStruct((nh, t, HEAD_PAD), BF16), jax.ShapeDtypeStruct((nh // 2, t, LANES), BF16)),
        grid=(t // tm,),
        in_specs=[pl.BlockSpec((tm, ckv.shape[1]), row), pl.BlockSpec((tm, LANES), row),
                  pl.BlockSpec(wk.shape, full3), pl.BlockSpec(wv.shape, full3),
                  pl.BlockSpec((8, LANES), lambda i: (0, 0))],
        out_specs=(pl.BlockSpec((nh, tm, HEAD_PAD), hrow), pl.BlockSpec((nh // 2, tm, LANES), hrow)),
        compiler_params=_cparams(("parallel",)),
        name="ctx_kv",
    )(ckv, kr, wk, wv, nrm)


def _attn_kernel(*refs, with_ctx, kblk):
    if with_ctx:
        q_ref, k_ref, v_ref, kc_ref, vc_ref, o_ref = refs
    else:
        q_ref, k_ref, v_ref, o_ref = refs
    seq = k_ref.shape[1]
    n_heads = q_ref.shape[0]
    blocks = [(kc_ref, vc_ref, 0, kc_ref.shape[1])] if with_ctx else []
    blocks += [(k_ref, v_ref, b0, kblk) for b0 in range(0, seq, kblk)]
    items = [(j, blk) for j in range(n_heads) for blk in blocks]

    def scores(item):
        j, (kr, _, b0, n) = item
        return _dot_nt(q_ref[j], kr[j, b0:b0 + n, :])

    outs = []
    m = l = acc = None
    s_next = scores(items[0])
    for idx, (j, (_, vr, b0, n)) in enumerate(items):
        s = s_next
        if idx + 1 < len(items):
            s_next = scores(items[idx + 1])
        mb = jnp.max(s, axis=-1, keepdims=True)
        m_new = mb if m is None else jnp.maximum(m, mb)
        p = jnp.exp(s - m_new)
        pv = _dot(p.astype(BF16), vr[j // 2, b0:b0 + n, :])
        ps = jnp.sum(p, axis=-1, keepdims=True)
        if m is None:
            l, acc = ps, pv
        else:
            a = jnp.exp(m - m_new)
            l, acc = a * l + ps, a * acc + pv
        m = m_new
        if idx + 1 == len(items) or items[idx + 1][0] != j:
            outs.append(acc / l)
            m = l = acc = None
    lane = lax.broadcasted_iota(jnp.int32, outs[0].shape, 1)
    for pr in range(n_heads // 2):
        o_ref[:, pr * LANES:(pr + 1) * LANES] = jnp.where(lane < V_DIM, outs[2 * pr], outs[2 * pr + 1]).astype(BF16)


def attention(q, k, v2, row0, n_b, seq, tq, kc=None, vc2=None, ctx_len=0, kblk=1024, pairs=1):
    assert row0 % seq == 0 and seq % tq == 0
    nq = seq // tq
    qb0 = row0 // tq
    kb0 = row0 // seq
    with_ctx = kc is not None
    hp_n = 2 * pairs
    in_specs = [pl.BlockSpec((hp_n, tq, HEAD_PAD), lambda b, hp, qi: (hp, qb0 + b * nq + qi, 0)),
                pl.BlockSpec((hp_n, seq, HEAD_PAD), lambda b, hp, qi: (hp, kb0 + b, 0)),
                pl.BlockSpec((pairs, seq, LANES), lambda b, hp, qi: (hp, kb0 + b, 0))]
    args = [q, k, v2]
    if with_ctx:
        in_specs += [pl.BlockSpec((hp_n, ctx_len, HEAD_PAD), lambda b, hp, qi: (hp, b, 0)),
                     pl.BlockSpec((pairs, ctx_len, LANES), lambda b, hp, qi: (hp, b, 0))]
        args += [kc, vc2]
    return pl.pallas_call(
        functools.partial(_attn_kernel, with_ctx=with_ctx, kblk=min(seq, kblk)),
        out_shape=jax.ShapeDtypeStruct((n_b * seq, (N_HEADS // 2) * LANES), BF16),
        grid=(n_b, N_HEADS // (2 * pairs), nq),
        in_specs=in_specs,
        out_specs=pl.BlockSpec((tq, pairs * LANES), lambda b, hp, qi: (b * nq + qi, hp)),
        compiler_params=_cparams(("parallel", "parallel", "parallel")),
        name="attn_ctx" if with_ctx else "attn",
    )(*args)


SSM_GPT = LANES // SSM_GROUP
SUBLANES = 8


def _lane_block_transpose(vs):
    n = len(vs)
    blk = lax.broadcasted_iota(jnp.int32, vs[0].shape, 1) // SSM_GROUP
    d = n // 2
    while d >= 1:
        keep = (blk & d) == 0
        new = list(vs)
        for a in range(n):
            if a & d == 0:
                b = a + d
                new[a] = jnp.where(keep, vs[a], pltpu.roll(vs[b], d * SSM_GROUP, 1))
                new[b] = jnp.where(keep, pltpu.roll(vs[a], LANES - d * SSM_GROUP, 1), vs[b])
        vs = new
        d //= 2
    return vs


def _ssm_in_kernel(y_ref, mod_ref, nw_ref, w_ref, u_ref, us_ref, ug_ref):
    m = mod_ref[...]
    h = _norm_mod(y_ref[...], nw_ref[...], m[0:1], m[1:2])
    z = _dot(h.astype(BF16), w_ref[...])
    n_lt = us_ref.shape[0]
    for j in range(n_lt):
        us_ref[j] = z[:, j * LANES:(j + 1) * LANES]
    n_ch = us_ref.shape[1] // SSM_CHUNK
    for j in range(n_lt):
        for rb in range(n_ch // SUBLANES):
            for hf in range(SSM_CHUNK // SSM_GPT):
                src = [us_ref[j, pl.ds(rb * SUBLANES * SSM_CHUNK + hf * SSM_GPT + tt, SUBLANES, stride=SSM_CHUNK), :]
                       for tt in range(SSM_GPT)]
                for gl, v in enumerate(_lane_block_transpose(src)):
                    ug_ref[j * SSM_GPT + gl, rb * SUBLANES:(rb + 1) * SUBLANES, hf * LANES:(hf + 1) * LANES] = v
    u_ref[...] = ug_ref[...].astype(BF16)


def ssm_in(lay, y, mod, nw, w_in):
    t, d = y.shape
    tm = lay.tm
    n = w_in.shape[1]
    g = n // SSM_GROUP
    n_ch = tm // SSM_CHUNK
    qk = SSM_CHUNK * SSM_GROUP
    row = lambda i: (i, 0)
    return pl.pallas_call(
        _ssm_in_kernel,
        out_shape=jax.ShapeDtypeStruct((g, t // SSM_CHUNK, qk), BF16),
        grid=(lay.n_tiles,),
        in_specs=[pl.BlockSpec((tm, d), row), pl.BlockSpec((None, 8, d), lambda i: (lay.seg(i), 0, 0)),
                  pl.BlockSpec((1, d), lambda i: (0, 0)), pl.BlockSpec((d, n), lambda i: (0, 0))],
        out_specs=pl.BlockSpec((g, n_ch, qk), lambda i: (0, i, 0)),
        scratch_shapes=[pltpu.VMEM((n // LANES, tm, LANES), F32), pltpu.VMEM((g, n_ch, qk), F32)],
        compiler_params=_cparams(("parallel",)),
        name="ssm_in",
    )(y, mod, nw, w_in)


def _ssm_prep_kernel(ar_ref, ai_ref, ld_ref, arc_ref, aic_ref, ldc_ref, btr_ref, bti_ref, ctr_ref, cti_ref, dsk_ref,
                     mt_ref, ett_ref, ft_ref, coef_ref):
    q = SSM_CHUNK
    k = SSM_GROUP
    p = SSM_STATE
    qk = q * k
    fwd = pl.program_id(0) == 0
    dt = jnp.exp(ld_ref[...])
    are = ar_ref[...]
    aim = ai_ref[...]
    mag = jnp.exp(dt * are)
    abr = mag * jnp.cos(dt * aim)
    abi = mag * jnp.sin(dt * aim)
    den = are * are + aim * aim
    nr = abr - 1.0
    cf_re = (nr * are + abi * aim) / den
    cf_im = (abi * are - nr * aim) / den
    btr = btr_ref[...]
    bti = bti_ref[...]
    bbt_re = cf_re * btr - cf_im * bti
    bbt_im = cf_re * bti + cf_im * btr
    bbt_re_t = jnp.concatenate([bbt_re] * q, axis=0)
    bbt_im_t = jnp.concatenate([bbt_im] * q, axis=0)
    s_idx = lax.broadcasted_iota(jnp.int32, (q, 1), 0).astype(F32)
    pw = jnp.where(fwd, (q - 1.0) - s_idx, s_idx)
    pm = jnp.exp(pw * dt * are)
    rep_rows = lambda x: jnp.concatenate([jnp.broadcast_to(x[s:s + 1], (k, p)) for s in range(q)], axis=0)
    pr = rep_rows(pm * jnp.cos(pw * dt * aim))
    pi = rep_rows(pm * jnp.sin(pw * dt * aim))
    et_re = pr * bbt_re_t - pi * bbt_im_t
    et_im = pr * bbt_im_t + pi * bbt_re_t
    ett_ref[...] = jnp.concatenate([et_re, et_im, et_im, et_re], axis=1).astype(BF16)
    qm = jnp.exp(q * dt * are)
    aq_re = qm * jnp.cos(q * dt * aim)
    aq_im = qm * jnp.sin(q * dt * aim)
    c1 = jnp.concatenate([aq_re, aq_re], axis=1)
    c2 = jnp.concatenate([-aq_im, aq_im], axis=1)
    coef_ref[...] = jnp.concatenate([c1, c2, jnp.zeros((6, 2 * p), F32)], axis=0)
    dtc = jnp.exp(ldc_ref[...])
    arec = arc_ref[...]
    aimc = aic_ref[...]
    ctr = ctr_ref[...]
    cti = cti_ref[...]
    tau = lax.broadcasted_iota(jnp.int32, (1, q), 1).astype(F32)
    tau0 = jnp.where(fwd, tau, (q - 1.0) - tau)
    m0 = jnp.exp(tau0 * dtc * arec)
    p0r_q = m0 * jnp.cos(tau0 * dtc * aimc)
    p0i_q = m0 * jnp.sin(tau0 * dtc * aimc)
    magc = jnp.exp(dtc * arec)
    abr_c = magc * jnp.cos(dtc * aimc)
    abi_c = magc * jnp.sin(dtc * aimc)
    spread = jnp.where(lax.broadcasted_iota(jnp.int32, (q, qk), 1) // k == lax.broadcasted_iota(jnp.int32, (q, qk), 0),
                       1.0, 0.0)
    p0r = _dot_hi(p0r_q, spread)
    p0i = _dot_hi(p0i_q, spread)
    p1r = _dot_hi(p0r_q * abr_c - p0i_q * abi_c, spread)
    p1i = _dot_hi(p0r_q * abi_c + p0i_q * abr_c, spread)
    ft_ref[...] = jnp.concatenate([ctr * p1r - cti * p1i, -(ctr * p1i + cti * p1r)], axis=0).astype(BF16)
    left_re = ctr * p0r - cti * p0i
    left_im = ctr * p0i + cti * p0r
    kt = _dot_hi(bbt_re, left_re) - _dot_hi(bbt_im, left_im)
    lane = lax.broadcasted_iota(jnp.int32, (k, qk), 1)
    rowi = lax.broadcasted_iota(jnp.int32, (k, qk), 0)
    kt_f = kt + jnp.where(lane == rowi, dsk_ref[...], 0.0)
    rows_f, rows_b = [], []
    for s in range(q):
        rows_f.append(kt_f if s == 0 else jnp.where(lane >= s * k, pltpu.roll(kt_f, s * k, 1), 0.0))
        sh = ((s + 1) * k) % qk
        rows_b.append(kt if sh == 0 else jnp.where(lane < (s + 1) * k, pltpu.roll(kt, sh, 1), 0.0))
    mt = jnp.where(fwd, jnp.concatenate(rows_f, axis=0), jnp.concatenate(rows_b, axis=0))
    mt_ref[...] = mt.astype(BF16)


def ssm_prep(a_re, a_im, log_dt, b_re, b_im, c_re, c_im, d_skip):
    nd, g, p = a_re.shape
    k = b_re.shape[-1]
    qk = SSM_CHUNK * k
    ld = jnp.broadcast_to(log_dt[..., None], (nd, g, p))
    dsk = jnp.broadcast_to(jnp.tile(d_skip.reshape(g, 1, k), (1, 1, SSM_CHUNK))[None], (nd, g, 1, qk))
    rowv = lambda x: x.reshape(nd, g, 1, p)
    colv = lambda x: x.reshape(nd, g, p, 1)
    bt = lambda x: jnp.swapaxes(x, -1, -2)
    ct = lambda x: jnp.tile(jnp.swapaxes(x, -1, -2), (1, 1, 1, SSM_CHUNK))
    spec = lambda r, c: pl.BlockSpec((None, None, r, c), lambda d, j: (d, j, 0, 0))
    return pl.pallas_call(
        _ssm_prep_kernel,
        out_shape=(jax.ShapeDtypeStruct((nd, g, qk, qk), BF16), jax.ShapeDtypeStruct((nd, g, qk, 4 * p), BF16),
                   jax.ShapeDtypeStruct((nd, g, 2 * p, qk), BF16), jax.ShapeDtypeStruct((nd, g, 8, 2 * p), F32)),
        grid=(nd, g),
        in_specs=[spec(1, p), spec(1, p), spec(1, p), spec(p, 1), spec(p, 1), spec(p, 1),
                  spec(k, p), spec(k, p), spec(p, qk), spec(p, qk), spec(1, qk)],
        out_specs=(spec(qk, qk), spec(qk, 4 * p), spec(2 * p, qk), spec(8, 2 * p)),
        compiler_params=_cparams(("parallel", "parallel")),
        name="ssm_prep",
    )(rowv(a_re), rowv(a_im), rowv(ld), colv(a_re), colv(a_im), colv(ld), bt(b_re), bt(b_im), ct(c_re), ct(c_im), dsk)


def _ssm_scan_kernel(u_ref, mt_ref, ett_ref, ft_ref, coef_ref, x0_ref, x0s_ref, y_ref, hfin_ref,
                     sx_ref, sxs_ref, hp_ref, *, gpb, ctx_b, ctx_chunks, lat_b, lat_chunks, rchunk):
    w = 2 * SSM_STATE
    n_rows = u_ref.shape[1]
    ctx_rows = ctx_b * ctx_chunks
    for d in range(2):
        for g in range(gpb):
            for r0 in range(0, n_rows, rchunk):
                ss = _dot(u_ref[g, r0:r0 + rchunk, :], ett_ref[d, g])
                sx_ref[g, r0:r0 + rchunk, :] = ss[:, :w]
                sxs_ref[g, r0:r0 + rchunk, :] = ss[:, w:]
        c1 = [coef_ref[d, g][0:1] for g in range(gpb)]
        c2 = [coef_ref[d, g][1:2] for g in range(gpb)]

        def make_body(base, nb, nc):
            def body(i, carry):
                c = i if d == 0 else nc - 1 - i
                rows = pl.ds(base + c, nb, stride=nc)
                out = []
                for g in range(gpb):
                    x, xs = carry[g]
                    hp_ref[g, rows, :] = x
                    xn = c1[g] * x + c2[g] * xs + sx_ref[g, rows, :]
                    xsn = c1[g] * xs - c2[g] * x + sxs_ref[g, rows, :]
                    out.append((xn, xsn))
                return tuple(out)
            return body

        z = jnp.zeros((ctx_b, w), F32)
        fin = lax.fori_loop(0, ctx_chunks, make_body(0, ctx_b, ctx_chunks), tuple((z, z) for _ in range(gpb)))
        hfin_ref[d] = jnp.concatenate([fin[g][0] for g in range(gpb)], axis=1)
        init = tuple((x0_ref[d][:, g * w:(g + 1) * w], x0s_ref[d][:, g * w:(g + 1) * w]) for g in range(gpb))
        lax.fori_loop(0, lat_chunks, make_body(ctx_rows, lat_b, lat_chunks), init, unroll=4)
        for g in range(gpb):
            for r0 in range(0, n_rows, rchunk):
                y = _dot(u_ref[g, r0:r0 + rchunk, :], mt_ref[d, g])
                y = y + _dot(hp_ref[g, r0:r0 + rchunk, :].astype(BF16), ft_ref[d, g])
                if d == 0:
                    y_ref[g, r0:r0 + rchunk, :] = y
                else:
                    y_ref[g, r0:r0 + rchunk, :] += y


def ssm_scan(u, mt, ett, ft, coef, x0, x0s, ctx_b, ctx_chunks, lat_b, lat_chunks, gpb=4):
    g, n_rows, qk = u.shape
    w = 2 * SSM_STATE
    rchunk = math.gcd(n_rows, 512)
    assert n_rows == ctx_b * ctx_chunks + lat_b * lat_chunks
    ublk = pl.BlockSpec((gpb, n_rows, qk), lambda j: (j, 0, 0))
    blk = lambda r, c: pl.BlockSpec((2, gpb, r, c), lambda j: (0, j, 0, 0))
    vec = lambda r: pl.BlockSpec((2, r, gpb * w), lambda j: (0, 0, j))
    return pl.pallas_call(
        functools.partial(_ssm_scan_kernel, gpb=gpb, ctx_b=ctx_b, ctx_chunks=ctx_chunks, lat_b=lat_b,
                          lat_chunks=lat_chunks, rchunk=rchunk),
        out_shape=(jax.ShapeDtypeStruct((g, n_rows, qk), F32), jax.ShapeDtypeStruct((2, ctx_b, g * w), F32)),
        grid=(g // gpb,),
        in_specs=[ublk, blk(qk, qk), blk(qk, 2 * w), blk(w, qk), blk(8, w), vec(lat_b), vec(lat_b)],
        out_specs=(ublk, vec(ctx_b)),
        scratch_shapes=[pltpu.VMEM((gpb, n_rows, w), F32), pltpu.VMEM((gpb, n_rows, w), F32),
                        pltpu.VMEM((gpb, n_rows, w), F32)],
        compiler_params=_cparams(("parallel",)),
        name="ssm_scan",
    )(u, mt, ett, ft, coef, x0, x0s)


def _ssm_out_kernel(y_ref, mod_ref, yc_ref, w_ref, nw_ref, wr_ref, br_ref, tri_ref, out_ref, h_ref, ri_ref, pc_ref,
                    s_ref, *, d_model):
    n_ch = yc_ref.shape[1]
    n_lt = s_ref.shape[0]
    for j in range(n_lt):
        for rb in range(n_ch // SUBLANES):
            for hf in range(SSM_CHUNK // SSM_GPT):
                src = [yc_ref[j * SSM_GPT + gl, rb * SUBLANES:(rb + 1) * SUBLANES, hf * LANES:(hf + 1) * LANES]
                       for gl in range(SSM_GPT)]
                for tt, v in enumerate(_lane_block_transpose(src)):
                    s_ref[j, pl.ds(rb * SUBLANES * SSM_CHUNK + hf * SSM_GPT + tt, SUBLANES, stride=SSM_CHUNK), :] = v
    s = jnp.concatenate([s_ref[j] for j in range(n_lt)], axis=1)
    a = jax.nn.gelu(s, approximate=True).astype(BF16)
    z = _dot(a, w_ref[...])
    m = mod_ref[...]
    y = y_ref[...] + m[2:3] * (z[:, :d_model] * jax.nn.sigmoid(z[:, d_model:]))
    out_ref[...] = y
    _route_core(y, m, nw_ref, wr_ref, br_ref, tri_ref, h_ref, ri_ref, pc_ref)


def ssm_out_route(lay, y, mod, ych, w_glu, route):
    t, d = y.shape
    tm = lay.tm
    g, _, qk = ych.shape
    n_ch = tm // SSM_CHUNK
    row = lambda i: (i, 0)
    r_args, r_in, r_shape, r_out = _route_io(lay, d, *route)
    return pl.pallas_call(
        functools.partial(_ssm_out_kernel, d_model=d),
        out_shape=[jax.ShapeDtypeStruct((t, d), F32)] + r_shape,
        grid=(lay.n_tiles,),
        in_specs=[pl.BlockSpec((tm, d), row), pl.BlockSpec((None, 8, d), lambda i: (lay.seg(i), 0, 0)),
                  pl.BlockSpec((g, n_ch, qk), lambda i: (0, i, 0)),
                  pl.BlockSpec(w_glu.shape, lambda i: (0, 0))] + r_in,
        out_specs=[pl.BlockSpec((tm, d), row)] + r_out,
        scratch_shapes=[pltpu.VMEM((g * SSM_GROUP // LANES, tm, LANES), F32)],
        input_output_aliases={0: 0},
        compiler_params=_cparams(("parallel",)),
        name="ssm_out_route",
    )(y, mod, ych, w_glu, *r_args)


def s5_layer(lay, y, mod, nw, route, state_re, state_im, w_in, a_re, a_im, log_dt, b_re, b_im, c_re, c_im, d_skip,
             w_glu):
    uch = ssm_in(lay, y, mod, nw, w_in.astype(BF16))
    g = a_re.shape[1]
    p = SSM_STATE
    mt, ett, ft, coef = ssm_prep(a_re, a_im, log_dt, b_re, b_im, c_re, c_im, d_skip)
    st = jnp.concatenate([state_re, state_im], axis=-1)
    sts = jnp.concatenate([state_im, state_re], axis=-1)
    x0 = jnp.transpose(st, (1, 0, 2, 3)).reshape(2, lay.n_lat_b, g * 2 * p)
    x0s = jnp.transpose(sts, (1, 0, 2, 3)).reshape(2, lay.n_lat_b, g * 2 * p)
    ych, hfin = ssm_scan(uch, mt, ett, ft, coef, x0, x0s, lay.n_ctx_b, lay.ctx_len // SSM_CHUNK,
                         lay.n_lat_b, lay.lat_len // SSM_CHUNK)
    routed = ssm_out_route(lay, y, mod, ych, w_glu.astype(BF16), route)
    hf = hfin.reshape(2, lay.n_ctx_b, g, 2, p)
    new_re = jnp.transpose(hf[:, :, :, 0], (1, 0, 2, 3))
    new_im = jnp.transpose(hf[:, :, :, 1], (1, 0, 2, 3))
    return routed, new_re, new_im


def _route_core(y, m, nw_ref, wr_ref, br_ref, tri_ref, h_ref, ri_ref, pc_ref):
    n_exp = br_ref.shape[0]
    h = _norm_mod(y, nw_ref[...], m[3:4], m[4:5])
    h_hi = h.astype(BF16)
    h_ref[...] = h_hi
    h_lo = (h - h_hi.astype(F32)).astype(BF16)
    wr = wr_ref[...]
    lt = _dot_nt(wr, h_hi)
    logits = lt[:n_exp] + lt[n_exp:] + _dot_nt(wr[:n_exp], h_lo)
    scores = jax.nn.sigmoid(logits)
    sel = scores + br_ref[...]
    epg = EXPERTS_PER_GROUP
    row = lambda x, e: x[e:e + 1, :]
    gscore = []
    for g in range(N_EXPERT_GROUPS):
        a, b, c, d = (row(sel, g * epg + j) for j in range(epg))
        m1, n1, m2, n2 = jnp.maximum(a, b), jnp.minimum(a, b), jnp.maximum(c, d), jnp.minimum(c, d)
        gscore.append(jnp.maximum(m1, m2) + jnp.maximum(jnp.minimum(m1, m2), jnp.maximum(n1, n2)))
    best = gscore[0]
    gi = jnp.zeros_like(best, dtype=jnp.int32)
    for g in range(1, N_EXPERT_GROUPS):
        better = gscore[g] > best
        gi = jnp.where(better, g, gi)
        best = jnp.where(better, gscore[g], best)

    def pick(x, j):
        out = row(x, j)
        for g in range(1, N_EXPERT_GROUPS):
            out = jnp.where(gi == g, row(x, g * epg + j), out)
        return out

    sv = [pick(sel, j) for j in range(epg)]
    cv = [pick(scores, j) for j in range(epg)]
    b1, i1, w1 = sv[0], jnp.zeros_like(gi), cv[0]
    for j in range(1, epg):
        better = sv[j] > b1
        i1 = jnp.where(better, j, i1)
        w1 = jnp.where(better, cv[j], w1)
        b1 = jnp.where(better, sv[j], b1)
    neg = jnp.full_like(b1, -jnp.inf)
    b2, i2, w2 = neg, jnp.zeros_like(gi), jnp.zeros_like(w1)
    for j in range(epg):
        better = (i1 != j) & (sv[j] > b2)
        i2 = jnp.where(better, j, i2)
        w2 = jnp.where(better, cv[j], w2)
        b2 = jnp.where(better, sv[j], b2)
    tot = w1 + w2
    e1 = gi * epg + i1
    e2 = gi * epg + i2
    eid = lax.broadcasted_iota(jnp.int32, logits.shape, 0)
    m1h = eid == e1
    m2h = eid == e2
    mc = jnp.where(m1h | m2h, 1.0, 0.0)
    pref = _dot(mc.astype(BF16), tri_ref[...])
    cnt = jnp.sum(mc, axis=1, keepdims=True)
    pc_al = jnp.ceil(cnt * (1.0 / SEG_ALIGN))
    pcb = jnp.broadcast_to(pc_al, (n_exp, LANES))
    er = lax.broadcasted_iota(jnp.int32, (n_exp, n_exp), 0)
    ec = lax.broadcasted_iota(jnp.int32, (n_exp, n_exp), 1)
    lower = jnp.where(ec < er, 1.0, 0.0).astype(BF16)
    seg = _dot(lower, pcb.astype(BF16))[:, 0:1] * SEG_ALIGN
    slot = seg + pref
    pos1 = jnp.sum(jnp.where(m1h, slot, 0.0), axis=0, keepdims=True)
    pos2 = jnp.sum(jnp.where(m2h, slot, 0.0), axis=0, keepdims=True)
    zero = jnp.zeros_like(pos1)
    ri_ref[...] = jnp.concatenate([pos1, pos2, w1 / tot, w2 / tot, zero, zero, zero, zero], axis=0)
    pc_ref[...] = pcb * SEG_ALIGN


def _route_io(lay, d, nw, wr_t, br, tri):
    tm = lay.tm
    n_exp = br.shape[0]
    full = lambda i: (0, 0)
    args = [nw, wr_t, br, tri]
    in_specs = [pl.BlockSpec((1, d), full), pl.BlockSpec(wr_t.shape, full), pl.BlockSpec((n_exp, 1), full),
                pl.BlockSpec((tm, tm), full)]
    out_shape = [jax.ShapeDtypeStruct((lay.t, d), BF16), jax.ShapeDtypeStruct((8, lay.t), F32),
                 jax.ShapeDtypeStruct((lay.n_tiles, n_exp, LANES), F32)]
    out_specs = [pl.BlockSpec((tm, d), lambda i: (i, 0)), pl.BlockSpec((8, tm), lambda i: (0, i)),
                 pl.BlockSpec((None, n_exp, LANES), lambda i: (i, 0, 0))]
    return args, in_specs, out_shape, out_specs


def _proj_route_kernel(*refs, lay, n_y):
    y_refs = refs[:n_y]
    o_refs = refs[n_y + 1:n_y + 3]
    mod_ref = refs[n_y]
    w_ref, nw_ref, wr_ref, br_ref, tri_ref, out_ref, h_ref, ri_ref, pc_ref = refs[n_y + 3:]
    m = mod_ref[...]
    y = lay.stream_load(y_refs) + m[2:3] * _dot(lay.stream_load(o_refs), w_ref[...])
    out_ref[...] = y
    _route_core(y, m, nw_ref, wr_ref, br_ref, tri_ref, h_ref, ri_ref, pc_ref)


def proj_route(lay, y, mod, o_pair, w, route):
    ys = _as_list(y)
    d = ys[0].shape[1]
    tm = lay.tm
    row = lambda i: (i, 0)
    r_args, r_in, r_shape, r_out = _route_io(lay, d, *route)
    return pl.pallas_call(
        functools.partial(_proj_route_kernel, lay=lay, n_y=len(ys)),
        out_shape=[jax.ShapeDtypeStruct((lay.t, d), F32)] + r_shape,
        grid=(lay.n_tiles,),
        in_specs=lay.stream_specs(d, len(ys) == 2) + [pl.BlockSpec((None, 8, d), lambda i: (lay.seg(i), 0, 0))]
        + lay.stream_specs(o_pair[0].shape[1], True) + [pl.BlockSpec(w.shape, lambda i: (0, 0))] + r_in,
        out_specs=[pl.BlockSpec((tm, d), row)] + r_out,
        input_output_aliases={0: 0} if len(ys) == 1 else {},
        compiler_params=_cparams(("parallel",)),
        name="proj_route",
    )(*ys, mod, *o_pair, w, *r_args)


def _segment_pieces(i, off_ref, pc_ref, n_exp, sizes, fn):
    local = 0
    for e in range(n_exp):
        n = pc_ref[i * n_exp + e]
        g0 = off_ref[i * n_exp + e]

        def emit(size, e=e, n=n, g0=g0, local=local):
            above = n & (-2 * size)

            @pl.when((n & size) != 0)
            def _():
                fn(e, pl.multiple_of(local + above, SEG_ALIGN), pl.multiple_of(g0 + above, SEG_ALIGN), size)

        for size in sizes:
            emit(size)
        local = local + n


def _pieces(n, sizes):
    for size in sizes:
        yield (n & size) != 0, n & (-2 * size), size


def _moe_permute_kernel(off_ref, pc_ref, goff_ref, gn_ref, h_ref, ri_ref, xs_ref, ys_ref, zp_ref, sem, gsem,
                        *, n_exp, sizes, gap_sizes, rchunk):
    i = pl.program_id(0)
    last = pl.num_programs(0) - 1
    slot = i % 2
    ri = ri_ref[...]
    pos1 = ri[0:1]
    pos2 = ri[1:2]
    h = h_ref[...]
    n_loc = ys_ref.shape[1]
    for r0 in range(0, n_loc, rchunk):
        r = (lax.broadcasted_iota(jnp.int32, (rchunk, h.shape[0]), 0) + r0).astype(F32)
        p = jnp.where(pos1 == r, 1.0, 0.0) + jnp.where(pos2 == r, 1.0, 0.0)
        ys_ref[slot, r0:r0 + rchunk, :] = _dot(p.astype(BF16), h).astype(BF16)

    def copies(tile, sl, act):
        def piece(e, lrow, grow, size):
            act(pltpu.make_async_copy(ys_ref.at[sl, pl.ds(lrow, size)], xs_ref.at[pl.ds(grow, size)], sem.at[sl, e]))
        _segment_pieces(tile, off_ref, pc_ref, n_exp, sizes, piece)

    start = lambda cp: cp.start()
    wait = lambda cp: cp.wait()
    copies(i, slot, start)

    @pl.when(i > 0)
    def _():
        copies(i - 1, 1 - slot, wait)

    @pl.when(i == last)
    def _():
        copies(i, slot, wait)
        zp_ref[...] = jnp.zeros_like(zp_ref)

        def gaps():
            for e in range(n_exp):
                g0 = goff_ref[e]
                for present, above, size in _pieces(gn_ref[e], gap_sizes):
                    yield present, pltpu.make_async_copy(
                        zp_ref.at[pl.ds(0, size)], xs_ref.at[pl.ds(pl.multiple_of(g0 + above, SEG_ALIGN), size)],
                        gsem.at[e])

        for present, cp in gaps():
            pl.when(present)(cp.start)
        for present, cp in gaps():
            pl.when(present)(cp.wait)

        zrows = zp_ref.shape[0]
        tail0 = goff_ref[n_exp]
        tail = lambda c: pltpu.make_async_copy(
            zp_ref, xs_ref.at[pl.ds(pl.multiple_of(tail0 + c * zrows, zrows), zrows)], gsem.at[0])

        @pl.loop(0, gn_ref[n_exp])
        def _(c):
            tail(c).start()

        @pl.loop(0, gn_ref[n_exp])
        def _(c):
            tail(c).wait()


def moe_permute(lay, h, rinfo, off, pcs, gap_off, gap_n, n_sorted, n_loc, n_exp, tmm):
    t, d = h.shape
    tm = lay.tm
    sizes = [s for s in (1 << b for b in range(12, 3, -1)) if s <= tm]
    gap_sizes = [s for s in (1 << b for b in range(12, 3, -1)) if s < tmm]
    return pl.pallas_call(
        functools.partial(_moe_permute_kernel, n_exp=n_exp, sizes=sizes, gap_sizes=gap_sizes, rchunk=256),
        out_shape=jax.ShapeDtypeStruct((n_sorted, d), BF16),
        grid_spec=pltpu.PrefetchScalarGridSpec(
            num_scalar_prefetch=4, grid=(lay.n_tiles,),
            in_specs=[pl.BlockSpec((tm, d), lambda i, *_: (i, 0)), pl.BlockSpec((8, tm), lambda i, *_: (0, i))],
            out_specs=pl.BlockSpec(memory_space=pl.ANY),
            scratch_shapes=[pltpu.VMEM((2, n_loc, d), BF16), pltpu.VMEM((gap_sizes[0], d), BF16),
                            pltpu.SemaphoreType.DMA((2, n_exp)), pltpu.SemaphoreType.DMA((n_exp,))]),
        compiler_params=_cparams(("arbitrary",)),
        name="moe_permute",
    )(off, pcs, gap_off, gap_n, h, rinfo)


def _moe_expert_kernel(te_ref, tv_ref, tf_ref, xb_ref, nx_ref, x_ref, wg_hbm, wu_hbm, wd_hbm, z_ref,
                       wgs_ref, wus_ref, wds_ref, wgb_ref, wub_ref, wdb_ref, sem, *, layer):
    del xb_ref
    r = pl.program_id(0)

    def fetch(e):
        return (pltpu.make_async_copy(wg_hbm.at[layer, e], wgs_ref, sem.at[0]),
                pltpu.make_async_copy(wu_hbm.at[layer, e], wus_ref, sem.at[1]),
                pltpu.make_async_copy(wd_hbm.at[layer, e], wds_ref, sem.at[2]))

    @pl.when(r == 0)
    def _():
        for cp in fetch(te_ref[0]):
            cp.start()

    @pl.when(tf_ref[r] == 1)
    def _():
        for cp in fetch(te_ref[r]):
            cp.wait()
        wgb_ref[...] = wgs_ref[...].astype(BF16)
        wub_ref[...] = wus_ref[...].astype(BF16)
        wdb_ref[...] = wds_ref[...].astype(BF16)

        @pl.when(nx_ref[r] >= 0)
        def _():
            for cp in fetch(nx_ref[r]):
                cp.start()

    @pl.when(tv_ref[r] > 0)
    def _():
        x = x_ref[...]
        a = _dot(x, wgb_ref[...])
        b = _dot(x, wub_ref[...])
        hid = (a * jax.nn.sigmoid(a)) * b
        z_ref[...] = _dot(hid.astype(BF16), wdb_ref[...]).astype(BF16)

    @pl.when(tv_ref[r] == 0)
    def _():
        z_ref[...] = jnp.zeros_like(z_ref)


def moe_experts(xs, tile_expert, tile_valid, tile_first, x_block, next_expert, layer, wg, wu, wd, tmm):
    n_sorted, d = xs.shape
    f = wg.shape[-1]
    hbm = pl.BlockSpec(memory_space=pl.ANY)
    return pl.pallas_call(
        functools.partial(_moe_expert_kernel, layer=layer),
        out_shape=jax.ShapeDtypeStruct((n_sorted, d), BF16),
        grid_spec=pltpu.PrefetchScalarGridSpec(
            num_scalar_prefetch=5, grid=(n_sorted // tmm,),
            in_specs=[pl.BlockSpec((tmm, d), lambda r, te, tv, tf, xb, nx: (xb[r], 0)), hbm, hbm, hbm],
            out_specs=pl.BlockSpec((tmm, d), lambda r, te, tv, tf, xb, nx: (r, 0)),
            scratch_shapes=[pltpu.VMEM((d, f), F32), pltpu.VMEM((d, f), F32), pltpu.VMEM((f, d), F32),
                            pltpu.VMEM((d, f), BF16), pltpu.VMEM((d, f), BF16), pltpu.VMEM((f, d), BF16),
                            pltpu.SemaphoreType.DMA((3,))]),
        compiler_params=_cparams(("arbitrary",)),
        name="moe_experts",
    )(tile_expert, tile_valid, tile_first, x_block, next_expert, xs, wg, wu, wd)


def _moe_combine_kernel(off_ref, pc_ref, y_ref, mod_ref, rit_ref, zs_ref, *rest, lay, n_exp, sizes):
    out_refs, (zt_ref, sem) = rest[:-2], rest[-2:]
    i = pl.program_id(0)
    slot = i % 2

    def copies(tile, sl, act):
        def piece(e, lrow, grow, size):
            act(pltpu.make_async_copy(zs_ref.at[pl.ds(grow, size)], zt_ref.at[sl, pl.ds(lrow, size)], sem.at[sl, e]))
        _segment_pieces(tile, off_ref, pc_ref, n_exp, sizes, piece)

    def fetch(tile, sl):
        used = pc_ref[tile * n_exp]
        for e in range(1, n_exp):
            used = used + pc_ref[tile * n_exp + e]

        @pl.loop(used // SEG_ALIGN, zt_ref.shape[1] // SEG_ALIGN)
        def _(c):
            zt_ref[sl, pl.ds(pl.multiple_of(c * SEG_ALIGN, SEG_ALIGN), SEG_ALIGN), :] = jnp.zeros(
                (SEG_ALIGN, zt_ref.shape[2]), BF16)
        copies(tile, sl, lambda cp: cp.start())

    @pl.when(i == 0)
    def _():
        fetch(i, slot)

    @pl.when(i + 1 < pl.num_programs(0))
    def _():
        fetch(i + 1, 1 - slot)

    rit = rit_ref[...]
    n_loc = zt_ref.shape[1]
    r = lax.broadcasted_iota(jnp.int32, (rit.shape[0], n_loc), 1).astype(F32)
    pw = jnp.where(rit[:, 0:1] == r, rit[:, 2:3], 0.0) + jnp.where(rit[:, 1:2] == r, rit[:, 3:4], 0.0)
    pw = pw.astype(BF16)
    copies(i, slot, lambda cp: cp.wait())
    lay.stream_store(out_refs, y_ref[...] + mod_ref[...][5:6] * _dot(pw, zt_ref[slot]))


def moe_combine(lay, y, mod, rinfo_t, zs, off, pcs, n_loc, n_exp, pair_out):
    t, d = y.shape
    tm = lay.tm
    sizes = [s for s in (1 << b for b in range(12, 3, -1)) if s <= tm]
    if pair_out:
        out_shape = (jax.ShapeDtypeStruct((lay.t_ctx, d), F32), jax.ShapeDtypeStruct((lay.t_lat, d), F32))
    else:
        out_shape = (jax.ShapeDtypeStruct((t, d), F32),)
    out = pl.pallas_call(
        functools.partial(_moe_combine_kernel, lay=lay, n_exp=n_exp, sizes=sizes),
        out_shape=out_shape,
        grid_spec=pltpu.PrefetchScalarGridSpec(
            num_scalar_prefetch=2, grid=(lay.n_tiles,),
            in_specs=[pl.BlockSpec((tm, d), lambda i, o, p: (i, 0)),
                      pl.BlockSpec((None, 8, d), lambda i, o, p: (lay.seg(i), 0, 0)),
                      pl.BlockSpec((tm, 8), lambda i, o, p: (i, 0)),
                      pl.BlockSpec(memory_space=pl.ANY)],
            out_specs=tuple(lay.stream_specs(d, pair_out)),
            scratch_shapes=[pltpu.VMEM((2, n_loc, d), BF16), pltpu.SemaphoreType.DMA((2, n_exp))]),
        input_output_aliases={} if pair_out else {2: 0},
        compiler_params=_cparams(("arbitrary",)),
        name="moe_combine",
    )(off, pcs, y, mod, rinfo_t, zs)
    return out if pair_out else out[0]


def _round_up(x, m):
    return (x + m - 1) // m * m


def moe_layer(lay, routed, mod, layer, wg, wu, wd, tmm, pair_out=False):
    y, h, rinfo, pc = routed
    t, d = y.shape
    n_exp = pc.shape[1]
    pc = pc[:, :, 0].astype(jnp.int32)
    n_loc = 2 * lay.tm + SEG_ALIGN * n_exp
    n_sorted = _round_up(2 * t + lay.n_tiles * n_exp * (SEG_ALIGN - 1) + n_exp * (tmm - 1), tmm)
    tot = jnp.sum(pc, axis=0)
    region = _round_up(tot, tmm)
    ends = jnp.cumsum(region)
    base = ends - region
    off = (base[None, :] + jnp.cumsum(pc, axis=0) - pc).reshape(-1)
    starts = jnp.arange(n_sorted // tmm, dtype=jnp.int32) * tmm
    te = jnp.minimum(jnp.sum(starts[:, None] >= ends[None, :], axis=1), n_exp - 1).astype(jnp.int32)
    tv = jnp.clip(tot[te] - (starts - base[te]), 0, tmm).astype(jnp.int32)
    tf = jnp.concatenate([jnp.ones((1,), jnp.int32), (te[1:] != te[:-1]).astype(jnp.int32)])
    xb = jnp.minimum(starts // tmm, jnp.maximum(ends[-1] // tmm - 1, 0)).astype(jnp.int32)
    n_rt = te.shape[0]
    first_at = jnp.where(tf == 1, jnp.arange(n_rt, dtype=jnp.int32), n_rt)
    next_first = jnp.concatenate([lax.cummin(first_at[::-1])[::-1][1:], jnp.full((1,), n_rt, jnp.int32)])
    nx = jnp.where(next_first < n_rt, te[jnp.minimum(next_first, n_rt - 1)], -1).astype(jnp.int32)
    pcs = pc.reshape(-1)
    assert tmm & (tmm - 1) == 0
    zrows = tmm // 2
    gap_off = jnp.concatenate([base + tot, ends[-1:]]).astype(jnp.int32)
    gap_n = jnp.concatenate([region - tot, (n_sorted - ends[-1:]) // zrows]).astype(jnp.int32)
    xs = moe_permute(lay, h, rinfo, off, pcs, gap_off, gap_n, n_sorted, n_loc, n_exp, tmm)
    zs = moe_experts(xs, te, tv, tf, xb, nx, layer, wg, wu, wd, tmm)
    return moe_combine(lay, y, mod, rinfo.T, zs, off, pcs, n_loc, n_exp, pair_out)


def _rope_perm():
    d = np.arange(QK_ROPE)
    return QK_NOPE + (d ^ (QK_ROPE // 4))


def _pad_cols(x, n):
    return jnp.pad(x, ((0, 0),) * (x.ndim - 1) + ((0, n - x.shape[-1]),))


def _mla_weights(w_q_a, w_q_b, w_kv_a, w_kv_b, q_norm, k_norm, kv_lora):
    perm = _rope_perm()
    nh = N_HEADS
    wkr = w_kv_a[:, kv_lora:]
    d = w_kv_a.shape[0]
    zl = jnp.zeros((d, QK_NOPE), F32)
    kr_blk = _pad_cols(jnp.concatenate([zl, wkr], axis=1), LANES)
    krp_blk = _pad_cols(jnp.concatenate([zl, wkr[:, perm - QK_NOPE]], axis=1), LANES)
    w_a = jnp.concatenate([w_q_a, w_kv_a[:, :kv_lora], kr_blk, krp_blk], axis=1).astype(BF16)
    wq = w_q_b.reshape(-1, nh, QK_DIM).transpose(1, 0, 2)
    wq_main = _pad_cols(wq, HEAD_PAD)
    wq_part = _pad_cols(jnp.concatenate([jnp.zeros_like(wq[..., :QK_NOPE]), wq[..., perm]], axis=-1), HEAD_PAD)
    wq_h = jnp.concatenate([wq_main, wq_part], axis=-1).astype(BF16)
    wkv = w_kv_b.reshape(-1, nh, QK_NOPE + V_DIM).transpose(1, 0, 2)
    wk_h = _pad_cols(wkv[..., :QK_NOPE], HEAD_PAD).astype(BF16)
    wv = wkv[..., QK_NOPE:]
    wv_h = jnp.concatenate([wv[0::2], wv[1::2]], axis=-1).astype(BF16)
    nrm = jnp.stack([_pad_cols(q_norm, LANES), _pad_cols(jnp.concatenate([jnp.zeros((QK_NOPE,), F32), q_norm[perm]]), LANES),
                     _pad_cols(k_norm, LANES), _pad_cols(jnp.concatenate([jnp.zeros((QK_NOPE,), F32), k_norm[perm]]), LANES)])
    nrm = jnp.pad(nrm, ((0, 4), (0, 0)))
    return w_a, wq_h, wk_h, wv_h, nrm


def _rope_tables(lat_len, tm):
    rows = lat_len // GRID_W
    t_row = np.repeat(np.arange(rows, dtype=np.float32), GRID_W)
    t_col = np.tile(np.arange(GRID_W, dtype=np.float32), rows)
    half = QK_ROPE // 2
    inv = (ROPE_BASE ** (-np.arange(0, half, 2, dtype=np.float32) / half)).astype(np.float32)
    ang = jnp.concatenate([jnp.asarray(t_row)[:, None] * inv, jnp.asarray(t_col)[:, None] * inv], axis=-1)
    cos, sin = jnp.cos(ang), jnp.sin(ang)
    nf = QK_ROPE // 4
    cos_l = jnp.concatenate([cos[:, :nf], cos[:, :nf], cos[:, nf:], cos[:, nf:]], axis=-1)
    sin_l = jnp.concatenate([-sin[:, :nf], sin[:, :nf], -sin[:, nf:], sin[:, nf:]], axis=-1)
    one = jnp.ones((lat_len, QK_NOPE), F32)
    cos_t = _pad_cols(jnp.concatenate([one, cos_l], axis=-1), LANES)
    cos_t = cos_t.at[:, QK_DIM:].set(1.0)
    sin_t = _pad_cols(jnp.concatenate([jnp.zeros((lat_len, QK_NOPE), F32), sin_l], axis=-1), LANES)
    cos_t = jnp.concatenate([jnp.ones((tm, LANES), F32), cos_t], axis=0)
    sin_t = jnp.concatenate([jnp.zeros((tm, LANES), F32), sin_t], axis=0)
    return cos_t, sin_t


def mla_layer(lay, y, mod, nw, route, cache_ckv, cache_krope, cos_t, sin_t, w_q_a, q_a_norm, w_q_b, w_kv_a,
              kv_a_norm, w_kv_b, q_norm, k_norm, w_o):
    q_lora = w_q_a.shape[1]
    kv_lora = kv_a_norm.shape[0]
    w_a, wq_h, wk_h, wv_h, nrm = _mla_weights(w_q_a, w_q_b, w_kv_a, w_kv_b, q_norm, k_norm, kv_lora)
    ckv, kr, q, k, v2 = mla_qkv(lay, y, mod, nw, w_a, q_a_norm.reshape(1, -1), kv_a_norm.reshape(1, -1),
                                wq_h, wk_h, wv_h, nrm, cos_t, sin_t, q_lora, kv_lora)
    past = cache_ckv.shape[1]
    ckr = jnp.pad(cache_krope.reshape(-1, QK_ROPE), ((0, 0), (QK_NOPE, LANES - QK_DIM)))
    kc, vc2 = ctx_kv(cache_ckv.reshape(-1, kv_lora), ckr, wk_h, wv_h, nrm, past)
    o_ctx = attention(q, k, v2, 0, lay.n_ctx_b, lay.ctx_len, min(lay.ctx_len, 256), pairs=N_HEADS // 2)
    o_lat = attention(q, k, v2, lay.t_ctx, lay.n_lat_b, lay.lat_len, min(lay.lat_len, 512), kc, vc2, past)
    routed = proj_route(lay, y, mod, (o_ctx, o_lat), w_o.astype(BF16), route)
    new_ckv = ckv[:lay.t_ctx].reshape(lay.n_ctx_b, lay.ctx_len, kv_lora)
    new_kr = kr[:lay.t_ctx, QK_NOPE:QK_DIM].reshape(lay.n_ctx_b, lay.ctx_len, QK_ROPE)
    return routed, new_ckv, new_kr


def kernel(x_prompt, x_sample, c, cache_ckv, cache_krope, state_ssm_re, state_ssm_im, c_ctx, w_mod, b_mod, norm1_w, norm2_w, mla_w_q_a, mla_q_a_norm, mla_w_q_b, mla_w_kv_a, mla_kv_a_norm, mla_w_kv_b, mla_q_norm, mla_k_norm, mla_w_o, ssm_w_in, ssm_a_re, ssm_a_im, ssm_log_dt, ssm_b_re, ssm_b_im, ssm_c_re, ssm_c_im, ssm_d, ssm_w_glu, w_router, b_router, moe_w_gate, moe_w_up, moe_w_down):
    n_ctx_b, ctx_len, d = x_prompt.shape
    n_lat_b, lat_len, _ = x_sample.shape
    depth = w_mod.shape[0]
    lay = Layout(n_ctx_b, ctx_len, n_lat_b, lat_len, tm=512)
    assert n_lat_b + 1 <= 8
    y = (x_prompt.reshape(-1, d), x_sample.reshape(-1, d))
    cond8 = jnp.pad(jnp.concatenate([c_ctx[None, :], c], axis=0), ((0, 7 - n_lat_b), (0, 0)))
    mods = adaln_all(cond8, w_mod, b_mod)
    mods = jnp.pad(mods.reshape(depth, 8, 6, d), ((0, 0), (0, 0), (0, 2), (0, 0)))
    cos_t, sin_t = _rope_tables(lat_len, lay.tm)
    n_exp = b_router.shape[0]
    wr_hi = w_router.astype(BF16)
    wr_lo = (w_router - wr_hi.astype(F32)).astype(BF16)
    wr_t = jnp.concatenate([wr_hi.T, wr_lo.T], axis=0)
    br = b_router.reshape(n_exp, 1)
    tri = jnp.asarray(np.triu(np.ones((lay.tm, lay.tm), np.float32), k=1), BF16)
    ckv_out, kr_out, sre_out, sim_out = [], [], [], []
    for i in range(depth):
        j = i // 2
        mod = mods[i]
        nw1 = norm1_w[i].reshape(1, d)
        route = (norm2_w[i].reshape(1, d), wr_t, br, tri)
        if i % 2 == 0:
            routed, ckv_p, kr_p = mla_layer(lay, y, mod, nw1, route, cache_ckv[:, j], cache_krope[:, j], cos_t, sin_t,
                                            mla_w_q_a[j], mla_q_a_norm[j], mla_w_q_b[j], mla_w_kv_a[j],
                                            mla_kv_a_norm[j], mla_w_kv_b[j], mla_q_norm[j], mla_k_norm[j], mla_w_o[j])
            ckv_out.append(ckv_p)
            kr_out.append(kr_p)
        else:
            routed, s_re, s_im = s5_layer(lay, y, mod, nw1, route, state_ssm_re[:, j], state_ssm_im[:, j], ssm_w_in[j],
                                          ssm_a_re[j], ssm_a_im[j], ssm_log_dt[j], ssm_b_re[j], ssm_b_im[j],
                                          ssm_c_re[j], ssm_c_im[j], ssm_d[j], ssm_w_glu[j])
            sre_out.append(s_re)
            sim_out.append(s_im)
        y = moe_layer(lay, routed, mod, i, moe_w_gate, moe_w_up, moe_w_down, tmm=512, pair_out=(i == depth - 1))
    yp = y[0].reshape(n_ctx_b, ctx_len, d)
    ys = y[1].reshape(n_lat_b, lat_len, d)
    return (yp, ys, jnp.stack(ckv_out, axis=1), jnp.stack(kr_out, axis=1),
            jnp.stack(sre_out, axis=1), jnp.stack(sim_out, axis=1))
```

```python
import functools
import math

import jax
import jax.numpy as jnp
import numpy as np
from jax import lax
from jax.experimental import pallas as pl
from jax.experimental.pallas import tpu as pltpu

F32 = jnp.float32
BF16 = jnp.bfloat16
EPS = 1e-6

GRID_W = 64
N_HEADS = 8
QK_NOPE = 64
QK_ROPE = 32
QK_DIM = QK_NOPE + QK_ROPE
V_DIM = 64
ROPE_BASE = 10000.0
SSM_GROUP = 16
SSM_STATE = 64
N_EXPERT_GROUPS = 4
EXPERTS_PER_GROUP = 4

LANES = 128
HEAD_PAD = LANES
SSM_CHUNK = 16
SEG_ALIGN = 16
VMEM_LIMIT = 48 * 1024 * 1024


def _cparams(sem, vmem=VMEM_LIMIT):
    return pltpu.CompilerParams(dimension_semantics=sem, vmem_limit_bytes=vmem)


def _dot(a, b):
    return jnp.dot(a, b, preferred_element_type=F32)


def _dot_nt(a, b):
    return lax.dot_general(a, b, (((1,), (1,)), ((), ())), preferred_element_type=F32)


def _dot_hi(a, b):
    return jnp.dot(a, b, preferred_element_type=F32, precision=lax.Precision.HIGHEST)


def _norm_mod(x, nw, shift, scale):
    ms = jnp.mean(x * x, axis=-1, keepdims=True)
    return (x * lax.rsqrt(ms + EPS) * nw) * (1.0 + scale) + shift


class Layout:
    def __init__(self, n_ctx_b, ctx_len, n_lat_b, lat_len, tm):
        self.n_ctx_b, self.ctx_len, self.n_lat_b, self.lat_len = n_ctx_b, ctx_len, n_lat_b, lat_len
        self.t_ctx = n_ctx_b * ctx_len
        self.t_lat = n_lat_b * lat_len
        self.t = self.t_ctx + self.t_lat
        self.tm = tm
        assert self.t_ctx % tm == 0 and lat_len % tm == 0
        self.ctx_tiles = self.t_ctx // tm
        self.lat_tiles_per_b = lat_len // tm
        self.n_tiles = self.t // tm

    def seg(self, i):
        return jnp.where(i < self.ctx_tiles, 0, 1 + (i - self.ctx_tiles) // self.lat_tiles_per_b)

    def rope_blk(self, i):
        return jnp.where(i < self.ctx_tiles, 0, 1 + (i - self.ctx_tiles) % self.lat_tiles_per_b)

    def stream_specs(self, d, pair):
        tm = self.tm
        if not pair:
            return [pl.BlockSpec((tm, d), lambda i, *_: (i, 0))]
        return [pl.BlockSpec((tm, d), lambda i, *_: (jnp.minimum(i, self.ctx_tiles - 1), 0)),
                pl.BlockSpec((tm, d), lambda i, *_: (jnp.maximum(i - self.ctx_tiles, 0), 0))]

    def stream_load(self, refs):
        if len(refs) == 1:
            return refs[0][...]
        return jnp.where(pl.program_id(0) < self.ctx_tiles, refs[0][...], refs[1][...])

    def stream_store(self, refs, val):
        if len(refs) == 1:
            refs[0][...] = val
            return
        i = pl.program_id(0)

        @pl.when(i < self.ctx_tiles)
        def _():
            refs[0][...] = val

        @pl.when(i >= self.ctx_tiles)
        def _():
            refs[1][...] = val


def _as_list(y):
    return list(y) if isinstance(y, (tuple, list)) else [y]


def _adaln_kernel(c_ref, w_ref, b_ref, o_ref):
    c = c_ref[...]
    s = c * jax.nn.sigmoid(c)
    o_ref[...] = _dot_hi(s, w_ref[...]) + b_ref[...]


def adaln_all(cond8, w_mod, b_mod, tn=1536):
    depth, d, n6 = w_mod.shape
    return pl.pallas_call(
        _adaln_kernel,
        out_shape=jax.ShapeDtypeStruct((depth, 8, n6), F32),
        grid=(depth, n6 // tn),
        in_specs=[pl.BlockSpec((8, d), lambda l, j: (0, 0)),
                  pl.BlockSpec((None, d, tn), lambda l, j: (l, 0, j)),
                  pl.BlockSpec((None, 1, tn), lambda l, j: (l, 0, j))],
        out_specs=pl.BlockSpec((None, 8, tn), lambda l, j: (l, 0, j)),
        compiler_params=_cparams(("parallel", "parallel")),
        name="adaln",
    )(cond8, w_mod, b_mod.reshape(depth, 1, n6))


def _kv_heads(ckv_b, kr, krp, wk_ref, wv_ref, knw, knwp, cos, sin, k_ref, v_ref):
    kc = knw * cos
    ks = knwp * sin
    for h in range(N_HEADS):
        kz = _dot(ckv_b, wk_ref[h]) + kr
        r = lax.rsqrt(jnp.sum(kz * kz, axis=-1, keepdims=True) * (1.0 / QK_DIM) + EPS)
        k_ref[h] = (r * (kz * kc + krp * ks)).astype(BF16)
    for hp in range(N_HEADS // 2):
        v_ref[hp] = _dot(ckv_b, wv_ref[hp]).astype(BF16)


def _mla_qkv_kernel(*refs, lay, n_y, q_lora, kv_lora):
    y_refs = refs[:n_y]
    (mod_ref, nw_ref, w_ref, qan_ref, kvan_ref, wq_ref, wk_ref, wv_ref, nrm_ref, cos_ref, sin_ref,
     ckv_ref, kr_ref, q_ref, k_ref, v_ref) = refs[n_y:]
    m = mod_ref[...]
    h = _norm_mod(lay.stream_load(y_refs), nw_ref[...], m[0:1], m[1:2])
    z = _dot(h.astype(BF16), w_ref[...])
    cq = z[:, :q_lora]
    cq = (cq * lax.rsqrt(jnp.mean(cq * cq, axis=-1, keepdims=True) + EPS) * qan_ref[...]).astype(BF16)
    ckv = z[:, q_lora:q_lora + kv_lora]
    ckv = ckv * lax.rsqrt(jnp.mean(ckv * ckv, axis=-1, keepdims=True) + EPS) * kvan_ref[...]
    ckv_ref[...] = ckv
    kr = z[:, q_lora + kv_lora:q_lora + kv_lora + LANES]
    kr_ref[...] = kr
    krp = z[:, q_lora + kv_lora + LANES:]
    nrm = nrm_ref[...]
    cos = cos_ref[...]
    sin = sin_ref[...]
    qc = nrm[0:1] * cos * (QK_DIM ** -0.5)
    qs = nrm[1:2] * sin * (QK_DIM ** -0.5)
    for hd in range(N_HEADS):
        zq = _dot(cq, wq_ref[hd])
        qm = zq[:, :HEAD_PAD]
        qp = zq[:, HEAD_PAD:]
        r = lax.rsqrt(jnp.sum(qm * qm, axis=-1, keepdims=True) * (1.0 / QK_DIM) + EPS)
        q_ref[hd] = (r * (qm * qc + qp * qs)).astype(BF16)
    _kv_heads(ckv.astype(BF16), kr, krp, wk_ref, wv_ref, nrm[2:3], nrm[3:4], cos, sin, k_ref, v_ref)


def mla_qkv(lay, y, mod, nw, w_a, qan, kvan, wq, wk, wv, nrm, cos_t, sin_t, q_lora, kv_lora):
    ys = _as_list(y)
    t, d = lay.t, ys[0].shape[1]
    tm = lay.tm
    row = lambda i: (i, 0)
    full = lambda i: (0, 0)
    hrow = lambda i: (0, i, 0)
    full3 = lambda i: (0, 0, 0)
    rope = lambda i: (lay.rope_blk(i), 0)
    nh = N_HEADS
    return pl.pallas_call(
        functools.partial(_mla_qkv_kernel, lay=lay, n_y=len(ys), q_lora=q_lora, kv_lora=kv_lora),
        out_shape=(jax.ShapeDtypeStruct((t, kv_lora), F32), jax.ShapeDtypeStruct((t, LANES), F32),
                   jax.ShapeDtypeStruct((nh, t, HEAD_PAD), BF16), jax.ShapeDtypeStruct((nh, t, HEAD_PAD), BF16),
                   jax.ShapeDtypeStruct((nh // 2, t, LANES), BF16)),
        grid=(lay.n_tiles,),
        in_specs=lay.stream_specs(d, len(ys) == 2) + [
            pl.BlockSpec((None, 8, d), lambda i: (lay.seg(i), 0, 0)),
            pl.BlockSpec((1, d), full), pl.BlockSpec(w_a.shape, full),
            pl.BlockSpec((1, q_lora), full), pl.BlockSpec((1, kv_lora), full),
            pl.BlockSpec(wq.shape, full3), pl.BlockSpec(wk.shape, full3), pl.BlockSpec(wv.shape, full3),
            pl.BlockSpec((8, LANES), full),
            pl.BlockSpec((tm, LANES), rope), pl.BlockSpec((tm, LANES), rope)],
        out_specs=(pl.BlockSpec((tm, kv_lora), row), pl.BlockSpec((tm, LANES), row),
                   pl.BlockSpec((nh, tm, HEAD_PAD), hrow), pl.BlockSpec((nh, tm, HEAD_PAD), hrow),
                   pl.BlockSpec((nh // 2, tm, LANES), hrow)),
        compiler_params=_cparams(("parallel",)),
        name="mla_qkv",
    )(*ys, mod, nw, w_a, qan, kvan, wq, wk, wv, nrm, cos_t, sin_t)


def _ctx_kv_kernel(ckv_ref, kr_ref, wk_ref, wv_ref, nrm_ref, k_ref, v_ref):
    nrm = nrm_ref[...]
    kr = kr_ref[...]
    one = jnp.ones((1, LANES), F32)
    _kv_heads(ckv_ref[...].astype(BF16), kr, kr, wk_ref, wv_ref, nrm[2:3], nrm[3:4],
              one, jnp.zeros((1, LANES), F32), k_ref, v_ref)


def ctx_kv(ckv, kr, wk, wv, nrm, tm):
    t = ckv.shape[0]
    row = lambda i: (i, 0)
    hrow = lambda i: (0, i, 0)
    full3 = lambda i: (0, 0, 0)
    nh = N_HEADS
    return pl.pallas_call(
        _ctx_kv_kernel,
        out_shape=(jax.ShapeDtypeStruct((nh, t, HEAD_PAD), BF16), jax.ShapeDtypeStruct((nh // 2, t, LANES), BF16)),
        grid=(t // tm,),
        in_specs=[pl.BlockSpec((tm, ckv.shape[1]), row), pl.BlockSpec((tm, LANES), row),
                  pl.BlockSpec(wk.shape, full3), pl.BlockSpec(wv.shape, full3),
                  pl.BlockSpec((8, LANES), lambda i: (0, 0))],
        out_specs=(pl.BlockSpec((nh, tm, HEAD_PAD), hrow), pl.BlockSpec((nh // 2, tm, LANES), hrow)),
        compiler_params=_cparams(("parallel",)),
        name="ctx_kv",
    )(ckv, kr, wk, wv, nrm)


def _attn_kernel(*refs, with_ctx, kblk):
    if with_ctx:
        q_ref, k_ref, v_ref, kc_ref, vc_ref, o_ref = refs
    else:
        q_ref, k_ref, v_ref, o_ref = refs
    seq = k_ref.shape[1]
    n_heads = q_ref.shape[0]
    blocks = [(kc_ref, vc_ref, 0, kc_ref.shape[1])] if with_ctx else []
    blocks += [(k_ref, v_ref, b0, kblk) for b0 in range(0, seq, kblk)]
    items = [(j, blk) for j in range(n_heads) for blk in blocks]

    def scores(item):
        j, (kr, _, b0, n) = item
        return _dot_nt(q_ref[j], kr[j, b0:b0 + n, :])

    outs = []
    m = l = acc = None
    s_next = scores(items[0])
    for idx, (j, (_, vr, b0, n)) in enumerate(items):
        s = s_next
        if idx + 1 < len(items):
            s_next = scores(items[idx + 1])
        mb = jnp.max(s, axis=-1, keepdims=True)
        m_new = mb if m is None else jnp.maximum(m, mb)
        p = jnp.exp(s - m_new)
        pv = _dot(p.astype(BF16), vr[j // 2, b0:b0 + n, :])
        ps = jnp.sum(p, axis=-1, keepdims=True)
        if m is None:
            l, acc = ps, pv
        else:
            a = jnp.exp(m - m_new)
            l, acc = a * l + ps, a * acc + pv
        m = m_new
        if idx + 1 == len(items) or items[idx + 1][0] != j:
            outs.append(acc / l)
            m = l = acc = None
    lane = lax.broadcasted_iota(jnp.int32, outs[0].shape, 1)
    for pr in range(n_heads // 2):
        o_ref[:, pr * LANES:(pr + 1) * LANES] = jnp.where(lane < V_DIM, outs[2 * pr], outs[2 * pr + 1]).astype(BF16)


def attention(q, k, v2, row0, n_b, seq, tq, kc=None, vc2=None, ctx_len=0, kblk=1024, pairs=1):
    assert row0 % seq == 0 and seq % tq == 0
    nq = seq // tq
    qb0 = row0 // tq
    kb0 = row0 // seq
    with_ctx = kc is not None
    hp_n = 2 * pairs
    in_specs = [pl.BlockSpec((hp_n, tq, HEAD_PAD), lambda b, hp, qi: (hp, qb0 + b * nq + qi, 0)),
                pl.BlockSpec((hp_n, seq, HEAD_PAD), lambda b, hp, qi: (hp, kb0 + b, 0)),
                pl.BlockSpec((pairs, seq, LANES), lambda b, hp, qi: (hp, kb0 + b, 0))]
    args = [q, k, v2]
    if with_ctx:
        in_specs += [pl.BlockSpec((hp_n, ctx_len, HEAD_PAD), lambda b, hp, qi: (hp, b, 0)),
                     pl.BlockSpec((pairs, ctx_len, LANES), lambda b, hp, qi: (hp, b, 0))]
        args += [kc, vc2]
    return pl.pallas_call(
        functools.partial(_attn_kernel, with_ctx=with_ctx, kblk=min(seq, kblk)),
        out_shape=jax.ShapeDtypeStruct((n_b * seq, (N_HEADS // 2) * LANES), BF16),
        grid=(n_b, N_HEADS // (2 * pairs), nq),
        in_specs=in_specs,
        out_specs=pl.BlockSpec((tq, pairs * LANES), lambda b, hp, qi: (b * nq + qi, hp)),
        compiler_params=_cparams(("parallel", "parallel", "parallel")),
        name="attn_ctx" if with_ctx else "attn",
    )(*args)


SSM_GPT = LANES // SSM_GROUP
SUBLANES = 8


def _lane_block_transpose(vs):
    n = len(vs)
    blk = lax.broadcasted_iota(jnp.int32, vs[0].shape, 1) // SSM_GROUP
    d = n // 2
    while d >= 1:
        keep = (blk & d) == 0
        new = list(vs)
        for a in range(n):
            if a & d == 0:
                b = a + d
                new[a] = jnp.where(keep, vs[a], pltpu.roll(vs[b], d * SSM_GROUP, 1))
                new[b] = jnp.where(keep, pltpu.roll(vs[a], LANES - d * SSM_GROUP, 1), vs[b])
        vs = new
        d //= 2
    return vs


def _ssm_in_kernel(y_ref, mod_ref, nw_ref, w_ref, u_ref, us_ref, ug_ref):
    m = mod_ref[...]
    h = _norm_mod(y_ref[...], nw_ref[...], m[0:1], m[1:2])
    z = _dot(h.astype(BF16), w_ref[...])
    n_lt = us_ref.shape[0]
    for j in range(n_lt):
        us_ref[j] = z[:, j * LANES:(j + 1) * LANES]
    n_ch = us_ref.shape[1] // SSM_CHUNK
    for j in range(n_lt):
        for rb in range(n_ch // SUBLANES):
            for hf in range(SSM_CHUNK // SSM_GPT):
                src = [us_ref[j, pl.ds(rb * SUBLANES * SSM_CHUNK + hf * SSM_GPT + tt, SUBLANES, stride=SSM_CHUNK), :]
                       for tt in range(SSM_GPT)]
                for gl, v in enumerate(_lane_block_transpose(src)):
                    ug_ref[j * SSM_GPT + gl, rb * SUBLANES:(rb + 1) * SUBLANES, hf * LANES:(hf + 1) * LANES] = v
    u_ref[...] = ug_ref[...].astype(BF16)


def ssm_in(lay, y, mod, nw, w_in):
    t, d = y.shape
    tm = lay.tm
    n = w_in.shape[1]
    g = n // SSM_GROUP
    n_ch = tm // SSM_CHUNK
    qk = SSM_CHUNK * SSM_GROUP
    row = lambda i: (i, 0)
    return pl.pallas_call(
        _ssm_in_kernel,
        out_shape=jax.ShapeDtypeStruct((g, t // SSM_CHUNK, qk), BF16),
        grid=(lay.n_tiles,),
        in_specs=[pl.BlockSpec((tm, d), row), pl.BlockSpec((None, 8, d), lambda i: (lay.seg(i), 0, 0)),
                  pl.BlockSpec((1, d), lambda i: (0, 0)), pl.BlockSpec((d, n), lambda i: (0, 0))],
        out_specs=pl.BlockSpec((g, n_ch, qk), lambda i: (0, i, 0)),
        scratch_shapes=[pltpu.VMEM((n // LANES, tm, LANES), F32), pltpu.VMEM((g, n_ch, qk), F32)],
        compiler_params=_cparams(("parallel",)),
        name="ssm_in",
    )(y, mod, nw, w_in)


def _ssm_prep_kernel(ar_ref, ai_ref, ld_ref, arc_ref, aic_ref, ldc_ref, btr_ref, bti_ref, ctr_ref, cti_ref, dsk_ref,
                     mt_ref, ett_ref, ft_ref, coef_ref):
    q = SSM_CHUNK
    k = SSM_GROUP
    p = SSM_STATE
    qk = q * k
    fwd = pl.program_id(0) == 0
    dt = jnp.exp(ld_ref[...])
    are = ar_ref[...]
    aim = ai_ref[...]
    mag = jnp.exp(dt * are)
    abr = mag * jnp.cos(dt * aim)
    abi = mag * jnp.sin(dt * aim)
    den = are * are + aim * aim
    nr = abr - 1.0
    cf_re = (nr * are + abi * aim) / den
    cf_im = (abi * are - nr * aim) / den
    btr = btr_ref[...]
    bti = bti_ref[...]
    bbt_re = cf_re * btr - cf_im * bti
    bbt_im = cf_re * bti + cf_im * btr
    bbt_re_t = jnp.concatenate([bbt_re] * q, axis=0)
    bbt_im_t = jnp.concatenate([bbt_im] * q, axis=0)
    s_idx = lax.broadcasted_iota(jnp.int32, (q, 1), 0).astype(F32)
    pw = jnp.where(fwd, (q - 1.0) - s_idx, s_idx)
    pm = jnp.exp(pw * dt * are)
    rep_rows = lambda x: jnp.concatenate([jnp.broadcast_to(x[s:s + 1], (k, p)) for s in range(q)], axis=0)
    pr = rep_rows(pm * jnp.cos(pw * dt * aim))
    pi = rep_rows(pm * jnp.sin(pw * dt * aim))
    et_re = pr * bbt_re_t - pi * bbt_im_t
    et_im = pr * bbt_im_t + pi * bbt_re_t
    ett_ref[...] = jnp.concatenate([et_re, et_im, et_im, et_re], axis=1).astype(BF16)
    qm = jnp.exp(q * dt * are)
    aq_re = qm * jnp.cos(q * dt * aim)
    aq_im = qm * jnp.sin(q * dt * aim)
    c1 = jnp.concatenate([aq_re, aq_re], axis=1)
    c2 = jnp.concatenate([-aq_im, aq_im], axis=1)
    coef_ref[...] = jnp.concatenate([c1, c2, jnp.zeros((6, 2 * p), F32)], axis=0)
    dtc = jnp.exp(ldc_ref[...])
    arec = arc_ref[...]
    aimc = aic_ref[...]
    ctr = ctr_ref[...]
    cti = cti_ref[...]
    tau = lax.broadcasted_iota(jnp.int32, (1, q), 1).astype(F32)
    tau0 = jnp.where(fwd, tau, (q - 1.0) - tau)
    m0 = jnp.exp(tau0 * dtc * arec)
    p0r_q = m0 * jnp.cos(tau0 * dtc * aimc)
    p0i_q = m0 * jnp.sin(tau0 * dtc * aimc)
    magc = jnp.exp(dtc * arec)
    abr_c = magc * jnp.cos(dtc * aimc)
    abi_c = magc * jnp.sin(dtc * aimc)
    spread = jnp.where(lax.broadcasted_iota(jnp.int32, (q, qk), 1) // k == lax.broadcasted_iota(jnp.int32, (q, qk), 0),
                       1.0, 0.0)
    p0r = _dot_hi(p0r_q, spread)
    p0i = _dot_hi(p0i_q, spread)
    p1r = _dot_hi(p0r_q * abr_c - p0i_q * abi_c, spread)
    p1i = _dot_hi(p0r_q * abi_c + p0i_q * abr_c, spread)
    ft_ref[...] = jnp.concatenate([ctr * p1r - cti * p1i, -(ctr * p1i + cti * p1r)], axis=0).astype(BF16)
    left_re = ctr * p0r - cti * p0i
    left_im = ctr * p0i + cti * p0r
    kt = _dot_hi(bbt_re, left_re) - _dot_hi(bbt_im, left_im)
    lane = lax.broadcasted_iota(jnp.int32, (k, qk), 1)
    rowi = lax.broadcasted_iota(jnp.int32, (k, qk), 0)
    kt_f = kt + jnp.where(lane == rowi, dsk_ref[...], 0.0)
    rows_f, rows_b = [], []
    for s in range(q):
        rows_f.append(kt_f if s == 0 else jnp.where(lane >= s * k, pltpu.roll(kt_f, s * k, 1), 0.0))
        sh = ((s + 1) * k) % qk
        rows_b.append(kt if sh == 0 else jnp.where(lane < (s + 1) * k, pltpu.roll(kt, sh, 1), 0.0))
    mt = jnp.where(fwd, jnp.concatenate(rows_f, axis=0), jnp.concatenate(rows_b, axis=0))
    mt_ref[...] = mt.astype(BF16)


def ssm_prep(a_re, a_im, log_dt, b_re, b_im, c_re, c_im, d_skip):
    nd, g, p = a_re.shape
    k = b_re.shape[-1]
    qk = SSM_CHUNK * k
    ld = jnp.broadcast_to(log_dt[..., None], (nd, g, p))
    dsk = jnp.broadcast_to(jnp.tile(d_skip.reshape(g, 1, k), (1, 1, SSM_CHUNK))[None], (nd, g, 1, qk))
    rowv = lambda x: x.reshape(nd, g, 1, p)
    colv = lambda x: x.reshape(nd, g, p, 1)
    bt = lambda x: jnp.swapaxes(x, -1, -2)
    ct = lambda x: jnp.tile(jnp.swapaxes(x, -1, -2), (1, 1, 1, SSM_CHUNK))
    spec = lambda r, c: pl.BlockSpec((None, None, r, c), lambda d, j: (d, j, 0, 0))
    return pl.pallas_call(
        _ssm_prep_kernel,
        out_shape=(jax.ShapeDtypeStruct((nd, g, qk, qk), BF16), jax.ShapeDtypeStruct((nd, g, qk, 4 * p), BF16),
                   jax.ShapeDtypeStruct((nd, g, 2 * p, qk), BF16), jax.ShapeDtypeStruct((nd, g, 8, 2 * p), F32)),
        grid=(nd, g),
        in_specs=[spec(1, p), spec(1, p), spec(1, p), spec(p, 1), spec(p, 1), spec(p, 1),
                  spec(k, p), spec(k, p), spec(p, qk), spec(p, qk), spec(1, qk)],
        out_specs=(spec(qk, qk), spec(qk, 4 * p), spec(2 * p, qk), spec(8, 2 * p)),
        compiler_params=_cparams(("parallel", "parallel")),
        name="ssm_prep",
    )(rowv(a_re), rowv(a_im), rowv(ld), colv(a_re), colv(a_im), colv(ld), bt(b_re), bt(b_im), ct(c_re), ct(c_im), dsk)


def _ssm_scan_kernel(u_ref, mt_ref, ett_ref, ft_ref, coef_ref, x0_ref, x0s_ref, y_ref, hfin_ref,
                     sx_ref, sxs_ref, hp_ref, *, gpb, ctx_b, ctx_chunks, lat_b, lat_chunks, rchunk):
    w = 2 * SSM_STATE
    n_rows = u_ref.shape[1]
    ctx_rows = ctx_b * ctx_chunks
    for d in range(2):
        for g in range(gpb):
            for r0 in range(0, n_rows, rchunk):
                ss = _dot(u_ref[g, r0:r0 + rchunk, :], ett_ref[d, g])
                sx_ref[g, r0:r0 + rchunk, :] = ss[:, :w]
                sxs_ref[g, r0:r0 + rchunk, :] = ss[:, w:]
        c1 = [coef_ref[d, g][0:1] for g in range(gpb)]
        c2 = [coef_ref[d, g][1:2] for g in range(gpb)]

        def make_body(base, nb, nc):
            def body(i, carry):
                c = i if d == 0 else nc - 1 - i
                rows = pl.ds(base + c, nb, stride=nc)
                out = []
                for g in range(gpb):
                    x, xs = carry[g]
                    hp_ref[g, rows, :] = x
                    xn = c1[g] * x + c2[g] * xs + sx_ref[g, rows, :]
                    xsn = c1[g] * xs - c2[g] * x + sxs_ref[g, rows, :]
                    out.append((xn, xsn))
                return tuple(out)
            return body

        z = jnp.zeros((ctx_b, w), F32)
        fin = lax.fori_loop(0, ctx_chunks, make_body(0, ctx_b, ctx_chunks), tuple((z, z) for _ in range(gpb)))
        hfin_ref[d] = jnp.concatenate([fin[g][0] for g in range(gpb)], axis=1)
        init = tuple((x0_ref[d][:, g * w:(g + 1) * w], x0s_ref[d][:, g * w:(g + 1) * w]) for g in range(gpb))
        lax.fori_loop(0, lat_chunks, make_body(ctx_rows, lat_b, lat_chunks), init, unroll=4)
        for g in range(gpb):
            for r0 in range(0, n_rows, rchunk):
                y = _dot(u_ref[g, r0:r0 + rchunk, :], mt_ref[d, g])
                y = y + _dot(hp_ref[g, r0:r0 + rchunk, :].astype(BF16), ft_ref[d, g])
                if d == 0:
                    y_ref[g, r0:r0 + rchunk, :] = y
                else:
                    y_ref[g, r0:r0 + rchunk, :] += y


def ssm_scan(u, mt, ett, ft, coef, x0, x0s, ctx_b, ctx_chunks, lat_b, lat_chunks, gpb=4):
    g, n_rows, qk = u.shape
    w = 2 * SSM_STATE
    rchunk = math.gcd(n_rows, 512)
    assert n_rows == ctx_b * ctx_chunks + lat_b * lat_chunks
    ublk = pl.BlockSpec((gpb, n_rows, qk), lambda j: (j, 0, 0))
    blk = lambda r, c: pl.BlockSpec((2, gpb, r, c), lambda j: (0, j, 0, 0))
    vec = lambda r: pl.BlockSpec((2, r, gpb * w), lambda j: (0, 0, j))
    return pl.pallas_call(
        functools.partial(_ssm_scan_kernel, gpb=gpb, ctx_b=ctx_b, ctx_chunks=ctx_chunks, lat_b=lat_b,
                          lat_chunks=lat_chunks, rchunk=rchunk),
        out_shape=(jax.ShapeDtypeStruct((g, n_rows, qk), F32), jax.ShapeDtypeStruct((2, ctx_b, g * w), F32)),
        grid=(g // gpb,),
        in_specs=[ublk, blk(qk, qk), blk(qk, 2 * w), blk(w, qk), blk(8, w), vec(lat_b), vec(lat_b)],
        out_specs=(ublk, vec(ctx_b)),
        scratch_shapes=[pltpu.VMEM((gpb, n_rows, w), F32), pltpu.VMEM((gpb, n_rows, w), F32),
                        pltpu.VMEM((gpb, n_rows, w), F32)],
        compiler_params=_cparams(("parallel",)),
        name="ssm_scan",
    )(u, mt, ett, ft, coef, x0, x0s)


def _ssm_out_kernel(y_ref, mod_ref, yc_ref, w_ref, nw_ref, wr_ref, br_ref, tri_ref, out_ref, h_ref, ri_ref, pc_ref,
                    s_ref, *, d_model):
    n_ch = yc_ref.shape[1]
    n_lt = s_ref.shape[0]
    for j in range(n_lt):
        for rb in range(n_ch // SUBLANES):
            for hf in range(SSM_CHUNK // SSM_GPT):
                src = [yc_ref[j * SSM_GPT + gl, rb * SUBLANES:(rb + 1) * SUBLANES, hf * LANES:(hf + 1) * LANES]
                       for gl in range(SSM_GPT)]
                for tt, v in enumerate(_lane_block_transpose(src)):
                    s_ref[j, pl.ds(rb * SUBLANES * SSM_CHUNK + hf * SSM_GPT + tt, SUBLANES, stride=SSM_CHUNK), :] = v
    s = jnp.concatenate([s_ref[j] for j in range(n_lt)], axis=1)
    a = jax.nn.gelu(s, approximate=True).astype(BF16)
    z = _dot(a, w_ref[...])
    m = mod_ref[...]
    y = y_ref[...] + m[2:3] * (z[:, :d_model] * jax.nn.sigmoid(z[:, d_model:]))
    out_ref[...] = y
    _route_core(y, m, nw_ref, wr_ref, br_ref, tri_ref, h_ref, ri_ref, pc_ref)


def ssm_out_route(lay, y, mod, ych, w_glu, route):
    t, d = y.shape
    tm = lay.tm
    g, _, qk = ych.shape
    n_ch = tm // SSM_CHUNK
    row = lambda i: (i, 0)
    r_args, r_in, r_shape, r_out = _route_io(lay, d, *route)
    return pl.pallas_call(
        functools.partial(_ssm_out_kernel, d_model=d),
        out_shape=[jax.ShapeDtypeStruct((t, d), F32)] + r_shape,
        grid=(lay.n_tiles,),
        in_specs=[pl.BlockSpec((tm, d), row), pl.BlockSpec((None, 8, d), lambda i: (lay.seg(i), 0, 0)),
                  pl.BlockSpec((g, n_ch, qk), lambda i: (0, i, 0)),
                  pl.BlockSpec(w_glu.shape, lambda i: (0, 0))] + r_in,
        out_specs=[pl.BlockSpec((tm, d), row)] + r_out,
        scratch_shapes=[pltpu.VMEM((g * SSM_GROUP // LANES, tm, LANES), F32)],
        input_output_aliases={0: 0},
        compiler_params=_cparams(("parallel",)),
        name="ssm_out_route",
    )(y, mod, ych, w_glu, *r_args)


def s5_layer(lay, y, mod, nw, route, state_re, state_im, w_in, a_re, a_im, log_dt, b_re, b_im, c_re, c_im, d_skip,
             w_glu):
    uch = ssm_in(lay, y, mod, nw, w_in.astype(BF16))
    g = a_re.shape[1]
    p = SSM_STATE
    mt, ett, ft, coef = ssm_prep(a_re, a_im, log_dt, b_re, b_im, c_re, c_im, d_skip)
    st = jnp.concatenate([state_re, state_im], axis=-1)
    sts = jnp.concatenate([state_im, state_re], axis=-1)
    x0 = jnp.transpose(st, (1, 0, 2, 3)).reshape(2, lay.n_lat_b, g * 2 * p)
    x0s = jnp.transpose(sts, (1, 0, 2, 3)).reshape(2, lay.n_lat_b, g * 2 * p)
    ych, hfin = ssm_scan(uch, mt, ett, ft, coef, x0, x0s, lay.n_ctx_b, lay.ctx_len // SSM_CHUNK,
                         lay.n_lat_b, lay.lat_len // SSM_CHUNK)
    routed = ssm_out_route(lay, y, mod, ych, w_glu.astype(BF16), route)
    hf = hfin.reshape(2, lay.n_ctx_b, g, 2, p)
    new_re = jnp.transpose(hf[:, :, :, 0], (1, 0, 2, 3))
    new_im = jnp.transpose(hf[:, :, :, 1], (1, 0, 2, 3))
    return routed, new_re, new_im


def _route_core(y, m, nw_ref, wr_ref, br_ref, tri_ref, h_ref, ri_ref, pc_ref):
    n_exp = br_ref.shape[0]
    h = _norm_mod(y, nw_ref[...], m[3:4], m[4:5])
    h_hi = h.astype(BF16)
    h_ref[...] = h_hi
    h_lo = (h - h_hi.astype(F32)).astype(BF16)
    wr = wr_ref[...]
    lt = _dot_nt(wr, h_hi)
    logits = lt[:n_exp] + lt[n_exp:] + _dot_nt(wr[:n_exp], h_lo)
    scores = jax.nn.sigmoid(logits)
    sel = scores + br_ref[...]
    epg = EXPERTS_PER_GROUP
    row = lambda x, e: x[e:e + 1, :]
    gscore = []
    for g in range(N_EXPERT_GROUPS):
        a, b, c, d = (row(sel, g * epg + j) for j in range(epg))
        m1, n1, m2, n2 = jnp.maximum(a, b), jnp.minimum(a, b), jnp.maximum(c, d), jnp.minimum(c, d)
        gscore.append(jnp.maximum(m1, m2) + jnp.maximum(jnp.minimum(m1, m2), jnp.maximum(n1, n2)))
    best = gscore[0]
    gi = jnp.zeros_like(best, dtype=jnp.int32)
    for g in range(1, N_EXPERT_GROUPS):
        better = gscore[g] > best
        gi = jnp.where(better, g, gi)
        best = jnp.where(better, gscore[g], best)

    def pick(x, j):
        out = row(x, j)
        for g in range(1, N_EXPERT_GROUPS):
            out = jnp.where(gi == g, row(x, g * epg + j), out)
        return out

    sv = [pick(sel, j) for j in range(epg)]
    cv = [pick(scores, j) for j in range(epg)]
    b1, i1, w1 = sv[0], jnp.zeros_like(gi), cv[0]
    for j in range(1, epg):
        better = sv[j] > b1
        i1 = jnp.where(better, j, i1)
        w1 = jnp.where(better, cv[j], w1)
        b1 = jnp.where(better, sv[j], b1)
    neg = jnp.full_like(b1, -jnp.inf)
    b2, i2, w2 = neg, jnp.zeros_like(gi), jnp.zeros_like(w1)
    for j in range(epg):
        better = (i1 != j) & (sv[j] > b2)
        i2 = jnp.where(better, j, i2)
        w2 = jnp.where(better, cv[j], w2)
        b2 = jnp.where(better, sv[j], b2)
    tot = w1 + w2
    e1 = gi * epg + i1
    e2 = gi * epg + i2
    eid = lax.broadcasted_iota(jnp.int32, logits.shape, 0)
    m1h = eid == e1
    m2h = eid == e2
    mc = jnp.where(m1h | m2h, 1.0, 0.0)
    pref = _dot(mc.astype(BF16), tri_ref[...])
    cnt = jnp.sum(mc, axis=1, keepdims=True)
    pc_al = jnp.ceil(cnt * (1.0 / SEG_ALIGN))
    pcb = jnp.broadcast_to(pc_al, (n_exp, LANES))
    er = lax.broadcasted_iota(jnp.int32, (n_exp, n_exp), 0)
    ec = lax.broadcasted_iota(jnp.int32, (n_exp, n_exp), 1)
    lower = jnp.where(ec < er, 1.0, 0.0).astype(BF16)
    seg = _dot(lower, pcb.astype(BF16))[:, 0:1] * SEG_ALIGN
    slot = seg + pref
    pos1 = jnp.sum(jnp.where(m1h, slot, 0.0), axis=0, keepdims=True)
    pos2 = jnp.sum(jnp.where(m2h, slot, 0.0), axis=0, keepdims=True)
    zero = jnp.zeros_like(pos1)
    ri_ref[...] = jnp.concatenate([pos1, pos2, w1 / tot, w2 / tot, zero, zero, zero, zero], axis=0)
    pc_ref[...] = pcb * SEG_ALIGN


def _route_io(lay, d, nw, wr_t, br, tri):
    tm = lay.tm
    n_exp = br.shape[0]
    full = lambda i: (0, 0)
    args = [nw, wr_t, br, tri]
    in_specs = [pl.BlockSpec((1, d), full), pl.BlockSpec(wr_t.shape, full), pl.BlockSpec((n_exp, 1), full),
                pl.BlockSpec((tm, tm), full)]
    out_shape = [jax.ShapeDtypeStruct((lay.t, d), BF16), jax.ShapeDtypeStruct((8, lay.t), F32),
                 jax.ShapeDtypeStruct((lay.n_tiles, n_exp, LANES), F32)]
    out_specs = [pl.BlockSpec((tm, d), lambda i: (i, 0)), pl.BlockSpec((8, tm), lambda i: (0, i)),
                 pl.BlockSpec((None, n_exp, LANES), lambda i: (i, 0, 0))]
    return args, in_specs, out_shape, out_specs


def _proj_route_kernel(*refs, lay, n_y):
    y_refs = refs[:n_y]
    o_refs = refs[n_y + 1:n_y + 3]
    mod_ref = refs[n_y]
    w_ref, nw_ref, wr_ref, br_ref, tri_ref, out_ref, h_ref, ri_ref, pc_ref = refs[n_y + 3:]
    m = mod_ref[...]
    y = lay.stream_load(y_refs) + m[2:3] * _dot(lay.stream_load(o_refs), w_ref[...])
    out_ref[...] = y
    _route_core(y, m, nw_ref, wr_ref, br_ref, tri_ref, h_ref, ri_ref, pc_ref)


def proj_route(lay, y, mod, o_pair, w, route):
    ys = _as_list(y)
    d = ys[0].shape[1]
    tm = lay.tm
    row = lambda i: (i, 0)
    r_args, r_in, r_shape, r_out = _route_io(lay, d, *route)
    return pl.pallas_call(
        functools.partial(_proj_route_kernel, lay=lay, n_y=len(ys)),
        out_shape=[jax.ShapeDtypeStruct((lay.t, d), F32)] + r_shape,
        grid=(lay.n_tiles,),
        in_specs=lay.stream_specs(d, len(ys) == 2) + [pl.BlockSpec((None, 8, d), lambda i: (lay.seg(i), 0, 0))]
        + lay.stream_specs(o_pair[0].shape[1], True) + [pl.BlockSpec(w.shape, lambda i: (0, 0))] + r_in,
        out_specs=[pl.BlockSpec((tm, d), row)] + r_out,
        input_output_aliases={0: 0} if len(ys) == 1 else {},
        compiler_params=_cparams(("parallel",)),
        name="proj_route",
    )(*ys, mod, *o_pair, w, *r_args)


def _segment_pieces(i, off_ref, pc_ref, n_exp, sizes, fn):
    local = 0
    for e in range(n_exp):
        n = pc_ref[i * n_exp + e]
        g0 = off_ref[i * n_exp + e]

        def emit(size, e=e, n=n, g0=g0, local=local):
            above = n & (-2 * size)

            @pl.when((n & size) != 0)
            def _():
                fn(e, pl.multiple_of(local + above, SEG_ALIGN), pl.multiple_of(g0 + above, SEG_ALIGN), size)

        for size in sizes:
            emit(size)
        local = local + n


def _pieces(n, sizes):
    for size in sizes:
        yield (n & size) != 0, n & (-2 * size), size


def _moe_permute_kernel(off_ref, pc_ref, goff_ref, gn_ref, h_ref, ri_ref, xs_ref, ys_ref, zp_ref, sem, gsem,
                        *, n_exp, sizes, gap_sizes, rchunk):
    i = pl.program_id(0)
    last = pl.num_programs(0) - 1
    slot = i % 2
    ri = ri_ref[...]
    pos1 = ri[0:1]
    pos2 = ri[1:2]
    h = h_ref[...]
    n_loc = ys_ref.shape[1]
    for r0 in range(0, n_loc, rchunk):
        r = (lax.broadcasted_iota(jnp.int32, (rchunk, h.shape[0]), 0) + r0).astype(F32)
        p = jnp.where(pos1 == r, 1.0, 0.0) + jnp.where(pos2 == r, 1.0, 0.0)
        ys_ref[slot, r0:r0 + rchunk, :] = _dot(p.astype(BF16), h).astype(BF16)

    def copies(tile, sl, act):
        def piece(e, lrow, grow, size):
            act(pltpu.make_async_copy(ys_ref.at[sl, pl.ds(lrow, size)], xs_ref.at[pl.ds(grow, size)], sem.at[sl, e]))
        _segment_pieces(tile, off_ref, pc_ref, n_exp, sizes, piece)

    start = lambda cp: cp.start()
    wait = lambda cp: cp.wait()
    copies(i, slot, start)

    @pl.when(i > 0)
    def _():
        copies(i - 1, 1 - slot, wait)

    @pl.when(i == last)
    def _():
        copies(i, slot, wait)
        zp_ref[...] = jnp.zeros_like(zp_ref)

        def gaps():
            for e in range(n_exp):
                g0 = goff_ref[e]
                for present, above, size in _pieces(gn_ref[e], gap_sizes):
                    yield present, pltpu.make_async_copy(
                        zp_ref.at[pl.ds(0, size)], xs_ref.at[pl.ds(pl.multiple_of(g0 + above, SEG_ALIGN), size)],
                        gsem.at[e])

        for present, cp in gaps():
            pl.when(present)(cp.start)
        for present, cp in gaps():
            pl.when(present)(cp.wait)

        zrows = zp_ref.shape[0]
        tail0 = goff_ref[n_exp]
        tail = lambda c: pltpu.make_async_copy(
            zp_ref, xs_ref.at[pl.ds(pl.multiple_of(tail0 + c * zrows, zrows), zrows)], gsem.at[0])

        @pl.loop(0, gn_ref[n_exp])
        def _(c):
            tail(c).start()

        @pl.loop(0, gn_ref[n_exp])
        def _(c):
            tail(c).wait()


def moe_permute(lay, h, rinfo, off, pcs, gap_off, gap_n, n_sorted, n_loc, n_exp, tmm):
    t, d = h.shape
    tm = lay.tm
    sizes = [s for s in (1 << b for b in range(12, 3, -1)) if s <= tm]
    gap_sizes = [s for s in (1 << b for b in range(12, 3, -1)) if s < tmm]
    return pl.pallas_call(
        functools.partial(_moe_permute_kernel, n_exp=n_exp, sizes=sizes, gap_sizes=gap_sizes, rchunk=256),
        out_shape=jax.ShapeDtypeStruct((n_sorted, d), BF16),
        grid_spec=pltpu.PrefetchScalarGridSpec(
            num_scalar_prefetch=4, grid=(lay.n_tiles,),
            in_specs=[pl.BlockSpec((tm, d), lambda i, *_: (i, 0)), pl.BlockSpec((8, tm), lambda i, *_: (0, i))],
            out_specs=pl.BlockSpec(memory_space=pl.ANY),
            scratch_shapes=[pltpu.VMEM((2, n_loc, d), BF16), pltpu.VMEM((gap_sizes[0], d), BF16),
                            pltpu.SemaphoreType.DMA((2, n_exp)), pltpu.SemaphoreType.DMA((n_exp,))]),
        compiler_params=_cparams(("arbitrary",)),
        name="moe_permute",
    )(off, pcs, gap_off, gap_n, h, rinfo)


def _moe_expert_kernel(te_ref, tv_ref, tf_ref, xb_ref, nx_ref, x_ref, wg_hbm, wu_hbm, wd_hbm, z_ref,
                       wgs_ref, wus_ref, wds_ref, wgb_ref, wub_ref, wdb_ref, sem, *, layer):
    del xb_ref
    r = pl.program_id(0)

    def fetch(e):
        return (pltpu.make_async_copy(wg_hbm.at[layer, e], wgs_ref, sem.at[0]),
                pltpu.make_async_copy(wu_hbm.at[layer, e], wus_ref, sem.at[1]),
                pltpu.make_async_copy(wd_hbm.at[layer, e], wds_ref, sem.at[2]))

    @pl.when(r == 0)
    def _():
        for cp in fetch(te_ref[0]):
            cp.start()

    @pl.when(tf_ref[r] == 1)
    def _():
        for cp in fetch(te_ref[r]):
            cp.wait()
        wgb_ref[...] = wgs_ref[...].astype(BF16)
        wub_ref[...] = wus_ref[...].astype(BF16)
        wdb_ref[...] = wds_ref[...].astype(BF16)

        @pl.when(nx_ref[r] >= 0)
        def _():
            for cp in fetch(nx_ref[r]):
                cp.start()

    @pl.when(tv_ref[r] > 0)
    def _():
        x = x_ref[...]
        a = _dot(x, wgb_ref[...])
        b = _dot(x, wub_ref[...])
        hid = (a * jax.nn.sigmoid(a)) * b
        z_ref[...] = _dot(hid.astype(BF16), wdb_ref[...]).astype(BF16)

    @pl.when(tv_ref[r] == 0)
    def _():
        z_ref[...] = jnp.zeros_like(z_ref)


def moe_experts(xs, tile_expert, tile_valid, tile_first, x_block, next_expert, layer, wg, wu, wd, tmm):
    n_sorted, d = xs.shape
    f = wg.shape[-1]
    hbm = pl.BlockSpec(memory_space=pl.ANY)
    return pl.pallas_call(
        functools.partial(_moe_expert_kernel, layer=layer),
        out_shape=jax.ShapeDtypeStruct((n_sorted, d), BF16),
        grid_spec=pltpu.PrefetchScalarGridSpec(
            num_scalar_prefetch=5, grid=(n_sorted // tmm,),
            in_specs=[pl.BlockSpec((tmm, d), lambda r, te, tv, tf, xb, nx: (xb[r], 0)), hbm, hbm, hbm],
            out_specs=pl.BlockSpec((tmm, d), lambda r, te, tv, tf, xb, nx: (r, 0)),
            scratch_shapes=[pltpu.VMEM((d, f), F32), pltpu.VMEM((d, f), F32), pltpu.VMEM((f, d), F32),
                            pltpu.VMEM((d, f), BF16), pltpu.VMEM((d, f), BF16), pltpu.VMEM((f, d), BF16),
                            pltpu.SemaphoreType.DMA((3,))]),
        compiler_params=_cparams(("arbitrary",)),
        name="moe_experts",
    )(tile_expert, tile_valid, tile_first, x_block, next_expert, xs, wg, wu, wd)


def _moe_combine_kernel(off_ref, pc_ref, y_ref, mod_ref, rit_ref, zs_ref, *rest, lay, n_exp, sizes):
    out_refs, (zt_ref, sem) = rest[:-2], rest[-2:]
    i = pl.program_id(0)
    slot = i % 2

    def copies(tile, sl, act):
        def piece(e, lrow, grow, size):
            act(pltpu.make_async_copy(zs_ref.at[pl.ds(grow, size)], zt_ref.at[sl, pl.ds(lrow, size)], sem.at[sl, e]))
        _segment_pieces(tile, off_ref, pc_ref, n_exp, sizes, piece)

    def fetch(tile, sl):
        used = pc_ref[tile * n_exp]
        for e in range(1, n_exp):
            used = used + pc_ref[tile * n_exp + e]

        @pl.loop(used // SEG_ALIGN, zt_ref.shape[1] // SEG_ALIGN)
        def _(c):
            zt_ref[sl, pl.ds(pl.multiple_of(c * SEG_ALIGN, SEG_ALIGN), SEG_ALIGN), :] = jnp.zeros(
                (SEG_ALIGN, zt_ref.shape[2]), BF16)
        copies(tile, sl, lambda cp: cp.start())

    @pl.when(i == 0)
    def _():
        fetch(i, slot)

    @pl.when(i + 1 < pl.num_programs(0))
    def _():
        fetch(i + 1, 1 - slot)

    rit = rit_ref[...]
    n_loc = zt_ref.shape[1]
    r = lax.broadcasted_iota(jnp.int32, (rit.shape[0], n_loc), 1).astype(F32)
    pw = jnp.where(rit[:, 0:1] == r, rit[:, 2:3], 0.0) + jnp.where(rit[:, 1:2] == r, rit[:, 3:4], 0.0)
    pw = pw.astype(BF16)
    copies(i, slot, lambda cp: cp.wait())
    lay.stream_store(out_refs, y_ref[...] + mod_ref[...][5:6] * _dot(pw, zt_ref[slot]))


def moe_combine(lay, y, mod, rinfo_t, zs, off, pcs, n_loc, n_exp, pair_out):
    t, d = y.shape
    tm = lay.tm
    sizes = [s for s in (1 << b for b in range(12, 3, -1)) if s <= tm]
    if pair_out:
        out_shape = (jax.ShapeDtypeStruct((lay.t_ctx, d), F32), jax.ShapeDtypeStruct((lay.t_lat, d), F32))
    else:
        out_shape = (jax.ShapeDtypeStruct((t, d), F32),)
    out = pl.pallas_call(
        functools.partial(_moe_combine_kernel, lay=lay, n_exp=n_exp, sizes=sizes),
        out_shape=out_shape,
        grid_spec=pltpu.PrefetchScalarGridSpec(
            num_scalar_prefetch=2, grid=(lay.n_tiles,),
            in_specs=[pl.BlockSpec((tm, d), lambda i, o, p: (i, 0)),
                      pl.BlockSpec((None, 8, d), lambda i, o, p: (lay.seg(i), 0, 0)),
                      pl.BlockSpec((tm, 8), lambda i, o, p: (i, 0)),
                      pl.BlockSpec(memory_space=pl.ANY)],
            out_specs=tuple(lay.stream_specs(d, pair_out)),
            scratch_shapes=[pltpu.VMEM((2, n_loc, d), BF16), pltpu.SemaphoreType.DMA((2, n_exp))]),
        input_output_aliases={} if pair_out else {2: 0},
        compiler_params=_cparams(("arbitrary",)),
        name="moe_combine",
    )(off, pcs, y, mod, rinfo_t, zs)
    return out if pair_out else out[0]


def _round_up(x, m):
    return (x + m - 1) // m * m


def _moe_plan_kernel(pc_ref, off_ref, te_ref, tv_ref, tf_ref, xb_ref, nx_ref, goff_ref, gn_ref,
                     *, n_tiles, n_exp, tmm, n_sorted):
    shift = tmm.bit_length() - 1
    n_rt = n_sorted // tmm
    zrows = tmm // 2
    base = jnp.int32(0)
    tots, bases, regions = [], [], []
    for e in range(n_exp):
        def seg_body(i, run, e=e, base=base):
            off_ref[i * n_exp + e] = base + run
            return run + pc_ref[i * n_exp + e]
        tot = lax.fori_loop(0, n_tiles, seg_body, jnp.int32(0))
        region = ((tot + (tmm - 1)) >> shift) << shift
        goff_ref[e] = base + tot
        gn_ref[e] = region - tot
        tots.append(tot)
        bases.append(base)
        regions.append(region)
        base = base + region
    goff_ref[n_exp] = base
    gn_ref[n_exp] = (n_sorted - base) // zrows
    last_active = jnp.maximum((base >> shift) - 1, 0)
    following = [None] * n_exp
    cur = jnp.int32(-1)
    for e in reversed(range(n_exp)):
        following[e] = cur
        cur = jnp.where(regions[e] > 0, jnp.int32(e), cur)

    def idle_body(r, _):
        te_ref[r] = n_exp - 1
        tv_ref[r] = 0
        tf_ref[r] = 0
        nx_ref[r] = -1
        xb_ref[r] = jnp.minimum(r, last_active)
        return 0
    lax.fori_loop(0, n_rt, idle_body, 0)
    for e in range(n_exp):
        t0 = bases[e] >> shift

        def tile_body(k, _, e=e, t0=t0):
            r = t0 + k
            te_ref[r] = e
            tv_ref[r] = jnp.clip(tots[e] - k * tmm, 0, tmm)
            tf_ref[r] = jnp.where(k == 0, 1, 0)
            nx_ref[r] = jnp.where(k == 0, following[e], -1)
            return 0
        lax.fori_loop(0, regions[e] >> shift, tile_body, 0)


def moe_plan(pcs, n_tiles, n_exp, tmm, n_sorted):
    assert tmm & (tmm - 1) == 0
    n_rt = n_sorted // tmm
    i32 = lambda n: jax.ShapeDtypeStruct((n,), jnp.int32)
    smem = pl.BlockSpec(memory_space=pltpu.SMEM)
    return pl.pallas_call(
        functools.partial(_moe_plan_kernel, n_tiles=n_tiles, n_exp=n_exp, tmm=tmm, n_sorted=n_sorted),
        out_shape=[i32(n_tiles * n_exp)] + [i32(n_rt)] * 5 + [i32(n_exp + 1)] * 2,
        in_specs=[smem],
        out_specs=[smem] * 8,
        name="moe_plan",
    )(pcs)


def moe_layer(lay, routed, mod, layer, wg, wu, wd, tmm, pair_out=False):
    y, h, rinfo, pc = routed
    t, d = y.shape
    n_exp = pc.shape[1]
    pcs = pc[:, :, 0].astype(jnp.int32).reshape(-1)
    n_loc = 2 * lay.tm + SEG_ALIGN * n_exp
    n_sorted = _round_up(2 * t + lay.n_tiles * n_exp * (SEG_ALIGN - 1) + n_exp * (tmm - 1), tmm)
    off, te, tv, tf, xb, nx, gap_off, gap_n = moe_plan(pcs, lay.n_tiles, n_exp, tmm, n_sorted)
    xs = moe_permute(lay, h, rinfo, off, pcs, gap_off, gap_n, n_sorted, n_loc, n_exp, tmm)
    zs = moe_experts(xs, te, tv, tf, xb, nx, layer, wg, wu, wd, tmm)
    return moe_combine(lay, y, mod, rinfo.T, zs, off, pcs, n_loc, n_exp, pair_out)


def _rope_perm():
    d = np.arange(QK_ROPE)
    return QK_NOPE + (d ^ (QK_ROPE // 4))


def _pad_cols(x, n):
    return jnp.pad(x, ((0, 0),) * (x.ndim - 1) + ((0, n - x.shape[-1]),))


def _mla_weights(w_q_a, w_q_b, w_kv_a, w_kv_b, q_norm, k_norm, kv_lora):
    perm = _rope_perm()
    nh = N_HEADS
    wkr = w_kv_a[:, kv_lora:]
    d = w_kv_a.shape[0]
    zl = jnp.zeros((d, QK_NOPE), F32)
    kr_blk = _pad_cols(jnp.concatenate([zl, wkr], axis=1), LANES)
    krp_blk = _pad_cols(jnp.concatenate([zl, wkr[:, perm - QK_NOPE]], axis=1), LANES)
    w_a = jnp.concatenate([w_q_a, w_kv_a[:, :kv_lora], kr_blk, krp_blk], axis=1).astype(BF16)
    wq = w_q_b.reshape(-1, nh, QK_DIM).transpose(1, 0, 2)
    wq_main = _pad_cols(wq, HEAD_PAD)
    wq_part = _pad_cols(jnp.concatenate([jnp.zeros_like(wq[..., :QK_NOPE]), wq[..., perm]], axis=-1), HEAD_PAD)
    wq_h = jnp.concatenate([wq_main, wq_part], axis=-1).astype(BF16)
    wkv = w_kv_b.reshape(-1, nh, QK_NOPE + V_DIM).transpose(1, 0, 2)
    wk_h = _pad_cols(wkv[..., :QK_NOPE], HEAD_PAD).astype(BF16)
    wv = wkv[..., QK_NOPE:]
    wv_h = jnp.concatenate([wv[0::2], wv[1::2]], axis=-1).astype(BF16)
    nrm = jnp.stack([_pad_cols(q_norm, LANES), _pad_cols(jnp.concatenate([jnp.zeros((QK_NOPE,), F32), q_norm[perm]]), LANES),
                     _pad_cols(k_norm, LANES), _pad_cols(jnp.concatenate([jnp.zeros((QK_NOPE,), F32), k_norm[perm]]), LANES)])
    nrm = jnp.pad(nrm, ((0, 4), (0, 0)))
    return w_a, wq_h, wk_h, wv_h, nrm


def _rope_tables(lat_len, tm):
    rows = lat_len // GRID_W
    t_row = np.repeat(np.arange(rows, dtype=np.float32), GRID_W)
    t_col = np.tile(np.arange(GRID_W, dtype=np.float32), rows)
    half = QK_ROPE // 2
    inv = (ROPE_BASE ** (-np.arange(0, half, 2, dtype=np.float32) / half)).astype(np.float32)
    ang = jnp.concatenate([jnp.asarray(t_row)[:, None] * inv, jnp.asarray(t_col)[:, None] * inv], axis=-1)
    cos, sin = jnp.cos(ang), jnp.sin(ang)
    nf = QK_ROPE // 4
    cos_l = jnp.concatenate([cos[:, :nf], cos[:, :nf], cos[:, nf:], cos[:, nf:]], axis=-1)
    sin_l = jnp.concatenate([-sin[:, :nf], sin[:, :nf], -sin[:, nf:], sin[:, nf:]], axis=-1)
    one = jnp.ones((lat_len, QK_NOPE), F32)
    cos_t = _pad_cols(jnp.concatenate([one, cos_l], axis=-1), LANES)
    cos_t = cos_t.at[:, QK_DIM:].set(1.0)
    sin_t = _pad_cols(jnp.concatenate([jnp.zeros((lat_len, QK_NOPE), F32), sin_l], axis=-1), LANES)
    cos_t = jnp.concatenate([jnp.ones((tm, LANES), F32), cos_t], axis=0)
    sin_t = jnp.concatenate([jnp.zeros((tm, LANES), F32), sin_t], axis=0)
    return cos_t, sin_t


def mla_layer(lay, y, mod, nw, route, cache_ckv, cache_krope, cos_t, sin_t, w_q_a, q_a_norm, w_q_b, w_kv_a,
              kv_a_norm, w_kv_b, q_norm, k_norm, w_o):
    q_lora = w_q_a.shape[1]
    kv_lora = kv_a_norm.shape[0]
    w_a, wq_h, wk_h, wv_h, nrm = _mla_weights(w_q_a, w_q_b, w_kv_a, w_kv_b, q_norm, k_norm, kv_lora)
    ckv, kr, q, k, v2 = mla_qkv(lay, y, mod, nw, w_a, q_a_norm.reshape(1, -1), kv_a_norm.reshape(1, -1),
                                wq_h, wk_h, wv_h, nrm, cos_t, sin_t, q_lora, kv_lora)
    past = cache_ckv.shape[1]
    ckr = jnp.pad(cache_krope.reshape(-1, QK_ROPE), ((0, 0), (QK_NOPE, LANES - QK_DIM)))
    kc, vc2 = ctx_kv(cache_ckv.reshape(-1, kv_lora), ckr, wk_h, wv_h, nrm, past)
    o_ctx = attention(q, k, v2, 0, lay.n_ctx_b, lay.ctx_len, min(lay.ctx_len, 256), pairs=N_HEADS // 2)
    o_lat = attention(q, k, v2, lay.t_ctx, lay.n_lat_b, lay.lat_len, min(lay.lat_len, 512), kc, vc2, past)
    routed = proj_route(lay, y, mod, (o_ctx, o_lat), w_o.astype(BF16), route)
    new_ckv = ckv[:lay.t_ctx].reshape(lay.n_ctx_b, lay.ctx_len, kv_lora)
    new_kr = kr[:lay.t_ctx, QK_NOPE:QK_DIM].reshape(lay.n_ctx_b, lay.ctx_len, QK_ROPE)
    return routed, new_ckv, new_kr


def kernel(x_prompt, x_sample, c, cache_ckv, cache_krope, state_ssm_re, state_ssm_im, c_ctx, w_mod, b_mod, norm1_w, norm2_w, mla_w_q_a, mla_q_a_norm, mla_w_q_b, mla_w_kv_a, mla_kv_a_norm, mla_w_kv_b, mla_q_norm, mla_k_norm, mla_w_o, ssm_w_in, ssm_a_re, ssm_a_im, ssm_log_dt, ssm_b_re, ssm_b_im, ssm_c_re, ssm_c_im, ssm_d, ssm_w_glu, w_router, b_router, moe_w_gate, moe_w_up, moe_w_down):
    n_ctx_b, ctx_len, d = x_prompt.shape
    n_lat_b, lat_len, _ = x_sample.shape
    depth = w_mod.shape[0]
    lay = Layout(n_ctx_b, ctx_len, n_lat_b, lat_len, tm=512)
    assert n_lat_b + 1 <= 8
    y = (x_prompt.reshape(-1, d), x_sample.reshape(-1, d))
    cond8 = jnp.pad(jnp.concatenate([c_ctx[None, :], c], axis=0), ((0, 7 - n_lat_b), (0, 0)))
    mods = adaln_all(cond8, w_mod, b_mod)
    mods = jnp.pad(mods.reshape(depth, 8, 6, d), ((0, 0), (0, 0), (0, 2), (0, 0)))
    cos_t, sin_t = _rope_tables(lat_len, lay.tm)
    n_exp = b_router.shape[0]
    wr_hi = w_router.astype(BF16)
    wr_lo = (w_router - wr_hi.astype(F32)).astype(BF16)
    wr_t = jnp.concatenate([wr_hi.T, wr_lo.T], axis=0)
    br = b_router.reshape(n_exp, 1)
    tri = jnp.asarray(np.triu(np.ones((lay.tm, lay.tm), np.float32), k=1), BF16)
    ckv_out, kr_out, sre_out, sim_out = [], [], [], []
    for i in range(depth):
        j = i // 2
        mod = mods[i]
        nw1 = norm1_w[i].reshape(1, d)
        route = (norm2_w[i].reshape(1, d), wr_t, br, tri)
        if i % 2 == 0:
            routed, ckv_p, kr_p = mla_layer(lay, y, mod, nw1, route, cache_ckv[:, j], cache_krope[:, j], cos_t, sin_t,
                                            mla_w_q_a[j], mla_q_a_norm[j], mla_w_q_b[j], mla_w_kv_a[j],
                                            mla_kv_a_norm[j], mla_w_kv_b[j], mla_q_norm[j], mla_k_norm[j], mla_w_o[j])
            ckv_out.append(ckv_p)
            kr_out.append(kr_p)
        else:
            routed, s_re, s_im = s5_layer(lay, y, mod, nw1, route, state_ssm_re[:, j], state_ssm_im[:, j], ssm_w_in[j],
                                          ssm_a_re[j], ssm_a_im[j], ssm_log_dt[j], ssm_b_re[j], ssm_b_im[j],
                                          ssm_c_re[j], ssm_c_im[j], ssm_d[j], ssm_w_glu[j])
            sre_out.append(s_re)
            sim_out.append(s_im)
        y = moe_layer(lay, routed, mod, i, moe_w_gate, moe_w_up, moe_w_down, tmm=512, pair_out=(i == depth - 1))
    yp = y[0].reshape(n_ctx_b, ctx_len, d)
    ys = y[1].reshape(n_lat_b, lat_len, d)
    return (yp, ys, jnp.stack(ckv_out, axis=1), jnp.stack(kr_out, axis=1),
            jnp.stack(sre_out, axis=1), jnp.stack(sim_out, axis=1))
```

```python
import functools
import math

import jax
import jax.numpy as jnp
import numpy as np
from jax import lax
from jax.experimental import pallas as pl
from jax.experimental.pallas import tpu as pltpu

F32 = jnp.float32
BF16 = jnp.bfloat16
EPS = 1e-6

GRID_W = 64
N_HEADS = 8
QK_NOPE = 64
QK_ROPE = 32
QK_DIM = QK_NOPE + QK_ROPE
V_DIM = 64
ROPE_BASE = 10000.0
SSM_GROUP = 16
SSM_STATE = 64
N_EXPERT_GROUPS = 4
EXPERTS_PER_GROUP = 4

LANES = 128
HEAD_PAD = LANES
SSM_CHUNK = 16
SEG_ALIGN = 16
VMEM_LIMIT = 48 * 1024 * 1024


def _cparams(sem, vmem=VMEM_LIMIT):
    return pltpu.CompilerParams(dimension_semantics=sem, vmem_limit_bytes=vmem)


def _dot(a, b):
    return jnp.dot(a, b, preferred_element_type=F32)


def _dot_nt(a, b):
    return lax.dot_general(a, b, (((1,), (1,)), ((), ())), preferred_element_type=F32)


def _dot_hi(a, b):
    return jnp.dot(a, b, preferred_element_type=F32, precision=lax.Precision.HIGHEST)


def _norm_mod(x, nw, shift, scale):
    ms = jnp.mean(x * x, axis=-1, keepdims=True)
    return (x * lax.rsqrt(ms + EPS) * nw) * (1.0 + scale) + shift


class Layout:
    def __init__(self, n_ctx_b, ctx_len, n_lat_b, lat_len, tm):
        self.n_ctx_b, self.ctx_len, self.n_lat_b, self.lat_len = n_ctx_b, ctx_len, n_lat_b, lat_len
        self.t_ctx = n_ctx_b * ctx_len
        self.t_lat = n_lat_b * lat_len
        self.t = self.t_ctx + self.t_lat
        self.tm = tm
        assert self.t_ctx % tm == 0 and lat_len % tm == 0
        self.ctx_tiles = self.t_ctx // tm
        self.lat_tiles_per_b = lat_len // tm
        self.n_tiles = self.t // tm

    def seg(self, i):
        return jnp.where(i < self.ctx_tiles, 0, 1 + (i - self.ctx_tiles) // self.lat_tiles_per_b)

    def rope_blk(self, i):
        return jnp.where(i < self.ctx_tiles, 0, 1 + (i - self.ctx_tiles) % self.lat_tiles_per_b)

    def stream_specs(self, d, pair):
        tm = self.tm
        if not pair:
            return [pl.BlockSpec((tm, d), lambda i, *_: (i, 0))]
        return [pl.BlockSpec((tm, d), lambda i, *_: (jnp.minimum(i, self.ctx_tiles - 1), 0)),
                pl.BlockSpec((tm, d), lambda i, *_: (jnp.maximum(i - self.ctx_tiles, 0), 0))]

    def stream_load(self, refs):
        if len(refs) == 1:
            return refs[0][...]
        return jnp.where(pl.program_id(0) < self.ctx_tiles, refs[0][...], refs[1][...])

    def stream_store(self, refs, val):
        if len(refs) == 1:
            refs[0][...] = val
            return
        i = pl.program_id(0)

        @pl.when(i < self.ctx_tiles)
        def _():
            refs[0][...] = val

        @pl.when(i >= self.ctx_tiles)
        def _():
            refs[1][...] = val


def _as_list(y):
    return list(y) if isinstance(y, (tuple, list)) else [y]


def _adaln_kernel(c_ref, w_ref, b_ref, o_ref):
    c = c_ref[...]
    s = c * jax.nn.sigmoid(c)
    o_ref[...] = _dot_hi(s, w_ref[...]) + b_ref[...]


def adaln_all(cond8, w_mod, b_mod, tn=1536):
    depth, d, n6 = w_mod.shape
    return pl.pallas_call(
        _adaln_kernel,
        out_shape=jax.ShapeDtypeStruct((depth, 8, n6), F32),
        grid=(depth, n6 // tn),
        in_specs=[pl.BlockSpec((8, d), lambda l, j: (0, 0)),
                  pl.BlockSpec((None, d, tn), lambda l, j: (l, 0, j)),
                  pl.BlockSpec((None, 1, tn), lambda l, j: (l, 0, j))],
        out_specs=pl.BlockSpec((None, 8, tn), lambda l, j: (l, 0, j)),
        compiler_params=_cparams(("parallel", "parallel")),
        name="adaln",
    )(cond8, w_mod, b_mod.reshape(depth, 1, n6))


def _kv_heads(ckv_b, kr, krp, wk_ref, wv_ref, knw, knwp, cos, sin, k_ref, v_ref):
    kc = knw * cos
    ks = knwp * sin
    for h in range(N_HEADS):
        kz = _dot(ckv_b, wk_ref[h]) + kr
        r = lax.rsqrt(jnp.sum(kz * kz, axis=-1, keepdims=True) * (1.0 / QK_DIM) + EPS)
        k_ref[h] = (r * (kz * kc + krp * ks)).astype(BF16)
    for hp in range(N_HEADS // 2):
        v_ref[hp] = _dot(ckv_b, wv_ref[hp]).astype(BF16)


def _mla_qkv_kernel(*refs, lay, n_y, q_lora, kv_lora):
    y_refs = refs[:n_y]
    (mod_ref, nw_ref, w_ref, qan_ref, kvan_ref, wq_ref, wk_ref, wv_ref, nrm_ref, cos_ref, sin_ref,
     ckv_ref, kr_ref, q_ref, k_ref, v_ref) = refs[n_y:]
    m = mod_ref[...]
    h = _norm_mod(lay.stream_load(y_refs), nw_ref[...], m[0:1], m[1:2])
    z = _dot(h.astype(BF16), w_ref[...])
    cq = z[:, :q_lora]
    cq = (cq * lax.rsqrt(jnp.mean(cq * cq, axis=-1, keepdims=True) + EPS) * qan_ref[...]).astype(BF16)
    ckv = z[:, q_lora:q_lora + kv_lora]
    ckv = ckv * lax.rsqrt(jnp.mean(ckv * ckv, axis=-1, keepdims=True) + EPS) * kvan_ref[...]
    ckv_ref[...] = ckv
    kr = z[:, q_lora + kv_lora:q_lora + kv_lora + LANES]
    kr_ref[...] = kr
    krp = z[:, q_lora + kv_lora + LANES:]
    nrm = nrm_ref[...]
    cos = cos_ref[...]
    sin = sin_ref[...]
    qc = nrm[0:1] * cos * (QK_DIM ** -0.5)
    qs = nrm[1:2] * sin * (QK_DIM ** -0.5)
    for hd in range(N_HEADS):
        zq = _dot(cq, wq_ref[hd])
        qm = zq[:, :HEAD_PAD]
        qp = zq[:, HEAD_PAD:]
        r = lax.rsqrt(jnp.sum(qm * qm, axis=-1, keepdims=True) * (1.0 / QK_DIM) + EPS)
        q_ref[hd] = (r * (qm * qc + qp * qs)).astype(BF16)
    _kv_heads(ckv.astype(BF16), kr, krp, wk_ref, wv_ref, nrm[2:3], nrm[3:4], cos, sin, k_ref, v_ref)


def mla_qkv(lay, y, mod, nw, w_a, qan, kvan, wq, wk, wv, nrm, cos_t, sin_t, q_lora, kv_lora):
    ys = _as_list(y)
    t, d = lay.t, ys[0].shape[1]
    tm = lay.tm
    row = lambda i: (i, 0)
    full = lambda i: (0, 0)
    hrow = lambda i: (0, i, 0)
    full3 = lambda i: (0, 0, 0)
    rope = lambda i: (lay.rope_blk(i), 0)
    nh = N_HEADS
    return pl.pallas_call(
        functools.partial(_mla_qkv_kernel, lay=lay, n_y=len(ys), q_lora=q_lora, kv_lora=kv_lora),
        out_shape=(jax.ShapeDtypeStruct((t, kv_lora), F32), jax.ShapeDtypeStruct((t, LANES), F32),
                   jax.ShapeDtypeStruct((nh, t, HEAD_PAD), BF16), jax.ShapeDtypeStruct((nh, t, HEAD_PAD), BF16),
                   jax.ShapeDtypeStruct((nh // 2, t, LANES), BF16)),
        grid=(lay.n_tiles,),
        in_specs=lay.stream_specs(d, len(ys) == 2) + [
            pl.BlockSpec((None, 8, d), lambda i: (lay.seg(i), 0, 0)),
            pl.BlockSpec((1, d), full), pl.BlockSpec(w_a.shape, full),
            pl.BlockSpec((1, q_lora), full), pl.BlockSpec((1, kv_lora), full),
            pl.BlockSpec(wq.shape, full3), pl.BlockSpec(wk.shape, full3), pl.BlockSpec(wv.shape, full3),
            pl.BlockSpec((8, LANES), full),
            pl.BlockSpec((tm, LANES), rope), pl.BlockSpec((tm, LANES), rope)],
        out_specs=(pl.BlockSpec((tm, kv_lora), row), pl.BlockSpec((tm, LANES), row),
                   pl.BlockSpec((nh, tm, HEAD_PAD), hrow), pl.BlockSpec((nh, tm, HEAD_PAD), hrow),
                   pl.BlockSpec((nh // 2, tm, LANES), hrow)),
        compiler_params=_cparams(("parallel",)),
        name="mla_qkv",
    )(*ys, mod, nw, w_a, qan, kvan, wq, wk, wv, nrm, cos_t, sin_t)


def _ctx_kv_kernel(ckv_ref, kr_ref, wk_ref, wv_ref, nrm_ref, k_ref, v_ref):
    nrm = nrm_ref[...]
    kr = kr_ref[...]
    one = jnp.ones((1, LANES), F32)
    _kv_heads(ckv_ref[...].astype(BF16), kr, kr, wk_ref, wv_ref, nrm[2:3], nrm[3:4],
              one, jnp.zeros((1, LANES), F32), k_ref, v_ref)


def ctx_kv(ckv, kr, wk, wv, nrm, tm):
    t = ckv.shape[0]
    row = lambda i: (i, 0)
    hrow = lambda i: (0, i, 0)
    full3 = lambda i: (0, 0, 0)
    nh = N_HEADS
    return pl.pallas_call(
        _ctx_kv_kernel,
        out_shape=(jax.ShapeDtypeStruct((nh, t, HEAD_PAD), BF16), jax.ShapeDtypeStruct((nh // 2, t, LANES), BF16)),
        grid=(t // tm,),
        in_specs=[pl.BlockSpec((tm, ckv.shape[1]), row), pl.BlockSpec((tm, LANES), row),
                  pl.BlockSpec(wk.shape, full3), pl.BlockSpec(wv.shape, full3),
                  pl.BlockSpec((8, LANES), lambda i: (0, 0))],
        out_specs=(pl.BlockSpec((nh, tm, HEAD_PAD), hrow), pl.BlockSpec((nh // 2, tm, LANES), hrow)),
        compiler_params=_cparams(("parallel",)),
        name="ctx_kv",
    )(ckv, kr, wk, wv, nrm)


def _attn_kernel(*refs, with_ctx, kblk):
    if with_ctx:
        q_ref, k_ref, v_ref, kc_ref, vc_ref, o_ref = refs
    else:
        q_ref, k_ref, v_ref, o_ref = refs
    seq = k_ref.shape[1]
    n_heads = q_ref.shape[0]
    blocks = [(kc_ref, vc_ref, 0, kc_ref.shape[1])] if with_ctx else []
    blocks += [(k_ref, v_ref, b0, kblk) for b0 in range(0, seq, kblk)]
    items = [(j, blk) for j in range(n_heads) for blk in blocks]

    def scores(item):
        j, (kr, _, b0, n) = item
        return _dot_nt(q_ref[j], kr[j, b0:b0 + n, :])

    outs = []
    m = l = acc = None
    s_next = scores(items[0])
    for idx, (j, (_, vr, b0, n)) in enumerate(items):
        s = s_next
        if idx + 1 < len(items):
            s_next = scores(items[idx + 1])
        mb = jnp.max(s, axis=-1, keepdims=True)
        m_new = mb if m is None else jnp.maximum(m, mb)
        p = jnp.exp(s - m_new)
        pv = _dot(p.astype(BF16), vr[j // 2, b0:b0 + n, :])
        ps = jnp.sum(p, axis=-1, keepdims=True)
        if m is None:
            l, acc = ps, pv
        else:
            a = jnp.exp(m - m_new)
            l, acc = a * l + ps, a * acc + pv
        m = m_new
        if idx + 1 == len(items) or items[idx + 1][0] != j:
            outs.append(acc / l)
            m = l = acc = None
    lane = lax.broadcasted_iota(jnp.int32, outs[0].shape, 1)
    for pr in range(n_heads // 2):
        o_ref[:, pr * LANES:(pr + 1) * LANES] = jnp.where(lane < V_DIM, outs[2 * pr], outs[2 * pr + 1]).astype(BF16)


def attention(q, k, v2, row0, n_b, seq, tq, kc=None, vc2=None, ctx_len=0, kblk=1024, pairs=1):
    assert row0 % seq == 0 and seq % tq == 0
    nq = seq // tq
    qb0 = row0 // tq
    kb0 = row0 // seq
    with_ctx = kc is not None
    hp_n = 2 * pairs
    in_specs = [pl.BlockSpec((hp_n, tq, HEAD_PAD), lambda b, hp, qi: (hp, qb0 + b * nq + qi, 0)),
                pl.BlockSpec((hp_n, seq, HEAD_PAD), lambda b, hp, qi: (hp, kb0 + b, 0)),
                pl.BlockSpec((pairs, seq, LANES), lambda b, hp, qi: (hp, kb0 + b, 0))]
    args = [q, k, v2]
    if with_ctx:
        in_specs += [pl.BlockSpec((hp_n, ctx_len, HEAD_PAD), lambda b, hp, qi: (hp, b, 0)),
                     pl.BlockSpec((pairs, ctx_len, LANES), lambda b, hp, qi: (hp, b, 0))]
        args += [kc, vc2]
    return pl.pallas_call(
        functools.partial(_attn_kernel, with_ctx=with_ctx, kblk=min(seq, kblk)),
        out_shape=jax.ShapeDtypeStruct((n_b * seq, (N_HEADS // 2) * LANES), BF16),
        grid=(n_b, N_HEADS // (2 * pairs), nq),
        in_specs=in_specs,
        out_specs=pl.BlockSpec((tq, pairs * LANES), lambda b, hp, qi: (b * nq + qi, hp)),
        compiler_params=_cparams(("parallel", "parallel", "parallel")),
        name="attn_ctx" if with_ctx else "attn",
    )(*args)


SSM_GPT = LANES // SSM_GROUP
SUBLANES = 8


def _lane_block_transpose(vs):
    n = len(vs)
    blk = lax.broadcasted_iota(jnp.int32, vs[0].shape, 1) // SSM_GROUP
    d = n // 2
    while d >= 1:
        keep = (blk & d) == 0
        new = list(vs)
        for a in range(n):
            if a & d == 0:
                b = a + d
                new[a] = jnp.where(keep, vs[a], pltpu.roll(vs[b], d * SSM_GROUP, 1))
                new[b] = jnp.where(keep, pltpu.roll(vs[a], LANES - d * SSM_GROUP, 1), vs[b])
        vs = new
        d //= 2
    return vs


def _ssm_in_kernel(y_ref, mod_ref, nw_ref, w_ref, u_ref, us_ref, ug_ref):
    m = mod_ref[...]
    h = _norm_mod(y_ref[...], nw_ref[...], m[0:1], m[1:2])
    z = _dot(h.astype(BF16), w_ref[...])
    n_lt = us_ref.shape[0]
    for j in range(n_lt):
        us_ref[j] = z[:, j * LANES:(j + 1) * LANES]
    n_ch = us_ref.shape[1] // SSM_CHUNK
    for j in range(n_lt):
        for rb in range(n_ch // SUBLANES):
            for hf in range(SSM_CHUNK // SSM_GPT):
                src = [us_ref[j, pl.ds(rb * SUBLANES * SSM_CHUNK + hf * SSM_GPT + tt, SUBLANES, stride=SSM_CHUNK), :]
                       for tt in range(SSM_GPT)]
                for gl, v in enumerate(_lane_block_transpose(src)):
                    ug_ref[j * SSM_GPT + gl, rb * SUBLANES:(rb + 1) * SUBLANES, hf * LANES:(hf + 1) * LANES] = v
    u_ref[...] = ug_ref[...].astype(BF16)


def ssm_in(lay, y, mod, nw, w_in):
    t, d = y.shape
    tm = lay.tm
    n = w_in.shape[1]
    g = n // SSM_GROUP
    n_ch = tm // SSM_CHUNK
    qk = SSM_CHUNK * SSM_GROUP
    row = lambda i: (i, 0)
    return pl.pallas_call(
        _ssm_in_kernel,
        out_shape=jax.ShapeDtypeStruct((g, t // SSM_CHUNK, qk), BF16),
        grid=(lay.n_tiles,),
        in_specs=[pl.BlockSpec((tm, d), row), pl.BlockSpec((None, 8, d), lambda i: (lay.seg(i), 0, 0)),
                  pl.BlockSpec((1, d), lambda i: (0, 0)), pl.BlockSpec((d, n), lambda i: (0, 0))],
        out_specs=pl.BlockSpec((g, n_ch, qk), lambda i: (0, i, 0)),
        scratch_shapes=[pltpu.VMEM((n // LANES, tm, LANES), F32), pltpu.VMEM((g, n_ch, qk), F32)],
        compiler_params=_cparams(("parallel",)),
        name="ssm_in",
    )(y, mod, nw, w_in)


def _ssm_prep_kernel(ar_ref, ai_ref, ld_ref, arc_ref, aic_ref, ldc_ref, btr_ref, bti_ref, ctr_ref, cti_ref, dsk_ref,
                     mt_ref, ett_ref, ft_ref, coef_ref):
    q = SSM_CHUNK
    k = SSM_GROUP
    p = SSM_STATE
    qk = q * k
    fwd = pl.program_id(0) == 0
    dt = jnp.exp(ld_ref[...])
    are = ar_ref[...]
    aim = ai_ref[...]
    mag = jnp.exp(dt * are)
    abr = mag * jnp.cos(dt * aim)
    abi = mag * jnp.sin(dt * aim)
    den = are * are + aim * aim
    nr = abr - 1.0
    cf_re = (nr * are + abi * aim) / den
    cf_im = (abi * are - nr * aim) / den
    btr = btr_ref[...]
    bti = bti_ref[...]
    bbt_re = cf_re * btr - cf_im * bti
    bbt_im = cf_re * bti + cf_im * btr
    bbt_re_t = jnp.concatenate([bbt_re] * q, axis=0)
    bbt_im_t = jnp.concatenate([bbt_im] * q, axis=0)
    s_idx = lax.broadcasted_iota(jnp.int32, (q, 1), 0).astype(F32)
    pw = jnp.where(fwd, (q - 1.0) - s_idx, s_idx)
    pm = jnp.exp(pw * dt * are)
    rep_rows = lambda x: jnp.concatenate([jnp.broadcast_to(x[s:s + 1], (k, p)) for s in range(q)], axis=0)
    pr = rep_rows(pm * jnp.cos(pw * dt * aim))
    pi = rep_rows(pm * jnp.sin(pw * dt * aim))
    et_re = pr * bbt_re_t - pi * bbt_im_t
    et_im = pr * bbt_im_t + pi * bbt_re_t
    ett_ref[...] = jnp.concatenate([et_re, et_im, et_im, et_re], axis=1).astype(BF16)
    qm = jnp.exp(q * dt * are)
    aq_re = qm * jnp.cos(q * dt * aim)
    aq_im = qm * jnp.sin(q * dt * aim)
    c1 = jnp.concatenate([aq_re, aq_re], axis=1)
    c2 = jnp.concatenate([-aq_im, aq_im], axis=1)
    coef_ref[...] = jnp.concatenate([c1, c2, jnp.zeros((6, 2 * p), F32)], axis=0)
    dtc = jnp.exp(ldc_ref[...])
    arec = arc_ref[...]
    aimc = aic_ref[...]
    ctr = ctr_ref[...]
    cti = cti_ref[...]
    tau = lax.broadcasted_iota(jnp.int32, (1, q), 1).astype(F32)
    tau0 = jnp.where(fwd, tau, (q - 1.0) - tau)
    m0 = jnp.exp(tau0 * dtc * arec)
    p0r_q = m0 * jnp.cos(tau0 * dtc * aimc)
    p0i_q = m0 * jnp.sin(tau0 * dtc * aimc)
    magc = jnp.exp(dtc * arec)
    abr_c = magc * jnp.cos(dtc * aimc)
    abi_c = magc * jnp.sin(dtc * aimc)
    spread = jnp.where(lax.broadcasted_iota(jnp.int32, (q, qk), 1) // k == lax.broadcasted_iota(jnp.int32, (q, qk), 0),
                       1.0, 0.0)
    p0r = _dot_hi(p0r_q, spread)
    p0i = _dot_hi(p0i_q, spread)
    p1r = _dot_hi(p0r_q * abr_c - p0i_q * abi_c, spread)
    p1i = _dot_hi(p0r_q * abi_c + p0i_q * abr_c, spread)
    ft_ref[...] = jnp.concatenate([ctr * p1r - cti * p1i, -(ctr * p1i + cti * p1r)], axis=0).astype(BF16)
    left_re = ctr * p0r - cti * p0i
    left_im = ctr * p0i + cti * p0r
    kt = _dot_hi(bbt_re, left_re) - _dot_hi(bbt_im, left_im)
    lane = lax.broadcasted_iota(jnp.int32, (k, qk), 1)
    rowi = lax.broadcasted_iota(jnp.int32, (k, qk), 0)
    kt_f = kt + jnp.where(lane == rowi, dsk_ref[...], 0.0)
    rows_f, rows_b = [], []
    for s in range(q):
        rows_f.append(kt_f if s == 0 else jnp.where(lane >= s * k, pltpu.roll(kt_f, s * k, 1), 0.0))
        sh = ((s + 1) * k) % qk
        rows_b.append(kt if sh == 0 else jnp.where(lane < (s + 1) * k, pltpu.roll(kt, sh, 1), 0.0))
    mt = jnp.where(fwd, jnp.concatenate(rows_f, axis=0), jnp.concatenate(rows_b, axis=0))
    mt_ref[...] = mt.astype(BF16)


def ssm_prep(a_re, a_im, log_dt, b_re, b_im, c_re, c_im, d_skip):
    nd, g, p = a_re.shape
    k = b_re.shape[-1]
    qk = SSM_CHUNK * k
    ld = jnp.broadcast_to(log_dt[..., None], (nd, g, p))
    dsk = jnp.broadcast_to(jnp.tile(d_skip.reshape(g, 1, k), (1, 1, SSM_CHUNK))[None], (nd, g, 1, qk))
    rowv = lambda x: x.reshape(nd, g, 1, p)
    colv = lambda x: x.reshape(nd, g, p, 1)
    bt = lambda x: jnp.swapaxes(x, -1, -2)
    ct = lambda x: jnp.tile(jnp.swapaxes(x, -1, -2), (1, 1, 1, SSM_CHUNK))
    spec = lambda r, c: pl.BlockSpec((None, None, r, c), lambda d, j: (d, j, 0, 0))
    return pl.pallas_call(
        _ssm_prep_kernel,
        out_shape=(jax.ShapeDtypeStruct((nd, g, qk, qk), BF16), jax.ShapeDtypeStruct((nd, g, qk, 4 * p), BF16),
                   jax.ShapeDtypeStruct((nd, g, 2 * p, qk), BF16), jax.ShapeDtypeStruct((nd, g, 8, 2 * p), F32)),
        grid=(nd, g),
        in_specs=[spec(1, p), spec(1, p), spec(1, p), spec(p, 1), spec(p, 1), spec(p, 1),
                  spec(k, p), spec(k, p), spec(p, qk), spec(p, qk), spec(1, qk)],
        out_specs=(spec(qk, qk), spec(qk, 4 * p), spec(2 * p, qk), spec(8, 2 * p)),
        compiler_params=_cparams(("parallel", "parallel")),
        name="ssm_prep",
    )(rowv(a_re), rowv(a_im), rowv(ld), colv(a_re), colv(a_im), colv(ld), bt(b_re), bt(b_im), ct(c_re), ct(c_im), dsk)


def _ssm_scan_kernel(u_ref, mt_ref, ett_ref, ft_ref, coef_ref, x0_ref, x0s_ref, y_ref, hfin_ref,
                     sx_ref, sxs_ref, hp_ref, *, gpb, ctx_b, ctx_chunks, lat_b, lat_chunks, rchunk):
    w = 2 * SSM_STATE
    n_rows = u_ref.shape[1]
    ctx_rows = ctx_b * ctx_chunks
    for d in range(2):
        for g in range(gpb):
            for r0 in range(0, n_rows, rchunk):
                ss = _dot(u_ref[g, r0:r0 + rchunk, :], ett_ref[d, g])
                sx_ref[g, r0:r0 + rchunk, :] = ss[:, :w]
                sxs_ref[g, r0:r0 + rchunk, :] = ss[:, w:]
        c1 = [coef_ref[d, g][0:1] for g in range(gpb)]
        c2 = [coef_ref[d, g][1:2] for g in range(gpb)]

        def make_body(base, nb, nc):
            def body(i, carry):
                c = i if d == 0 else nc - 1 - i
                rows = pl.ds(base + c, nb, stride=nc)
                out = []
                for g in range(gpb):
                    x, xs = carry[g]
                    hp_ref[g, rows, :] = x
                    xn = c1[g] * x + c2[g] * xs + sx_ref[g, rows, :]
                    xsn = c1[g] * xs - c2[g] * x + sxs_ref[g, rows, :]
                    out.append((xn, xsn))
                return tuple(out)
            return body

        z = jnp.zeros((ctx_b, w), F32)
        fin = lax.fori_loop(0, ctx_chunks, make_body(0, ctx_b, ctx_chunks), tuple((z, z) for _ in range(gpb)))
        hfin_ref[d] = jnp.concatenate([fin[g][0] for g in range(gpb)], axis=1)
        init = tuple((x0_ref[d][:, g * w:(g + 1) * w], x0s_ref[d][:, g * w:(g + 1) * w]) for g in range(gpb))
        lax.fori_loop(0, lat_chunks, make_body(ctx_rows, lat_b, lat_chunks), init, unroll=4)
        for g in range(gpb):
            for r0 in range(0, n_rows, rchunk):
                y = _dot(u_ref[g, r0:r0 + rchunk, :], mt_ref[d, g])
                y = y + _dot(hp_ref[g, r0:r0 + rchunk, :].astype(BF16), ft_ref[d, g])
                if d == 0:
                    y_ref[g, r0:r0 + rchunk, :] = y
                else:
                    y_ref[g, r0:r0 + rchunk, :] += y


def ssm_scan(u, mt, ett, ft, coef, x0, x0s, ctx_b, ctx_chunks, lat_b, lat_chunks, gpb=4):
    g, n_rows, qk = u.shape
    w = 2 * SSM_STATE
    rchunk = math.gcd(n_rows, 512)
    assert n_rows == ctx_b * ctx_chunks + lat_b * lat_chunks
    ublk = pl.BlockSpec((gpb, n_rows, qk), lambda j: (j, 0, 0))
    blk = lambda r, c: pl.BlockSpec((2, gpb, r, c), lambda j: (0, j, 0, 0))
    vec = lambda r: pl.BlockSpec((2, r, gpb * w), lambda j: (0, 0, j))
    return pl.pallas_call(
        functools.partial(_ssm_scan_kernel, gpb=gpb, ctx_b=ctx_b, ctx_chunks=ctx_chunks, lat_b=lat_b,
                          lat_chunks=lat_chunks, rchunk=rchunk),
        out_shape=(jax.ShapeDtypeStruct((g, n_rows, qk), F32), jax.ShapeDtypeStruct((2, ctx_b, g * w), F32)),
        grid=(g // gpb,),
        in_specs=[ublk, blk(qk, qk), blk(qk, 2 * w), blk(w, qk), blk(8, w), vec(lat_b), vec(lat_b)],
        out_specs=(ublk, vec(ctx_b)),
        scratch_shapes=[pltpu.VMEM((gpb, n_rows, w), F32), pltpu.VMEM((gpb, n_rows, w), F32),
                        pltpu.VMEM((gpb, n_rows, w), F32)],
        compiler_params=_cparams(("parallel",)),
        name="ssm_scan",
    )(u, mt, ett, ft, coef, x0, x0s)


def _ssm_out_kernel(y_ref, mod_ref, yc_ref, w_ref, nw_ref, wr_ref, br_ref, tri_ref, out_ref, h_ref, ri_ref, pc_ref,
                    s_ref, *, d_model):
    n_ch = yc_ref.shape[1]
    n_lt = s_ref.shape[0]
    for j in range(n_lt):
        for rb in range(n_ch // SUBLANES):
            for hf in range(SSM_CHUNK // SSM_GPT):
                src = [yc_ref[j * SSM_GPT + gl, rb * SUBLANES:(rb + 1) * SUBLANES, hf * LANES:(hf + 1) * LANES]
                       for gl in range(SSM_GPT)]
                for tt, v in enumerate(_lane_block_transpose(src)):
                    s_ref[j, pl.ds(rb * SUBLANES * SSM_CHUNK + hf * SSM_GPT + tt, SUBLANES, stride=SSM_CHUNK), :] = v
    s = jnp.concatenate([s_ref[j] for j in range(n_lt)], axis=1)
    a = jax.nn.gelu(s, approximate=True).astype(BF16)
    z = _dot(a, w_ref[...])
    m = mod_ref[...]
    y = y_ref[...] + m[2:3] * (z[:, :d_model] * jax.nn.sigmoid(z[:, d_model:]))
    out_ref[...] = y
    _route_core(y, m, nw_ref, wr_ref, br_ref, tri_ref, h_ref, ri_ref, pc_ref)


def ssm_out_route(lay, y, mod, ych, w_glu, route):
    t, d = y.shape
    tm = lay.tm
    g, _, qk = ych.shape
    n_ch = tm // SSM_CHUNK
    row = lambda i: (i, 0)
    r_args, r_in, r_shape, r_out = _route_io(lay, d, *route)
    return pl.pallas_call(
        functools.partial(_ssm_out_kernel, d_model=d),
        out_shape=[jax.ShapeDtypeStruct((t, d), F32)] + r_shape,
        grid=(lay.n_tiles,),
        in_specs=[pl.BlockSpec((tm, d), row), pl.BlockSpec((None, 8, d), lambda i: (lay.seg(i), 0, 0)),
                  pl.BlockSpec((g, n_ch, qk), lambda i: (0, i, 0)),
                  pl.BlockSpec(w_glu.shape, lambda i: (0, 0))] + r_in,
        out_specs=[pl.BlockSpec((tm, d), row)] + r_out,
        scratch_shapes=[pltpu.VMEM((g * SSM_GROUP // LANES, tm, LANES), F32)],
        input_output_aliases={0: 0},
        compiler_params=_cparams(("parallel",)),
        name="ssm_out_route",
    )(y, mod, ych, w_glu, *r_args)


def s5_layer(lay, y, mod, nw, route, state_re, state_im, w_in, a_re, a_im, log_dt, b_re, b_im, c_re, c_im, d_skip,
             w_glu):
    uch = ssm_in(lay, y, mod, nw, w_in.astype(BF16))
    g = a_re.shape[1]
    p = SSM_STATE
    mt, ett, ft, coef = ssm_prep(a_re, a_im, log_dt, b_re, b_im, c_re, c_im, d_skip)
    st = jnp.concatenate([state_re, state_im], axis=-1)
    sts = jnp.concatenate([state_im, state_re], axis=-1)
    x0 = jnp.transpose(st, (1, 0, 2, 3)).reshape(2, lay.n_lat_b, g * 2 * p)
    x0s = jnp.transpose(sts, (1, 0, 2, 3)).reshape(2, lay.n_lat_b, g * 2 * p)
    ych, hfin = ssm_scan(uch, mt, ett, ft, coef, x0, x0s, lay.n_ctx_b, lay.ctx_len // SSM_CHUNK,
                         lay.n_lat_b, lay.lat_len // SSM_CHUNK)
    routed = ssm_out_route(lay, y, mod, ych, w_glu.astype(BF16), route)
    hf = hfin.reshape(2, lay.n_ctx_b, g, 2, p)
    new_re = jnp.transpose(hf[:, :, :, 0], (1, 0, 2, 3))
    new_im = jnp.transpose(hf[:, :, :, 1], (1, 0, 2, 3))
    return routed, new_re, new_im


def _route_core(y, m, nw_ref, wr_ref, br_ref, tri_ref, h_ref, ri_ref, pc_ref):
    n_exp = br_ref.shape[0]
    h = _norm_mod(y, nw_ref[...], m[3:4], m[4:5])
    h_hi = h.astype(BF16)
    h_ref[...] = h_hi
    h_lo = (h - h_hi.astype(F32)).astype(BF16)
    wr = wr_ref[...]
    lt = _dot_nt(wr, h_hi)
    logits = lt[:n_exp] + lt[n_exp:] + _dot_nt(wr[:n_exp], h_lo)
    scores = jax.nn.sigmoid(logits)
    sel = scores + br_ref[...]
    epg = EXPERTS_PER_GROUP
    row = lambda x, e: x[e:e + 1, :]
    gscore = []
    for g in range(N_EXPERT_GROUPS):
        a, b, c, d = (row(sel, g * epg + j) for j in range(epg))
        m1, n1, m2, n2 = jnp.maximum(a, b), jnp.minimum(a, b), jnp.maximum(c, d), jnp.minimum(c, d)
        gscore.append(jnp.maximum(m1, m2) + jnp.maximum(jnp.minimum(m1, m2), jnp.maximum(n1, n2)))
    best = gscore[0]
    gi = jnp.zeros_like(best, dtype=jnp.int32)
    for g in range(1, N_EXPERT_GROUPS):
        better = gscore[g] > best
        gi = jnp.where(better, g, gi)
        best = jnp.where(better, gscore[g], best)

    def pick(x, j):
        out = row(x, j)
        for g in range(1, N_EXPERT_GROUPS):
            out = jnp.where(gi == g, row(x, g * epg + j), out)
        return out

    sv = [pick(sel, j) for j in range(epg)]
    cv = [pick(scores, j) for j in range(epg)]
    b1, i1, w1 = sv[0], jnp.zeros_like(gi), cv[0]
    for j in range(1, epg):
        better = sv[j] > b1
        i1 = jnp.where(better, j, i1)
        w1 = jnp.where(better, cv[j], w1)
        b1 = jnp.where(better, sv[j], b1)
    neg = jnp.full_like(b1, -jnp.inf)
    b2, i2, w2 = neg, jnp.zeros_like(gi), jnp.zeros_like(w1)
    for j in range(epg):
        better = (i1 != j) & (sv[j] > b2)
        i2 = jnp.where(better, j, i2)
        w2 = jnp.where(better, cv[j], w2)
        b2 = jnp.where(better, sv[j], b2)
    tot = w1 + w2
    e1 = gi * epg + i1
    e2 = gi * epg + i2
    eid = lax.broadcasted_iota(jnp.int32, logits.shape, 0)
    m1h = eid == e1
    m2h = eid == e2
    mc = jnp.where(m1h | m2h, 1.0, 0.0)
    pref = _dot(mc.astype(BF16), tri_ref[...])
    cnt = jnp.sum(mc, axis=1, keepdims=True)
    pc_al = jnp.ceil(cnt * (1.0 / SEG_ALIGN))
    pcb = jnp.broadcast_to(pc_al, (n_exp, LANES))
    er = lax.broadcasted_iota(jnp.int32, (n_exp, n_exp), 0)
    ec = lax.broadcasted_iota(jnp.int32, (n_exp, n_exp), 1)
    lower = jnp.where(ec < er, 1.0, 0.0).astype(BF16)
    seg = _dot(lower, pcb.astype(BF16))[:, 0:1] * SEG_ALIGN
    slot = seg + pref
    pos1 = jnp.sum(jnp.where(m1h, slot, 0.0), axis=0, keepdims=True)
    pos2 = jnp.sum(jnp.where(m2h, slot, 0.0), axis=0, keepdims=True)
    zero = jnp.zeros_like(pos1)
    ri_ref[...] = jnp.concatenate([pos1, pos2, w1 / tot, w2 / tot, zero, zero, zero, zero], axis=0)
    pc_ref[...] = pcb * SEG_ALIGN


def _route_io(lay, d, nw, wr_t, br, tri):
    tm = lay.tm
    n_exp = br.shape[0]
    full = lambda i: (0, 0)
    args = [nw, wr_t, br, tri]
    in_specs = [pl.BlockSpec((1, d), full), pl.BlockSpec(wr_t.shape, full), pl.BlockSpec((n_exp, 1), full),
                pl.BlockSpec((tm, tm), full)]
    out_shape = [jax.ShapeDtypeStruct((lay.t, d), BF16), jax.ShapeDtypeStruct((8, lay.t), F32),
                 jax.ShapeDtypeStruct((lay.n_tiles, n_exp, LANES), F32)]
    out_specs = [pl.BlockSpec((tm, d), lambda i: (i, 0)), pl.BlockSpec((8, tm), lambda i: (0, i)),
                 pl.BlockSpec((None, n_exp, LANES), lambda i: (i, 0, 0))]
    return args, in_specs, out_shape, out_specs


def _proj_route_kernel(*refs, lay, n_y):
    y_refs = refs[:n_y]
    o_refs = refs[n_y + 1:n_y + 3]
    mod_ref = refs[n_y]
    w_ref, nw_ref, wr_ref, br_ref, tri_ref, out_ref, h_ref, ri_ref, pc_ref = refs[n_y + 3:]
    m = mod_ref[...]
    y = lay.stream_load(y_refs) + m[2:3] * _dot(lay.stream_load(o_refs), w_ref[...])
    out_ref[...] = y
    _route_core(y, m, nw_ref, wr_ref, br_ref, tri_ref, h_ref, ri_ref, pc_ref)


def proj_route(lay, y, mod, o_pair, w, route):
    ys = _as_list(y)
    d = ys[0].shape[1]
    tm = lay.tm
    row = lambda i: (i, 0)
    r_args, r_in, r_shape, r_out = _route_io(lay, d, *route)
    return pl.pallas_call(
        functools.partial(_proj_route_kernel, lay=lay, n_y=len(ys)),
        out_shape=[jax.ShapeDtypeStruct((lay.t, d), F32)] + r_shape,
        grid=(lay.n_tiles,),
        in_specs=lay.stream_specs(d, len(ys) == 2) + [pl.BlockSpec((None, 8, d), lambda i: (lay.seg(i), 0, 0))]
        + lay.stream_specs(o_pair[0].shape[1], True) + [pl.BlockSpec(w.shape, lambda i: (0, 0))] + r_in,
        out_specs=[pl.BlockSpec((tm, d), row)] + r_out,
        input_output_aliases={0: 0} if len(ys) == 1 else {},
        compiler_params=_cparams(("parallel",)),
        name="proj_route",
    )(*ys, mod, *o_pair, w, *r_args)


def _segment_pieces(i, off_ref, pc_ref, n_exp, sizes, fn):
    local = 0
    for e in range(n_exp):
        n = pc_ref[i * n_exp + e]
        g0 = off_ref[i * n_exp + e]

        def emit(size, e=e, n=n, g0=g0, local=local):
            above = n & (-2 * size)

            @pl.when((n & size) != 0)
            def _():
                fn(e, pl.multiple_of(local + above, SEG_ALIGN), pl.multiple_of(g0 + above, SEG_ALIGN), size)

        for size in sizes:
            emit(size)
        local = local + n


def _pieces(n, sizes):
    for size in sizes:
        yield (n & size) != 0, n & (-2 * size), size


def _moe_permute_kernel(off_ref, pc_ref, goff_ref, gn_ref, h_ref, ri_ref, xs_ref, ys_ref, zp_ref, sem, gsem,
                        *, n_exp, sizes, gap_sizes, rchunk):
    i = pl.program_id(0)
    last = pl.num_programs(0) - 1
    slot = i % 2
    ri = ri_ref[...]
    pos1 = ri[0:1]
    pos2 = ri[1:2]
    h = h_ref[...]
    n_loc = ys_ref.shape[1]
    for r0 in range(0, n_loc, rchunk):
        r = (lax.broadcasted_iota(jnp.int32, (rchunk, h.shape[0]), 0) + r0).astype(F32)
        p = jnp.where(pos1 == r, 1.0, 0.0) + jnp.where(pos2 == r, 1.0, 0.0)
        ys_ref[slot, r0:r0 + rchunk, :] = _dot(p.astype(BF16), h).astype(BF16)

    def copies(tile, sl, act):
        def piece(e, lrow, grow, size):
            act(pltpu.make_async_copy(ys_ref.at[sl, pl.ds(lrow, size)], xs_ref.at[pl.ds(grow, size)], sem.at[sl, e]))
        _segment_pieces(tile, off_ref, pc_ref, n_exp, sizes, piece)

    start = lambda cp: cp.start()
    wait = lambda cp: cp.wait()
    copies(i, slot, start)

    @pl.when(i > 0)
    def _():
        copies(i - 1, 1 - slot, wait)

    @pl.when(i == last)
    def _():
        copies(i, slot, wait)
        zp_ref[...] = jnp.zeros_like(zp_ref)

        def gaps():
            for e in range(n_exp):
                g0 = goff_ref[e]
                for present, above, size in _pieces(gn_ref[e], gap_sizes):
                    yield present, pltpu.make_async_copy(
                        zp_ref.at[pl.ds(0, size)], xs_ref.at[pl.ds(pl.multiple_of(g0 + above, SEG_ALIGN), size)],
                        gsem.at[e])

        for present, cp in gaps():
            pl.when(present)(cp.start)
        for present, cp in gaps():
            pl.when(present)(cp.wait)

        zrows = zp_ref.shape[0]
        tail0 = goff_ref[n_exp]
        tail = lambda c: pltpu.make_async_copy(
            zp_ref, xs_ref.at[pl.ds(pl.multiple_of(tail0 + c * zrows, zrows), zrows)], gsem.at[0])

        @pl.loop(0, gn_ref[n_exp])
        def _(c):
            tail(c).start()

        @pl.loop(0, gn_ref[n_exp])
        def _(c):
            tail(c).wait()


def moe_permute(lay, h, rinfo, off, pcs, gap_off, gap_n, n_sorted, n_loc, n_exp, tmm):
    t, d = h.shape
    tm = lay.tm
    sizes = [s for s in (1 << b for b in range(12, 3, -1)) if s <= tm]
    gap_sizes = [s for s in (1 << b for b in range(12, 3, -1)) if s < tmm]
    return pl.pallas_call(
        functools.partial(_moe_permute_kernel, n_exp=n_exp, sizes=sizes, gap_sizes=gap_sizes, rchunk=256),
        out_shape=jax.ShapeDtypeStruct((n_sorted, d), BF16),
        grid_spec=pltpu.PrefetchScalarGridSpec(
            num_scalar_prefetch=4, grid=(lay.n_tiles,),
            in_specs=[pl.BlockSpec((tm, d), lambda i, *_: (i, 0)), pl.BlockSpec((8, tm), lambda i, *_: (0, i))],
            out_specs=pl.BlockSpec(memory_space=pl.ANY),
            scratch_shapes=[pltpu.VMEM((2, n_loc, d), BF16), pltpu.VMEM((gap_sizes[0], d), BF16),
                            pltpu.SemaphoreType.DMA((2, n_exp)), pltpu.SemaphoreType.DMA((n_exp,))]),
        compiler_params=_cparams(("arbitrary",)),
        name="moe_permute",
    )(off, pcs, gap_off, gap_n, h, rinfo)


def _moe_expert_kernel(te_ref, tv_ref, tf_ref, xb_ref, nx_ref, x_ref, wg_hbm, wu_hbm, wd_hbm, z_ref,
                       wgs_ref, wus_ref, wds_ref, wgb_ref, wub_ref, wdb_ref, sem, *, layer):
    del xb_ref
    r = pl.program_id(0)

    def fetch(e):
        return (pltpu.make_async_copy(wg_hbm.at[layer, e], wgs_ref, sem.at[0]),
                pltpu.make_async_copy(wu_hbm.at[layer, e], wus_ref, sem.at[1]),
                pltpu.make_async_copy(wd_hbm.at[layer, e], wds_ref, sem.at[2]))

    @pl.when(r == 0)
    def _():
        for cp in fetch(te_ref[0]):
            cp.start()

    @pl.when(tf_ref[r] == 1)
    def _():
        for cp in fetch(te_ref[r]):
            cp.wait()
        wgb_ref[...] = wgs_ref[...].astype(BF16)
        wub_ref[...] = wus_ref[...].astype(BF16)
        wdb_ref[...] = wds_ref[...].astype(BF16)

        @pl.when(nx_ref[r] >= 0)
        def _():
            for cp in fetch(nx_ref[r]):
                cp.start()

    @pl.when(tv_ref[r] > 0)
    def _():
        x = x_ref[...]
        a = _dot(x, wgb_ref[...])
        b = _dot(x, wub_ref[...])
        hid = (a * jax.nn.sigmoid(a)) * b
        z_ref[...] = _dot(hid.astype(BF16), wdb_ref[...]).astype(BF16)

    @pl.when(tv_ref[r] == 0)
    def _():
        z_ref[...] = jnp.zeros_like(z_ref)


def moe_experts(xs, tile_expert, tile_valid, tile_first, x_block, next_expert, layer, wg, wu, wd, tmm):
    n_sorted, d = xs.shape
    f = wg.shape[-1]
    hbm = pl.BlockSpec(memory_space=pl.ANY)
    return pl.pallas_call(
        functools.partial(_moe_expert_kernel, layer=layer),
        out_shape=jax.ShapeDtypeStruct((n_sorted, d), BF16),
        grid_spec=pltpu.PrefetchScalarGridSpec(
            num_scalar_prefetch=5, grid=(n_sorted // tmm,),
            in_specs=[pl.BlockSpec((tmm, d), lambda r, te, tv, tf, xb, nx: (xb[r], 0)), hbm, hbm, hbm],
            out_specs=pl.BlockSpec((tmm, d), lambda r, te, tv, tf, xb, nx: (r, 0)),
            scratch_shapes=[pltpu.VMEM((d, f), F32), pltpu.VMEM((d, f), F32), pltpu.VMEM((f, d), F32),
                            pltpu.VMEM((d, f), BF16), pltpu.VMEM((d, f), BF16), pltpu.VMEM((f, d), BF16),
                            pltpu.SemaphoreType.DMA((3,))]),
        compiler_params=_cparams(("arbitrary",)),
        name="moe_experts",
    )(tile_expert, tile_valid, tile_first, x_block, next_expert, xs, wg, wu, wd)


def _moe_combine_kernel(off_ref, pc_ref, y_ref, mod_ref, rit_ref, zs_ref, *rest, lay, n_exp, sizes):
    out_refs, (zt_ref, sem) = rest[:-2], rest[-2:]
    i = pl.program_id(0)
    slot = i % 2

    def copies(tile, sl, act):
        def piece(e, lrow, grow, size):
            act(pltpu.make_async_copy(zs_ref.at[pl.ds(grow, size)], zt_ref.at[sl, pl.ds(lrow, size)], sem.at[sl, e]))
        _segment_pieces(tile, off_ref, pc_ref, n_exp, sizes, piece)

    def fetch(tile, sl):
        used = pc_ref[tile * n_exp]
        for e in range(1, n_exp):
            used = used + pc_ref[tile * n_exp + e]

        @pl.loop(used // SEG_ALIGN, zt_ref.shape[1] // SEG_ALIGN)
        def _(c):
            zt_ref[sl, pl.ds(pl.multiple_of(c * SEG_ALIGN, SEG_ALIGN), SEG_ALIGN), :] = jnp.zeros(
                (SEG_ALIGN, zt_ref.shape[2]), BF16)
        copies(tile, sl, lambda cp: cp.start())

    @pl.when(i == 0)
    def _():
        fetch(i, slot)

    @pl.when(i + 1 < pl.num_programs(0))
    def _():
        fetch(i + 1, 1 - slot)

    rit = rit_ref[...]
    n_loc = zt_ref.shape[1]
    r = lax.broadcasted_iota(jnp.int32, (rit.shape[0], n_loc), 1).astype(F32)
    pw = jnp.where(rit[:, 0:1] == r, rit[:, 2:3], 0.0) + jnp.where(rit[:, 1:2] == r, rit[:, 3:4], 0.0)
    pw = pw.astype(BF16)
    copies(i, slot, lambda cp: cp.wait())
    lay.stream_store(out_refs, y_ref[...] + mod_ref[...][5:6] * _dot(pw, zt_ref[slot]))


def moe_combine(lay, y, mod, rinfo_t, zs, off, pcs, n_loc, n_exp, pair_out):
    t, d = y.shape
    tm = lay.tm
    sizes = [s for s in (1 << b for b in range(12, 3, -1)) if s <= tm]
    if pair_out:
        out_shape = (jax.ShapeDtypeStruct((lay.t_ctx, d), F32), jax.ShapeDtypeStruct((lay.t_lat, d), F32))
    else:
        out_shape = (jax.ShapeDtypeStruct((t, d), F32),)
    out = pl.pallas_call(
        functools.partial(_moe_combine_kernel, lay=lay, n_exp=n_exp, sizes=sizes),
        out_shape=out_shape,
        grid_spec=pltpu.PrefetchScalarGridSpec(
            num_scalar_prefetch=2, grid=(lay.n_tiles,),
            in_specs=[pl.BlockSpec((tm, d), lambda i, o, p: (i, 0)),
                      pl.BlockSpec((None, 8, d), lambda i, o, p: (lay.seg(i), 0, 0)),
                      pl.BlockSpec((tm, 8), lambda i, o, p: (i, 0)),
                      pl.BlockSpec(memory_space=pl.ANY)],
            out_specs=tuple(lay.stream_specs(d, pair_out)),
            scratch_shapes=[pltpu.VMEM((2, n_loc, d), BF16), pltpu.SemaphoreType.DMA((2, n_exp))]),
        input_output_aliases={} if pair_out else {2: 0},
        compiler_params=_cparams(("arbitrary",)),
        name="moe_combine",
    )(off, pcs, y, mod, rinfo_t, zs)
    return out if pair_out else out[0]


def _round_up(x, m):
    return (x + m - 1) // m * m


def _moe_plan_kernel(pc_ref, off_ref, te_ref, tv_ref, tf_ref, xb_ref, nx_ref, goff_ref, gn_ref,
                     *, n_tiles, n_exp, tmm, n_sorted):
    shift = tmm.bit_length() - 1
    n_rt = n_sorted // tmm
    zrows = tmm // 2
    base = jnp.int32(0)
    tots, bases, regions = [], [], []
    for e in range(n_exp):
        def seg_body(i, run, e=e, base=base):
            off_ref[i * n_exp + e] = base + run
            return run + pc_ref[i * n_exp + e]
        tot = lax.fori_loop(0, n_tiles, seg_body, jnp.int32(0))
        region = ((tot + (tmm - 1)) >> shift) << shift
        goff_ref[e] = base + tot
        gn_ref[e] = region - tot
        tots.append(tot)
        bases.append(base)
        regions.append(region)
        base = base + region
    goff_ref[n_exp] = base
    gn_ref[n_exp] = (n_sorted - base) // zrows
    last_active = jnp.maximum((base >> shift) - 1, 0)
    following = [None] * n_exp
    cur = jnp.int32(-1)
    for e in reversed(range(n_exp)):
        following[e] = cur
        cur = jnp.where(regions[e] > 0, jnp.int32(e), cur)

    def idle_body(r, _):
        te_ref[r] = n_exp - 1
        tv_ref[r] = 0
        tf_ref[r] = 0
        nx_ref[r] = -1
        xb_ref[r] = jnp.minimum(r, last_active)
        return 0
    lax.fori_loop(0, n_rt, idle_body, 0)
    for e in range(n_exp):
        t0 = bases[e] >> shift

        def tile_body(k, _, e=e, t0=t0):
            r = t0 + k
            te_ref[r] = e
            tv_ref[r] = jnp.clip(tots[e] - k * tmm, 0, tmm)
            tf_ref[r] = jnp.where(k == 0, 1, 0)
            nx_ref[r] = jnp.where(k == 0, following[e], -1)
            return 0
        lax.fori_loop(0, regions[e] >> shift, tile_body, 0)


def moe_plan(pcs, n_tiles, n_exp, tmm, n_sorted):
    assert tmm & (tmm - 1) == 0
    n_rt = n_sorted // tmm
    i32 = lambda n: jax.ShapeDtypeStruct((n,), jnp.int32)
    smem = pl.BlockSpec(memory_space=pltpu.SMEM)
    return pl.pallas_call(
        functools.partial(_moe_plan_kernel, n_tiles=n_tiles, n_exp=n_exp, tmm=tmm, n_sorted=n_sorted),
        out_shape=[i32(n_tiles * n_exp)] + [i32(n_rt)] * 5 + [i32(n_exp + 1)] * 2,
        in_specs=[smem],
        out_specs=[smem] * 8,
        name="moe_plan",
    )(pcs)


def moe_layer(lay, routed, mod, layer, wg, wu, wd, tmm, pair_out=False):
    y, h, rinfo, pc = routed
    t, d = y.shape
    n_exp = pc.shape[1]
    pcs = pc[:, :, 0].astype(jnp.int32).reshape(-1)
    n_loc = 2 * lay.tm + SEG_ALIGN * n_exp
    n_sorted = _round_up(2 * t + lay.n_tiles * n_exp * (SEG_ALIGN - 1) + n_exp * (tmm - 1), tmm)
    off, te, tv, tf, xb, nx, gap_off, gap_n = moe_plan(pcs, lay.n_tiles, n_exp, tmm, n_sorted)
    xs = moe_permute(lay, h, rinfo, off, pcs, gap_off, gap_n, n_sorted, n_loc, n_exp, tmm)
    zs = moe_experts(xs, te, tv, tf, xb, nx, layer, wg, wu, wd, tmm)
    return moe_combine(lay, y, mod, rinfo.T, zs, off, pcs, n_loc, n_exp, pair_out)


def _rope_perm():
    d = np.arange(QK_ROPE)
    return QK_NOPE + (d ^ (QK_ROPE // 4))


def _pad_cols(x, n):
    return jnp.pad(x, ((0, 0),) * (x.ndim - 1) + ((0, n - x.shape[-1]),))


def _mla_weights(w_q_a, w_q_b, w_kv_a, w_kv_b, q_norm, k_norm, kv_lora):
    perm = _rope_perm()
    nh = N_HEADS
    wkr = w_kv_a[:, kv_lora:]
    d = w_kv_a.shape[0]
    zl = jnp.zeros((d, QK_NOPE), F32)
    kr_blk = _pad_cols(jnp.concatenate([zl, wkr], axis=1), LANES)
    krp_blk = _pad_cols(jnp.concatenate([zl, wkr[:, perm - QK_NOPE]], axis=1), LANES)
    w_a = jnp.concatenate([w_q_a, w_kv_a[:, :kv_lora], kr_blk, krp_blk], axis=1).astype(BF16)
    wq = w_q_b.reshape(-1, nh, QK_DIM).transpose(1, 0, 2)
    wq_main = _pad_cols(wq, HEAD_PAD)
    wq_part = _pad_cols(jnp.concatenate([jnp.zeros_like(wq[..., :QK_NOPE]), wq[..., perm]], axis=-1), HEAD_PAD)
    wq_h = jnp.concatenate([wq_main, wq_part], axis=-1).astype(BF16)
    wkv = w_kv_b.reshape(-1, nh, QK_NOPE + V_DIM).transpose(1, 0, 2)
    wk_h = _pad_cols(wkv[..., :QK_NOPE], HEAD_PAD).astype(BF16)
    wv = wkv[..., QK_NOPE:]
    wv_h = jnp.concatenate([wv[0::2], wv[1::2]], axis=-1).astype(BF16)
    nrm = jnp.stack([_pad_cols(q_norm, LANES), _pad_cols(jnp.concatenate([jnp.zeros((QK_NOPE,), F32), q_norm[perm]]), LANES),
                     _pad_cols(k_norm, LANES), _pad_cols(jnp.concatenate([jnp.zeros((QK_NOPE,), F32), k_norm[perm]]), LANES)])
    nrm = jnp.pad(nrm, ((0, 4), (0, 0)))
    return w_a, wq_h, wk_h, wv_h, nrm


def _rope_tables(lat_len, tm):
    rows = lat_len // GRID_W
    t_row = np.repeat(np.arange(rows, dtype=np.float32), GRID_W)
    t_col = np.tile(np.arange(GRID_W, dtype=np.float32), rows)
    half = QK_ROPE // 2
    inv = (ROPE_BASE ** (-np.arange(0, half, 2, dtype=np.float32) / half)).astype(np.float32)
    ang = jnp.concatenate([jnp.asarray(t_row)[:, None] * inv, jnp.asarray(t_col)[:, None] * inv], axis=-1)
    cos, sin = jnp.cos(ang), jnp.sin(ang)
    nf = QK_ROPE // 4
    cos_l = jnp.concatenate([cos[:, :nf], cos[:, :nf], cos[:, nf:], cos[:, nf:]], axis=-1)
    sin_l = jnp.concatenate([-sin[:, :nf], sin[:, :nf], -sin[:, nf:], sin[:, nf:]], axis=-1)
    one = jnp.ones((lat_len, QK_NOPE), F32)
    cos_t = _pad_cols(jnp.concatenate([one, cos_l], axis=-1), LANES)
    cos_t = cos_t.at[:, QK_DIM:].set(1.0)
    sin_t = _pad_cols(jnp.concatenate([jnp.zeros((lat_len, QK_NOPE), F32), sin_l], axis=-1), LANES)
    cos_t = jnp.concatenate([jnp.ones((tm, LANES), F32), cos_t], axis=0)
    sin_t = jnp.concatenate([jnp.zeros((tm, LANES), F32), sin_t], axis=0)
    return cos_t, sin_t


def mla_layer(lay, y, mod, nw, route, cache_ckv, cache_krope, cos_t, sin_t, w_q_a, q_a_norm, w_q_b, w_kv_a,
              kv_a_norm, w_kv_b, q_norm, k_norm, w_o):
    q_lora = w_q_a.shape[1]
    kv_lora = kv_a_norm.shape[0]
    w_a, wq_h, wk_h, wv_h, nrm = _mla_weights(w_q_a, w_q_b, w_kv_a, w_kv_b, q_norm, k_norm, kv_lora)
    ckv, kr, q, k, v2 = mla_qkv(lay, y, mod, nw, w_a, q_a_norm.reshape(1, -1), kv_a_norm.reshape(1, -1),
                                wq_h, wk_h, wv_h, nrm, cos_t, sin_t, q_lora, kv_lora)
    past = cache_ckv.shape[1]
    ckr = jnp.pad(cache_krope.reshape(-1, QK_ROPE), ((0, 0), (QK_NOPE, LANES - QK_DIM)))
    kc, vc2 = ctx_kv(cache_ckv.reshape(-1, kv_lora), ckr, wk_h, wv_h, nrm, past)
    o_ctx = attention(q, k, v2, 0, lay.n_ctx_b, lay.ctx_len, min(lay.ctx_len, 256), pairs=N_HEADS // 2)
    o_lat = attention(q, k, v2, lay.t_ctx, lay.n_lat_b, lay.lat_len, min(lay.lat_len, 512), kc, vc2, past)
    routed = proj_route(lay, y, mod, (o_ctx, o_lat), w_o.astype(BF16), route)
    new_ckv = ckv[:lay.t_ctx].reshape(lay.n_ctx_b, lay.ctx_len, kv_lora)
    new_kr = kr[:lay.t_ctx, QK_NOPE:QK_DIM].reshape(lay.n_ctx_b, lay.ctx_len, QK_ROPE)
    return routed, new_ckv, new_kr


def kernel(x_prompt, x_sample, c, cache_ckv, cache_krope, state_ssm_re, state_ssm_im, c_ctx, w_mod, b_mod, norm1_w, norm2_w, mla_w_q_a, mla_q_a_norm, mla_w_q_b, mla_w_kv_a, mla_kv_a_norm, mla_w_kv_b, mla_q_norm, mla_k_norm, mla_w_o, ssm_w_in, ssm_a_re, ssm_a_im, ssm_log_dt, ssm_b_re, ssm_b_im, ssm_c_re, ssm_c_im, ssm_d, ssm_w_glu, w_router, b_router, moe_w_gate, moe_w_up, moe_w_down):
    n_ctx_b, ctx_len, d = x_prompt.shape
    n_lat_b, lat_len, _ = x_sample.shape
    depth = w_mod.shape[0]
    lay = Layout(n_ctx_b, ctx_len, n_lat_b, lat_len, tm=512)
    assert n_lat_b + 1 <= 8
    y = (x_prompt.reshape(-1, d), x_sample.reshape(-1, d))
    cond8 = jnp.pad(jnp.concatenate([c_ctx[None, :], c], axis=0), ((0, 7 - n_lat_b), (0, 0)))
    mods = adaln_all(cond8, w_mod, b_mod)
    mods = jnp.pad(mods.reshape(depth, 8, 6, d), ((0, 0), (0, 0), (0, 2), (0, 0)))
    cos_t, sin_t = _rope_tables(lat_len, lay.tm)
    n_exp = b_router.shape[0]
    wr_hi = w_router.astype(BF16)
    wr_lo = (w_router - wr_hi.astype(F32)).astype(BF16)
    wr_t = jnp.concatenate([wr_hi.T, wr_lo.T], axis=0)
    br = b_router.reshape(n_exp, 1)
    tri = jnp.asarray(np.triu(np.ones((lay.tm, lay.tm), np.float32), k=1), BF16)
    ckv_out, kr_out, sre_out, sim_out = [], [], [], []
    for i in range(depth):
        j = i // 2
        mod = mods[i]
        nw1 = norm1_w[i].reshape(1, d)
        route = (norm2_w[i].reshape(1, d), wr_t, br, tri)
        if i % 2 == 0:
            routed, ckv_p, kr_p = mla_layer(lay, y, mod, nw1, route, cache_ckv[:, j], cache_krope[:, j], cos_t, sin_t,
                                            mla_w_q_a[j], mla_q_a_norm[j], mla_w_q_b[j], mla_w_kv_a[j],
                                            mla_kv_a_norm[j], mla_w_kv_b[j], mla_q_norm[j], mla_k_norm[j], mla_w_o[j])
            ckv_out.append(ckv_p)
            kr_out.append(kr_p)
        else:
            routed, s_re, s_im = s5_layer(lay, y, mod, nw1, route, state_ssm_re[:, j], state_ssm_im[:, j], ssm_w_in[j],
                                          ssm_a_re[j], ssm_a_im[j], ssm_log_dt[j], ssm_b_re[j], ssm_b_im[j],
                                          ssm_c_re[j], ssm_c_im[j], ssm_d[j], ssm_w_glu[j])
            sre_out.append(s_re)
            sim_out.append(s_im)
        y = moe_layer(lay, routed, mod, i, moe_w_gate, moe_w_up, moe_w_down, tmm=1024, pair_out=(i == depth - 1))
    yp = y[0].reshape(n_ctx_b, ctx_len, d)
    ys = y[1].reshape(n_lat_b, lat_len, d)
    return (yp, ys, jnp.stack(ckv_out, axis=1), jnp.stack(kr_out, axis=1),
            jnp.stack(sre_out, axis=1), jnp.stack(sim_out, axis=1))
```

```python
import functools
import math

import jax
import jax.numpy as jnp
import numpy as np
from jax import lax
from jax.experimental import pallas as pl
from jax.experimental.pallas import tpu as pltpu

F32 = jnp.float32
BF16 = jnp.bfloat16
EPS = 1e-6

GRID_W = 64
N_HEADS = 8
QK_NOPE = 64
QK_ROPE = 32
QK_DIM = QK_NOPE + QK_ROPE
V_DIM = 64
ROPE_BASE = 10000.0
SSM_GROUP = 16
SSM_STATE = 64
N_EXPERT_GROUPS = 4
EXPERTS_PER_GROUP = 4

LANES = 128
HEAD_PAD = LANES
SSM_CHUNK = 16
SEG_ALIGN = 16
VMEM_LIMIT = 48 * 1024 * 1024


def _cparams(sem, vmem=VMEM_LIMIT):
    return pltpu.CompilerParams(dimension_semantics=sem, vmem_limit_bytes=vmem)


def _dot(a, b):
    return jnp.dot(a, b, preferred_element_type=F32)


def _dot_nt(a, b):
    return lax.dot_general(a, b, (((1,), (1,)), ((), ())), preferred_element_type=F32)


def _dot_hi(a, b):
    return jnp.dot(a, b, preferred_element_type=F32, precision=lax.Precision.HIGHEST)


def _norm_mod(x, nw, shift, scale):
    ms = jnp.mean(x * x, axis=-1, keepdims=True)
    return (x * lax.rsqrt(ms + EPS) * nw) * (1.0 + scale) + shift


class Layout:
    def __init__(self, n_ctx_b, ctx_len, n_lat_b, lat_len, tm):
        self.n_ctx_b, self.ctx_len, self.n_lat_b, self.lat_len = n_ctx_b, ctx_len, n_lat_b, lat_len
        self.t_ctx = n_ctx_b * ctx_len
        self.t_lat = n_lat_b * lat_len
        self.t = self.t_ctx + self.t_lat
        self.tm = tm
        assert self.t_ctx % tm == 0 and lat_len % tm == 0
        self.ctx_tiles = self.t_ctx // tm
        self.lat_tiles_per_b = lat_len // tm
        self.n_tiles = self.t // tm

    def seg(self, i):
        return jnp.where(i < self.ctx_tiles, 0, 1 + (i - self.ctx_tiles) // self.lat_tiles_per_b)

    def rope_blk(self, i):
        return jnp.where(i < self.ctx_tiles, 0, 1 + (i - self.ctx_tiles) % self.lat_tiles_per_b)

    def stream_specs(self, d, pair):
        tm = self.tm
        if not pair:
            return [pl.BlockSpec((tm, d), lambda i, *_: (i, 0))]
        return [pl.BlockSpec((tm, d), lambda i, *_: (jnp.minimum(i, self.ctx_tiles - 1), 0)),
                pl.BlockSpec((tm, d), lambda i, *_: (jnp.maximum(i - self.ctx_tiles, 0), 0))]

    def stream_load(self, refs):
        if len(refs) == 1:
            return refs[0][...]
        return jnp.where(pl.program_id(0) < self.ctx_tiles, refs[0][...], refs[1][...])

    def stream_store(self, refs, val):
        if len(refs) == 1:
            refs[0][...] = val
            return
        i = pl.program_id(0)

        @pl.when(i < self.ctx_tiles)
        def _():
            refs[0][...] = val

        @pl.when(i >= self.ctx_tiles)
        def _():
            refs[1][...] = val


def _as_list(y):
    return list(y) if isinstance(y, (tuple, list)) else [y]


def _adaln_kernel(c_ref, w_ref, b_ref, o_ref):
    c = c_ref[...]
    s = c * jax.nn.sigmoid(c)
    o_ref[...] = _dot_hi(s, w_ref[...]) + b_ref[...]


def adaln_all(cond8, w_mod, b_mod, tn=1536):
    depth, d, n6 = w_mod.shape
    return pl.pallas_call(
        _adaln_kernel,
        out_shape=jax.ShapeDtypeStruct((depth, 8, n6), F32),
        grid=(depth, n6 // tn),
        in_specs=[pl.BlockSpec((8, d), lambda l, j: (0, 0)),
                  pl.BlockSpec((None, d, tn), lambda l, j: (l, 0, j)),
                  pl.BlockSpec((None, 1, tn), lambda l, j: (l, 0, j))],
        out_specs=pl.BlockSpec((None, 8, tn), lambda l, j: (l, 0, j)),
        compiler_params=_cparams(("parallel", "parallel")),
        name="adaln",
    )(cond8, w_mod, b_mod.reshape(depth, 1, n6))


def _kv_heads(ckv_b, kr, krp, wk_ref, wv_ref, knw, knwp, cos, sin, k_ref, v_ref):
    kc = knw * cos
    ks = knwp * sin
    for h in range(N_HEADS):
        kz = _dot(ckv_b, wk_ref[h]) + kr
        r = lax.rsqrt(jnp.sum(kz * kz, axis=-1, keepdims=True) * (1.0 / QK_DIM) + EPS)
        k_ref[h] = (r * (kz * kc + krp * ks)).astype(BF16)
    for hp in range(N_HEADS // 2):
        v_ref[hp] = _dot(ckv_b, wv_ref[hp]).astype(BF16)


def _mla_qkv_kernel(*refs, lay, n_y, q_lora, kv_lora):
    y_refs = refs[:n_y]
    (mod_ref, nw_ref, w_ref, qan_ref, kvan_ref, wq_ref, wk_ref, wv_ref, nrm_ref, cos_ref, sin_ref,
     ckv_ref, kr_ref, q_ref, k_ref, v_ref) = refs[n_y:]
    m = mod_ref[...]
    h = _norm_mod(lay.stream_load(y_refs), nw_ref[...], m[0:1], m[1:2])
    z = _dot(h.astype(BF16), w_ref[...])
    cq = z[:, :q_lora]
    cq = (cq * lax.rsqrt(jnp.mean(cq * cq, axis=-1, keepdims=True) + EPS) * qan_ref[...]).astype(BF16)
    ckv = z[:, q_lora:q_lora + kv_lora]
    ckv = ckv * lax.rsqrt(jnp.mean(ckv * ckv, axis=-1, keepdims=True) + EPS) * kvan_ref[...]
    ckv_ref[...] = ckv
    kr = z[:, q_lora + kv_lora:q_lora + kv_lora + LANES]
    kr_ref[...] = kr
    krp = z[:, q_lora + kv_lora + LANES:]
    nrm = nrm_ref[...]
    cos = cos_ref[...]
    sin = sin_ref[...]
    qc = nrm[0:1] * cos * (QK_DIM ** -0.5)
    qs = nrm[1:2] * sin * (QK_DIM ** -0.5)
    for hd in range(N_HEADS):
        zq = _dot(cq, wq_ref[hd])
        qm = zq[:, :HEAD_PAD]
        qp = zq[:, HEAD_PAD:]
        r = lax.rsqrt(jnp.sum(qm * qm, axis=-1, keepdims=True) * (1.0 / QK_DIM) + EPS)
        q_ref[hd] = (r * (qm * qc + qp * qs)).astype(BF16)
    _kv_heads(ckv.astype(BF16), kr, krp, wk_ref, wv_ref, nrm[2:3], nrm[3:4], cos, sin, k_ref, v_ref)


def mla_qkv(lay, y, mod, nw, w_a, qan, kvan, wq, wk, wv, nrm, cos_t, sin_t, q_lora, kv_lora):
    ys = _as_list(y)
    t, d = lay.t, ys[0].shape[1]
    tm = lay.tm
    row = lambda i: (i, 0)
    full = lambda i: (0, 0)
    hrow = lambda i: (0, i, 0)
    full3 = lambda i: (0, 0, 0)
    rope = lambda i: (lay.rope_blk(i), 0)
    nh = N_HEADS
    return pl.pallas_call(
        functools.partial(_mla_qkv_kernel, lay=lay, n_y=len(ys), q_lora=q_lora, kv_lora=kv_lora),
        out_shape=(jax.ShapeDtypeStruct((t, kv_lora), F32), jax.ShapeDtypeStruct((t, LANES), F32),
                   jax.ShapeDtypeStruct((nh, t, HEAD_PAD), BF16), jax.ShapeDtypeStruct((nh, t, HEAD_PAD), BF16),
                   jax.ShapeDtypeStruct((nh // 2, t, LANES), BF16)),
        grid=(lay.n_tiles,),
        in_specs=lay.stream_specs(d, len(ys) == 2) + [
            pl.BlockSpec((None, 8, d), lambda i: (lay.seg(i), 0, 0)),
            pl.BlockSpec((1, d), full), pl.BlockSpec(w_a.shape, full),
            pl.BlockSpec((1, q_lora), full), pl.BlockSpec((1, kv_lora), full),
            pl.BlockSpec(wq.shape, full3), pl.BlockSpec(wk.shape, full3), pl.BlockSpec(wv.shape, full3),
            pl.BlockSpec((8, LANES), full),
            pl.BlockSpec((tm, LANES), rope), pl.BlockSpec((tm, LANES), rope)],
        out_specs=(pl.BlockSpec((tm, kv_lora), row), pl.BlockSpec((tm, LANES), row),
                   pl.BlockSpec((nh, tm, HEAD_PAD), hrow), pl.BlockSpec((nh, tm, HEAD_PAD), hrow),
                   pl.BlockSpec((nh // 2, tm, LANES), hrow)),
        compiler_params=_cparams(("parallel",)),
        name="mla_qkv",
    )(*ys, mod, nw, w_a, qan, kvan, wq, wk, wv, nrm, cos_t, sin_t)


def _ctx_kv_kernel(ckv_ref, kr_ref, wk_ref, wv_ref, nrm_ref, k_ref, v_ref):
    nrm = nrm_ref[...]
    kr = kr_ref[...]
    one = jnp.ones((1, LANES), F32)
    _kv_heads(ckv_ref[...].astype(BF16), kr, kr, wk_ref, wv_ref, nrm[2:3], nrm[3:4],
              one, jnp.zeros((1, LANES), F32), k_ref, v_ref)


def ctx_kv(ckv, kr, wk, wv, nrm, tm):
    t = ckv.shape[0]
    row = lambda i: (i, 0)
    hrow = lambda i: (0, i, 0)
    full3 = lambda i: (0, 0, 0)
    nh = N_HEADS
    return pl.pallas_call(
        _ctx_kv_kernel,
        out_shape=(jax.ShapeDtypeStruct((nh, t, HEAD_PAD), BF16), jax.ShapeDtypeStruct((nh // 2, t, LANES), BF16)),
        grid=(t // tm,),
        in_specs=[pl.BlockSpec((tm, ckv.shape[1]), row), pl.BlockSpec((tm, LANES), row),
                  pl.BlockSpec(wk.shape, full3), pl.BlockSpec(wv.shape, full3),
                  pl.BlockSpec((8, LANES), lambda i: (0, 0))],
        out_specs=(pl.BlockSpec((nh, tm, HEAD_PAD), hrow), pl.BlockSpec((nh // 2, tm, LANES), hrow)),
        compiler_params=_cparams(("parallel",)),
        name="ctx_kv",
    )(ckv, kr, wk, wv, nrm)


def _attn_kernel(*refs, with_ctx, kblk):
    if with_ctx:
        q_ref, k_ref, v_ref, kc_ref, vc_ref, o_ref = refs
    else:
        q_ref, k_ref, v_ref, o_ref = refs
    seq = k_ref.shape[1]
    n_heads = q_ref.shape[0]
    blocks = [(kc_ref, vc_ref, 0, kc_ref.shape[1])] if with_ctx else []
    blocks += [(k_ref, v_ref, b0, kblk) for b0 in range(0, seq, kblk)]
    items = [(j, blk) for j in range(n_heads) for blk in blocks]

    def scores(item):
        j, (kr, _, b0, n) = item
        return _dot_nt(q_ref[j], kr[j, b0:b0 + n, :])

    outs = []
    m = l = acc = None
    s_next = scores(items[0])
    for idx, (j, (_, vr, b0, n)) in enumerate(items):
        s = s_next
        if idx + 1 < len(items):
            s_next = scores(items[idx + 1])
        mb = jnp.max(s, axis=-1, keepdims=True)
        m_new = mb if m is None else jnp.maximum(m, mb)
        p = jnp.exp(s - m_new)
        pv = _dot(p.astype(BF16), vr[j // 2, b0:b0 + n, :])
        ps = jnp.sum(p, axis=-1, keepdims=True)
        if m is None:
            l, acc = ps, pv
        else:
            a = jnp.exp(m - m_new)
            l, acc = a * l + ps, a * acc + pv
        m = m_new
        if idx + 1 == len(items) or items[idx + 1][0] != j:
            outs.append(acc / l)
            m = l = acc = None
    lane = lax.broadcasted_iota(jnp.int32, outs[0].shape, 1)
    for pr in range(n_heads // 2):
        o_ref[:, pr * LANES:(pr + 1) * LANES] = jnp.where(lane < V_DIM, outs[2 * pr], outs[2 * pr + 1]).astype(BF16)


def attention(q, k, v2, row0, n_b, seq, tq, kc=None, vc2=None, ctx_len=0, kblk=1024, pairs=1):
    assert row0 % seq == 0 and seq % tq == 0
    nq = seq // tq
    qb0 = row0 // tq
    kb0 = row0 // seq
    with_ctx = kc is not None
    hp_n = 2 * pairs
    in_specs = [pl.BlockSpec((hp_n, tq, HEAD_PAD), lambda b, hp, qi: (hp, qb0 + b * nq + qi, 0)),
                pl.BlockSpec((hp_n, seq, HEAD_PAD), lambda b, hp, qi: (hp, kb0 + b, 0)),
                pl.BlockSpec((pairs, seq, LANES), lambda b, hp, qi: (hp, kb0 + b, 0))]
    args = [q, k, v2]
    if with_ctx:
        in_specs += [pl.BlockSpec((hp_n, ctx_len, HEAD_PAD), lambda b, hp, qi: (hp, b, 0)),
                     pl.BlockSpec((pairs, ctx_len, LANES), lambda b, hp, qi: (hp, b, 0))]
        args += [kc, vc2]
    return pl.pallas_call(
        functools.partial(_attn_kernel, with_ctx=with_ctx, kblk=min(seq, kblk)),
        out_shape=jax.ShapeDtypeStruct((n_b * seq, (N_HEADS // 2) * LANES), BF16),
        grid=(n_b, N_HEADS // (2 * pairs), nq),
        in_specs=in_specs,
        out_specs=pl.BlockSpec((tq, pairs * LANES), lambda b, hp, qi: (b * nq + qi, hp)),
        compiler_params=_cparams(("parallel", "parallel", "parallel")),
        name="attn_ctx" if with_ctx else "attn",
    )(*args)


SSM_GPT = LANES // SSM_GROUP
SUBLANES = 8


def _lane_block_transpose(vs):
    n = len(vs)
    blk = lax.broadcasted_iota(jnp.int32, vs[0].shape, 1) // SSM_GROUP
    d = n // 2
    while d >= 1:
        keep = (blk & d) == 0
        new = list(vs)
        for a in range(n):
            if a & d == 0:
                b = a + d
                new[a] = jnp.where(keep, vs[a], pltpu.roll(vs[b], d * SSM_GROUP, 1))
                new[b] = jnp.where(keep, pltpu.roll(vs[a], LANES - d * SSM_GROUP, 1), vs[b])
        vs = new
        d //= 2
    return vs


def _ssm_in_kernel(y_ref, mod_ref, nw_ref, w_ref, u_ref, us_ref, ug_ref):
    m = mod_ref[...]
    h = _norm_mod(y_ref[...], nw_ref[...], m[0:1], m[1:2])
    z = _dot(h.astype(BF16), w_ref[...])
    n_lt = us_ref.shape[0]
    for j in range(n_lt):
        us_ref[j] = z[:, j * LANES:(j + 1) * LANES]
    n_ch = us_ref.shape[1] // SSM_CHUNK
    for j in range(n_lt):
        for rb in range(n_ch // SUBLANES):
            for hf in range(SSM_CHUNK // SSM_GPT):
                src = [us_ref[j, pl.ds(rb * SUBLANES * SSM_CHUNK + hf * SSM_GPT + tt, SUBLANES, stride=SSM_CHUNK), :]
                       for tt in range(SSM_GPT)]
                for gl, v in enumerate(_lane_block_transpose(src)):
                    ug_ref[j * SSM_GPT + gl, rb * SUBLANES:(rb + 1) * SUBLANES, hf * LANES:(hf + 1) * LANES] = v
    u_ref[...] = ug_ref[...].astype(BF16)


def ssm_in(lay, y, mod, nw, w_in):
    t, d = y.shape
    tm = lay.tm
    n = w_in.shape[1]
    g = n // SSM_GROUP
    n_ch = tm // SSM_CHUNK
    qk = SSM_CHUNK * SSM_GROUP
    row = lambda i: (i, 0)
    return pl.pallas_call(
        _ssm_in_kernel,
        out_shape=jax.ShapeDtypeStruct((g, t // SSM_CHUNK, qk), BF16),
        grid=(lay.n_tiles,),
        in_specs=[pl.BlockSpec((tm, d), row), pl.BlockSpec((None, 8, d), lambda i: (lay.seg(i), 0, 0)),
                  pl.BlockSpec((1, d), lambda i: (0, 0)), pl.BlockSpec((d, n), lambda i: (0, 0))],
        out_specs=pl.BlockSpec((g, n_ch, qk), lambda i: (0, i, 0)),
        scratch_shapes=[pltpu.VMEM((n // LANES, tm, LANES), F32), pltpu.VMEM((g, n_ch, qk), F32)],
        compiler_params=_cparams(("parallel",)),
        name="ssm_in",
    )(y, mod, nw, w_in)


def _ssm_prep_kernel(ar_ref, ai_ref, ld_ref, arc_ref, aic_ref, ldc_ref, btr_ref, bti_ref, ctr_ref, cti_ref, dsk_ref,
                     mt_ref, ett_ref, ft_ref, coef_ref):
    q = SSM_CHUNK
    k = SSM_GROUP
    p = SSM_STATE
    qk = q * k
    fwd = pl.program_id(0) == 0
    dt = jnp.exp(ld_ref[...])
    are = ar_ref[...]
    aim = ai_ref[...]
    mag = jnp.exp(dt * are)
    abr = mag * jnp.cos(dt * aim)
    abi = mag * jnp.sin(dt * aim)
    den = are * are + aim * aim
    nr = abr - 1.0
    cf_re = (nr * are + abi * aim) / den
    cf_im = (abi * are - nr * aim) / den
    btr = btr_ref[...]
    bti = bti_ref[...]
    bbt_re = cf_re * btr - cf_im * bti
    bbt_im = cf_re * bti + cf_im * btr
    bbt_re_t = jnp.concatenate([bbt_re] * q, axis=0)
    bbt_im_t = jnp.concatenate([bbt_im] * q, axis=0)
    s_idx = lax.broadcasted_iota(jnp.int32, (q, 1), 0).astype(F32)
    pw = jnp.where(fwd, (q - 1.0) - s_idx, s_idx)
    pm = jnp.exp(pw * dt * are)
    rep_rows = lambda x: jnp.concatenate([jnp.broadcast_to(x[s:s + 1], (k, p)) for s in range(q)], axis=0)
    pr = rep_rows(pm * jnp.cos(pw * dt * aim))
    pi = rep_rows(pm * jnp.sin(pw * dt * aim))
    et_re = pr * bbt_re_t - pi * bbt_im_t
    et_im = pr * bbt_im_t + pi * bbt_re_t
    ett_ref[...] = jnp.concatenate([et_re, et_im, et_im, et_re], axis=1).astype(BF16)
    qm = jnp.exp(q * dt * are)
    aq_re = qm * jnp.cos(q * dt * aim)
    aq_im = qm * jnp.sin(q * dt * aim)
    c1 = jnp.concatenate([aq_re, aq_re], axis=1)
    c2 = jnp.concatenate([-aq_im, aq_im], axis=1)
    coef_ref[...] = jnp.concatenate([c1, c2, jnp.zeros((6, 2 * p), F32)], axis=0)
    dtc = jnp.exp(ldc_ref[...])
    arec = arc_ref[...]
    aimc = aic_ref[...]
    ctr = ctr_ref[...]
    cti = cti_ref[...]
    tau = lax.broadcasted_iota(jnp.int32, (1, q), 1).astype(F32)
    tau0 = jnp.where(fwd, tau, (q - 1.0) - tau)
    m0 = jnp.exp(tau0 * dtc * arec)
    p0r_q = m0 * jnp.cos(tau0 * dtc * aimc)
    p0i_q = m0 * jnp.sin(tau0 * dtc * aimc)
    magc = jnp.exp(dtc * arec)
    abr_c = magc * jnp.cos(dtc * aimc)
    abi_c = magc * jnp.sin(dtc * aimc)
    spread = jnp.where(lax.broadcasted_iota(jnp.int32, (q, qk), 1) // k == lax.broadcasted_iota(jnp.int32, (q, qk), 0),
                       1.0, 0.0)
    p0r = _dot_hi(p0r_q, spread)
    p0i = _dot_hi(p0i_q, spread)
    p1r = _dot_hi(p0r_q * abr_c - p0i_q * abi_c, spread)
    p1i = _dot_hi(p0r_q * abi_c + p0i_q * abr_c, spread)
    ft_ref[...] = jnp.concatenate([ctr * p1r - cti * p1i, -(ctr * p1i + cti * p1r)], axis=0).astype(BF16)
    left_re = ctr * p0r - cti * p0i
    left_im = ctr * p0i + cti * p0r
    kt = _dot_hi(bbt_re, left_re) - _dot_hi(bbt_im, left_im)
    lane = lax.broadcasted_iota(jnp.int32, (k, qk), 1)
    rowi = lax.broadcasted_iota(jnp.int32, (k, qk), 0)
    kt_f = kt + jnp.where(lane == rowi, dsk_ref[...], 0.0)
    rows_f, rows_b = [], []
    for s in range(q):
        rows_f.append(kt_f if s == 0 else jnp.where(lane >= s * k, pltpu.roll(kt_f, s * k, 1), 0.0))
        sh = ((s + 1) * k) % qk
        rows_b.append(kt if sh == 0 else jnp.where(lane < (s + 1) * k, pltpu.roll(kt, sh, 1), 0.0))
    mt = jnp.where(fwd, jnp.concatenate(rows_f, axis=0), jnp.concatenate(rows_b, axis=0))
    mt_ref[...] = mt.astype(BF16)


def ssm_prep(a_re, a_im, log_dt, b_re, b_im, c_re, c_im, d_skip):
    nd, g, p = a_re.shape
    k = b_re.shape[-1]
    qk = SSM_CHUNK * k
    ld = jnp.broadcast_to(log_dt[..., None], (nd, g, p))
    dsk = jnp.broadcast_to(jnp.tile(d_skip.reshape(g, 1, k), (1, 1, SSM_CHUNK))[None], (nd, g, 1, qk))
    rowv = lambda x: x.reshape(nd, g, 1, p)
    colv = lambda x: x.reshape(nd, g, p, 1)
    bt = lambda x: jnp.swapaxes(x, -1, -2)
    ct = lambda x: jnp.tile(jnp.swapaxes(x, -1, -2), (1, 1, 1, SSM_CHUNK))
    spec = lambda r, c: pl.BlockSpec((None, None, r, c), lambda d, j: (d, j, 0, 0))
    return pl.pallas_call(
        _ssm_prep_kernel,
        out_shape=(jax.ShapeDtypeStruct((nd, g, qk, qk), BF16), jax.ShapeDtypeStruct((nd, g, qk, 4 * p), BF16),
                   jax.ShapeDtypeStruct((nd, g, 2 * p, qk), BF16), jax.ShapeDtypeStruct((nd, g, 8, 2 * p), F32)),
        grid=(nd, g),
        in_specs=[spec(1, p), spec(1, p), spec(1, p), spec(p, 1), spec(p, 1), spec(p, 1),
                  spec(k, p), spec(k, p), spec(p, qk), spec(p, qk), spec(1, qk)],
        out_specs=(spec(qk, qk), spec(qk, 4 * p), spec(2 * p, qk), spec(8, 2 * p)),
        compiler_params=_cparams(("parallel", "parallel")),
        name="ssm_prep",
    )(rowv(a_re), rowv(a_im), rowv(ld), colv(a_re), colv(a_im), colv(ld), bt(b_re), bt(b_im), ct(c_re), ct(c_im), dsk)


def _ssm_scan_kernel(u_ref, mt_ref, ett_ref, ft_ref, coef_ref, x0_ref, x0s_ref, y_ref, hfin_ref,
                     sx_ref, sxs_ref, hp_ref, *, gpb, ctx_b, ctx_chunks, lat_b, lat_chunks, rchunk):
    w = 2 * SSM_STATE
    n_rows = u_ref.shape[1]
    ctx_rows = ctx_b * ctx_chunks
    for d in range(2):
        for g in range(gpb):
            for r0 in range(0, n_rows, rchunk):
                ss = _dot(u_ref[g, r0:r0 + rchunk, :], ett_ref[d, g])
                sx_ref[g, r0:r0 + rchunk, :] = ss[:, :w]
                sxs_ref[g, r0:r0 + rchunk, :] = ss[:, w:]
        c1 = [coef_ref[d, g][0:1] for g in range(gpb)]
        c2 = [coef_ref[d, g][1:2] for g in range(gpb)]

        def make_body(base, nb, nc):
            def body(i, carry):
                c = i if d == 0 else nc - 1 - i
                rows = pl.ds(base + c, nb, stride=nc)
                out = []
                for g in range(gpb):
                    x, xs = carry[g]
                    hp_ref[g, rows, :] = x
                    xn = c1[g] * x + c2[g] * xs + sx_ref[g, rows, :]
                    xsn = c1[g] * xs - c2[g] * x + sxs_ref[g, rows, :]
                    out.append((xn, xsn))
                return tuple(out)
            return body

        z = jnp.zeros((ctx_b, w), F32)
        fin = lax.fori_loop(0, ctx_chunks, make_body(0, ctx_b, ctx_chunks), tuple((z, z) for _ in range(gpb)))
        hfin_ref[d] = jnp.concatenate([fin[g][0] for g in range(gpb)], axis=1)
        init = tuple((x0_ref[d][:, g * w:(g + 1) * w], x0s_ref[d][:, g * w:(g + 1) * w]) for g in range(gpb))
        lax.fori_loop(0, lat_chunks, make_body(ctx_rows, lat_b, lat_chunks), init, unroll=4)
        for g in range(gpb):
            for r0 in range(0, n_rows, rchunk):
                y = _dot(u_ref[g, r0:r0 + rchunk, :], mt_ref[d, g])
                y = y + _dot(hp_ref[g, r0:r0 + rchunk, :].astype(BF16), ft_ref[d, g])
                if d == 0:
                    y_ref[g, r0:r0 + rchunk, :] = y
                else:
                    y_ref[g, r0:r0 + rchunk, :] += y


def ssm_scan(u, mt, ett, ft, coef, x0, x0s, ctx_b, ctx_chunks, lat_b, lat_chunks, gpb=4):
    g, n_rows, qk = u.shape
    w = 2 * SSM_STATE
    rchunk = math.gcd(n_rows, 512)
    assert n_rows == ctx_b * ctx_chunks + lat_b * lat_chunks
    ublk = pl.BlockSpec((gpb, n_rows, qk), lambda j: (j, 0, 0))
    blk = lambda r, c: pl.BlockSpec((2, gpb, r, c), lambda j: (0, j, 0, 0))
    vec = lambda r: pl.BlockSpec((2, r, gpb * w), lambda j: (0, 0, j))
    return pl.pallas_call(
        functools.partial(_ssm_scan_kernel, gpb=gpb, ctx_b=ctx_b, ctx_chunks=ctx_chunks, lat_b=lat_b,
                          lat_chunks=lat_chunks, rchunk=rchunk),
        out_shape=(jax.ShapeDtypeStruct((g, n_rows, qk), F32), jax.ShapeDtypeStruct((2, ctx_b, g * w), F32)),
        grid=(g // gpb,),
        in_specs=[ublk, blk(qk, qk), blk(qk, 2 * w), blk(w, qk), blk(8, w), vec(lat_b), vec(lat_b)],
        out_specs=(ublk, vec(ctx_b)),
        scratch_shapes=[pltpu.VMEM((gpb, n_rows, w), F32), pltpu.VMEM((gpb, n_rows, w), F32),
                        pltpu.VMEM((gpb, n_rows, w), F32)],
        compiler_params=_cparams(("parallel",)),
        name="ssm_scan",
    )(u, mt, ett, ft, coef, x0, x0s)


def _ssm_out_kernel(y_ref, mod_ref, yc_ref, w_ref, nw_ref, wr_ref, br_ref, tri_ref, out_ref, h_ref, ri_ref, pc_ref,
                    s_ref, *, d_model):
    n_ch = yc_ref.shape[1]
    n_lt = s_ref.shape[0]
    for j in range(n_lt):
        for rb in range(n_ch // SUBLANES):
            for hf in range(SSM_CHUNK // SSM_GPT):
                src = [yc_ref[j * SSM_GPT + gl, rb * SUBLANES:(rb + 1) * SUBLANES, hf * LANES:(hf + 1) * LANES]
                       for gl in range(SSM_GPT)]
                for tt, v in enumerate(_lane_block_transpose(src)):
                    s_ref[j, pl.ds(rb * SUBLANES * SSM_CHUNK + hf * SSM_GPT + tt, SUBLANES, stride=SSM_CHUNK), :] = v
    s = jnp.concatenate([s_ref[j] for j in range(n_lt)], axis=1)
    a = jax.nn.gelu(s, approximate=True).astype(BF16)
    z = _dot(a, w_ref[...])
    m = mod_ref[...]
    y = y_ref[...] + m[2:3] * (z[:, :d_model] * jax.nn.sigmoid(z[:, d_model:]))
    out_ref[...] = y
    _route_core(y, m, nw_ref, wr_ref, br_ref, tri_ref, h_ref, ri_ref, pc_ref)


def ssm_out_route(lay, y, mod, ych, w_glu, route):
    t, d = y.shape
    tm = lay.tm
    g, _, qk = ych.shape
    n_ch = tm // SSM_CHUNK
    row = lambda i: (i, 0)
    r_args, r_in, r_shape, r_out = _route_io(lay, d, *route)
    return pl.pallas_call(
        functools.partial(_ssm_out_kernel, d_model=d),
        out_shape=[jax.ShapeDtypeStruct((t, d), F32)] + r_shape,
        grid=(lay.n_tiles,),
        in_specs=[pl.BlockSpec((tm, d), row), pl.BlockSpec((None, 8, d), lambda i: (lay.seg(i), 0, 0)),
                  pl.BlockSpec((g, n_ch, qk), lambda i: (0, i, 0)),
                  pl.BlockSpec(w_glu.shape, lambda i: (0, 0))] + r_in,
        out_specs=[pl.BlockSpec((tm, d), row)] + r_out,
        scratch_shapes=[pltpu.VMEM((g * SSM_GROUP // LANES, tm, LANES), F32)],
        input_output_aliases={0: 0},
        compiler_params=_cparams(("parallel",)),
        name="ssm_out_route",
    )(y, mod, ych, w_glu, *r_args)


def s5_layer(lay, y, mod, nw, route, state_re, state_im, w_in, a_re, a_im, log_dt, b_re, b_im, c_re, c_im, d_skip,
             w_glu):
    uch = ssm_in(lay, y, mod, nw, w_in.astype(BF16))
    g = a_re.shape[1]
    p = SSM_STATE
    mt, ett, ft, coef = ssm_prep(a_re, a_im, log_dt, b_re, b_im, c_re, c_im, d_skip)
    st = jnp.concatenate([state_re, state_im], axis=-1)
    sts = jnp.concatenate([state_im, state_re], axis=-1)
    x0 = jnp.transpose(st, (1, 0, 2, 3)).reshape(2, lay.n_lat_b, g * 2 * p)
    x0s = jnp.transpose(sts, (1, 0, 2, 3)).reshape(2, lay.n_lat_b, g * 2 * p)
    ych, hfin = ssm_scan(uch, mt, ett, ft, coef, x0, x0s, lay.n_ctx_b, lay.ctx_len // SSM_CHUNK,
                         lay.n_lat_b, lay.lat_len // SSM_CHUNK)
    routed = ssm_out_route(lay, y, mod, ych, w_glu.astype(BF16), route)
    hf = hfin.reshape(2, lay.n_ctx_b, g, 2, p)
    new_re = jnp.transpose(hf[:, :, :, 0], (1, 0, 2, 3))
    new_im = jnp.transpose(hf[:, :, :, 1], (1, 0, 2, 3))
    return routed, new_re, new_im


def _route_core(y, m, nw_ref, wr_ref, br_ref, tri_ref, h_ref, ri_ref, pc_ref):
    n_exp = br_ref.shape[0]
    h = _norm_mod(y, nw_ref[...], m[3:4], m[4:5])
    h_hi = h.astype(BF16)
    h_ref[...] = h_hi
    h_lo = (h - h_hi.astype(F32)).astype(BF16)
    wr = wr_ref[...]
    lt = _dot_nt(wr, h_hi)
    logits = lt[:n_exp] + lt[n_exp:] + _dot_nt(wr[:n_exp], h_lo)
    scores = jax.nn.sigmoid(logits)
    sel = scores + br_ref[...]
    epg = EXPERTS_PER_GROUP
    row = lambda x, e: x[e:e + 1, :]
    gscore = []
    for g in range(N_EXPERT_GROUPS):
        a, b, c, d = (row(sel, g * epg + j) for j in range(epg))
        m1, n1, m2, n2 = jnp.maximum(a, b), jnp.minimum(a, b), jnp.maximum(c, d), jnp.minimum(c, d)
        gscore.append(jnp.maximum(m1, m2) + jnp.maximum(jnp.minimum(m1, m2), jnp.maximum(n1, n2)))
    best = gscore[0]
    gi = jnp.zeros_like(best, dtype=jnp.int32)
    for g in range(1, N_EXPERT_GROUPS):
        better = gscore[g] > best
        gi = jnp.where(better, g, gi)
        best = jnp.where(better, gscore[g], best)

    def pick(x, j):
        out = row(x, j)
        for g in range(1, N_EXPERT_GROUPS):
            out = jnp.where(gi == g, row(x, g * epg + j), out)
        return out

    sv = [pick(sel, j) for j in range(epg)]
    cv = [pick(scores, j) for j in range(epg)]
    b1, i1, w1 = sv[0], jnp.zeros_like(gi), cv[0]
    for j in range(1, epg):
        better = sv[j] > b1
        i1 = jnp.where(better, j, i1)
        w1 = jnp.where(better, cv[j], w1)
        b1 = jnp.where(better, sv[j], b1)
    neg = jnp.full_like(b1, -jnp.inf)
    b2, i2, w2 = neg, jnp.zeros_like(gi), jnp.zeros_like(w1)
    for j in range(epg):
        better = (i1 != j) & (sv[j] > b2)
        i2 = jnp.where(better, j, i2)
        w2 = jnp.where(better, cv[j], w2)
        b2 = jnp.where(better, sv[j], b2)
    tot = w1 + w2
    e1 = gi * epg + i1
    e2 = gi * epg + i2
    eid = lax.broadcasted_iota(jnp.int32, logits.shape, 0)
    m1h = eid == e1
    m2h = eid == e2
    mc = jnp.where(m1h | m2h, 1.0, 0.0)
    pref = _dot(mc.astype(BF16), tri_ref[...])
    cnt = jnp.sum(mc, axis=1, keepdims=True)
    pc_al = jnp.ceil(cnt * (1.0 / SEG_ALIGN))
    pcb = jnp.broadcast_to(pc_al, (n_exp, LANES))
    er = lax.broadcasted_iota(jnp.int32, (n_exp, n_exp), 0)
    ec = lax.broadcasted_iota(jnp.int32, (n_exp, n_exp), 1)
    lower = jnp.where(ec < er, 1.0, 0.0).astype(BF16)
    seg = _dot(lower, pcb.astype(BF16))[:, 0:1] * SEG_ALIGN
    slot = seg + pref
    pos1 = jnp.sum(jnp.where(m1h, slot, 0.0), axis=0, keepdims=True)
    pos2 = jnp.sum(jnp.where(m2h, slot, 0.0), axis=0, keepdims=True)
    zero = jnp.zeros_like(pos1)
    ri_ref[...] = jnp.concatenate([pos1, pos2, w1 / tot, w2 / tot, zero, zero, zero, zero], axis=0)
    pc_ref[...] = pcb * SEG_ALIGN


def _route_io(lay, d, nw, wr_t, br, tri):
    tm = lay.tm
    n_exp = br.shape[0]
    full = lambda i: (0, 0)
    args = [nw, wr_t, br, tri]
    in_specs = [pl.BlockSpec((1, d), full), pl.BlockSpec(wr_t.shape, full), pl.BlockSpec((n_exp, 1), full),
                pl.BlockSpec((tm, tm), full)]
    out_shape = [jax.ShapeDtypeStruct((lay.t, d), BF16), jax.ShapeDtypeStruct((8, lay.t), F32),
                 jax.ShapeDtypeStruct((lay.n_tiles, n_exp, LANES), F32)]
    out_specs = [pl.BlockSpec((tm, d), lambda i: (i, 0)), pl.BlockSpec((8, tm), lambda i: (0, i)),
                 pl.BlockSpec((None, n_exp, LANES), lambda i: (i, 0, 0))]
    return args, in_specs, out_shape, out_specs


def _proj_route_kernel(*refs, lay, n_y):
    y_refs = refs[:n_y]
    o_refs = refs[n_y + 1:n_y + 3]
    mod_ref = refs[n_y]
    w_ref, nw_ref, wr_ref, br_ref, tri_ref, out_ref, h_ref, ri_ref, pc_ref = refs[n_y + 3:]
    m = mod_ref[...]
    y = lay.stream_load(y_refs) + m[2:3] * _dot(lay.stream_load(o_refs), w_ref[...])
    out_ref[...] = y
    _route_core(y, m, nw_ref, wr_ref, br_ref, tri_ref, h_ref, ri_ref, pc_ref)


def proj_route(lay, y, mod, o_pair, w, route):
    ys = _as_list(y)
    d = ys[0].shape[1]
    tm = lay.tm
    row = lambda i: (i, 0)
    r_args, r_in, r_shape, r_out = _route_io(lay, d, *route)
    return pl.pallas_call(
        functools.partial(_proj_route_kernel, lay=lay, n_y=len(ys)),
        out_shape=[jax.ShapeDtypeStruct((lay.t, d), F32)] + r_shape,
        grid=(lay.n_tiles,),
        in_specs=lay.stream_specs(d, len(ys) == 2) + [pl.BlockSpec((None, 8, d), lambda i: (lay.seg(i), 0, 0))]
        + lay.stream_specs(o_pair[0].shape[1], True) + [pl.BlockSpec(w.shape, lambda i: (0, 0))] + r_in,
        out_specs=[pl.BlockSpec((tm, d), row)] + r_out,
        input_output_aliases={0: 0} if len(ys) == 1 else {},
        compiler_params=_cparams(("parallel",)),
        name="proj_route",
    )(*ys, mod, *o_pair, w, *r_args)


def _segment_pieces(i, off_ref, pc_ref, n_exp, sizes, fn):
    local = 0
    for e in range(n_exp):
        n = pc_ref[i * n_exp + e]
        g0 = off_ref[i * n_exp + e]

        def emit(size, e=e, n=n, g0=g0, local=local):
            above = n & (-2 * size)

            @pl.when((n & size) != 0)
            def _():
                fn(e, pl.multiple_of(local + above, SEG_ALIGN), pl.multiple_of(g0 + above, SEG_ALIGN), size)

        for size in sizes:
            emit(size)
        local = local + n


def _pieces(n, sizes):
    for size in sizes:
        yield (n & size) != 0, n & (-2 * size), size


def _moe_permute_kernel(off_ref, pc_ref, goff_ref, gn_ref, h_ref, ri_ref, xs_ref, ys_ref, zp_ref, sem, gsem,
                        *, n_exp, sizes, gap_sizes, rchunk):
    i = pl.program_id(0)
    last = pl.num_programs(0) - 1
    slot = i % 2
    ri = ri_ref[...]
    pos1 = ri[0:1]
    pos2 = ri[1:2]
    h = h_ref[...]
    n_loc = ys_ref.shape[1]
    for r0 in range(0, n_loc, rchunk):
        r = (lax.broadcasted_iota(jnp.int32, (rchunk, h.shape[0]), 0) + r0).astype(F32)
        p = jnp.where(pos1 == r, 1.0, 0.0) + jnp.where(pos2 == r, 1.0, 0.0)
        ys_ref[slot, r0:r0 + rchunk, :] = _dot(p.astype(BF16), h).astype(BF16)

    def copies(tile, sl, act):
        def piece(e, lrow, grow, size):
            act(pltpu.make_async_copy(ys_ref.at[sl, pl.ds(lrow, size)], xs_ref.at[pl.ds(grow, size)], sem.at[sl, e]))
        _segment_pieces(tile, off_ref, pc_ref, n_exp, sizes, piece)

    start = lambda cp: cp.start()
    wait = lambda cp: cp.wait()
    copies(i, slot, start)

    @pl.when(i > 0)
    def _():
        copies(i - 1, 1 - slot, wait)

    @pl.when(i == last)
    def _():
        copies(i, slot, wait)
        zp_ref[...] = jnp.zeros_like(zp_ref)

        def gaps():
            for e in range(n_exp):
                g0 = goff_ref[e]
                for present, above, size in _pieces(gn_ref[e], gap_sizes):
                    yield present, pltpu.make_async_copy(
                        zp_ref.at[pl.ds(0, size)], xs_ref.at[pl.ds(pl.multiple_of(g0 + above, SEG_ALIGN), size)],
                        gsem.at[e])

        for present, cp in gaps():
            pl.when(present)(cp.start)
        for present, cp in gaps():
            pl.when(present)(cp.wait)

        zrows = zp_ref.shape[0]
        tail0 = goff_ref[n_exp]
        tail = lambda c: pltpu.make_async_copy(
            zp_ref, xs_ref.at[pl.ds(pl.multiple_of(tail0 + c * zrows, zrows), zrows)], gsem.at[0])

        @pl.loop(0, gn_ref[n_exp])
        def _(c):
            tail(c).start()

        @pl.loop(0, gn_ref[n_exp])
        def _(c):
            tail(c).wait()


def moe_permute(lay, h, rinfo, off, pcs, gap_off, gap_n, n_sorted, n_loc, n_exp, tmm):
    t, d = h.shape
    tm = lay.tm
    sizes = [s for s in (1 << b for b in range(12, 3, -1)) if s <= tm]
    gap_sizes = [s for s in (1 << b for b in range(12, 3, -1)) if s < tmm]
    return pl.pallas_call(
        functools.partial(_moe_permute_kernel, n_exp=n_exp, sizes=sizes, gap_sizes=gap_sizes, rchunk=256),
        out_shape=jax.ShapeDtypeStruct((n_sorted, d), BF16),
        grid_spec=pltpu.PrefetchScalarGridSpec(
            num_scalar_prefetch=4, grid=(lay.n_tiles,),
            in_specs=[pl.BlockSpec((tm, d), lambda i, *_: (i, 0)), pl.BlockSpec((8, tm), lambda i, *_: (0, i))],
            out_specs=pl.BlockSpec(memory_space=pl.ANY),
            scratch_shapes=[pltpu.VMEM((2, n_loc, d), BF16), pltpu.VMEM((gap_sizes[0], d), BF16),
                            pltpu.SemaphoreType.DMA((2, n_exp)), pltpu.SemaphoreType.DMA((n_exp,))]),
        compiler_params=_cparams(("arbitrary",)),
        name="moe_permute",
    )(off, pcs, gap_off, gap_n, h, rinfo)


def _moe_expert_kernel(te_ref, tv_ref, tf_ref, xb_ref, nx_ref, x_ref, wg_hbm, wu_hbm, wd_hbm, z_ref,
                       wgs_ref, wus_ref, wds_ref, wgb_ref, wub_ref, wdb_ref, sem, *, layer):
    del xb_ref
    r = pl.program_id(0)

    def fetch(e):
        return (pltpu.make_async_copy(wg_hbm.at[layer, e], wgs_ref, sem.at[0]),
                pltpu.make_async_copy(wu_hbm.at[layer, e], wus_ref, sem.at[1]),
                pltpu.make_async_copy(wd_hbm.at[layer, e], wds_ref, sem.at[2]))

    @pl.when(r == 0)
    def _():
        for cp in fetch(te_ref[0]):
            cp.start()

    @pl.when(tf_ref[r] == 1)
    def _():
        for cp in fetch(te_ref[r]):
            cp.wait()
        wgb_ref[...] = wgs_ref[...].astype(BF16)
        wub_ref[...] = wus_ref[...].astype(BF16)
        wdb_ref[...] = wds_ref[...].astype(BF16)

        @pl.when(nx_ref[r] >= 0)
        def _():
            for cp in fetch(nx_ref[r]):
                cp.start()

    @pl.when(tv_ref[r] > 0)
    def _():
        x = x_ref[...]
        a = _dot(x, wgb_ref[...])
        b = _dot(x, wub_ref[...])
        hid = (a * jax.nn.sigmoid(a)) * b
        z_ref[...] = _dot(hid.astype(BF16), wdb_ref[...]).astype(BF16)

    @pl.when(tv_ref[r] == 0)
    def _():
        z_ref[...] = jnp.zeros_like(z_ref)


def moe_experts(xs, tile_expert, tile_valid, tile_first, x_block, next_expert, layer, wg, wu, wd, tmm):
    n_sorted, d = xs.shape
    f = wg.shape[-1]
    hbm = pl.BlockSpec(memory_space=pl.ANY)
    return pl.pallas_call(
        functools.partial(_moe_expert_kernel, layer=layer),
        out_shape=jax.ShapeDtypeStruct((n_sorted, d), BF16),
        grid_spec=pltpu.PrefetchScalarGridSpec(
            num_scalar_prefetch=5, grid=(n_sorted // tmm,),
            in_specs=[pl.BlockSpec((tmm, d), lambda r, te, tv, tf, xb, nx: (xb[r], 0)), hbm, hbm, hbm],
            out_specs=pl.BlockSpec((tmm, d), lambda r, te, tv, tf, xb, nx: (r, 0)),
            scratch_shapes=[pltpu.VMEM((d, f), F32), pltpu.VMEM((d, f), F32), pltpu.VMEM((f, d), F32),
                            pltpu.VMEM((d, f), BF16), pltpu.VMEM((d, f), BF16), pltpu.VMEM((f, d), BF16),
                            pltpu.SemaphoreType.DMA((3,))]),
        compiler_params=_cparams(("arbitrary",)),
        name="moe_experts",
    )(tile_expert, tile_valid, tile_first, x_block, next_expert, xs, wg, wu, wd)


def _moe_combine_kernel(off_ref, pc_ref, y_ref, mod_ref, rit_ref, zs_ref, *rest, lay, n_exp, sizes):
    out_refs, (zt_ref, sem) = rest[:-2], rest[-2:]
    i = pl.program_id(0)
    slot = i % 2

    def copies(tile, sl, act):
        def piece(e, lrow, grow, size):
            act(pltpu.make_async_copy(zs_ref.at[pl.ds(grow, size)], zt_ref.at[sl, pl.ds(lrow, size)], sem.at[sl, e]))
        _segment_pieces(tile, off_ref, pc_ref, n_exp, sizes, piece)

    def fetch(tile, sl):
        used = pc_ref[tile * n_exp]
        for e in range(1, n_exp):
            used = used + pc_ref[tile * n_exp + e]

        @pl.loop(used // SEG_ALIGN, zt_ref.shape[1] // SEG_ALIGN)
        def _(c):
            zt_ref[sl, pl.ds(pl.multiple_of(c * SEG_ALIGN, SEG_ALIGN), SEG_ALIGN), :] = jnp.zeros(
                (SEG_ALIGN, zt_ref.shape[2]), BF16)
        copies(tile, sl, lambda cp: cp.start())

    @pl.when(i == 0)
    def _():
        fetch(i, slot)

    @pl.when(i + 1 < pl.num_programs(0))
    def _():
        fetch(i + 1, 1 - slot)

    rit = rit_ref[...]
    n_loc = zt_ref.shape[1]
    r = lax.broadcasted_iota(jnp.int32, (rit.shape[0], n_loc), 1).astype(F32)
    pw = jnp.where(rit[:, 0:1] == r, rit[:, 2:3], 0.0) + jnp.where(rit[:, 1:2] == r, rit[:, 3:4], 0.0)
    pw = pw.astype(BF16)
    copies(i, slot, lambda cp: cp.wait())
    lay.stream_store(out_refs, y_ref[...] + mod_ref[...][5:6] * _dot(pw, zt_ref[slot]))


def moe_combine(lay, y, mod, rinfo_t, zs, off, pcs, n_loc, n_exp, pair_out):
    t, d = y.shape
    tm = lay.tm
    sizes = [s for s in (1 << b for b in range(12, 3, -1)) if s <= tm]
    if pair_out:
        out_shape = (jax.ShapeDtypeStruct((lay.t_ctx, d), F32), jax.ShapeDtypeStruct((lay.t_lat, d), F32))
    else:
        out_shape = (jax.ShapeDtypeStruct((t, d), F32),)
    out = pl.pallas_call(
        functools.partial(_moe_combine_kernel, lay=lay, n_exp=n_exp, sizes=sizes),
        out_shape=out_shape,
        grid_spec=pltpu.PrefetchScalarGridSpec(
            num_scalar_prefetch=2, grid=(lay.n_tiles,),
            in_specs=[pl.BlockSpec((tm, d), lambda i, o, p: (i, 0)),
                      pl.BlockSpec((None, 8, d), lambda i, o, p: (lay.seg(i), 0, 0)),
                      pl.BlockSpec((tm, 8), lambda i, o, p: (i, 0)),
                      pl.BlockSpec(memory_space=pl.ANY)],
            out_specs=tuple(lay.stream_specs(d, pair_out)),
            scratch_shapes=[pltpu.VMEM((2, n_loc, d), BF16), pltpu.SemaphoreType.DMA((2, n_exp))]),
        input_output_aliases={} if pair_out else {2: 0},
        compiler_params=_cparams(("arbitrary",)),
        name="moe_combine",
    )(off, pcs, y, mod, rinfo_t, zs)
    return out if pair_out else out[0]


def _round_up(x, m):
    return (x + m - 1) // m * m


def _moe_plan_kernel(pc_ref, off_ref, te_ref, tv_ref, tf_ref, xb_ref, nx_ref, goff_ref, gn_ref,
                     *, n_tiles, n_exp, tmm, n_sorted):
    shift = tmm.bit_length() - 1
    n_rt = n_sorted // tmm
    zrows = tmm // 2
    base = jnp.int32(0)
    tots, bases, regions = [], [], []
    for e in range(n_exp):
        def seg_body(i, run, e=e, base=base):
            off_ref[i * n_exp + e] = base + run
            return run + pc_ref[i * n_exp + e]
        tot = lax.fori_loop(0, n_tiles, seg_body, jnp.int32(0))
        region = ((tot + (tmm - 1)) >> shift) << shift
        goff_ref[e] = base + tot
        gn_ref[e] = region - tot
        tots.append(tot)
        bases.append(base)
        regions.append(region)
        base = base + region
    goff_ref[n_exp] = base
    gn_ref[n_exp] = (n_sorted - base) // zrows
    last_active = jnp.maximum((base >> shift) - 1, 0)
    following = [None] * n_exp
    cur = jnp.int32(-1)
    for e in reversed(range(n_exp)):
        following[e] = cur
        cur = jnp.where(regions[e] > 0, jnp.int32(e), cur)

    def idle_body(r, _):
        te_ref[r] = n_exp - 1
        tv_ref[r] = 0
        tf_ref[r] = 0
        nx_ref[r] = -1
        xb_ref[r] = jnp.minimum(r, last_active)
        return 0
    lax.fori_loop(0, n_rt, idle_body, 0)
    for e in range(n_exp):
        t0 = bases[e] >> shift

        def tile_body(k, _, e=e, t0=t0):
            r = t0 + k
            te_ref[r] = e
            tv_ref[r] = jnp.clip(tots[e] - k * tmm, 0, tmm)
            tf_ref[r] = jnp.where(k == 0, 1, 0)
            nx_ref[r] = jnp.where(k == 0, following[e], -1)
            return 0
        lax.fori_loop(0, regions[e] >> shift, tile_body, 0)


def moe_plan(pcs, n_tiles, n_exp, tmm, n_sorted):
    assert tmm & (tmm - 1) == 0
    n_rt = n_sorted // tmm
    i32 = lambda n: jax.ShapeDtypeStruct((n,), jnp.int32)
    smem = pl.BlockSpec(memory_space=pltpu.SMEM)
    return pl.pallas_call(
        functools.partial(_moe_plan_kernel, n_tiles=n_tiles, n_exp=n_exp, tmm=tmm, n_sorted=n_sorted),
        out_shape=[i32(n_tiles * n_exp)] + [i32(n_rt)] * 5 + [i32(n_exp + 1)] * 2,
        in_specs=[smem],
        out_specs=[smem] * 8,
        name="moe_plan",
    )(pcs)


def moe_layer(lay, routed, mod, layer, wg, wu, wd, tmm, pair_out=False):
    y, h, rinfo, pc = routed
    t, d = y.shape
    n_exp = pc.shape[1]
    pcs = pc[:, :, 0].astype(jnp.int32).reshape(-1)
    n_loc = 2 * lay.tm + SEG_ALIGN * n_exp
    n_sorted = _round_up(2 * t + lay.n_tiles * n_exp * (SEG_ALIGN - 1) + n_exp * (tmm - 1), tmm)
    off, te, tv, tf, xb, nx, gap_off, gap_n = moe_plan(pcs, lay.n_tiles, n_exp, tmm, n_sorted)
    xs = moe_permute(lay, h, rinfo, off, pcs, gap_off, gap_n, n_sorted, n_loc, n_exp, tmm)
    zs = moe_experts(xs, te, tv, tf, xb, nx, layer, wg, wu, wd, tmm)
    return moe_combine(lay, y, mod, rinfo.T, zs, off, pcs, n_loc, n_exp, pair_out)


def _rope_perm():
    d = np.arange(QK_ROPE)
    return QK_NOPE + (d ^ (QK_ROPE // 4))


def _pad_cols(x, n):
    return jnp.pad(x, ((0, 0),) * (x.ndim - 1) + ((0, n - x.shape[-1]),))


def _mla_weights(w_q_a, w_q_b, w_kv_a, w_kv_b, q_norm, k_norm, kv_lora):
    perm = _rope_perm()
    nh = N_HEADS
    wkr = w_kv_a[:, kv_lora:]
    d = w_kv_a.shape[0]
    zl = jnp.zeros((d, QK_NOPE), F32)
    kr_blk = _pad_cols(jnp.concatenate([zl, wkr], axis=1), LANES)
    krp_blk = _pad_cols(jnp.concatenate([zl, wkr[:, perm - QK_NOPE]], axis=1), LANES)
    w_a = jnp.concatenate([w_q_a, w_kv_a[:, :kv_lora], kr_blk, krp_blk], axis=1).astype(BF16)
    wq = w_q_b.reshape(-1, nh, QK_DIM).transpose(1, 0, 2)
    wq_main = _pad_cols(wq, HEAD_PAD)
    wq_part = _pad_cols(jnp.concatenate([jnp.zeros_like(wq[..., :QK_NOPE]), wq[..., perm]], axis=-1), HEAD_PAD)
    wq_h = jnp.concatenate([wq_main, wq_part], axis=-1).astype(BF16)
    wkv = w_kv_b.reshape(-1, nh, QK_NOPE + V_DIM).transpose(1, 0, 2)
    wk_h = _pad_cols(wkv[..., :QK_NOPE], HEAD_PAD).astype(BF16)
    wv = wkv[..., QK_NOPE:]
    wv_h = jnp.concatenate([wv[0::2], wv[1::2]], axis=-1).astype(BF16)
    nrm = jnp.stack([_pad_cols(q_norm, LANES), _pad_cols(jnp.concatenate([jnp.zeros((QK_NOPE,), F32), q_norm[perm]]), LANES),
                     _pad_cols(k_norm, LANES), _pad_cols(jnp.concatenate([jnp.zeros((QK_NOPE,), F32), k_norm[perm]]), LANES)])
    nrm = jnp.pad(nrm, ((0, 4), (0, 0)))
    return w_a, wq_h, wk_h, wv_h, nrm


def _rope_tables(lat_len, tm):
    rows = lat_len // GRID_W
    t_row = np.repeat(np.arange(rows, dtype=np.float32), GRID_W)
    t_col = np.tile(np.arange(GRID_W, dtype=np.float32), rows)
    half = QK_ROPE // 2
    inv = (ROPE_BASE ** (-np.arange(0, half, 2, dtype=np.float32) / half)).astype(np.float32)
    ang = jnp.concatenate([jnp.asarray(t_row)[:, None] * inv, jnp.asarray(t_col)[:, None] * inv], axis=-1)
    cos, sin = jnp.cos(ang), jnp.sin(ang)
    nf = QK_ROPE // 4
    cos_l = jnp.concatenate([cos[:, :nf], cos[:, :nf], cos[:, nf:], cos[:, nf:]], axis=-1)
    sin_l = jnp.concatenate([-sin[:, :nf], sin[:, :nf], -sin[:, nf:], sin[:, nf:]], axis=-1)
    one = jnp.ones((lat_len, QK_NOPE), F32)
    cos_t = _pad_cols(jnp.concatenate([one, cos_l], axis=-1), LANES)
    cos_t = cos_t.at[:, QK_DIM:].set(1.0)
    sin_t = _pad_cols(jnp.concatenate([jnp.zeros((lat_len, QK_NOPE), F32), sin_l], axis=-1), LANES)
    cos_t = jnp.concatenate([jnp.ones((tm, LANES), F32), cos_t], axis=0)
    sin_t = jnp.concatenate([jnp.zeros((tm, LANES), F32), sin_t], axis=0)
    return cos_t, sin_t


def mla_layer(lay, y, mod, nw, route, cache_ckv, cache_krope, cos_t, sin_t, w_q_a, q_a_norm, w_q_b, w_kv_a,
              kv_a_norm, w_kv_b, q_norm, k_norm, w_o):
    q_lora = w_q_a.shape[1]
    kv_lora = kv_a_norm.shape[0]
    w_a, wq_h, wk_h, wv_h, nrm = _mla_weights(w_q_a, w_q_b, w_kv_a, w_kv_b, q_norm, k_norm, kv_lora)
    ckv, kr, q, k, v2 = mla_qkv(lay, y, mod, nw, w_a, q_a_norm.reshape(1, -1), kv_a_norm.reshape(1, -1),
                                wq_h, wk_h, wv_h, nrm, cos_t, sin_t, q_lora, kv_lora)
    past = cache_ckv.shape[1]
    ckr = jnp.pad(cache_krope.reshape(-1, QK_ROPE), ((0, 0), (QK_NOPE, LANES - QK_DIM)))
    kc, vc2 = ctx_kv(cache_ckv.reshape(-1, kv_lora), ckr, wk_h, wv_h, nrm, past)
    o_ctx = attention(q, k, v2, 0, lay.n_ctx_b, lay.ctx_len, min(lay.ctx_len, 256), pairs=N_HEADS // 2)
    o_lat = attention(q, k, v2, lay.t_ctx, lay.n_lat_b, lay.lat_len, min(lay.lat_len, 512), kc, vc2, past, pairs=2)
    routed = proj_route(lay, y, mod, (o_ctx, o_lat), w_o.astype(BF16), route)
    new_ckv = ckv[:lay.t_ctx].reshape(lay.n_ctx_b, lay.ctx_len, kv_lora)
    new_kr = kr[:lay.t_ctx, QK_NOPE:QK_DIM].reshape(lay.n_ctx_b, lay.ctx_len, QK_ROPE)
    return routed, new_ckv, new_kr


def kernel(x_prompt, x_sample, c, cache_ckv, cache_krope, state_ssm_re, state_ssm_im, c_ctx, w_mod, b_mod, norm1_w, norm2_w, mla_w_q_a, mla_q_a_norm, mla_w_q_b, mla_w_kv_a, mla_kv_a_norm, mla_w_kv_b, mla_q_norm, mla_k_norm, mla_w_o, ssm_w_in, ssm_a_re, ssm_a_im, ssm_log_dt, ssm_b_re, ssm_b_im, ssm_c_re, ssm_c_im, ssm_d, ssm_w_glu, w_router, b_router, moe_w_gate, moe_w_up, moe_w_down):
    n_ctx_b, ctx_len, d = x_prompt.shape
    n_lat_b, lat_len, _ = x_sample.shape
    depth = w_mod.shape[0]
    lay = Layout(n_ctx_b, ctx_len, n_lat_b, lat_len, tm=512)
    assert n_lat_b + 1 <= 8
    y = (x_prompt.reshape(-1, d), x_sample.reshape(-1, d))
    cond8 = jnp.pad(jnp.concatenate([c_ctx[None, :], c], axis=0), ((0, 7 - n_lat_b), (0, 0)))
    mods = adaln_all(cond8, w_mod, b_mod)
    mods = jnp.pad(mods.reshape(depth, 8, 6, d), ((0, 0), (0, 0), (0, 2), (0, 0)))
    cos_t, sin_t = _rope_tables(lat_len, lay.tm)
    n_exp = b_router.shape[0]
    wr_hi = w_router.astype(BF16)
    wr_lo = (w_router - wr_hi.astype(F32)).astype(BF16)
    wr_t = jnp.concatenate([wr_hi.T, wr_lo.T], axis=0)
    br = b_router.reshape(n_exp, 1)
    tri = jnp.asarray(np.triu(np.ones((lay.tm, lay.tm), np.float32), k=1), BF16)
    ckv_out, kr_out, sre_out, sim_out = [], [], [], []
    for i in range(depth):
        j = i // 2
        mod = mods[i]
        nw1 = norm1_w[i].reshape(1, d)
        route = (norm2_w[i].reshape(1, d), wr_t, br, tri)
        if i % 2 == 0:
            routed, ckv_p, kr_p = mla_layer(lay, y, mod, nw1, route, cache_ckv[:, j], cache_krope[:, j], cos_t, sin_t,
                                            mla_w_q_a[j], mla_q_a_norm[j], mla_w_q_b[j], mla_w_kv_a[j],
                                            mla_kv_a_norm[j], mla_w_kv_b[j], mla_q_norm[j], mla_k_norm[j], mla_w_o[j])
            ckv_out.append(ckv_p)
            kr_out.append(kr_p)
        else:
            routed, s_re, s_im = s5_layer(lay, y, mod, nw1, route, state_ssm_re[:, j], state_ssm_im[:, j], ssm_w_in[j],
                                          ssm_a_re[j], ssm_a_im[j], ssm_log_dt[j], ssm_b_re[j], ssm_b_im[j],
                                          ssm_c_re[j], ssm_c_im[j], ssm_d[j], ssm_w_glu[j])
            sre_out.append(s_re)
            sim_out.append(s_im)
        y = moe_layer(lay, routed, mod, i, moe_w_gate, moe_w_up, moe_w_down, tmm=512, pair_out=(i == depth - 1))
    yp = y[0].reshape(n_ctx_b, ctx_len, d)
    ys = y[1].reshape(n_lat_b, lat_len, d)
    return (yp, ys, jnp.stack(ckv_out, axis=1), jnp.stack(kr_out, axis=1),
            jnp.stack(sre_out, axis=1), jnp.stack(sim_out, axis=1))
```
